```python
import math
import jax, jax.numpy as jnp
from jax import lax
import numpy as np

D_MODEL = 1024
BATCH = 16
SEQ = 256
DEPTH = 1
DEC_BATCH = 8
DEC_SEQ = 1024
PAST_LEN = 256

GRID_W = 64
CHUNK = 64
EPS = 1e-6
N_MOD = 9
D_FF = 2816
GLA_HEADS = 4
GLA_DK = 128
GLA_DV = 256
GLA_LOWRANK = 16
GLA_TAU = 16.0
GLA_QK = GLA_HEADS * GLA_DK
GLA_V = GLA_HEADS * GLA_DV
SSD_HEADS = 16
SSD_HEAD_DIM = 64
SSD_GROUPS = 2
SSD_STATE = 128
SSD_INNER = SSD_HEADS * SSD_HEAD_DIM
SSD_BC = SSD_GROUPS * SSD_STATE
SSD_CONV_DIM = SSD_INNER + 2 * SSD_BC
CONV_K = 3
D_MIX = GLA_V + SSD_INNER
IN_SIZES = (GLA_QK, GLA_QK, GLA_V, GLA_V, GLA_LOWRANK, GLA_LOWRANK,
            SSD_INNER, SSD_CONV_DIM, SSD_HEADS, SSD_HEADS)
N_IN = sum(IN_SIZES)

kernel_name = 'hymba_gla_ssd_macaron_dit_step'


def _rmsnorm(x, w):
    x32 = x.astype(jnp.float32)
    y = x32 * lax.rsqrt(jnp.mean(x32 * x32, axis=-1, keepdims=True) + EPS)
    return (y * w.astype(jnp.float32)).astype(x.dtype)


def _swiglu(h, w_in, w_out):
    g, u = jnp.split(h @ w_in, 2, axis=-1)
    return (jax.nn.silu(g) * u) @ w_out


def _gla_scan(q, k, v, log_a, h0):
    b, l, h, dk = q.shape
    dv = v.shape[-1]
    nc = l // CHUNK
    f32 = jnp.float32
    q = q.astype(f32).reshape(b, nc, CHUNK, h, dk)
    k = k.astype(f32).reshape(b, nc, CHUNK, h, dk)
    v = v.astype(f32).reshape(b, nc, CHUNK, h, dv)
    g = jnp.cumsum(log_a.astype(f32).reshape(b, nc, CHUNK, h, dk), axis=2)
    g_ref = g[:, :, CHUNK // 2:CHUNK // 2 + 1]
    q_i = q * jnp.exp(g - g_ref)
    k_i = k * jnp.exp(g_ref - g)
    causal = jnp.tril(jnp.ones((CHUNK, CHUNK), bool))
    scores = jnp.einsum('bcthd,bcshd->bchts', q_i, k_i)
    scores = jnp.where(causal, scores, 0.0)
    o_intra = jnp.einsum('bchts,bcshv->bcthv', scores, v)
    g_last = g[:, :, -1]
    k_state = k * jnp.exp(g_last[:, :, None] - g)
    chunk_kv = jnp.einsum('bcshd,bcshv->bchdv', k_state, v)
    chunk_decay = jnp.exp(g_last)

    def step(state, inp):
        kv_c, dec_c = inp
        return state * dec_c[..., None] + kv_c, state

    final, prev = lax.scan(step, h0.astype(f32),
                           (jnp.moveaxis(chunk_kv, 1, 0), jnp.moveaxis(chunk_decay, 1, 0)))
    prev = jnp.moveaxis(prev, 0, 1)
    o_inter = jnp.einsum('bcthd,bchdv->bcthv', q * jnp.exp(g), prev)
    return (o_intra + o_inter).reshape(b, l, h, dv), final


def _ssd_scan(x, dt, a, bm, cm, h0):
    b, l, h, p = x.shape
    n = bm.shape[-1]
    nc = l // CHUNK
    f32 = jnp.float32
    x = x.astype(f32).reshape(b, nc, CHUNK, h, p)
    dt = dt.astype(f32).reshape(b, nc, CHUNK, h)
    bm = bm.astype(f32).reshape(b, nc, CHUNK, h, n)
    cm = cm.astype(f32).reshape(b, nc, CHUNK, h, n)
    cum = jnp.cumsum(dt * a, axis=2)
    causal = jnp.tril(jnp.ones((CHUNK, CHUNK), bool))[None, None, :, :, None]
    seg = cum[:, :, :, None, :] - cum[:, :, None, :, :]
    decay_ts = jnp.exp(jnp.where(causal, seg, -jnp.inf))
    scores = jnp.einsum('bcthn,bcshn->bctsh', cm, bm) * decay_ts * dt[:, :, None, :, :]
    y_intra = jnp.einsum('bctsh,bcshp->bcthp', scores, x)
    w_end = jnp.exp(cum[:, :, -1:, :] - cum) * dt
    chunk_state = jnp.einsum('bcsh,bcshn,bcshp->bchpn', w_end, bm, x)
    chunk_decay = jnp.exp(cum[:, :, -1, :])

    def step(state, inp):
        st_c, dec_c = inp
        return state * dec_c[:, :, None, None] + st_c, state

    final, prev = lax.scan(step, h0.astype(f32),
                           (jnp.moveaxis(chunk_state, 1, 0), jnp.moveaxis(chunk_decay, 1, 0)))
    prev = jnp.moveaxis(prev, 0, 1)
    y_inter = jnp.einsum('bcthn,bchpn->bcthp', cm, prev) * jnp.exp(cum)[..., None]
    return (y_intra + y_inter).reshape(b, l, h, p), final


def _dwconv_grid(u, w, bias, rows):
    b, l, ch = u.shape
    grid = u.reshape(b, rows, l // rows, ch)
    out = lax.conv_general_dilated(grid, w[:, :, None, :].astype(u.dtype), (1, 1), 'SAME',
                                   dimension_numbers=('NHWC', 'HWIO', 'NHWC'),
                                   feature_group_count=ch)
    return out.reshape(b, l, ch) + bias


def _mixer(h, lw, gla_h0, ssd_h0, rows):
    b, l, _ = h.shape
    f32 = jnp.float32
    flip = lambda t: jnp.flip(t, 1)
    split_idx = np.cumsum(IN_SIZES)[:-1].tolist()
    q, k, v, r, af, ab, z, xbc, dtf, dtb = jnp.split(h @ lw['w_in'], split_idx, axis=-1)
    q = q.reshape(b, l, GLA_HEADS, GLA_DK) * (GLA_DK ** -0.5)
    k = k.reshape(b, l, GLA_HEADS, GLA_DK)
    v = v.reshape(b, l, GLA_HEADS, GLA_DV)

    def log_gate(lr, d):
        pre = (lr @ lw['gla_w_a2'][d] + lw['gla_b_a'][d]).astype(f32)
        return (jax.nn.log_sigmoid(pre) / GLA_TAU).reshape(b, l, GLA_HEADS, GLA_DK)

    o_f, sg_f = _gla_scan(q, k, v, log_gate(af, 0), gla_h0[:, 0])
    o_b, sg_b = _gla_scan(flip(q), flip(k), flip(v), flip(log_gate(ab, 1)), gla_h0[:, 1])
    o = (o_f + flip(o_b)).astype(h.dtype)
    o = _rmsnorm(o, lw['gla_norm_w']) * jax.nn.silu(r.reshape(b, l, GLA_HEADS, GLA_DV))
    xbc = jax.nn.silu(_dwconv_grid(xbc, lw['conv_w'], lw['conv_b'], rows))
    xs, bm, cm = jnp.split(xbc, [SSD_INNER, SSD_INNER + SSD_BC], axis=-1)
    xs = xs.reshape(b, l, SSD_HEADS, SSD_HEAD_DIM)
    rep = SSD_HEADS // SSD_GROUPS
    bm = jnp.repeat(bm.reshape(b, l, SSD_GROUPS, SSD_STATE), rep, axis=2)
    cm = jnp.repeat(cm.reshape(b, l, SSD_GROUPS, SSD_STATE), rep, axis=2)

    def dt_a(raw, d):
        dt = jax.nn.softplus(raw.astype(f32) + lw['dt_bias'][d].astype(f32))
        return dt, -jnp.exp(lw['a_log'][d].astype(f32))

    dt_f, a_f = dt_a(dtf, 0)
    dt_b, a_b = dt_a(dtb, 1)
    y_f, ss_f = _ssd_scan(xs, dt_f, a_f, bm, cm, ssd_h0[:, 0])
    y_b, ss_b = _ssd_scan(flip(xs), flip(dt_b), a_b, flip(bm), flip(cm), ssd_h0[:, 1])
    y = y_f + flip(y_b) + xs.astype(f32) * lw['d_skip'].astype(f32)[:, None]
    y = _rmsnorm(y.reshape(b, l, SSD_INNER).astype(h.dtype) * jax.nn.silu(z), lw['ssd_norm_w'])
    out = jnp.concatenate([o.reshape(b, l, GLA_V), y], axis=-1) @ lw['w_out']
    return out, jnp.stack([sg_f, sg_b], axis=1), jnp.stack([ss_f, ss_b], axis=1)


def _layer(x, mod, lw, gla_h0, ssd_h0, rows):
    sh1, sc1, g1, sh2, sc2, g2, sh3, sc3, g3 = jnp.split(mod, N_MOD, axis=-1)
    h = _rmsnorm(x, lw['norm_ffn1']) * (1 + sc1) + sh1
    x = x + 0.5 * g1 * _swiglu(h, lw['ffn1_w_in'], lw['ffn1_w_out'])
    h = _rmsnorm(x, lw['norm_mix']) * (1 + sc2) + sh2
    m, sg, ss = _mixer(h, lw, gla_h0, ssd_h0, rows)
    x = x + g2 * m
    h = _rmsnorm(x, lw['norm_ffn2']) * (1 + sc3) + sh3
    x = x + 0.5 * g3 * _swiglu(h, lw['ffn2_w_in'], lw['ffn2_w_out'])
    return x, sg, ss


def setup_inputs(seed: int = 0) -> dict:
    key = jax.random.key(seed)
    ks = jax.random.split(key, 32)
    f32 = jnp.float32
    nrm = lambda k, shape, scale: jax.random.normal(k, shape, f32) * scale
    dt0 = jnp.exp(jax.random.uniform(ks[20], (DEPTH, 2, SSD_HEADS), f32,
                                     math.log(1e-3), math.log(1e-1)))
    return {
        'x_prompt': nrm(ks[0], (BATCH, SEQ, D_MODEL), 1.0),
        'x_sample': nrm(ks[1], (DEC_BATCH, DEC_SEQ, D_MODEL), 1.0),
        'state_gla': nrm(ks[2], (DEC_BATCH, DEPTH, 2, GLA_HEADS, GLA_DK, GLA_DV), 1.0),
        'state_ssd': nrm(ks[3], (DEC_BATCH, DEPTH, 2, SSD_HEADS, SSD_HEAD_DIM, SSD_STATE), 0.1),
        'c': nrm(ks[4], (DEC_BATCH, D_MODEL), 1.0),
        'c_ctx': nrm(ks[5], (D_MODEL,), 1.0),
        'norm_ffn1': 1.0 + nrm(ks[6], (DEPTH, D_MODEL), 0.02),
        'norm_mix': 1.0 + nrm(ks[7], (DEPTH, D_MODEL), 0.02),
        'norm_ffn2': 1.0 + nrm(ks[8], (DEPTH, D_MODEL), 0.02),
        'w_mod': nrm(ks[9], (DEPTH, D_MODEL, N_MOD * D_MODEL), 0.5 * D_MODEL ** -0.5),
        'b_mod': nrm(ks[10], (DEPTH, N_MOD * D_MODEL), 0.02),
        'ffn1_w_in': nrm(ks[11], (DEPTH, D_MODEL, 2 * D_FF), D_MODEL ** -0.5),
        'ffn1_w_out': nrm(ks[12], (DEPTH, D_FF, D_MODEL), D_FF ** -0.5),
        'ffn2_w_in': nrm(ks[13], (DEPTH, D_MODEL, 2 * D_FF), D_MODEL ** -0.5),
        'ffn2_w_out': nrm(ks[14], (DEPTH, D_FF, D_MODEL), D_FF ** -0.5),
        'w_in': nrm(ks[15], (DEPTH, D_MODEL, N_IN), D_MODEL ** -0.5),
        'gla_w_a2': nrm(ks[16], (DEPTH, 2, GLA_LOWRANK, GLA_QK), GLA_LOWRANK ** -0.5),
        'gla_b_a': nrm(ks[17], (DEPTH, 2, GLA_QK), 0.1),
        'gla_norm_w': 1.0 + nrm(ks[18], (DEPTH, GLA_DV), 0.02),
        'conv_w': nrm(ks[19], (DEPTH, CONV_K, CONV_K, SSD_CONV_DIM), 1.0 / CONV_K),
        'conv_b': nrm(ks[21], (DEPTH, SSD_CONV_DIM), 0.02),
        'dt_bias': dt0 + jnp.log(-jnp.expm1(-dt0)),
        'a_log': jnp.log(jax.random.uniform(ks[22], (DEPTH, 2, SSD_HEADS), f32, 1.0, 16.0)),
        'd_skip': 1.0 + nrm(ks[23], (DEPTH, SSD_HEADS), 0.1),
        'ssd_norm_w': 1.0 + nrm(ks[24], (DEPTH, SSD_INNER), 0.02),
        'w_out': nrm(ks[25], (DEPTH, D_MIX, D_MODEL), D_MIX ** -0.5),
        'final_norm': 1.0 + nrm(ks[26], (D_MODEL,), 0.02),
    }


def reference(x_prompt, x_sample, state_gla, state_ssd, c, c_ctx, norm_ffn1, norm_mix,
              norm_ffn2, w_mod, b_mod, ffn1_w_in, ffn1_w_out, ffn2_w_in, ffn2_w_out, w_in,
              gla_w_a2, gla_b_a, gla_norm_w, conv_w, conv_b, dt_bias, a_log, d_skip,
              ssd_norm_w, w_out, final_norm):
    xp, xs = x_prompt, x_sample
    rows = xs.shape[1] // GRID_W
    gla_states, ssd_states = [], []
    for i in range(DEPTH):
        lw = {'norm_ffn1': norm_ffn1[i], 'norm_mix': norm_mix[i], 'norm_ffn2': norm_ffn2[i],
              'ffn1_w_in': ffn1_w_in[i], 'ffn1_w_out': ffn1_w_out[i],
              'ffn2_w_in': ffn2_w_in[i], 'ffn2_w_out': ffn2_w_out[i],
              'w_in': w_in[i], 'gla_w_a2': gla_w_a2[i], 'gla_b_a': gla_b_a[i],
              'gla_norm_w': gla_norm_w[i], 'conv_w': conv_w[i], 'conv_b': conv_b[i],
              'dt_bias': dt_bias[i], 'a_log': a_log[i], 'd_skip': d_skip[i],
              'ssd_norm_w': ssd_norm_w[i], 'w_out': w_out[i]}
        mod_ctx = (jax.nn.silu(c_ctx) @ w_mod[i] + b_mod[i])[None, None, :]
        mod_lat = (jax.nn.silu(c) @ w_mod[i] + b_mod[i])[:, None, :]
        nb = xp.shape[0]
        gla0 = jnp.zeros((nb, 2, GLA_HEADS, GLA_DK, GLA_DV), jnp.float32)
        ssd0 = jnp.zeros((nb, 2, SSD_HEADS, SSD_HEAD_DIM, SSD_STATE), jnp.float32)
        xp, sg, ss = _layer(xp, mod_ctx, lw, gla0, ssd0, 1)
        gla_states.append(sg)
        ssd_states.append(ss)
        xs, _, _ = _layer(xs, mod_lat, lw, state_gla[:, i], state_ssd[:, i], rows)
    y_prompt = _rmsnorm(xp, final_norm)
    y_sample = _rmsnorm(xs, final_norm)
    new_state_gla = jnp.stack(gla_states, axis=1)
    new_state_ssd = jnp.stack(ssd_states, axis=1)
    return (y_prompt, y_sample, new_state_gla, new_state_ssd)
```

```python
import functools

import jax
import jax.numpy as jnp
from jax import lax
from jax.experimental import pallas as pl
from jax.experimental.pallas import tpu as pltpu

F32 = jnp.float32
BF16 = jnp.bfloat16

D_MODEL = 1024
DEPTH = 1
GRID_W = 64
CHUNK = 64
EPS = 1e-6
N_MOD = 9
D_FF = 2816
GLA_HEADS = 4
GLA_DK = 128
GLA_DV = 256
GLA_LOWRANK = 16
GLA_TAU = 16.0
GLA_QK = GLA_HEADS * GLA_DK
GLA_V = GLA_HEADS * GLA_DV
SSD_HEADS = 16
SSD_HEAD_DIM = 64
SSD_GROUPS = 2
SSD_STATE = 128
SSD_INNER = SSD_HEADS * SSD_HEAD_DIM
SSD_BC = SSD_GROUPS * SSD_STATE
SSD_CONV_DIM = SSD_INNER + 2 * SSD_BC
CONV_K = 3
D_MIX = GLA_V + SSD_INNER

LANES = 128
SUBLANES = 8
VMEM_LIMIT_BYTES = 56 * 1024 * 1024

COL_Q = 0
COL_K = COL_Q + GLA_QK
COL_V = COL_K + GLA_QK
COL_R = COL_V + GLA_V
COL_Z = COL_R + GLA_V
COL_XBC = COL_Z + SSD_INNER
COL_SMALL = COL_XBC + SSD_CONV_DIM
N_PROJ = COL_SMALL + LANES
SM_AF = 0
SM_AB = SM_AF + GLA_LOWRANK
SM_DTF = SM_AB + GLA_LOWRANK
SM_DTB = SM_DTF + SSD_HEADS
HEADS_PER_GROUP = SSD_HEADS // SSD_GROUPS
GROUP_W = HEADS_PER_GROUP * SSD_HEAD_DIM

TOKEN_TILE = 256


def _dot(a, b):
    return jnp.dot(a, b, preferred_element_type=F32)


def _dot_nt(a, b):
    return lax.dot_general(a, b, (((1,), (1,)), ((), ())), preferred_element_type=F32)


def _dot_tn(a, b):
    return lax.dot_general(a, b, (((0,), (0,)), ((), ())), preferred_element_type=F32)


def _silu(x):
    return x * jax.nn.sigmoid(x)


def _softplus(x):
    return jnp.maximum(x, 0.0) + jnp.log1p(jnp.exp(-jnp.abs(x)))


def _log_sigmoid(x):
    return jnp.minimum(x, 0.0) - jnp.log1p(jnp.exp(-jnp.abs(x)))


def _rmsnorm(x, w):
    ms = jnp.mean(x * x, axis=-1, keepdims=True)
    return x * lax.rsqrt(ms + EPS) * w


def _split3(x):
    hi = x.astype(BF16)
    r1 = x - hi.astype(F32)
    mid = r1.astype(BF16)
    lo = (r1 - mid.astype(F32)).astype(BF16)
    return hi, mid, lo


def _exact_dot(sel_bf16, x):
    hi, mid, lo = _split3(x)
    return (_dot(sel_bf16, lo) + _dot(sel_bf16, mid)) + _dot(sel_bf16, hi)


def _resident(shape):
    nd = len(shape)
    return pl.BlockSpec(shape, lambda *_: (0,) * nd, pipeline_mode=pl.Buffered(1))


def _params(n_axes):
    return pltpu.CompilerParams(dimension_semantics=("arbitrary",) * n_axes,
                                vmem_limit_bytes=VMEM_LIMIT_BYTES)


def _mod_body(c_ref, w_ref, b_ref, out_ref):
    a = _silu(c_ref[...]).astype(BF16)
    out_ref[...] = _dot(a, w_ref[...].astype(BF16)) + b_ref[...]


def _mod_call(cc, w_mod, b_mod):
    n_rows = cc.shape[0]
    tn = D_MODEL
    return pl.pallas_call(
        _mod_body,
        grid=(N_MOD * D_MODEL // tn,),
        in_specs=[pl.BlockSpec((n_rows, D_MODEL), lambda j: (0, 0)),
                  pl.BlockSpec((D_MODEL, tn), lambda j: (0, j)),
                  pl.BlockSpec((1, tn), lambda j: (0, j))],
        out_specs=pl.BlockSpec((n_rows, tn), lambda j: (0, j)),
        out_shape=jax.ShapeDtypeStruct((n_rows, N_MOD * D_MODEL), F32),
        compiler_params=_params(1),
        name="mod",
    )(cc, w_mod, b_mod)


def _ffn_body(*refs, sub, has_mix, has_final):
    it = iter(refs)
    x_ref, mod_ref, nw_ref, win_ref, wout_ref = (next(it) for _ in range(5))
    if has_mix:
        o_ref, y_ref, z_ref, snw_ref, wo_ref = (next(it) for _ in range(5))
    if has_final:
        fn_ref = next(it)
    out_ref = next(it)

    x = x_ref[...]
    if has_mix:
        g2 = mod_ref[0, 5:6, :]
        yn = _rmsnorm(y_ref[...] * _silu(z_ref[...]), snw_ref[...])
        m = _dot(o_ref[...], wo_ref[:GLA_V, :]) + _dot(yn.astype(BF16), wo_ref[GLA_V:, :])
        x = x + g2 * m
    sh = mod_ref[0, 3 * sub:3 * sub + 1, :]
    sc = mod_ref[0, 3 * sub + 1:3 * sub + 2, :]
    gate = mod_ref[0, 3 * sub + 2:3 * sub + 3, :]
    h = (_rmsnorm(x, nw_ref[...]) * (1.0 + sc) + sh).astype(BF16)
    g = _dot(h, win_ref[:, :D_FF])
    u = _dot(h, win_ref[:, D_FF:])
    act = (_silu(g) * u).astype(BF16)
    x = x + (0.5 * gate) * _dot(act, wout_ref[...])
    if has_final:
        x = _rmsnorm(x, fn_ref[...])
    out_ref[...] = x


def _ffn_call(x, mod, norm_w, w_in, w_out, *, sub, seq_len, mix=None, final_w=None):
    m_tok = x.shape[0]
    tm = TOKEN_TILE
    tiles_per_seq = seq_len // tm
    shared_mod = mod.shape[0] == 1
    mod_map = (lambda i: (0, 0, 0)) if shared_mod else (lambda i: (i // tiles_per_seq, 0, 0))
    tok = lambda width: pl.BlockSpec((tm, width), lambda i: (i, 0))
    in_specs = [tok(D_MODEL), pl.BlockSpec((1, N_MOD, D_MODEL), mod_map),
                _resident((1, D_MODEL)), _resident(w_in.shape), _resident(w_out.shape)]
    args = [x, mod, norm_w, w_in, w_out]
    if mix is not None:
        o, y, proj, ssd_norm_w, w_mix_out = mix
        in_specs += [tok(GLA_V), tok(SSD_INNER),
                     pl.BlockSpec((tm, SSD_INNER), lambda i: (i, COL_Z // SSD_INNER)),
                     _resident((1, SSD_INNER)), _resident(w_mix_out.shape)]
        args += [o, y, proj, ssd_norm_w, w_mix_out]
    if final_w is not None:
        in_specs.append(_resident((1, D_MODEL)))
        args.append(final_w)
    body = functools.partial(_ffn_body, sub=sub, has_mix=mix is not None,
                             has_final=final_w is not None)
    return pl.pallas_call(
        body,
        grid=(m_tok // tm,),
        in_specs=in_specs,
        out_specs=tok(D_MODEL),
        out_shape=jax.ShapeDtypeStruct((m_tok, D_MODEL), F32),
        compiler_params=_params(1),
        name="ffn_mix" if mix is not None else "ffn",
    )(*args)


def _proj_body(x_ref, mod_ref, nw_ref, w_ref, out_ref):
    sh = mod_ref[0, 3:4, :]
    sc = mod_ref[0, 4:5, :]
    h = (_rmsnorm(x_ref[...], nw_ref[...]) * (1.0 + sc) + sh).astype(BF16)
    out_ref[...] = _dot(h, w_ref[...])


def _proj_call(x, mod, norm_w, w_proj, *, seq_len):
    m_tok = x.shape[0]
    tm = TOKEN_TILE
    tiles_per_seq = seq_len // tm
    shared_mod = mod.shape[0] == 1
    mod_map = (lambda i: (0, 0, 0)) if shared_mod else (lambda i: (i // tiles_per_seq, 0, 0))
    return pl.pallas_call(
        _proj_body,
        grid=(m_tok // tm,),
        in_specs=[pl.BlockSpec((tm, D_MODEL), lambda i: (i, 0)),
                  pl.BlockSpec((1, N_MOD, D_MODEL), mod_map),
                  _resident((1, D_MODEL)), _resident(w_proj.shape)],
        out_specs=pl.BlockSpec((tm, N_PROJ), lambda i: (i, 0)),
        out_shape=jax.ShapeDtypeStruct((m_tok, N_PROJ), F32),
        compiler_params=_params(1),
        name="proj",
    )(x, mod, norm_w, w_proj)


def _tri_masks():
    row = lax.broadcasted_iota(jnp.int32, (CHUNK, CHUNK), 0)
    col = lax.broadcasted_iota(jnp.int32, (CHUNK, CHUNK), 1)
    return row >= col, row <= col


def _gla_body(*refs, seq_len, has_h0, want_state):
    it = iter(refs)
    q_ref, k_ref, v_ref, r_ref, sm_ref, wa_ref, ba_ref, nw_ref = (next(it) for _ in range(8))
    h0_ref = next(it) if has_h0 else None
    o_ref = next(it)
    st_ref = next(it) if want_state else None
    acc_scr, sf_scr, sb_scr = (next(it) for _ in range(3))

    nc = seq_len // CHUNK
    lower, upper = _tri_masks()
    lower_b = jnp.where(lower, 1.0, 0.0).astype(BF16)
    upper_b = jnp.where(upper, 1.0, 0.0).astype(BF16)

    if has_h0:
        sf_scr[...] = h0_ref[0, 0, 0].T
        sb_scr[...] = h0_ref[0, 1, 0].T
    else:
        sf_scr[...] = jnp.zeros_like(sf_scr)
        sb_scr[...] = jnp.zeros_like(sb_scr)
    acc_scr[...] = jnp.zeros_like(acc_scr)

    def chunk(r0, d, s_scr, tri_b, mask, ref_i, last_i):
        sl = pl.ds(r0, CHUNK)
        qc = q_ref[sl, :] * (GLA_DK ** -0.5)
        kc = k_ref[sl, :]
        vc = v_ref[sl, :].astype(BF16)
        pre = _dot(sm_ref[sl, :].astype(BF16), wa_ref[d].astype(BF16)) + ba_ref[d]
        la = _log_sigmoid(pre) * (1.0 / GLA_TAU)
        g = _exact_dot(tri_b, la)
        g_mid = g[ref_i:ref_i + 1, :]
        g_last = g[last_i:last_i + 1, :]
        q_i = (qc * jnp.exp(g - g_mid)).astype(BF16)
        k_i = (kc * jnp.exp(g_mid - g)).astype(BF16)
        scores = jnp.where(mask, _dot_nt(q_i, k_i), 0.0).astype(BF16)
        state = s_scr[...]
        q_g = (qc * jnp.exp(g)).astype(BF16)
        o = _dot(scores, vc) + _dot_nt(q_g, state.astype(BF16))
        k_s = (kc * jnp.exp(g_last - g)).astype(BF16)
        s_scr[...] = state * jnp.exp(g_last) + _dot_tn(vc, k_s)
        acc_scr[sl, :] += o

    def step(c, carry):
        rf = pl.multiple_of(c * CHUNK, CHUNK)
        rb = pl.multiple_of((nc - 1 - c) * CHUNK, CHUNK)
        chunk(rf, 0, sf_scr, lower_b, lower, CHUNK // 2, CHUNK - 1)
        chunk(rb, 1, sb_scr, upper_b, upper, CHUNK - 1 - CHUNK // 2, 0)
        return carry

    lax.fori_loop(0, nc, step, 0)

    o = _rmsnorm(acc_scr[...], nw_ref[...]) * _silu(r_ref[...])
    o_ref[...] = o.astype(o_ref.dtype)
    if want_state:
        st_ref[0, 0, 0] = sf_scr[...].T
        st_ref[0, 1, 0] = sb_scr[...].T


def _gla_call(proj, wa_pad, ba, norm_w, h0, *, n_seq, seq_len, want_state):
    L = seq_len
    blk = lambda width, base: pl.BlockSpec((L, width), lambda s, h: (s, base // width + h))
    in_specs = [blk(GLA_DK, COL_Q), blk(GLA_DK, COL_K), blk(GLA_DV, COL_V), blk(GLA_DV, COL_R),
                pl.BlockSpec((L, LANES), lambda s, h: (s, COL_SMALL // LANES)),
                pl.BlockSpec((2, LANES, GLA_DK), lambda s, h: (0, 0, h)),
                pl.BlockSpec((2, 1, GLA_DK), lambda s, h: (0, 0, h)),
                pl.BlockSpec((1, GLA_DV), lambda s, h: (0, 0))]
    args = [proj, proj, proj, proj, proj, wa_pad, ba, norm_w]
    state_spec = pl.BlockSpec((1, 2, 1, GLA_DK, GLA_DV), lambda s, h: (s, 0, h, 0, 0))
    if h0 is not None:
        in_specs.append(state_spec)
        args.append(h0)
    out_specs = [pl.BlockSpec((L, GLA_DV), lambda s, h: (s, h))]
    out_shape = [jax.ShapeDtypeStruct((n_seq * L, GLA_V), BF16)]
    if want_state:
        out_specs.append(state_spec)
        out_shape.append(jax.ShapeDtypeStruct((n_seq, 2, GLA_HEADS, GLA_DK, GLA_DV), F32))
    body = functools.partial(_gla_body, seq_len=L, has_h0=h0 is not None, want_state=want_state)
    return pl.pallas_call(
        body,
        grid=(n_seq, GLA_HEADS),
        in_specs=in_specs,
        out_specs=out_specs,
        out_shape=out_shape,
        scratch_shapes=[pltpu.VMEM((L, GLA_DV), F32),
                        pltpu.VMEM((GLA_DV, GLA_DK), F32),
                        pltpu.VMEM((GLA_DV, GLA_DK), F32)],
        compiler_params=_params(2),
        name="gla",
    )(*args)


def _ssd_body(*refs, seq_len, grid_rows, has_h0, want_state):
    it = iter(refs)
    (xs_ref, b_ref, c_ref, sm_ref, cwx_ref, cwb_ref, cwc_ref, cbx_ref, cbb_ref, cbc_ref,
     par_ref, dsk_ref) = (next(it) for _ in range(12))
    h0_ref = next(it) if has_h0 else None
    y_ref = next(it)
    st_ref = next(it) if want_state else None
    pad_scr, xs_s, b_s, c_s, stf_scr, stb_scr = (next(it) for _ in range(6))

    L = seq_len
    nc = L // CHUNK
    width = L // grid_rows
    pad = pad_scr.shape[0] - L
    pad //= 2
    grp = pl.program_id(1)

    def conv_into(src_ref, cw_ref, cb_ref, dst_ref):
        ch = src_ref.shape[1]
        pad_scr[0:pad, 0:ch] = jnp.zeros((pad, ch), F32)
        pad_scr[pad + L:pad + L + pad, 0:ch] = jnp.zeros((pad, ch), F32)
        pad_scr[pad:pad + L, 0:ch] = src_ref[...]
        rc = min(2 * CHUNK, L)
        col = lax.broadcasted_iota(jnp.int32, (rc, ch), 0) % width
        di_taps = range(CONV_K) if grid_rows > 1 else (CONV_K // 2,)
        for r0 in range(0, L, rc):
            acc = jnp.broadcast_to(cb_ref[...], (rc, ch))
            for dj in range(CONV_K):
                inner = None
                for di in di_taps:
                    off = pad + r0 + (di - 1) * width + (dj - 1)
                    term = cw_ref[di, dj:dj + 1, :] * pad_scr[off:off + rc, 0:ch]
                    inner = term if inner is None else inner + term
                if grid_rows > 1 and dj == 0:
                    inner = jnp.where(col >= 1, inner, 0.0)
                if grid_rows > 1 and dj == CONV_K - 1:
                    inner = jnp.where(col <= width - 2, inner, 0.0)
                acc = acc + inner
            dst_ref[r0:r0 + rc, :] = _silu(acc)

    conv_into(xs_ref, cwx_ref, cbx_ref, xs_s)
    conv_into(b_ref, cwb_ref, cbb_ref, b_s)
    conv_into(c_ref, cwc_ref, cbc_ref, c_s)

    bias_row = par_ref[0:1, :]
    a_row = -jnp.exp(par_ref[1:2, :]) * par_ref[2:3, :]
    lower, upper = _tri_masks()
    lower_b = jnp.where(lower, 1.0, 0.0).astype(BF16)
    upper_b = jnp.where(upper, 1.0, 0.0).astype(BF16)
    lane_src = lax.broadcasted_iota(jnp.int32, (LANES, GROUP_W), 0)
    lane_head = lax.shift_right_logical(lax.broadcasted_iota(jnp.int32, (LANES, GROUP_W), 1), 6)
    head0 = grp * HEADS_PER_GROUP
    expand_f = jnp.where(lane_src == SM_DTF + head0 + lane_head, 1.0, 0.0).astype(BF16)
    expand_b = jnp.where(lane_src == SM_DTB + head0 + lane_head, 1.0, 0.0).astype(BF16)
    t_idx = lax.broadcasted_iota(jnp.int32, (CHUNK, GROUP_W), 0)
    s_idx = lax.broadcasted_iota(jnp.int32, (CHUNK, GROUP_W), 1) & (CHUNK - 1)
    half = GROUP_W // 2
    blk_r = lax.shift_right_logical(lax.broadcasted_iota(jnp.int32, (half, half), 0), 6)
    blk_c = lax.shift_right_logical(lax.broadcasted_iota(jnp.int32, (half, half), 1), 6)
    same_head = blk_r == blk_c

    if has_h0:
        stf_scr[...] = h0_ref[0, 0]
        stb_scr[...] = h0_ref[0, 1]
    else:
        stf_scr[...] = jnp.zeros_like(stf_scr)
        stb_scr[...] = jnp.zeros_like(stb_scr)
    y_ref[...] = xs_s[...] * dsk_ref[...]

    def chunk(r0, st_scr, tri_b, expand, causal, last_i):
        sl = pl.ds(r0, CHUNK)
        xc = xs_s[sl, :]
        bc = b_s[sl, :].astype(BF16)
        cc = c_s[sl, :].astype(BF16)
        dt = _softplus(sm_ref[sl, :] + bias_row)
        cum = _exact_dot(tri_b, dt * a_row)
        parts = jnp.concatenate(list(_split3(cum)) + list(_split3(dt)), axis=0)
        ex = _dot(parts, expand)
        cum_e = (ex[2 * CHUNK:3 * CHUNK] + ex[CHUNK:2 * CHUNK]) + ex[0:CHUNK]
        dt_e = (ex[5 * CHUNK:6 * CHUNK] + ex[4 * CHUNK:5 * CHUNK]) + ex[3 * CHUNK:4 * CHUNK]
        cum_row = jnp.sum(jnp.where(t_idx == s_idx, cum_e, 0.0), axis=0, keepdims=True)
        decay = jnp.exp(jnp.where(causal, cum_e - cum_row, -jnp.inf))
        cb = _dot_nt(cc, jnp.concatenate([bc] * HEADS_PER_GROUP, axis=0))
        scores = (cb * decay).astype(BF16)
        xdt = (xc * dt_e).astype(BF16)
        y_parts = []
        for j in range(2):
            xh = xdt[:, j * half:(j + 1) * half]
            rep = jnp.concatenate([xh] * (half // CHUNK), axis=0)
            bd = jnp.where(same_head, rep, jnp.zeros_like(rep))
            y_parts.append(_dot(scores[:, j * half:(j + 1) * half], bd))
        y_intra = jnp.concatenate(y_parts, axis=1)
        state = st_scr[...]
        y_inter = _dot(cc, state.astype(BF16)) * jnp.exp(cum_e)
        cum_last = cum_e[last_i:last_i + 1, :]
        xw = (xc * (jnp.exp(cum_last - cum_e) * dt_e)).astype(BF16)
        st_scr[...] = state * jnp.exp(cum_last) + _dot_tn(bc, xw)
        y_ref[sl, :] += y_intra + y_inter

    def step(c, carry):
        rf = pl.multiple_of(c * CHUNK, CHUNK)
        rb = pl.multiple_of((nc - 1 - c) * CHUNK, CHUNK)
        chunk(rf, stf_scr, lower_b, expand_f, t_idx >= s_idx, CHUNK - 1)
        chunk(rb, stb_scr, upper_b, expand_b, t_idx <= s_idx, 0)
        return carry

    lax.fori_loop(0, nc, step, 0)

    if want_state:
        st_ref[0, 0] = stf_scr[...]
        st_ref[0, 1] = stb_scr[...]


def _ssd_call(proj, conv_w, conv_b, par, dsk, h0, *, n_seq, seq_len, grid_rows, want_state):
    L = seq_len
    x_base = COL_XBC
    b_base = COL_XBC + SSD_INNER
    c_base = b_base + SSD_BC
    in_specs = [pl.BlockSpec((L, GROUP_W), lambda s, g: (s, x_base // GROUP_W + g)),
                pl.BlockSpec((L, SSD_STATE), lambda s, g: (s, b_base // SSD_STATE + g)),
                pl.BlockSpec((L, SSD_STATE), lambda s, g: (s, c_base // SSD_STATE + g)),
                pl.BlockSpec((L, LANES), lambda s, g: (s, COL_SMALL // LANES)),
                pl.BlockSpec((CONV_K, CONV_K, GROUP_W), lambda s, g: (0, 0, g)),
                pl.BlockSpec((CONV_K, CONV_K, SSD_STATE),
                             lambda s, g: (0, 0, SSD_INNER // SSD_STATE + g)),
                pl.BlockSpec((CONV_K, CONV_K, SSD_STATE),
                             lambda s, g: (0, 0, (SSD_INNER + SSD_BC) // SSD_STATE + g)),
                pl.BlockSpec((1, GROUP_W), lambda s, g: (0, g)),
                pl.BlockSpec((1, SSD_STATE), lambda s, g: (0, SSD_INNER // SSD_STATE + g)),
                pl.BlockSpec((1, SSD_STATE),
                             lambda s, g: (0, (SSD_INNER + SSD_BC) // SSD_STATE + g)),
                pl.BlockSpec((SUBLANES, LANES), lambda s, g: (0, 0)),
                pl.BlockSpec((1, GROUP_W), lambda s, g: (0, g))]
    args = [proj, proj, proj, proj, conv_w, conv_w, conv_w, conv_b, conv_b, conv_b, par, dsk]
    state_spec = pl.BlockSpec((1, 2, SSD_STATE, GROUP_W), lambda s, g: (s, 0, 0, g))
    if h0 is not None:
        in_specs.append(state_spec)
        args.append(h0)
    out_specs = [pl.BlockSpec((L, GROUP_W), lambda s, g: (s, g))]
    out_shape = [jax.ShapeDtypeStruct((n_seq * L, SSD_INNER), F32)]
    if want_state:
        out_specs.append(state_spec)
        out_shape.append(jax.ShapeDtypeStruct((n_seq, 2, SSD_STATE, SSD_INNER), F32))
    conv_pad = (L // grid_rows + SUBLANES) if grid_rows > 1 else SUBLANES
    body = functools.partial(_ssd_body, seq_len=L, grid_rows=grid_rows,
                             has_h0=h0 is not None, want_state=want_state)
    return pl.pallas_call(
        body,
        grid=(n_seq, SSD_GROUPS),
        in_specs=in_specs,
        out_specs=out_specs,
        out_shape=out_shape,
        scratch_shapes=[pltpu.VMEM((L + 2 * conv_pad, GROUP_W), F32),
                        pltpu.VMEM((L, GROUP_W), F32),
                        pltpu.VMEM((L, SSD_STATE), F32),
                        pltpu.VMEM((L, SSD_STATE), F32),
                        pltpu.VMEM((SSD_STATE, GROUP_W), F32),
                        pltpu.VMEM((SSD_STATE, GROUP_W), F32)],
        compiler_params=_params(2),
        name="ssd",
    )(*args)


def _layer_path(x, mod, lw, h0_gla, h0_ssd, *, n_seq, seq_len, grid_rows, want_state, final_w):
    x1 = _ffn_call(x, mod, lw["norm_ffn1"], lw["ffn1_w_in"], lw["ffn1_w_out"],
                   sub=0, seq_len=seq_len)
    proj = _proj_call(x1, mod, lw["norm_mix"], lw["w_proj"], seq_len=seq_len)
    gla_out = _gla_call(proj, lw["wa_pad"], lw["ba"], lw["gla_norm_w"], h0_gla,
                        n_seq=n_seq, seq_len=seq_len, want_state=want_state)
    ssd_out = _ssd_call(proj, lw["conv_w"], lw["conv_b"], lw["ssd_par"], lw["d_skip_row"], h0_ssd,
                        n_seq=n_seq, seq_len=seq_len, grid_rows=grid_rows, want_state=want_state)
    o, y = gla_out[0], ssd_out[0]
    out = _ffn_call(x1, mod, lw["norm_ffn2"], lw["ffn2_w_in"], lw["ffn2_w_out"],
                    sub=2, seq_len=seq_len, mix=(o, y, proj, lw["ssd_norm_w"], lw["w_out"]),
                    final_w=final_w)
    if want_state:
        return out, gla_out[1], ssd_out[1]
    return out, None, None


def kernel(x_prompt, x_sample, state_gla, state_ssd, c, c_ctx, norm_ffn1, norm_mix, norm_ffn2, w_mod, b_mod, ffn1_w_in, ffn1_w_out, ffn2_w_in, ffn2_w_out, w_in, gla_w_a2, gla_b_a, gla_norm_w, conv_w, conv_b, dt_bias, a_log, d_skip, ssd_norm_w, w_out, final_norm):
    nb, seq, _ = x_prompt.shape
    db, dseq, _ = x_sample.shape
    grid_rows = dseq // GRID_W
    xp = x_prompt.reshape(nb * seq, D_MODEL)
    xs = x_sample.reshape(db * dseq, D_MODEL)
    row = lambda v: v.reshape(1, -1)
    gla_states, ssd_states = [], []
    for i in range(DEPTH):
        last = i == DEPTH - 1
        n_rows = -(-(db + 1) // SUBLANES) * SUBLANES
        cc = jnp.concatenate([c, c_ctx[None, :], jnp.zeros((n_rows - db - 1, D_MODEL), F32)], 0)
        mod = _mod_call(cc, w_mod[i], row(b_mod[i])).reshape(n_rows, N_MOD, D_MODEL)
        mod_lat, mod_ctx = mod[:db], mod[db:db + 1]

        wi = w_in[i]
        o_af = COL_R + GLA_V
        o_z = o_af + 2 * GLA_LOWRANK
        o_xbc = o_z + SSD_INNER
        o_dt = o_xbc + SSD_CONV_DIM
        w_proj = jnp.concatenate(
            [wi[:, :o_af], wi[:, o_z:o_dt], wi[:, o_af:o_z], wi[:, o_dt:],
             jnp.zeros((D_MODEL, LANES - 2 * GLA_LOWRANK - 2 * SSD_HEADS), F32)], axis=1).astype(BF16)
        wa_pad = jnp.zeros((2, LANES, GLA_QK), F32)
        wa_pad = wa_pad.at[0, SM_AF:SM_AF + GLA_LOWRANK].set(gla_w_a2[i, 0])
        wa_pad = wa_pad.at[1, SM_AB:SM_AB + GLA_LOWRANK].set(gla_w_a2[i, 1])
        ssd_par = jnp.zeros((SUBLANES, LANES), F32)
        ssd_par = ssd_par.at[0, SM_DTF:SM_DTB + SSD_HEADS].set(dt_bias[i].reshape(-1))
        ssd_par = ssd_par.at[1, SM_DTF:SM_DTB + SSD_HEADS].set(a_log[i].reshape(-1))
        ssd_par = ssd_par.at[2, SM_DTF:SM_DTB + SSD_HEADS].set(1.0)
        lw = {
            "norm_ffn1": row(norm_ffn1[i]), "norm_mix": row(norm_mix[i]),
            "norm_ffn2": row(norm_ffn2[i]),
            "ffn1_w_in": ffn1_w_in[i].astype(BF16), "ffn1_w_out": ffn1_w_out[i].astype(BF16),
            "ffn2_w_in": ffn2_w_in[i].astype(BF16), "ffn2_w_out": ffn2_w_out[i].astype(BF16),
            "w_proj": w_proj, "wa_pad": wa_pad, "ba": gla_b_a[i].reshape(2, 1, GLA_QK),
            "gla_norm_w": row(gla_norm_w[i]), "conv_w": conv_w[i], "conv_b": row(conv_b[i]),
            "ssd_par": ssd_par, "d_skip_row": row(jnp.repeat(d_skip[i], SSD_HEAD_DIM)),
            "ssd_norm_w": row(ssd_norm_w[i]), "w_out": w_out[i].astype(BF16),
        }
        fw = row(final_norm) if last else None

        h0_ssd = state_ssd[:, i].transpose(0, 1, 4, 2, 3).reshape(db, 2, SSD_STATE, SSD_INNER)
        xp, sg, ss = _layer_path(xp, mod_ctx, lw, None, None, n_seq=nb, seq_len=seq,
                                 grid_rows=1, want_state=True, final_w=fw)
        xs, _, _ = _layer_path(xs, mod_lat, lw, state_gla[:, i], h0_ssd, n_seq=db, seq_len=dseq,
                               grid_rows=grid_rows, want_state=False, final_w=fw)
        gla_states.append(sg)
        ssd_states.append(ss.reshape(nb, 2, SSD_STATE, SSD_HEADS, SSD_HEAD_DIM)
                          .transpose(0, 1, 3, 4, 2))
    y_prompt = xp.reshape(nb, seq, D_MODEL)
    y_sample = xs.reshape(db, dseq, D_MODEL)
    return (y_prompt, y_sample, jnp.stack(gla_states, axis=1), jnp.stack(ssd_states, axis=1))
```

```python
import functools

import jax
import jax.numpy as jnp
from jax import lax
from jax.experimental import pallas as pl
from jax.experimental.pallas import tpu as pltpu

F32 = jnp.float32
BF16 = jnp.bfloat16

D_MODEL = 1024
DEPTH = 1
GRID_W = 64
CHUNK = 64
EPS = 1e-6
N_MOD = 9
D_FF = 2816
GLA_HEADS = 4
GLA_DK = 128
GLA_DV = 256
GLA_LOWRANK = 16
GLA_TAU = 16.0
GLA_QK = GLA_HEADS * GLA_DK
GLA_V = GLA_HEADS * GLA_DV
SSD_HEADS = 16
SSD_HEAD_DIM = 64
SSD_GROUPS = 2
SSD_STATE = 128
SSD_INNER = SSD_HEADS * SSD_HEAD_DIM
SSD_BC = SSD_GROUPS * SSD_STATE
SSD_CONV_DIM = SSD_INNER + 2 * SSD_BC
CONV_K = 3
D_MIX = GLA_V + SSD_INNER

LANES = 128
SUBLANES = 8
VMEM_LIMIT_BYTES = 56 * 1024 * 1024

COL_Q = 0
COL_K = COL_Q + GLA_QK
COL_V = COL_K + GLA_QK
COL_R = COL_V + GLA_V
COL_Z = COL_R + GLA_V
COL_XBC = COL_Z + SSD_INNER
COL_SMALL = COL_XBC + SSD_CONV_DIM
N_PROJ = COL_SMALL + LANES
SM_AF = 0
SM_AB = SM_AF + GLA_LOWRANK
SM_DTF = SM_AB + GLA_LOWRANK
SM_DTB = SM_DTF + SSD_HEADS
HEADS_PER_GROUP = SSD_HEADS // SSD_GROUPS
GROUP_W = HEADS_PER_GROUP * SSD_HEAD_DIM

TOKEN_TILE = 256
GLA_UNROLL = 4


def _dot(a, b):
    return jnp.dot(a, b, preferred_element_type=F32)


def _dot_nt(a, b):
    return lax.dot_general(a, b, (((1,), (1,)), ((), ())), preferred_element_type=F32)


def _dot_tn(a, b):
    return lax.dot_general(a, b, (((0,), (0,)), ((), ())), preferred_element_type=F32)


def _silu(x):
    return x * jax.nn.sigmoid(x)


def _softplus(x):
    return jnp.maximum(x, 0.0) + jnp.log1p(jnp.exp(-jnp.abs(x)))


def _log_sigmoid(x):
    return jnp.minimum(x, 0.0) - jnp.log1p(jnp.exp(-jnp.abs(x)))


def _rmsnorm(x, w):
    ms = jnp.mean(x * x, axis=-1, keepdims=True)
    return x * lax.rsqrt(ms + EPS) * w


def _split3(x):
    hi = x.astype(BF16)
    r1 = x - hi.astype(F32)
    mid = r1.astype(BF16)
    lo = (r1 - mid.astype(F32)).astype(BF16)
    return hi, mid, lo


def _exact_dot(sel_bf16, x):
    hi, mid, lo = _split3(x)
    return (_dot(sel_bf16, lo) + _dot(sel_bf16, mid)) + _dot(sel_bf16, hi)


def _resident(shape):
    nd = len(shape)
    return pl.BlockSpec(shape, lambda *_: (0,) * nd, pipeline_mode=pl.Buffered(1))


def _params(n_axes):
    return pltpu.CompilerParams(dimension_semantics=("arbitrary",) * n_axes,
                                vmem_limit_bytes=VMEM_LIMIT_BYTES)


def _mod_body(c_ref, w_ref, b_ref, out_ref):
    a = _silu(c_ref[...]).astype(BF16)
    out_ref[...] = _dot(a, w_ref[...].astype(BF16)) + b_ref[...]


def _mod_call(cc, w_mod, b_mod):
    n_rows = cc.shape[0]
    tn = D_MODEL
    return pl.pallas_call(
        _mod_body,
        grid=(N_MOD * D_MODEL // tn,),
        in_specs=[pl.BlockSpec((n_rows, D_MODEL), lambda j: (0, 0)),
                  pl.BlockSpec((D_MODEL, tn), lambda j: (0, j)),
                  pl.BlockSpec((1, tn), lambda j: (0, j))],
        out_specs=pl.BlockSpec((n_rows, tn), lambda j: (0, j)),
        out_shape=jax.ShapeDtypeStruct((n_rows, N_MOD * D_MODEL), F32),
        compiler_params=_params(1),
        name="mod",
    )(cc, w_mod, b_mod)


def _ffn_body(*refs, sub, has_mix, has_final):
    it = iter(refs)
    x_ref, mod_ref, nw_ref, win_ref, wout_ref = (next(it) for _ in range(5))
    if has_mix:
        o_ref, y_ref, z_ref, snw_ref, wo_ref = (next(it) for _ in range(5))
    if has_final:
        fn_ref = next(it)
    out_ref = next(it)

    x = x_ref[...]
    if has_mix:
        g2 = mod_ref[0, 5:6, :]
        yn = _rmsnorm(y_ref[...] * _silu(z_ref[...]), snw_ref[...])
        m = _dot(o_ref[...], wo_ref[:GLA_V, :]) + _dot(yn.astype(BF16), wo_ref[GLA_V:, :])
        x = x + g2 * m
    sh = mod_ref[0, 3 * sub:3 * sub + 1, :]
    sc = mod_ref[0, 3 * sub + 1:3 * sub + 2, :]
    gate = mod_ref[0, 3 * sub + 2:3 * sub + 3, :]
    h = (_rmsnorm(x, nw_ref[...]) * (1.0 + sc) + sh).astype(BF16)
    g = _dot(h, win_ref[:, :D_FF])
    u = _dot(h, win_ref[:, D_FF:])
    act = (_silu(g) * u).astype(BF16)
    x = x + (0.5 * gate) * _dot(act, wout_ref[...])
    if has_final:
        x = _rmsnorm(x, fn_ref[...])
    out_ref[...] = x


def _ffn_call(x, mod, norm_w, w_in, w_out, *, sub, seq_len, mix=None, final_w=None):
    m_tok = x.shape[0]
    tm = TOKEN_TILE
    tiles_per_seq = seq_len // tm
    shared_mod = mod.shape[0] == 1
    mod_map = (lambda i: (0, 0, 0)) if shared_mod else (lambda i: (i // tiles_per_seq, 0, 0))
    tok = lambda width: pl.BlockSpec((tm, width), lambda i: (i, 0))
    in_specs = [tok(D_MODEL), pl.BlockSpec((1, N_MOD, D_MODEL), mod_map),
                _resident((1, D_MODEL)), _resident(w_in.shape), _resident(w_out.shape)]
    args = [x, mod, norm_w, w_in, w_out]
    if mix is not None:
        o, y, proj, ssd_norm_w, w_mix_out = mix
        in_specs += [tok(GLA_V), tok(SSD_INNER),
                     pl.BlockSpec((tm, SSD_INNER), lambda i: (i, COL_Z // SSD_INNER)),
                     _resident((1, SSD_INNER)), _resident(w_mix_out.shape)]
        args += [o, y, proj, ssd_norm_w, w_mix_out]
    if final_w is not None:
        in_specs.append(_resident((1, D_MODEL)))
        args.append(final_w)
    body = functools.partial(_ffn_body, sub=sub, has_mix=mix is not None,
                             has_final=final_w is not None)
    return pl.pallas_call(
        body,
        grid=(m_tok // tm,),
        in_specs=in_specs,
        out_specs=tok(D_MODEL),
        out_shape=jax.ShapeDtypeStruct((m_tok, D_MODEL), F32),
        compiler_params=_params(1),
        name="ffn_mix" if mix is not None else "ffn",
    )(*args)


def _proj_body(x_ref, mod_ref, nw_ref, w_ref, out_ref):
    sh = mod_ref[0, 3:4, :]
    sc = mod_ref[0, 4:5, :]
    h = (_rmsnorm(x_ref[...], nw_ref[...]) * (1.0 + sc) + sh).astype(BF16)
    out_ref[...] = _dot(h, w_ref[...])


def _proj_call(x, mod, norm_w, w_proj, *, seq_len):
    m_tok = x.shape[0]
    tm = TOKEN_TILE
    tiles_per_seq = seq_len // tm
    shared_mod = mod.shape[0] == 1
    mod_map = (lambda i: (0, 0, 0)) if shared_mod else (lambda i: (i // tiles_per_seq, 0, 0))
    return pl.pallas_call(
        _proj_body,
        grid=(m_tok // tm,),
        in_specs=[pl.BlockSpec((tm, D_MODEL), lambda i: (i, 0)),
                  pl.BlockSpec((1, N_MOD, D_MODEL), mod_map),
                  _resident((1, D_MODEL)), _resident(w_proj.shape)],
        out_specs=pl.BlockSpec((tm, N_PROJ), lambda i: (i, 0)),
        out_shape=jax.ShapeDtypeStruct((m_tok, N_PROJ), F32),
        compiler_params=_params(1),
        name="proj",
    )(x, mod, norm_w, w_proj)


def _tri_masks():
    row = lax.broadcasted_iota(jnp.int32, (CHUNK, CHUNK), 0)
    col = lax.broadcasted_iota(jnp.int32, (CHUNK, CHUNK), 1)
    return row >= col, row <= col


def _gla_body(*refs, seq_len, has_h0, want_state):
    it = iter(refs)
    q_ref, k_ref, v_ref, r_ref, sm_ref, wa_ref, ba_ref, nw_ref = (next(it) for _ in range(8))
    h0_ref = next(it) if has_h0 else None
    o_ref = next(it)
    st_ref = next(it) if want_state else None
    (la_scr, qk_scr, qg_scr, ks_scr, vb_scr, sc_scr, kv_scr, dec_scr, sprev_scr,
     state_scr) = (next(it) for _ in range(10))

    nc = seq_len // CHUNK
    dk = GLA_DK
    lower, upper = _tri_masks()
    lower_b = jnp.where(lower, 1.0, 0.0).astype(BF16)
    upper_b = jnp.where(upper, 1.0, 0.0).astype(BF16)
    unroll = min(GLA_UNROLL, nc)
    rows = lambda c: pl.ds(pl.multiple_of(c * CHUNK, CHUNK), CHUNK)

    pre = _dot(sm_ref[...].astype(BF16), wa_ref[...].astype(BF16)) + ba_ref[...]
    la_scr[...] = _log_sigmoid(pre) * (1.0 / GLA_TAU)

    def prep(c, carry):
        sl = rows(c)
        qc = q_ref[sl, :] * (GLA_DK ** -0.5)
        kc = k_ref[sl, :]
        vb_scr[sl, :] = v_ref[sl, :].astype(BF16)
        la = la_scr[sl, :]
        hi = la.astype(BF16)
        lo = (la - hi.astype(F32)).astype(BF16)

        def cumulative(tri_b, a):
            p = _dot(tri_b, jnp.concatenate([hi[:, a:a + dk], lo[:, a:a + dk]], axis=1))
            return p[:, 0:dk] + p[:, dk:2 * dk]

        g_f = cumulative(lower_b, 0)
        g_b = cumulative(upper_b, dk)
        mid_f = g_f[CHUNK // 2:CHUNK // 2 + 1, :]
        mid_b = g_b[CHUNK - 1 - CHUNK // 2:CHUNK - CHUNK // 2, :]
        end_f = g_f[CHUNK - 1:CHUNK, :]
        end_b = g_b[0:1, :]
        qk_scr[sl, 0:dk] = (qc * jnp.exp(g_f - mid_f)).astype(BF16)
        qk_scr[sl, dk:2 * dk] = (qc * jnp.exp(g_b - mid_b)).astype(BF16)
        qk_scr[sl, 2 * dk:3 * dk] = (kc * jnp.exp(mid_f - g_f)).astype(BF16)
        qk_scr[sl, 3 * dk:4 * dk] = (kc * jnp.exp(mid_b - g_b)).astype(BF16)
        qg_scr[sl, 0:dk] = (qc * jnp.exp(g_f)).astype(BF16)
        qg_scr[sl, dk:2 * dk] = (qc * jnp.exp(g_b)).astype(BF16)
        ks_scr[sl, 0:dk] = (kc * jnp.exp(end_f - g_f)).astype(BF16)
        ks_scr[sl, dk:2 * dk] = (kc * jnp.exp(end_b - g_b)).astype(BF16)
        dec = jnp.exp(jnp.concatenate([end_f, end_b], axis=1))
        dec_scr[c] = jnp.broadcast_to(dec, (SUBLANES, 2 * dk))
        return carry

    lax.fori_loop(0, nc, prep, 0, unroll=unroll)

    def products(c, carry):
        sl = rows(c)
        s_f = _dot_nt(qk_scr[sl, 0:dk], qk_scr[sl, 2 * dk:3 * dk])
        s_b = _dot_nt(qk_scr[sl, dk:2 * dk], qk_scr[sl, 3 * dk:4 * dk])
        sc_scr[sl, :] = (jnp.where(lower, s_f, 0.0) + jnp.where(upper, s_b, 0.0)).astype(BF16)
        kv_scr[c] = _dot_tn(vb_scr[sl, :], ks_scr[sl, :])
        return carry

    lax.fori_loop(0, nc, products, 0, unroll=unroll)

    if has_h0:
        state_scr[:, 0:dk] = h0_ref[0, 0, 0].T
        state_scr[:, dk:2 * dk] = h0_ref[0, 1, 0].T
    else:
        state_scr[...] = jnp.zeros_like(state_scr)

    def recur(i, carry):
        for c, a in ((i, 0), (nc - 1 - i, dk)):
            state = state_scr[:, a:a + dk]
            sprev_scr[c, :, a:a + dk] = state.astype(BF16)
            state_scr[:, a:a + dk] = state * dec_scr[c, 0:1, a:a + dk] + kv_scr[c, :, a:a + dk]
        return carry

    lax.fori_loop(0, nc, recur, 0)

    def finish(c, carry):
        sl = rows(c)
        o = _dot(sc_scr[sl, :], vb_scr[sl, :]) + _dot_nt(qg_scr[sl, :], sprev_scr[c])
        o_ref[sl, :] = (_rmsnorm(o, nw_ref[...]) * _silu(r_ref[sl, :])).astype(o_ref.dtype)
        return carry

    lax.fori_loop(0, nc, finish, 0, unroll=unroll)

    if want_state:
        st_ref[0, 0, 0] = state_scr[:, 0:dk].T
        st_ref[0, 1, 0] = state_scr[:, dk:2 * dk].T


def _gla_call(proj, wa_cat, ba_cat, norm_w, h0, *, n_seq, seq_len, want_state):
    L = seq_len
    nc = L // CHUNK
    blk = lambda width, base: pl.BlockSpec((L, width), lambda s, h: (s, base // width + h))
    in_specs = [blk(GLA_DK, COL_Q), blk(GLA_DK, COL_K), blk(GLA_DV, COL_V), blk(GLA_DV, COL_R),
                pl.BlockSpec((L, LANES), lambda s, h: (s, COL_SMALL // LANES)),
                pl.BlockSpec((LANES, 2 * GLA_DK), lambda s, h: (0, h)),
                pl.BlockSpec((1, 2 * GLA_DK), lambda s, h: (0, h)),
                pl.BlockSpec((1, GLA_DV), lambda s, h: (0, 0))]
    args = [proj, proj, proj, proj, proj, wa_cat, ba_cat, norm_w]
    state_spec = pl.BlockSpec((1, 2, 1, GLA_DK, GLA_DV), lambda s, h: (s, 0, h, 0, 0))
    if h0 is not None:
        in_specs.append(state_spec)
        args.append(h0)
    out_specs = [pl.BlockSpec((L, GLA_DV), lambda s, h: (s, h))]
    out_shape = [jax.ShapeDtypeStruct((n_seq * L, GLA_V), BF16)]
    if want_state:
        out_specs.append(state_spec)
        out_shape.append(jax.ShapeDtypeStruct((n_seq, 2, GLA_HEADS, GLA_DK, GLA_DV), F32))
    body = functools.partial(_gla_body, seq_len=L, has_h0=h0 is not None, want_state=want_state)
    return pl.pallas_call(
        body,
        grid=(n_seq, GLA_HEADS),
        in_specs=in_specs,
        out_specs=out_specs,
        out_shape=out_shape,
        scratch_shapes=[pltpu.VMEM((L, 2 * GLA_DK), F32),
                        pltpu.VMEM((L, 4 * GLA_DK), BF16),
                        pltpu.VMEM((L, 2 * GLA_DK), BF16),
                        pltpu.VMEM((L, 2 * GLA_DK), BF16),
                        pltpu.VMEM((L, GLA_DV), BF16),
                        pltpu.VMEM((L, CHUNK), BF16),
                        pltpu.VMEM((nc, GLA_DV, 2 * GLA_DK), F32),
                        pltpu.VMEM((nc, SUBLANES, 2 * GLA_DK), F32),
                        pltpu.VMEM((nc, GLA_DV, 2 * GLA_DK), BF16),
                        pltpu.VMEM((GLA_DV, 2 * GLA_DK), F32)],
        compiler_params=_params(2),
        name="gla",
    )(*args)


def _ssd_body(*refs, seq_len, grid_rows, has_h0, want_state):
    it = iter(refs)
    (xs_ref, b_ref, c_ref, sm_ref, cwx_ref, cwb_ref, cwc_ref, cbx_ref, cbb_ref, cbc_ref,
     par_ref, dsk_ref) = (next(it) for _ in range(12))
    h0_ref = next(it) if has_h0 else None
    y_ref = next(it)
    st_ref = next(it) if want_state else None
    pad_scr, xs_s, b_s, c_s, stf_scr, stb_scr = (next(it) for _ in range(6))

    L = seq_len
    nc = L // CHUNK
    width = L // grid_rows
    pad = pad_scr.shape[0] - L
    pad //= 2
    grp = pl.program_id(1)

    def conv_into(src_ref, cw_ref, cb_ref, dst_ref):
        ch = src_ref.shape[1]
        pad_scr[0:pad, 0:ch] = jnp.zeros((pad, ch), F32)
        pad_scr[pad + L:pad + L + pad, 0:ch] = jnp.zeros((pad, ch), F32)
        pad_scr[pad:pad + L, 0:ch] = src_ref[...]
        rc = min(2 * CHUNK, L)
        col = lax.broadcasted_iota(jnp.int32, (rc, ch), 0) % width
        di_taps = range(CONV_K) if grid_rows > 1 else (CONV_K // 2,)
        for r0 in range(0, L, rc):
            acc = jnp.broadcast_to(cb_ref[...], (rc, ch))
            for dj in range(CONV_K):
                inner = None
                for di in di_taps:
                    off = pad + r0 + (di - 1) * width + (dj - 1)
                    term = cw_ref[di, dj:dj + 1, :] * pad_scr[off:off + rc, 0:ch]
                    inner = term if inner is None else inner + term
                if grid_rows > 1 and dj == 0:
                    inner = jnp.where(col >= 1, inner, 0.0)
                if grid_rows > 1 and dj == CONV_K - 1:
                    inner = jnp.where(col <= width - 2, inner, 0.0)
                acc = acc + inner
            dst_ref[r0:r0 + rc, :] = _silu(acc)

    conv_into(xs_ref, cwx_ref, cbx_ref, xs_s)
    conv_into(b_ref, cwb_ref, cbb_ref, b_s)
    conv_into(c_ref, cwc_ref, cbc_ref, c_s)

    bias_row = par_ref[0:1, :]
    a_row = -jnp.exp(par_ref[1:2, :]) * par_ref[2:3, :]
    lower, upper = _tri_masks()
    lower_b = jnp.where(lower, 1.0, 0.0).astype(BF16)
    upper_b = jnp.where(upper, 1.0, 0.0).astype(BF16)
    lane_src = lax.broadcasted_iota(jnp.int32, (LANES, GROUP_W), 0)
    lane_head = lax.shift_right_logical(lax.broadcasted_iota(jnp.int32, (LANES, GROUP_W), 1), 6)
    head0 = grp * HEADS_PER_GROUP
    expand_f = jnp.where(lane_src == SM_DTF + head0 + lane_head, 1.0, 0.0).astype(BF16)
    expand_b = jnp.where(lane_src == SM_DTB + head0 + lane_head, 1.0, 0.0).astype(BF16)
    t_idx = lax.broadcasted_iota(jnp.int32, (CHUNK, GROUP_W), 0)
    s_idx = lax.broadcasted_iota(jnp.int32, (CHUNK, GROUP_W), 1) & (CHUNK - 1)
    half = GROUP_W // 2
    blk_r = lax.shift_right_logical(lax.broadcasted_iota(jnp.int32, (half, half), 0), 6)
    blk_c = lax.shift_right_logical(lax.broadcasted_iota(jnp.int32, (half, half), 1), 6)
    same_head = blk_r == blk_c

    if has_h0:
        stf_scr[...] = h0_ref[0, 0]
        stb_scr[...] = h0_ref[0, 1]
    else:
        stf_scr[...] = jnp.zeros_like(stf_scr)
        stb_scr[...] = jnp.zeros_like(stb_scr)
    y_ref[...] = xs_s[...] * dsk_ref[...]

    def chunk(r0, st_scr, tri_b, expand, causal, last_i):
        sl = pl.ds(r0, CHUNK)
        xc = xs_s[sl, :]
        bc = b_s[sl, :].astype(BF16)
        cc = c_s[sl, :].astype(BF16)
        dt = _softplus(sm_ref[sl, :] + bias_row)
        cum = _exact_dot(tri_b, dt * a_row)
        parts = jnp.concatenate(list(_split3(cum)) + list(_split3(dt)), axis=0)
        ex = _dot(parts, expand)
        cum_e = (ex[2 * CHUNK:3 * CHUNK] + ex[CHUNK:2 * CHUNK]) + ex[0:CHUNK]
        dt_e = (ex[5 * CHUNK:6 * CHUNK] + ex[4 * CHUNK:5 * CHUNK]) + ex[3 * CHUNK:4 * CHUNK]
        cum_row = jnp.sum(jnp.where(t_idx == s_idx, cum_e, 0.0), axis=0, keepdims=True)
        decay = jnp.exp(jnp.where(causal, cum_e - cum_row, -jnp.inf))
        cb = _dot_nt(cc, jnp.concatenate([bc] * HEADS_PER_GROUP, axis=0))
        scores = (cb * decay).astype(BF16)
        xdt = (xc * dt_e).astype(BF16)
        y_parts = []
        for j in range(2):
            xh = xdt[:, j * half:(j + 1) * half]
            rep = jnp.concatenate([xh] * (half // CHUNK), axis=0)
            bd = jnp.where(same_head, rep, jnp.zeros_like(rep))
            y_parts.append(_dot(scores[:, j * half:(j + 1) * half], bd))
        y_intra = jnp.concatenate(y_parts, axis=1)
        state = st_scr[...]
        y_inter = _dot(cc, state.astype(BF16)) * jnp.exp(cum_e)
        cum_last = cum_e[last_i:last_i + 1, :]
        xw = (xc * (jnp.exp(cum_last - cum_e) * dt_e)).astype(BF16)
        st_scr[...] = state * jnp.exp(cum_last) + _dot_tn(bc, xw)
        y_ref[sl, :] += y_intra + y_inter

    def step(c, carry):
        rf = pl.multiple_of(c * CHUNK, CHUNK)
        rb = pl.multiple_of((nc - 1 - c) * CHUNK, CHUNK)
        chunk(rf, stf_scr, lower_b, expand_f, t_idx >= s_idx, CHUNK - 1)
        chunk(rb, stb_scr, upper_b, expand_b, t_idx <= s_idx, 0)
        return carry

    lax.fori_loop(0, nc, step, 0)

    if want_state:
        st_ref[0, 0] = stf_scr[...]
        st_ref[0, 1] = stb_scr[...]


def _ssd_call(proj, conv_w, conv_b, par, dsk, h0, *, n_seq, seq_len, grid_rows, want_state):
    L = seq_len
    x_base = COL_XBC
    b_base = COL_XBC + SSD_INNER
    c_base = b_base + SSD_BC
    in_specs = [pl.BlockSpec((L, GROUP_W), lambda s, g: (s, x_base // GROUP_W + g)),
                pl.BlockSpec((L, SSD_STATE), lambda s, g: (s, b_base // SSD_STATE + g)),
                pl.BlockSpec((L, SSD_STATE), lambda s, g: (s, c_base // SSD_STATE + g)),
                pl.BlockSpec((L, LANES), lambda s, g: (s, COL_SMALL // LANES)),
                pl.BlockSpec((CONV_K, CONV_K, GROUP_W), lambda s, g: (0, 0, g)),
                pl.BlockSpec((CONV_K, CONV_K, SSD_STATE),
                             lambda s, g: (0, 0, SSD_INNER // SSD_STATE + g)),
                pl.BlockSpec((CONV_K, CONV_K, SSD_STATE),
                             lambda s, g: (0, 0, (SSD_INNER + SSD_BC) // SSD_STATE + g)),
                pl.BlockSpec((1, GROUP_W), lambda s, g: (0, g)),
                pl.BlockSpec((1, SSD_STATE), lambda s, g: (0, SSD_INNER // SSD_STATE + g)),
                pl.BlockSpec((1, SSD_STATE),
                             lambda s, g: (0, (SSD_INNER + SSD_BC) // SSD_STATE + g)),
                pl.BlockSpec((SUBLANES, LANES), lambda s, g: (0, 0)),
                pl.BlockSpec((1, GROUP_W), lambda s, g: (0, g))]
    args = [proj, proj, proj, proj, conv_w, conv_w, conv_w, conv_b, conv_b, conv_b, par, dsk]
    state_spec = pl.BlockSpec((1, 2, SSD_STATE, GROUP_W), lambda s, g: (s, 0, 0, g))
    if h0 is not None:
        in_specs.append(state_spec)
        args.append(h0)
    out_specs = [pl.BlockSpec((L, GROUP_W), lambda s, g: (s, g))]
    out_shape = [jax.ShapeDtypeStruct((n_seq * L, SSD_INNER), F32)]
    if want_state:
        out_specs.append(state_spec)
        out_shape.append(jax.ShapeDtypeStruct((n_seq, 2, SSD_STATE, SSD_INNER), F32))
    conv_pad = (L // grid_rows + SUBLANES) if grid_rows > 1 else SUBLANES
    body = functools.partial(_ssd_body, seq_len=L, grid_rows=grid_rows,
                             has_h0=h0 is not None, want_state=want_state)
    return pl.pallas_call(
        body,
        grid=(n_seq, SSD_GROUPS),
        in_specs=in_specs,
        out_specs=out_specs,
        out_shape=out_shape,
        scratch_shapes=[pltpu.VMEM((L + 2 * conv_pad, GROUP_W), F32),
                        pltpu.VMEM((L, GROUP_W), F32),
                        pltpu.VMEM((L, SSD_STATE), F32),
                        pltpu.VMEM((L, SSD_STATE), F32),
                        pltpu.VMEM((SSD_STATE, GROUP_W), F32),
                        pltpu.VMEM((SSD_STATE, GROUP_W), F32)],
        compiler_params=_params(2),
        name="ssd",
    )(*args)


def _layer_path(x, mod, lw, h0_gla, h0_ssd, *, n_seq, seq_len, grid_rows, want_state, final_w):
    x1 = _ffn_call(x, mod, lw["norm_ffn1"], lw["ffn1_w_in"], lw["ffn1_w_out"],
                   sub=0, seq_len=seq_len)
    proj = _proj_call(x1, mod, lw["norm_mix"], lw["w_proj"], seq_len=seq_len)
    gla_out = _gla_call(proj, lw["wa_cat"], lw["ba_cat"], lw["gla_norm_w"], h0_gla,
                        n_seq=n_seq, seq_len=seq_len, want_state=want_state)
    ssd_out = _ssd_call(proj, lw["conv_w"], lw["conv_b"], lw["ssd_par"], lw["d_skip_row"], h0_ssd,
                        n_seq=n_seq, seq_len=seq_len, grid_rows=grid_rows, want_state=want_state)
    o, y = gla_out[0], ssd_out[0]
    out = _ffn_call(x1, mod, lw["norm_ffn2"], lw["ffn2_w_in"], lw["ffn2_w_out"],
                    sub=2, seq_len=seq_len, mix=(o, y, proj, lw["ssd_norm_w"], lw["w_out"]),
                    final_w=final_w)
    if want_state:
        return out, gla_out[1], ssd_out[1]
    return out, None, None


def kernel(x_prompt, x_sample, state_gla, state_ssd, c, c_ctx, norm_ffn1, norm_mix, norm_ffn2, w_mod, b_mod, ffn1_w_in, ffn1_w_out, ffn2_w_in, ffn2_w_out, w_in, gla_w_a2, gla_b_a, gla_norm_w, conv_w, conv_b, dt_bias, a_log, d_skip, ssd_norm_w, w_out, final_norm):
    nb, seq, _ = x_prompt.shape
    db, dseq, _ = x_sample.shape
    grid_rows = dseq // GRID_W
    xp = x_prompt.reshape(nb * seq, D_MODEL)
    xs = x_sample.reshape(db * dseq, D_MODEL)
    row = lambda v: v.reshape(1, -1)
    gla_states, ssd_states = [], []
    for i in range(DEPTH):
        last = i == DEPTH - 1
        n_rows = -(-(db + 1) // SUBLANES) * SUBLANES
        cc = jnp.concatenate([c, c_ctx[None, :], jnp.zeros((n_rows - db - 1, D_MODEL), F32)], 0)
        mod = _mod_call(cc, w_mod[i], row(b_mod[i])).reshape(n_rows, N_MOD, D_MODEL)
        mod_lat, mod_ctx = mod[:db], mod[db:db + 1]

        wi = w_in[i]
        o_af = COL_R + GLA_V
        o_z = o_af + 2 * GLA_LOWRANK
        o_xbc = o_z + SSD_INNER
        o_dt = o_xbc + SSD_CONV_DIM
        w_proj = jnp.concatenate(
            [wi[:, :o_af], wi[:, o_z:o_dt], wi[:, o_af:o_z], wi[:, o_dt:],
             jnp.zeros((D_MODEL, LANES - 2 * GLA_LOWRANK - 2 * SSD_HEADS), F32)], axis=1).astype(BF16)
        wa_cat = jnp.zeros((LANES, GLA_HEADS, 2, GLA_DK), F32)
        wa_cat = wa_cat.at[SM_AF:SM_AF + GLA_LOWRANK, :, 0, :].set(
            gla_w_a2[i, 0].reshape(GLA_LOWRANK, GLA_HEADS, GLA_DK))
        wa_cat = wa_cat.at[SM_AB:SM_AB + GLA_LOWRANK, :, 1, :].set(
            gla_w_a2[i, 1].reshape(GLA_LOWRANK, GLA_HEADS, GLA_DK))
        wa_cat = wa_cat.reshape(LANES, 2 * GLA_QK)
        ba_cat = jnp.stack([gla_b_a[i, 0].reshape(GLA_HEADS, GLA_DK),
                            gla_b_a[i, 1].reshape(GLA_HEADS, GLA_DK)], axis=1).reshape(1, 2 * GLA_QK)
        ssd_par = jnp.zeros((SUBLANES, LANES), F32)
        ssd_par = ssd_par.at[0, SM_DTF:SM_DTB + SSD_HEADS].set(dt_bias[i].reshape(-1))
        ssd_par = ssd_par.at[1, SM_DTF:SM_DTB + SSD_HEADS].set(a_log[i].reshape(-1))
        ssd_par = ssd_par.at[2, SM_DTF:SM_DTB + SSD_HEADS].set(1.0)
        lw = {
            "norm_ffn1": row(norm_ffn1[i]), "norm_mix": row(norm_mix[i]),
            "norm_ffn2": row(norm_ffn2[i]),
            "ffn1_w_in": ffn1_w_in[i].astype(BF16), "ffn1_w_out": ffn1_w_out[i].astype(BF16),
            "ffn2_w_in": ffn2_w_in[i].astype(BF16), "ffn2_w_out": ffn2_w_out[i].astype(BF16),
            "w_proj": w_proj, "wa_cat": wa_cat, "ba_cat": ba_cat,
            "gla_norm_w": row(gla_norm_w[i]), "conv_w": conv_w[i], "conv_b": row(conv_b[i]),
            "ssd_par": ssd_par, "d_skip_row": row(jnp.repeat(d_skip[i], SSD_HEAD_DIM)),
            "ssd_norm_w": row(ssd_norm_w[i]), "w_out": w_out[i].astype(BF16),
        }
        fw = row(final_norm) if last else None

        h0_ssd = state_ssd[:, i].transpose(0, 1, 4, 2, 3).reshape(db, 2, SSD_STATE, SSD_INNER)
        xp, sg, ss = _layer_path(xp, mod_ctx, lw, None, None, n_seq=nb, seq_len=seq,
                                 grid_rows=1, want_state=True, final_w=fw)
        xs, _, _ = _layer_path(xs, mod_lat, lw, state_gla[:, i], h0_ssd, n_seq=db, seq_len=dseq,
                               grid_rows=grid_rows, want_state=False, final_w=fw)
        gla_states.append(sg)
        ssd_states.append(ss.reshape(nb, 2, SSD_STATE, SSD_HEADS, SSD_HEAD_DIM)
                          .transpose(0, 1, 3, 4, 2))
    y_prompt = xp.reshape(nb, seq, D_MODEL)
    y_sample = xs.reshape(db, dseq, D_MODEL)
    return (y_prompt, y_sample, jnp.stack(gla_states, axis=1), jnp.stack(ssd_states, axis=1))
```

```python
import functools

import jax
import jax.numpy as jnp
from jax import lax
from jax.experimental import pallas as pl
from jax.experimental.pallas import tpu as pltpu

F32 = jnp.float32
BF16 = jnp.bfloat16

D_MODEL = 1024
DEPTH = 1
GRID_W = 64
CHUNK = 64
EPS = 1e-6
N_MOD = 9
D_FF = 2816
GLA_HEADS = 4
GLA_DK = 128
GLA_DV = 256
GLA_LOWRANK = 16
GLA_TAU = 16.0
GLA_QK = GLA_HEADS * GLA_DK
GLA_V = GLA_HEADS * GLA_DV
SSD_HEADS = 16
SSD_HEAD_DIM = 64
SSD_GROUPS = 2
SSD_STATE = 128
SSD_INNER = SSD_HEADS * SSD_HEAD_DIM
SSD_BC = SSD_GROUPS * SSD_STATE
SSD_CONV_DIM = SSD_INNER + 2 * SSD_BC
CONV_K = 3
D_MIX = GLA_V + SSD_INNER

LANES = 128
SUBLANES = 8
VMEM_LIMIT_BYTES = 56 * 1024 * 1024

COL_Q = 0
COL_K = COL_Q + GLA_QK
COL_V = COL_K + GLA_QK
COL_R = COL_V + GLA_V
COL_Z = COL_R + GLA_V
COL_XBC = COL_Z + SSD_INNER
COL_SMALL = COL_XBC + SSD_CONV_DIM
N_PROJ = COL_SMALL + LANES
SM_AF = 0
SM_AB = SM_AF + GLA_LOWRANK
SM_DTF = SM_AB + GLA_LOWRANK
SM_DTB = SM_DTF + SSD_HEADS
HEADS_PER_GROUP = SSD_HEADS // SSD_GROUPS
GROUP_W = HEADS_PER_GROUP * SSD_HEAD_DIM

TOKEN_TILE = 256
GLA_UNROLL = 4
SSD_UNROLL = 4
MXU_TILE = 256
TERM_STRIDE = 2 * SSD_HEADS


def _dot(a, b):
    return jnp.dot(a, b, preferred_element_type=F32)


def _dot_nt(a, b):
    return lax.dot_general(a, b, (((1,), (1,)), ((), ())), preferred_element_type=F32)


def _dot_tn(a, b):
    return lax.dot_general(a, b, (((0,), (0,)), ((), ())), preferred_element_type=F32)


def _silu(x):
    return x * jax.nn.sigmoid(x)


def _softplus(x):
    return jnp.maximum(x, 0.0) + jnp.log1p(jnp.exp(-jnp.abs(x)))


def _log_sigmoid(x):
    return jnp.minimum(x, 0.0) - jnp.log1p(jnp.exp(-jnp.abs(x)))


def _rmsnorm(x, w):
    ms = jnp.mean(x * x, axis=-1, keepdims=True)
    return x * lax.rsqrt(ms + EPS) * w


def _split3(x):
    hi = x.astype(BF16)
    r1 = x - hi.astype(F32)
    mid = r1.astype(BF16)
    lo = (r1 - mid.astype(F32)).astype(BF16)
    return hi, mid, lo


def _exact_dot(sel_bf16, x):
    hi, mid, lo = _split3(x)
    return (_dot(sel_bf16, lo) + _dot(sel_bf16, mid)) + _dot(sel_bf16, hi)


def _resident(shape):
    nd = len(shape)
    return pl.BlockSpec(shape, lambda *_: (0,) * nd, pipeline_mode=pl.Buffered(1))


def _params(n_axes):
    return pltpu.CompilerParams(dimension_semantics=("arbitrary",) * n_axes,
                                vmem_limit_bytes=VMEM_LIMIT_BYTES)


def _mod_body(c_ref, w_ref, b_ref, out_ref):
    a = _silu(c_ref[...]).astype(BF16)
    out_ref[...] = _dot(a, w_ref[...].astype(BF16)) + b_ref[...]


def _mod_call(cc, w_mod, b_mod):
    n_rows = cc.shape[0]
    tn = D_MODEL
    return pl.pallas_call(
        _mod_body,
        grid=(N_MOD * D_MODEL // tn,),
        in_specs=[pl.BlockSpec((n_rows, D_MODEL), lambda j: (0, 0)),
                  pl.BlockSpec((D_MODEL, tn), lambda j: (0, j)),
                  pl.BlockSpec((1, tn), lambda j: (0, j))],
        out_specs=pl.BlockSpec((n_rows, tn), lambda j: (0, j)),
        out_shape=jax.ShapeDtypeStruct((n_rows, N_MOD * D_MODEL), F32),
        compiler_params=_params(1),
        name="mod",
    )(cc, w_mod, b_mod)


def _ffn_body(*refs, sub, has_mix, has_final):
    it = iter(refs)
    x_ref, mod_ref, nw_ref, win_ref, wout_ref = (next(it) for _ in range(5))
    if has_mix:
        o_ref, y_ref, z_ref, snw_ref, wo_ref = (next(it) for _ in range(5))
    if has_final:
        fn_ref = next(it)
    out_ref = next(it)

    x = x_ref[...]
    if has_mix:
        g2 = mod_ref[0, 5:6, :]
        yn = _rmsnorm(y_ref[...] * _silu(z_ref[...]), snw_ref[...])
        m = _dot(o_ref[...], wo_ref[:GLA_V, :]) + _dot(yn.astype(BF16), wo_ref[GLA_V:, :])
        x = x + g2 * m
    sh = mod_ref[0, 3 * sub:3 * sub + 1, :]
    sc = mod_ref[0, 3 * sub + 1:3 * sub + 2, :]
    gate = mod_ref[0, 3 * sub + 2:3 * sub + 3, :]
    h = (_rmsnorm(x, nw_ref[...]) * (1.0 + sc) + sh).astype(BF16)
    g = _dot(h, win_ref[:, :D_FF])
    u = _dot(h, win_ref[:, D_FF:])
    act = (_silu(g) * u).astype(BF16)
    x = x + (0.5 * gate) * _dot(act, wout_ref[...])
    if has_final:
        x = _rmsnorm(x, fn_ref[...])
    out_ref[...] = x


def _ffn_call(x, mod, norm_w, w_in, w_out, *, sub, seq_len, mix=None, final_w=None):
    m_tok = x.shape[0]
    tm = TOKEN_TILE
    tiles_per_seq = seq_len // tm
    shared_mod = mod.shape[0] == 1
    mod_map = (lambda i: (0, 0, 0)) if shared_mod else (lambda i: (i // tiles_per_seq, 0, 0))
    tok = lambda width: pl.BlockSpec((tm, width), lambda i: (i, 0))
    in_specs = [tok(D_MODEL), pl.BlockSpec((1, N_MOD, D_MODEL), mod_map),
                _resident((1, D_MODEL)), _resident(w_in.shape), _resident(w_out.shape)]
    args = [x, mod, norm_w, w_in, w_out]
    if mix is not None:
        o, y, proj, ssd_norm_w, w_mix_out = mix
        in_specs += [tok(GLA_V), tok(SSD_INNER),
                     pl.BlockSpec((tm, SSD_INNER), lambda i: (i, COL_Z // SSD_INNER)),
                     _resident((1, SSD_INNER)), _resident(w_mix_out.shape)]
        args += [o, y, proj, ssd_norm_w, w_mix_out]
    if final_w is not None:
        in_specs.append(_resident((1, D_MODEL)))
        args.append(final_w)
    body = functools.partial(_ffn_body, sub=sub, has_mix=mix is not None,
                             has_final=final_w is not None)
    return pl.pallas_call(
        body,
        grid=(m_tok // tm,),
        in_specs=in_specs,
        out_specs=tok(D_MODEL),
        out_shape=jax.ShapeDtypeStruct((m_tok, D_MODEL), F32),
        compiler_params=_params(1),
        name="ffn_mix" if mix is not None else "ffn",
    )(*args)


def _proj_body(x_ref, mod_ref, nw_ref, w_ref, out_ref):
    sh = mod_ref[0, 3:4, :]
    sc = mod_ref[0, 4:5, :]
    h = (_rmsnorm(x_ref[...], nw_ref[...]) * (1.0 + sc) + sh).astype(BF16)
    out_ref[...] = _dot(h, w_ref[...])


def _proj_call(x, mod, norm_w, w_proj, *, seq_len):
    m_tok = x.shape[0]
    tm = TOKEN_TILE
    tiles_per_seq = seq_len // tm
    shared_mod = mod.shape[0] == 1
    mod_map = (lambda i: (0, 0, 0)) if shared_mod else (lambda i: (i // tiles_per_seq, 0, 0))
    return pl.pallas_call(
        _proj_body,
        grid=(m_tok // tm,),
        in_specs=[pl.BlockSpec((tm, D_MODEL), lambda i: (i, 0)),
                  pl.BlockSpec((1, N_MOD, D_MODEL), mod_map),
                  _resident((1, D_MODEL)), _resident(w_proj.shape)],
        out_specs=pl.BlockSpec((tm, N_PROJ), lambda i: (i, 0)),
        out_shape=jax.ShapeDtypeStruct((m_tok, N_PROJ), F32),
        compiler_params=_params(1),
        name="proj",
    )(x, mod, norm_w, w_proj)


def _tri_masks():
    row = lax.broadcasted_iota(jnp.int32, (CHUNK, CHUNK), 0)
    col = lax.broadcasted_iota(jnp.int32, (CHUNK, CHUNK), 1)
    return row >= col, row <= col


def _gla_body(*refs, seq_len, has_h0, want_state):
    it = iter(refs)
    q_ref, k_ref, v_ref, r_ref, sm_ref, wa_ref, ba_ref, nw_ref = (next(it) for _ in range(8))
    h0_ref = next(it) if has_h0 else None
    o_ref = next(it)
    st_ref = next(it) if want_state else None
    (la_scr, qk_scr, qg_scr, ks_scr, vb_scr, sc_scr, kv_scr, dec_scr, sprev_scr,
     state_scr) = (next(it) for _ in range(10))

    nc = seq_len // CHUNK
    dk = GLA_DK
    lower, upper = _tri_masks()
    lower_b = jnp.where(lower, 1.0, 0.0).astype(BF16)
    upper_b = jnp.where(upper, 1.0, 0.0).astype(BF16)
    unroll = min(GLA_UNROLL, nc)
    rows = lambda c: pl.ds(pl.multiple_of(c * CHUNK, CHUNK), CHUNK)

    pre = _dot(sm_ref[...].astype(BF16), wa_ref[...].astype(BF16)) + ba_ref[...]
    la_scr[...] = _log_sigmoid(pre) * (1.0 / GLA_TAU)

    def prep(c, carry):
        sl = rows(c)
        qc = q_ref[sl, :] * (GLA_DK ** -0.5)
        kc = k_ref[sl, :]
        vb_scr[sl, :] = v_ref[sl, :].astype(BF16)
        la = la_scr[sl, :]
        hi = la.astype(BF16)
        lo = (la - hi.astype(F32)).astype(BF16)

        def cumulative(tri_b, a):
            p = _dot(tri_b, jnp.concatenate([hi[:, a:a + dk], lo[:, a:a + dk]], axis=1))
            return p[:, 0:dk] + p[:, dk:2 * dk]

        g_f = cumulative(lower_b, 0)
        g_b = cumulative(upper_b, dk)
        mid_f = g_f[CHUNK // 2:CHUNK // 2 + 1, :]
        mid_b = g_b[CHUNK - 1 - CHUNK // 2:CHUNK - CHUNK // 2, :]
        end_f = g_f[CHUNK - 1:CHUNK, :]
        end_b = g_b[0:1, :]
        qk_scr[sl, 0:dk] = (qc * jnp.exp(g_f - mid_f)).astype(BF16)
        qk_scr[sl, dk:2 * dk] = (qc * jnp.exp(g_b - mid_b)).astype(BF16)
        qk_scr[sl, 2 * dk:3 * dk] = (kc * jnp.exp(mid_f - g_f)).astype(BF16)
        qk_scr[sl, 3 * dk:4 * dk] = (kc * jnp.exp(mid_b - g_b)).astype(BF16)
        qg_scr[sl, 0:dk] = (qc * jnp.exp(g_f)).astype(BF16)
        qg_scr[sl, dk:2 * dk] = (qc * jnp.exp(g_b)).astype(BF16)
        ks_scr[sl, 0:dk] = (kc * jnp.exp(end_f - g_f)).astype(BF16)
        ks_scr[sl, dk:2 * dk] = (kc * jnp.exp(end_b - g_b)).astype(BF16)
        dec = jnp.exp(jnp.concatenate([end_f, end_b], axis=1))
        dec_scr[c] = jnp.broadcast_to(dec, (SUBLANES, 2 * dk))
        return carry

    lax.fori_loop(0, nc, prep, 0, unroll=unroll)

    def products(c, carry):
        sl = rows(c)
        s_f = _dot_nt(qk_scr[sl, 0:dk], qk_scr[sl, 2 * dk:3 * dk])
        s_b = _dot_nt(qk_scr[sl, dk:2 * dk], qk_scr[sl, 3 * dk:4 * dk])
        sc_scr[sl, :] = (jnp.where(lower, s_f, 0.0) + jnp.where(upper, s_b, 0.0)).astype(BF16)
        kv_scr[c] = _dot_tn(vb_scr[sl, :], ks_scr[sl, :])
        return carry

    lax.fori_loop(0, nc, products, 0, unroll=unroll)

    if has_h0:
        state_scr[:, 0:dk] = h0_ref[0, 0, 0].T
        state_scr[:, dk:2 * dk] = h0_ref[0, 1, 0].T
    else:
        state_scr[...] = jnp.zeros_like(state_scr)

    def recur(i, carry):
        for c, a in ((i, 0), (nc - 1 - i, dk)):
            state = state_scr[:, a:a + dk]
            sprev_scr[c, :, a:a + dk] = state.astype(BF16)
            state_scr[:, a:a + dk] = state * dec_scr[c, 0:1, a:a + dk] + kv_scr[c, :, a:a + dk]
        return carry

    lax.fori_loop(0, nc, recur, 0)

    def finish(c, carry):
        sl = rows(c)
        o = _dot(sc_scr[sl, :], vb_scr[sl, :]) + _dot_nt(qg_scr[sl, :], sprev_scr[c])
        o_ref[sl, :] = (_rmsnorm(o, nw_ref[...]) * _silu(r_ref[sl, :])).astype(o_ref.dtype)
        return carry

    lax.fori_loop(0, nc, finish, 0, unroll=unroll)

    if want_state:
        st_ref[0, 0, 0] = state_scr[:, 0:dk].T
        st_ref[0, 1, 0] = state_scr[:, dk:2 * dk].T


def _gla_call(proj, wa_cat, ba_cat, norm_w, h0, *, n_seq, seq_len, want_state):
    L = seq_len
    nc = L // CHUNK
    blk = lambda width, base: pl.BlockSpec((L, width), lambda s, h: (s, base // width + h))
    in_specs = [blk(GLA_DK, COL_Q), blk(GLA_DK, COL_K), blk(GLA_DV, COL_V), blk(GLA_DV, COL_R),
                pl.BlockSpec((L, LANES), lambda s, h: (s, COL_SMALL // LANES)),
                pl.BlockSpec((LANES, 2 * GLA_DK), lambda s, h: (0, h)),
                pl.BlockSpec((1, 2 * GLA_DK), lambda s, h: (0, h)),
                pl.BlockSpec((1, GLA_DV), lambda s, h: (0, 0))]
    args = [proj, proj, proj, proj, proj, wa_cat, ba_cat, norm_w]
    state_spec = pl.BlockSpec((1, 2, 1, GLA_DK, GLA_DV), lambda s, h: (s, 0, h, 0, 0))
    if h0 is not None:
        in_specs.append(state_spec)
        args.append(h0)
    out_specs = [pl.BlockSpec((L, GLA_DV), lambda s, h: (s, h))]
    out_shape = [jax.ShapeDtypeStruct((n_seq * L, GLA_V), BF16)]
    if want_state:
        out_specs.append(state_spec)
        out_shape.append(jax.ShapeDtypeStruct((n_seq, 2, GLA_HEADS, GLA_DK, GLA_DV), F32))
    body = functools.partial(_gla_body, seq_len=L, has_h0=h0 is not None, want_state=want_state)
    return pl.pallas_call(
        body,
        grid=(n_seq, GLA_HEADS),
        in_specs=in_specs,
        out_specs=out_specs,
        out_shape=out_shape,
        scratch_shapes=[pltpu.VMEM((L, 2 * GLA_DK), F32),
                        pltpu.VMEM((L, 4 * GLA_DK), BF16),
                        pltpu.VMEM((L, 2 * GLA_DK), BF16),
                        pltpu.VMEM((L, 2 * GLA_DK), BF16),
                        pltpu.VMEM((L, GLA_DV), BF16),
                        pltpu.VMEM((L, CHUNK), BF16),
                        pltpu.VMEM((nc, GLA_DV, 2 * GLA_DK), F32),
                        pltpu.VMEM((nc, SUBLANES, 2 * GLA_DK), F32),
                        pltpu.VMEM((nc, GLA_DV, 2 * GLA_DK), BF16),
                        pltpu.VMEM((GLA_DV, 2 * GLA_DK), F32)],
        compiler_params=_params(2),
        name="gla",
    )(*args)


def _ssd_body(*refs, seq_len, grid_rows, has_h0, want_state):
    it = iter(refs)
    (xs_ref, b_ref, c_ref, sm_ref, cwx_ref, cwb_ref, cwc_ref, cbx_ref, cbb_ref, cbc_ref,
     par_ref, dsk_ref) = (next(it) for _ in range(12))
    h0_ref = next(it) if has_h0 else None
    y_ref = next(it)
    st_ref = next(it) if want_state else None
    (pad_scr, xs_s, b_s, c_s, dt_scr, xy_scr, sc_scr, ed_scr, xw_scr, dec_scr, cs_scr, sprev_scr,
     state_scr) = (next(it) for _ in range(13))

    L = seq_len
    nc = L // CHUNK
    width = L // grid_rows
    pad = pad_scr.shape[0] - L
    pad //= 2
    grp = pl.program_id(1)

    def conv_into(src_ref, cw_ref, cb_ref, dst_ref):
        ch = src_ref.shape[1]
        pad_scr[0:pad, 0:ch] = jnp.zeros((pad, ch), F32)
        pad_scr[pad + L:pad + L + pad, 0:ch] = jnp.zeros((pad, ch), F32)
        pad_scr[pad:pad + L, 0:ch] = src_ref[...]
        rc = min(2 * CHUNK, L)
        col = lax.broadcasted_iota(jnp.int32, (rc, ch), 0) % width
        di_taps = range(CONV_K) if grid_rows > 1 else (CONV_K // 2,)
        for r0 in range(0, L, rc):
            acc = jnp.broadcast_to(cb_ref[...], (rc, ch))
            for dj in range(CONV_K):
                inner = None
                for di in di_taps:
                    off = pad + r0 + (di - 1) * width + (dj - 1)
                    term = cw_ref[di, dj:dj + 1, :] * pad_scr[off:off + rc, 0:ch]
                    inner = term if inner is None else inner + term
                if grid_rows > 1 and dj == 0:
                    inner = jnp.where(col >= 1, inner, 0.0)
                if grid_rows > 1 and dj == CONV_K - 1:
                    inner = jnp.where(col <= width - 2, inner, 0.0)
                acc = acc + inner
            dst_ref[r0:r0 + rc, :] = _silu(acc).astype(dst_ref.dtype)

    conv_into(xs_ref, cwx_ref, cbx_ref, xs_s)
    conv_into(b_ref, cwb_ref, cbb_ref, b_s)
    conv_into(c_ref, cwc_ref, cbc_ref, c_s)

    gw = GROUP_W
    unroll = min(SSD_UNROLL, nc)
    rows = lambda c: pl.ds(pl.multiple_of(c * CHUNK, CHUNK), CHUNK)
    bias_row = par_ref[0:1, :]
    a_row = -jnp.exp(par_ref[1:2, :]) * par_ref[2:3, :]
    lower, upper = _tri_masks()
    lower_b = jnp.where(lower, 1.0, 0.0).astype(BF16)
    upper_b = jnp.where(upper, 1.0, 0.0).astype(BF16)
    lane = lax.broadcasted_iota(jnp.int32, (CHUNK, LANES), 1)
    dt_lanes = (lane >= SM_DTF) & (lane < SM_DTF + TERM_STRIDE)
    src = lax.broadcasted_iota(jnp.int32, (LANES, 2 * gw), 0)
    dst = lax.broadcasted_iota(jnp.int32, (LANES, 2 * gw), 1)
    dst_slot = (grp * HEADS_PER_GROUP + (lax.shift_right_logical(dst, 6) & (HEADS_PER_GROUP - 1))
                + jnp.where(dst >= gw, SSD_HEADS, 0))
    expand = jnp.where((src >= SM_DTF) & ((src & (TERM_STRIDE - 1)) == dst_slot),
                       1.0, 0.0).astype(BF16)
    t_idx = lax.broadcasted_iota(jnp.int32, (CHUNK, gw), 0)
    s_idx = lax.broadcasted_iota(jnp.int32, (CHUNK, gw), 1) & (CHUNK - 1)
    diag = t_idx == s_idx
    blk_r = lax.shift_right_logical(lax.broadcasted_iota(jnp.int32, (MXU_TILE, MXU_TILE), 0), 6)
    blk_c = lax.shift_right_logical(lax.broadcasted_iota(jnp.int32, (MXU_TILE, MXU_TILE), 1), 6)
    same_head = blk_r == blk_c

    dt_scr[...] = _softplus(sm_ref[...] + bias_row)

    def place3(v):
        hi = v.astype(BF16).astype(F32)
        rest = v - hi
        mid = rest.astype(BF16).astype(F32)
        lo = rest - mid
        keep = lambda t: jnp.where(dt_lanes, t, 0.0)
        out = (keep(hi) + pltpu.roll(keep(mid), TERM_STRIDE, axis=1)
               + pltpu.roll(keep(lo), 2 * TERM_STRIDE, axis=1))
        return out.astype(BF16)

    def cumulate(c, carry):
        sl = rows(c)
        dt = dt_scr[sl, :]
        da = dt * a_row
        hi = da.astype(BF16)
        both = jnp.concatenate([hi, (da - hi.astype(F32)).astype(BF16)], axis=1)
        pf = _dot(lower_b, both)
        pb = _dot(upper_b, both)
        cum = jnp.where(lane < SM_DTB, pf[:, 0:LANES] + pf[:, LANES:], pb[:, 0:LANES] + pb[:, LANES:])
        xy_scr[sl, 0:LANES] = place3(cum)
        xy_scr[sl, LANES:2 * LANES] = place3(dt)
        return carry

    lax.fori_loop(0, nc, cumulate, 0, unroll=unroll)

    def weights(c, carry):
        sl = rows(c)
        xc = xs_s[sl, :]
        cum_e = _dot(xy_scr[sl, 0:LANES], expand)
        dt_e = _dot(xy_scr[sl, LANES:2 * LANES], expand)
        cb = _dot_nt(c_s[sl, :], jnp.concatenate([b_s[sl, :]] * HEADS_PER_GROUP, axis=0))
        w = None
        dec_rows = []
        for a, causal, last_i in ((0, t_idx >= s_idx, CHUNK - 1), (gw, t_idx <= s_idx, 0)):
            ce = cum_e[:, a:a + gw]
            de = dt_e[:, a:a + gw]
            cum_row = jnp.sum(jnp.where(diag, ce, 0.0), axis=0, keepdims=True)
            dt_row = jnp.sum(jnp.where(diag, de, 0.0), axis=0, keepdims=True)
            wd = jnp.exp(jnp.where(causal, ce - cum_row, -jnp.inf)) * dt_row
            w = wd if w is None else w + wd
            cum_last = ce[last_i:last_i + 1, :]
            ed_scr[sl, a:a + gw] = jnp.exp(ce)
            xw_scr[sl, a:a + gw] = (xc * (jnp.exp(cum_last - ce) * de)).astype(BF16)
            dec_rows.append(jnp.exp(cum_last))
        sc_scr[sl, :] = (cb * w).astype(BF16)
        dec_scr[c] = jnp.broadcast_to(jnp.concatenate(dec_rows, axis=1), (SUBLANES, 2 * gw))
        return carry

    lax.fori_loop(0, nc, weights, 0, unroll=unroll)

    def products(c, carry):
        sl = rows(c)
        xc = xs_s[sl, :]
        xb = xc.astype(BF16)
        parts = []
        for j in range(gw // MXU_TILE):
            xh = xb[:, j * MXU_TILE:(j + 1) * MXU_TILE]
            rep = jnp.concatenate([xh] * (MXU_TILE // SSD_HEAD_DIM), axis=0)
            bd = jnp.where(same_head, rep, jnp.zeros_like(rep))
            parts.append(_dot(sc_scr[sl, j * MXU_TILE:(j + 1) * MXU_TILE], bd))
        y_ref[sl, :] = jnp.concatenate(parts, axis=1) + xc * dsk_ref[...]
        cs_scr[c] = _dot_tn(b_s[sl, :], xw_scr[sl, :])
        return carry

    lax.fori_loop(0, nc, products, 0, unroll=unroll)

    if has_h0:
        state_scr[:, 0:gw] = h0_ref[0, 0]
        state_scr[:, gw:2 * gw] = h0_ref[0, 1]
    else:
        state_scr[...] = jnp.zeros_like(state_scr)

    def recur(i, carry):
        for c, a in ((i, 0), (nc - 1 - i, gw)):
            state = state_scr[:, a:a + gw]
            sprev_scr[c, :, a:a + gw] = state.astype(BF16)
            state_scr[:, a:a + gw] = state * dec_scr[c, 0:1, a:a + gw] + cs_scr[c, :, a:a + gw]
        return carry

    lax.fori_loop(0, nc, recur, 0)

    def finish(c, carry):
        sl = rows(c)
        yi = _dot(c_s[sl, :], sprev_scr[c]) * ed_scr[sl, :]
        y_ref[sl, :] += yi[:, 0:gw] + yi[:, gw:2 * gw]
        return carry

    lax.fori_loop(0, nc, finish, 0, unroll=unroll)

    if want_state:
        st_ref[0, 0] = state_scr[:, 0:gw]
        st_ref[0, 1] = state_scr[:, gw:2 * gw]


def _ssd_call(proj, conv_w, conv_b, par, dsk, h0, *, n_seq, seq_len, grid_rows, want_state):
    L = seq_len
    x_base = COL_XBC
    b_base = COL_XBC + SSD_INNER
    c_base = b_base + SSD_BC
    in_specs = [pl.BlockSpec((L, GROUP_W), lambda s, g: (s, x_base // GROUP_W + g)),
                pl.BlockSpec((L, SSD_STATE), lambda s, g: (s, b_base // SSD_STATE + g)),
                pl.BlockSpec((L, SSD_STATE), lambda s, g: (s, c_base // SSD_STATE + g)),
                pl.BlockSpec((L, LANES), lambda s, g: (s, COL_SMALL // LANES)),
                pl.BlockSpec((CONV_K, CONV_K, GROUP_W), lambda s, g: (0, 0, g)),
                pl.BlockSpec((CONV_K, CONV_K, SSD_STATE),
                             lambda s, g: (0, 0, SSD_INNER // SSD_STATE + g)),
                pl.BlockSpec((CONV_K, CONV_K, SSD_STATE),
                             lambda s, g: (0, 0, (SSD_INNER + SSD_BC) // SSD_STATE + g)),
                pl.BlockSpec((1, GROUP_W), lambda s, g: (0, g)),
                pl.BlockSpec((1, SSD_STATE), lambda s, g: (0, SSD_INNER // SSD_STATE + g)),
                pl.BlockSpec((1, SSD_STATE),
                             lambda s, g: (0, (SSD_INNER + SSD_BC) // SSD_STATE + g)),
                pl.BlockSpec((SUBLANES, LANES), lambda s, g: (0, 0)),
                pl.BlockSpec((1, GROUP_W), lambda s, g: (0, g))]
    args = [proj, proj, proj, proj, conv_w, conv_w, conv_w, conv_b, conv_b, conv_b, par, dsk]
    state_spec = pl.BlockSpec((1, 2, SSD_STATE, GROUP_W), lambda s, g: (s, 0, 0, g))
    if h0 is not None:
        in_specs.append(state_spec)
        args.append(h0)
    out_specs = [pl.BlockSpec((L, GROUP_W), lambda s, g: (s, g))]
    out_shape = [jax.ShapeDtypeStruct((n_seq * L, SSD_INNER), F32)]
    if want_state:
        out_specs.append(state_spec)
        out_shape.append(jax.ShapeDtypeStruct((n_seq, 2, SSD_STATE, SSD_INNER), F32))
    nc = L // CHUNK
    conv_pad = (L // grid_rows + SUBLANES) if grid_rows > 1 else SUBLANES
    body = functools.partial(_ssd_body, seq_len=L, grid_rows=grid_rows,
                             has_h0=h0 is not None, want_state=want_state)
    return pl.pallas_call(
        body,
        grid=(n_seq, SSD_GROUPS),
        in_specs=in_specs,
        out_specs=out_specs,
        out_shape=out_shape,
        scratch_shapes=[pltpu.VMEM((L + 2 * conv_pad, GROUP_W), F32),
                        pltpu.VMEM((L, GROUP_W), F32),
                        pltpu.VMEM((L, SSD_STATE), BF16),
                        pltpu.VMEM((L, SSD_STATE), BF16),
                        pltpu.VMEM((L, LANES), F32),
                        pltpu.VMEM((L, 2 * LANES), BF16),
                        pltpu.VMEM((L, GROUP_W), BF16),
                        pltpu.VMEM((L, 2 * GROUP_W), F32),
                        pltpu.VMEM((L, 2 * GROUP_W), BF16),
                        pltpu.VMEM((nc, SUBLANES, 2 * GROUP_W), F32),
                        pltpu.VMEM((nc, SSD_STATE, 2 * GROUP_W), F32),
                        pltpu.VMEM((nc, SSD_STATE, 2 * GROUP_W), BF16),
                        pltpu.VMEM((SSD_STATE, 2 * GROUP_W), F32)],
        compiler_params=_params(2),
        name="ssd",
    )(*args)


def _layer_path(x, mod, lw, h0_gla, h0_ssd, *, n_seq, seq_len, grid_rows, want_state, final_w):
    x1 = _ffn_call(x, mod, lw["norm_ffn1"], lw["ffn1_w_in"], lw["ffn1_w_out"],
                   sub=0, seq_len=seq_len)
    proj = _proj_call(x1, mod, lw["norm_mix"], lw["w_proj"], seq_len=seq_len)
    gla_out = _gla_call(proj, lw["wa_cat"], lw["ba_cat"], lw["gla_norm_w"], h0_gla,
                        n_seq=n_seq, seq_len=seq_len, want_state=want_state)
    ssd_out = _ssd_call(proj, lw["conv_w"], lw["conv_b"], lw["ssd_par"], lw["d_skip_row"], h0_ssd,
                        n_seq=n_seq, seq_len=seq_len, grid_rows=grid_rows, want_state=want_state)
    o, y = gla_out[0], ssd_out[0]
    out = _ffn_call(x1, mod, lw["norm_ffn2"], lw["ffn2_w_in"], lw["ffn2_w_out"],
                    sub=2, seq_len=seq_len, mix=(o, y, proj, lw["ssd_norm_w"], lw["w_out"]),
                    final_w=final_w)
    if want_state:
        return out, gla_out[1], ssd_out[1]
    return out, None, None


def kernel(x_prompt, x_sample, state_gla, state_ssd, c, c_ctx, norm_ffn1, norm_mix, norm_ffn2, w_mod, b_mod, ffn1_w_in, ffn1_w_out, ffn2_w_in, ffn2_w_out, w_in, gla_w_a2, gla_b_a, gla_norm_w, conv_w, conv_b, dt_bias, a_log, d_skip, ssd_norm_w, w_out, final_norm):
    nb, seq, _ = x_prompt.shape
    db, dseq, _ = x_sample.shape
    grid_rows = dseq // GRID_W
    xp = x_prompt.reshape(nb * seq, D_MODEL)
    xs = x_sample.reshape(db * dseq, D_MODEL)
    row = lambda v: v.reshape(1, -1)
    gla_states, ssd_states = [], []
    for i in range(DEPTH):
        last = i == DEPTH - 1
        n_rows = -(-(db + 1) // SUBLANES) * SUBLANES
        cc = jnp.concatenate([c, c_ctx[None, :], jnp.zeros((n_rows - db - 1, D_MODEL), F32)], 0)
        mod = _mod_call(cc, w_mod[i], row(b_mod[i])).reshape(n_rows, N_MOD, D_MODEL)
        mod_lat, mod_ctx = mod[:db], mod[db:db + 1]

        wi = w_in[i]
        o_af = COL_R + GLA_V
        o_z = o_af + 2 * GLA_LOWRANK
        o_xbc = o_z + SSD_INNER
        o_dt = o_xbc + SSD_CONV_DIM
        w_proj = jnp.concatenate(
            [wi[:, :o_af], wi[:, o_z:o_dt], wi[:, o_af:o_z], wi[:, o_dt:],
             jnp.zeros((D_MODEL, LANES - 2 * GLA_LOWRANK - 2 * SSD_HEADS), F32)], axis=1).astype(BF16)
        wa_cat = jnp.zeros((LANES, GLA_HEADS, 2, GLA_DK), F32)
        wa_cat = wa_cat.at[SM_AF:SM_AF + GLA_LOWRANK, :, 0, :].set(
            gla_w_a2[i, 0].reshape(GLA_LOWRANK, GLA_HEADS, GLA_DK))
        wa_cat = wa_cat.at[SM_AB:SM_AB + GLA_LOWRANK, :, 1, :].set(
            gla_w_a2[i, 1].reshape(GLA_LOWRANK, GLA_HEADS, GLA_DK))
        wa_cat = wa_cat.reshape(LANES, 2 * GLA_QK)
        ba_cat = jnp.stack([gla_b_a[i, 0].reshape(GLA_HEADS, GLA_DK),
                            gla_b_a[i, 1].reshape(GLA_HEADS, GLA_DK)], axis=1).reshape(1, 2 * GLA_QK)
        ssd_par = jnp.zeros((SUBLANES, LANES), F32)
        ssd_par = ssd_par.at[0, SM_DTF:SM_DTB + SSD_HEADS].set(dt_bias[i].reshape(-1))
        ssd_par = ssd_par.at[1, SM_DTF:SM_DTB + SSD_HEADS].set(a_log[i].reshape(-1))
        ssd_par = ssd_par.at[2, SM_DTF:SM_DTB + SSD_HEADS].set(1.0)
        lw = {
            "norm_ffn1": row(norm_ffn1[i]), "norm_mix": row(norm_mix[i]),
            "norm_ffn2": row(norm_ffn2[i]),
            "ffn1_w_in": ffn1_w_in[i].astype(BF16), "ffn1_w_out": ffn1_w_out[i].astype(BF16),
            "ffn2_w_in": ffn2_w_in[i].astype(BF16), "ffn2_w_out": ffn2_w_out[i].astype(BF16),
            "w_proj": w_proj, "wa_cat": wa_cat, "ba_cat": ba_cat,
            "gla_norm_w": row(gla_norm_w[i]), "conv_w": conv_w[i], "conv_b": row(conv_b[i]),
            "ssd_par": ssd_par, "d_skip_row": row(jnp.repeat(d_skip[i], SSD_HEAD_DIM)),
            "ssd_norm_w": row(ssd_norm_w[i]), "w_out": w_out[i].astype(BF16),
        }
        fw = row(final_norm) if last else None

        h0_ssd = state_ssd[:, i].transpose(0, 1, 4, 2, 3).reshape(db, 2, SSD_STATE, SSD_INNER)
        xp, sg, ss = _layer_path(xp, mod_ctx, lw, None, None, n_seq=nb, seq_len=seq,
                                 grid_rows=1, want_state=True, final_w=fw)
        xs, _, _ = _layer_path(xs, mod_lat, lw, state_gla[:, i], h0_ssd, n_seq=db, seq_len=dseq,
                               grid_rows=grid_rows, want_state=False, final_w=fw)
        gla_states.append(sg)
        ssd_states.append(ss.reshape(nb, 2, SSD_STATE, SSD_HEADS, SSD_HEAD_DIM)
                          .transpose(0, 1, 3, 4, 2))
    y_prompt = xp.reshape(nb, seq, D_MODEL)
    y_sample = xs.reshape(db, dseq, D_MODEL)
    return (y_prompt, y_sample, jnp.stack(gla_states, axis=1), jnp.stack(ssd_states, axis=1))
```

```python
import functools

import jax
import jax.numpy as jnp
from jax import lax
from jax.experimental import pallas as pl
from jax.experimental.pallas import tpu as pltpu

F32 = jnp.float32
BF16 = jnp.bfloat16

D_MODEL = 1024
DEPTH = 1
GRID_W = 64
CHUNK = 64
EPS = 1e-6
N_MOD = 9
D_FF = 2816
GLA_HEADS = 4
GLA_DK = 128
GLA_DV = 256
GLA_LOWRANK = 16
GLA_TAU = 16.0
GLA_QK = GLA_HEADS * GLA_DK
GLA_V = GLA_HEADS * GLA_DV
SSD_HEADS = 16
SSD_HEAD_DIM = 64
SSD_GROUPS = 2
SSD_STATE = 128
SSD_INNER = SSD_HEADS * SSD_HEAD_DIM
SSD_BC = SSD_GROUPS * SSD_STATE
SSD_CONV_DIM = SSD_INNER + 2 * SSD_BC
CONV_K = 3
D_MIX = GLA_V + SSD_INNER

LANES = 128
SUBLANES = 8
VMEM_LIMIT_BYTES = 56 * 1024 * 1024

COL_Q = 0
COL_K = COL_Q + GLA_QK
COL_V = COL_K + GLA_QK
COL_R = COL_V + GLA_V
COL_Z = COL_R + GLA_V
COL_XBC = COL_Z + SSD_INNER
COL_SMALL = COL_XBC + SSD_CONV_DIM
N_PROJ = COL_SMALL + LANES
SM_AF = 0
SM_AB = SM_AF + GLA_LOWRANK
SM_DTF = SM_AB + GLA_LOWRANK
SM_DTB = SM_DTF + SSD_HEADS
HEADS_PER_GROUP = SSD_HEADS // SSD_GROUPS
GROUP_W = HEADS_PER_GROUP * SSD_HEAD_DIM

TOKEN_TILE = 256
GLA_UNROLL = 16
SSD_UNROLL = 16
MXU_TILE = 256
TERM_STRIDE = 2 * SSD_HEADS


def _dot(a, b):
    return jnp.dot(a, b, preferred_element_type=F32)


def _dot_nt(a, b):
    return lax.dot_general(a, b, (((1,), (1,)), ((), ())), preferred_element_type=F32)


def _dot_tn(a, b):
    return lax.dot_general(a, b, (((0,), (0,)), ((), ())), preferred_element_type=F32)


def _silu(x):
    return x * jax.nn.sigmoid(x)


def _log1p_exp_neg_abs(x):
    return jnp.log(1.0 + jnp.exp(-jnp.abs(x)))


def _softplus(x):
    return jnp.maximum(x, 0.0) + _log1p_exp_neg_abs(x)


def _log_sigmoid(x):
    return jnp.minimum(x, 0.0) - _log1p_exp_neg_abs(x)


def _rmsnorm(x, w):
    ms = jnp.mean(x * x, axis=-1, keepdims=True)
    return x * lax.rsqrt(ms + EPS) * w


def _split3(x):
    hi = x.astype(BF16)
    r1 = x - hi.astype(F32)
    mid = r1.astype(BF16)
    lo = (r1 - mid.astype(F32)).astype(BF16)
    return hi, mid, lo


def _exact_dot(sel_bf16, x):
    hi, mid, lo = _split3(x)
    return (_dot(sel_bf16, lo) + _dot(sel_bf16, mid)) + _dot(sel_bf16, hi)


def _resident(shape):
    nd = len(shape)
    return pl.BlockSpec(shape, lambda *_: (0,) * nd, pipeline_mode=pl.Buffered(1))


def _params(n_axes):
    return pltpu.CompilerParams(dimension_semantics=("arbitrary",) * n_axes,
                                vmem_limit_bytes=VMEM_LIMIT_BYTES)


def _mod_body(c_ref, w_ref, b_ref, out_ref):
    a = _silu(c_ref[...]).astype(BF16)
    out_ref[...] = _dot(a, w_ref[...].astype(BF16)) + b_ref[...]


def _mod_call(cc, w_mod, b_mod):
    n_rows = cc.shape[0]
    tn = D_MODEL
    return pl.pallas_call(
        _mod_body,
        grid=(N_MOD * D_MODEL // tn,),
        in_specs=[pl.BlockSpec((n_rows, D_MODEL), lambda j: (0, 0)),
                  pl.BlockSpec((D_MODEL, tn), lambda j: (0, j)),
                  pl.BlockSpec((1, tn), lambda j: (0, j))],
        out_specs=pl.BlockSpec((n_rows, tn), lambda j: (0, j)),
        out_shape=jax.ShapeDtypeStruct((n_rows, N_MOD * D_MODEL), F32),
        compiler_params=_params(1),
        name="mod",
    )(cc, w_mod, b_mod)


def _ffn_body(*refs, sub, has_mix, has_final):
    it = iter(refs)
    x_ref, mod_ref, nw_ref, win_ref, wout_ref = (next(it) for _ in range(5))
    if has_mix:
        o_ref, y_ref, z_ref, snw_ref, wo_ref = (next(it) for _ in range(5))
    if has_final:
        fn_ref = next(it)
    out_ref = next(it)

    x = x_ref[...]
    if has_mix:
        g2 = mod_ref[0, 5:6, :]
        yn = _rmsnorm(y_ref[...] * _silu(z_ref[...]), snw_ref[...])
        m = _dot(o_ref[...], wo_ref[:GLA_V, :]) + _dot(yn.astype(BF16), wo_ref[GLA_V:, :])
        x = x + g2 * m
    sh = mod_ref[0, 3 * sub:3 * sub + 1, :]
    sc = mod_ref[0, 3 * sub + 1:3 * sub + 2, :]
    gate = mod_ref[0, 3 * sub + 2:3 * sub + 3, :]
    h = (_rmsnorm(x, nw_ref[...]) * (1.0 + sc) + sh).astype(BF16)
    g = _dot(h, win_ref[:, :D_FF])
    u = _dot(h, win_ref[:, D_FF:])
    act = (_silu(g) * u).astype(BF16)
    x = x + (0.5 * gate) * _dot(act, wout_ref[...])
    if has_final:
        x = _rmsnorm(x, fn_ref[...])
    out_ref[...] = x


def _ffn_call(x, mod, norm_w, w_in, w_out, *, sub, seq_len, mix=None, final_w=None):
    m_tok = x.shape[0]
    tm = TOKEN_TILE
    tiles_per_seq = seq_len // tm
    shared_mod = mod.shape[0] == 1
    mod_map = (lambda i: (0, 0, 0)) if shared_mod else (lambda i: (i // tiles_per_seq, 0, 0))
    tok = lambda width: pl.BlockSpec((tm, width), lambda i: (i, 0))
    in_specs = [tok(D_MODEL), pl.BlockSpec((1, N_MOD, D_MODEL), mod_map),
                _resident((1, D_MODEL)), _resident(w_in.shape), _resident(w_out.shape)]
    args = [x, mod, norm_w, w_in, w_out]
    if mix is not None:
        o, y, proj, ssd_norm_w, w_mix_out = mix
        in_specs += [tok(GLA_V), tok(SSD_INNER),
                     pl.BlockSpec((tm, SSD_INNER), lambda i: (i, COL_Z // SSD_INNER)),
                     _resident((1, SSD_INNER)), _resident(w_mix_out.shape)]
        args += [o, y, proj, ssd_norm_w, w_mix_out]
    if final_w is not None:
        in_specs.append(_resident((1, D_MODEL)))
        args.append(final_w)
    body = functools.partial(_ffn_body, sub=sub, has_mix=mix is not None,
                             has_final=final_w is not None)
    return pl.pallas_call(
        body,
        grid=(m_tok // tm,),
        in_specs=in_specs,
        out_specs=tok(D_MODEL),
        out_shape=jax.ShapeDtypeStruct((m_tok, D_MODEL), F32),
        compiler_params=_params(1),
        name="ffn_mix" if mix is not None else "ffn",
    )(*args)


def _proj_body(x_ref, mod_ref, nw_ref, w_ref, out_ref):
    sh = mod_ref[0, 3:4, :]
    sc = mod_ref[0, 4:5, :]
    h = (_rmsnorm(x_ref[...], nw_ref[...]) * (1.0 + sc) + sh).astype(BF16)
    out_ref[...] = _dot(h, w_ref[...])


def _proj_call(x, mod, norm_w, w_proj, *, seq_len):
    m_tok = x.shape[0]
    tm = TOKEN_TILE
    tiles_per_seq = seq_len // tm
    shared_mod = mod.shape[0] == 1
    mod_map = (lambda i: (0, 0, 0)) if shared_mod else (lambda i: (i // tiles_per_seq, 0, 0))
    return pl.pallas_call(
        _proj_body,
        grid=(m_tok // tm,),
        in_specs=[pl.BlockSpec((tm, D_MODEL), lambda i: (i, 0)),
                  pl.BlockSpec((1, N_MOD, D_MODEL), mod_map),
                  _resident((1, D_MODEL)), _resident(w_proj.shape)],
        out_specs=pl.BlockSpec((tm, N_PROJ), lambda i: (i, 0)),
        out_shape=jax.ShapeDtypeStruct((m_tok, N_PROJ), F32),
        compiler_params=_params(1),
        name="proj",
    )(x, mod, norm_w, w_proj)


def _tri_masks():
    row = lax.broadcasted_iota(jnp.int32, (CHUNK, CHUNK), 0)
    col = lax.broadcasted_iota(jnp.int32, (CHUNK, CHUNK), 1)
    return row >= col, row <= col


def _gla_body(*refs, seq_len, has_h0, want_state):
    it = iter(refs)
    q_ref, k_ref, v_ref, r_ref, sm_ref, wa_ref, ba_ref, nw_ref = (next(it) for _ in range(8))
    h0_ref = next(it) if has_h0 else None
    o_ref = next(it)
    st_ref = next(it) if want_state else None
    (la_scr, qk_scr, qg_scr, ks_scr, vb_scr, sc_scr, kv_scr, dec_scr, sprev_scr,
     state_scr) = (next(it) for _ in range(10))

    nc = seq_len // CHUNK
    dk = GLA_DK
    lower, upper = _tri_masks()
    lower_b = jnp.where(lower, 1.0, 0.0).astype(BF16)
    upper_b = jnp.where(upper, 1.0, 0.0).astype(BF16)
    unroll = min(GLA_UNROLL, nc)
    rows = lambda c: pl.ds(pl.multiple_of(c * CHUNK, CHUNK), CHUNK)

    pre = _dot(sm_ref[...].astype(BF16), wa_ref[...].astype(BF16)) + ba_ref[...]
    la_scr[...] = _log_sigmoid(pre) * (1.0 / GLA_TAU)

    def prep(c, carry):
        sl = rows(c)
        qc = q_ref[sl, :] * (GLA_DK ** -0.5)
        kc = k_ref[sl, :]
        vb_scr[sl, :] = v_ref[sl, :].astype(BF16)
        la = la_scr[sl, :]
        hi = la.astype(BF16)
        lo = (la - hi.astype(F32)).astype(BF16)

        def cumulative(tri_b, a):
            p = _dot(tri_b, jnp.concatenate([hi[:, a:a + dk], lo[:, a:a + dk]], axis=1))
            return p[:, 0:dk] + p[:, dk:2 * dk]

        g_f = cumulative(lower_b, 0)
        g_b = cumulative(upper_b, dk)
        mid_f = g_f[CHUNK // 2:CHUNK // 2 + 1, :]
        mid_b = g_b[CHUNK - 1 - CHUNK // 2:CHUNK - CHUNK // 2, :]
        end_f = g_f[CHUNK - 1:CHUNK, :]
        end_b = g_b[0:1, :]
        qk_scr[sl, 0:dk] = (qc * jnp.exp(g_f - mid_f)).astype(BF16)
        qk_scr[sl, dk:2 * dk] = (qc * jnp.exp(g_b - mid_b)).astype(BF16)
        qk_scr[sl, 2 * dk:3 * dk] = (kc * jnp.exp(mid_f - g_f)).astype(BF16)
        qk_scr[sl, 3 * dk:4 * dk] = (kc * jnp.exp(mid_b - g_b)).astype(BF16)
        qg_scr[sl, 0:dk] = (qc * jnp.exp(g_f)).astype(BF16)
        qg_scr[sl, dk:2 * dk] = (qc * jnp.exp(g_b)).astype(BF16)
        ks_scr[sl, 0:dk] = (kc * jnp.exp(end_f - g_f)).astype(BF16)
        ks_scr[sl, dk:2 * dk] = (kc * jnp.exp(end_b - g_b)).astype(BF16)
        dec = jnp.exp(jnp.concatenate([end_f, end_b], axis=1))
        dec_scr[c] = jnp.broadcast_to(dec, (SUBLANES, 2 * dk))
        return carry

    lax.fori_loop(0, nc, prep, 0, unroll=unroll)

    def products(c, carry):
        sl = rows(c)
        s_f = _dot_nt(qk_scr[sl, 0:dk], qk_scr[sl, 2 * dk:3 * dk])
        s_b = _dot_nt(qk_scr[sl, dk:2 * dk], qk_scr[sl, 3 * dk:4 * dk])
        sc_scr[sl, :] = (jnp.where(lower, s_f, 0.0) + jnp.where(upper, s_b, 0.0)).astype(BF16)
        kv_scr[c] = _dot_tn(vb_scr[sl, :], ks_scr[sl, :])
        return carry

    lax.fori_loop(0, nc, products, 0, unroll=unroll)

    if has_h0:
        state_scr[:, 0:dk] = h0_ref[0, 0, 0].T
        state_scr[:, dk:2 * dk] = h0_ref[0, 1, 0].T
    else:
        state_scr[...] = jnp.zeros_like(state_scr)

    def recur(i, carry):
        for c, a in ((i, 0), (nc - 1 - i, dk)):
            state = state_scr[:, a:a + dk]
            sprev_scr[c, :, a:a + dk] = state.astype(BF16)
            state_scr[:, a:a + dk] = state * dec_scr[c, 0:1, a:a + dk] + kv_scr[c, :, a:a + dk]
        return carry

    lax.fori_loop(0, nc, recur, 0)

    def finish(c, carry):
        sl = rows(c)
        o = _dot(sc_scr[sl, :], vb_scr[sl, :]) + _dot_nt(qg_scr[sl, :], sprev_scr[c])
        o_ref[sl, :] = (_rmsnorm(o, nw_ref[...]) * _silu(r_ref[sl, :])).astype(o_ref.dtype)
        return carry

    lax.fori_loop(0, nc, finish, 0, unroll=unroll)

    if want_state:
        st_ref[0, 0, 0] = state_scr[:, 0:dk].T
        st_ref[0, 1, 0] = state_scr[:, dk:2 * dk].T


def _gla_call(proj, wa_cat, ba_cat, norm_w, h0, *, n_seq, seq_len, want_state):
    L = seq_len
    nc = L // CHUNK
    blk = lambda width, base: pl.BlockSpec((L, width), lambda s, h: (s, base // width + h))
    in_specs = [blk(GLA_DK, COL_Q), blk(GLA_DK, COL_K), blk(GLA_DV, COL_V), blk(GLA_DV, COL_R),
                pl.BlockSpec((L, LANES), lambda s, h: (s, COL_SMALL // LANES)),
                pl.BlockSpec((LANES, 2 * GLA_DK), lambda s, h: (0, h)),
                pl.BlockSpec((1, 2 * GLA_DK), lambda s, h: (0, h)),
                pl.BlockSpec((1, GLA_DV), lambda s, h: (0, 0))]
    args = [proj, proj, proj, proj, proj, wa_cat, ba_cat, norm_w]
    state_spec = pl.BlockSpec((1, 2, 1, GLA_DK, GLA_DV), lambda s, h: (s, 0, h, 0, 0))
    if h0 is not None:
        in_specs.append(state_spec)
        args.append(h0)
    out_specs = [pl.BlockSpec((L, GLA_DV), lambda s, h: (s, h))]
    out_shape = [jax.ShapeDtypeStruct((n_seq * L, GLA_V), BF16)]
    if want_state:
        out_specs.append(state_spec)
        out_shape.append(jax.ShapeDtypeStruct((n_seq, 2, GLA_HEADS, GLA_DK, GLA_DV), F32))
    body = functools.partial(_gla_body, seq_len=L, has_h0=h0 is not None, want_state=want_state)
    return pl.pallas_call(
        body,
        grid=(n_seq, GLA_HEADS),
        in_specs=in_specs,
        out_specs=out_specs,
        out_shape=out_shape,
        scratch_shapes=[pltpu.VMEM((L, 2 * GLA_DK), F32),
                        pltpu.VMEM((L, 4 * GLA_DK), BF16),
                        pltpu.VMEM((L, 2 * GLA_DK), BF16),
                        pltpu.VMEM((L, 2 * GLA_DK), BF16),
                        pltpu.VMEM((L, GLA_DV), BF16),
                        pltpu.VMEM((L, CHUNK), BF16),
                        pltpu.VMEM((nc, GLA_DV, 2 * GLA_DK), F32),
                        pltpu.VMEM((nc, SUBLANES, 2 * GLA_DK), F32),
                        pltpu.VMEM((nc, GLA_DV, 2 * GLA_DK), BF16),
                        pltpu.VMEM((GLA_DV, 2 * GLA_DK), F32)],
        compiler_params=_params(2),
        name="gla",
    )(*args)


def _ssd_body(*refs, seq_len, grid_rows, has_h0, want_state):
    it = iter(refs)
    (xs_ref, b_ref, c_ref, sm_ref, cwx_ref, cwb_ref, cwc_ref, cbx_ref, cbb_ref, cbc_ref,
     par_ref, dsk_ref) = (next(it) for _ in range(12))
    h0_ref = next(it) if has_h0 else None
    y_ref = next(it)
    st_ref = next(it) if want_state else None
    (pad_scr, xs_s, b_s, c_s, dt_scr, xy_scr, sc_scr, ed_scr, xw_scr, dec_scr, cs_scr, sprev_scr,
     state_scr) = (next(it) for _ in range(13))

    L = seq_len
    nc = L // CHUNK
    width = L // grid_rows
    pad = pad_scr.shape[0] - L
    pad //= 2
    grp = pl.program_id(1)

    def conv_into(src_ref, cw_ref, cb_ref, dst_ref):
        ch = src_ref.shape[1]
        pad_scr[0:pad, 0:ch] = jnp.zeros((pad, ch), F32)
        pad_scr[pad + L:pad + L + pad, 0:ch] = jnp.zeros((pad, ch), F32)
        pad_scr[pad:pad + L, 0:ch] = src_ref[...]
        rc = min(2 * CHUNK, L)
        col = lax.broadcasted_iota(jnp.int32, (rc, ch), 0) % width
        di_taps = range(CONV_K) if grid_rows > 1 else (CONV_K // 2,)
        for r0 in range(0, L, rc):
            acc = jnp.broadcast_to(cb_ref[...], (rc, ch))
            for dj in range(CONV_K):
                inner = None
                for di in di_taps:
                    off = pad + r0 + (di - 1) * width + (dj - 1)
                    term = cw_ref[di, dj:dj + 1, :] * pad_scr[off:off + rc, 0:ch]
                    inner = term if inner is None else inner + term
                if grid_rows > 1 and dj == 0:
                    inner = jnp.where(col >= 1, inner, 0.0)
                if grid_rows > 1 and dj == CONV_K - 1:
                    inner = jnp.where(col <= width - 2, inner, 0.0)
                acc = acc + inner
            dst_ref[r0:r0 + rc, :] = _silu(acc).astype(dst_ref.dtype)

    conv_into(xs_ref, cwx_ref, cbx_ref, xs_s)
    conv_into(b_ref, cwb_ref, cbb_ref, b_s)
    conv_into(c_ref, cwc_ref, cbc_ref, c_s)

    gw = GROUP_W
    unroll = min(SSD_UNROLL, nc)
    rows = lambda c: pl.ds(pl.multiple_of(c * CHUNK, CHUNK), CHUNK)
    bias_row = par_ref[0:1, :]
    a_row = -jnp.exp(par_ref[1:2, :]) * par_ref[2:3, :]
    lower, upper = _tri_masks()
    lower_b = jnp.where(lower, 1.0, 0.0).astype(BF16)
    upper_b = jnp.where(upper, 1.0, 0.0).astype(BF16)
    lane = lax.broadcasted_iota(jnp.int32, (CHUNK, LANES), 1)
    dt_lanes = (lane >= SM_DTF) & (lane < SM_DTF + TERM_STRIDE)
    src = lax.broadcasted_iota(jnp.int32, (LANES, 2 * gw), 0)
    dst = lax.broadcasted_iota(jnp.int32, (LANES, 2 * gw), 1)
    dst_slot = (grp * HEADS_PER_GROUP + (lax.shift_right_logical(dst, 6) & (HEADS_PER_GROUP - 1))
                + jnp.where(dst >= gw, SSD_HEADS, 0))
    expand = jnp.where((src >= SM_DTF) & ((src & (TERM_STRIDE - 1)) == dst_slot),
                       1.0, 0.0).astype(BF16)
    t_idx = lax.broadcasted_iota(jnp.int32, (CHUNK, gw), 0)
    s_idx = lax.broadcasted_iota(jnp.int32, (CHUNK, gw), 1) & (CHUNK - 1)
    diag = t_idx == s_idx
    blk_r = lax.shift_right_logical(lax.broadcasted_iota(jnp.int32, (MXU_TILE, MXU_TILE), 0), 6)
    blk_c = lax.shift_right_logical(lax.broadcasted_iota(jnp.int32, (MXU_TILE, MXU_TILE), 1), 6)
    same_head = blk_r == blk_c

    dt_scr[...] = _softplus(sm_ref[...] + bias_row)

    def place3(v):
        hi = v.astype(BF16).astype(F32)
        rest = v - hi
        mid = rest.astype(BF16).astype(F32)
        lo = rest - mid
        keep = lambda t: jnp.where(dt_lanes, t, 0.0)
        out = (keep(hi) + pltpu.roll(keep(mid), TERM_STRIDE, axis=1)
               + pltpu.roll(keep(lo), 2 * TERM_STRIDE, axis=1))
        return out.astype(BF16)

    def cumulate(c, carry):
        sl = rows(c)
        dt = dt_scr[sl, :]
        da = dt * a_row
        hi = da.astype(BF16)
        both = jnp.concatenate([hi, (da - hi.astype(F32)).astype(BF16)], axis=1)
        pf = _dot(lower_b, both)
        pb = _dot(upper_b, both)
        cum = jnp.where(lane < SM_DTB, pf[:, 0:LANES] + pf[:, LANES:], pb[:, 0:LANES] + pb[:, LANES:])
        xy_scr[sl, 0:LANES] = place3(cum)
        xy_scr[sl, LANES:2 * LANES] = place3(dt)
        return carry

    lax.fori_loop(0, nc, cumulate, 0, unroll=unroll)

    def weights(c, carry):
        sl = rows(c)
        xc = xs_s[sl, :]
        cum_e = _dot(xy_scr[sl, 0:LANES], expand)
        dt_e = _dot(xy_scr[sl, LANES:2 * LANES], expand)
        cb = _dot_nt(c_s[sl, :], jnp.concatenate([b_s[sl, :]] * HEADS_PER_GROUP, axis=0))
        w = None
        dec_rows = []
        for a, causal, last_i in ((0, t_idx >= s_idx, CHUNK - 1), (gw, t_idx <= s_idx, 0)):
            ce = cum_e[:, a:a + gw]
            de = dt_e[:, a:a + gw]
            cum_row = jnp.sum(jnp.where(diag, ce, 0.0), axis=0, keepdims=True)
            dt_row = jnp.sum(jnp.where(diag, de, 0.0), axis=0, keepdims=True)
            wd = jnp.exp(jnp.where(causal, ce - cum_row, -jnp.inf)) * dt_row
            w = wd if w is None else w + wd
            cum_last = ce[last_i:last_i + 1, :]
            ed_scr[sl, a:a + gw] = jnp.exp(ce)
            xw_scr[sl, a:a + gw] = (xc * (jnp.exp(cum_last - ce) * de)).astype(BF16)
            dec_rows.append(jnp.exp(cum_last))
        sc_scr[sl, :] = (cb * w).astype(BF16)
        dec_scr[c] = jnp.broadcast_to(jnp.concatenate(dec_rows, axis=1), (SUBLANES, 2 * gw))
        return carry

    lax.fori_loop(0, nc, weights, 0, unroll=unroll)

    def products(c, carry):
        sl = rows(c)
        xc = xs_s[sl, :]
        xb = xc.astype(BF16)
        parts = []
        for j in range(gw // MXU_TILE):
            xh = xb[:, j * MXU_TILE:(j + 1) * MXU_TILE]
            rep = jnp.concatenate([xh] * (MXU_TILE // SSD_HEAD_DIM), axis=0)
            bd = jnp.where(same_head, rep, jnp.zeros_like(rep))
            parts.append(_dot(sc_scr[sl, j * MXU_TILE:(j + 1) * MXU_TILE], bd))
        y_ref[sl, :] = jnp.concatenate(parts, axis=1) + xc * dsk_ref[...]
        cs_scr[c] = _dot_tn(b_s[sl, :], xw_scr[sl, :])
        return carry

    lax.fori_loop(0, nc, products, 0, unroll=unroll)

    if has_h0:
        state_scr[:, 0:gw] = h0_ref[0, 0]
        state_scr[:, gw:2 * gw] = h0_ref[0, 1]
    else:
        state_scr[...] = jnp.zeros_like(state_scr)

    def recur(i, carry):
        for c, a in ((i, 0), (nc - 1 - i, gw)):
            state = state_scr[:, a:a + gw]
            sprev_scr[c, :, a:a + gw] = state.astype(BF16)
            state_scr[:, a:a + gw] = state * dec_scr[c, 0:1, a:a + gw] + cs_scr[c, :, a:a + gw]
        return carry

    lax.fori_loop(0, nc, recur, 0)

    def finish(c, carry):
        sl = rows(c)
        yi = _dot(c_s[sl, :], sprev_scr[c]) * ed_scr[sl, :]
        y_ref[sl, :] += yi[:, 0:gw] + yi[:, gw:2 * gw]
        return carry

    lax.fori_loop(0, nc, finish, 0, unroll=unroll)

    if want_state:
        st_ref[0, 0] = state_scr[:, 0:gw]
        st_ref[0, 1] = state_scr[:, gw:2 * gw]


def _ssd_call(proj, conv_w, conv_b, par, dsk, h0, *, n_seq, seq_len, grid_rows, want_state):
    L = seq_len
    x_base = COL_XBC
    b_base = COL_XBC + SSD_INNER
    c_base = b_base + SSD_BC
    in_specs = [pl.BlockSpec((L, GROUP_W), lambda s, g: (s, x_base // GROUP_W + g)),
                pl.BlockSpec((L, SSD_STATE), lambda s, g: (s, b_base // SSD_STATE + g)),
                pl.BlockSpec((L, SSD_STATE), lambda s, g: (s, c_base // SSD_STATE + g)),
                pl.BlockSpec((L, LANES), lambda s, g: (s, COL_SMALL // LANES)),
                pl.BlockSpec((CONV_K, CONV_K, GROUP_W), lambda s, g: (0, 0, g)),
                pl.BlockSpec((CONV_K, CONV_K, SSD_STATE),
                             lambda s, g: (0, 0, SSD_INNER // SSD_STATE + g)),
                pl.BlockSpec((CONV_K, CONV_K, SSD_STATE),
                             lambda s, g: (0, 0, (SSD_INNER + SSD_BC) // SSD_STATE + g)),
                pl.BlockSpec((1, GROUP_W), lambda s, g: (0, g)),
                pl.BlockSpec((1, SSD_STATE), lambda s, g: (0, SSD_INNER // SSD_STATE + g)),
                pl.BlockSpec((1, SSD_STATE),
                             lambda s, g: (0, (SSD_INNER + SSD_BC) // SSD_STATE + g)),
                pl.BlockSpec((SUBLANES, LANES), lambda s, g: (0, 0)),
                pl.BlockSpec((1, GROUP_W), lambda s, g: (0, g))]
    args = [proj, proj, proj, proj, conv_w, conv_w, conv_w, conv_b, conv_b, conv_b, par, dsk]
    state_spec = pl.BlockSpec((1, 2, SSD_STATE, GROUP_W), lambda s, g: (s, 0, 0, g))
    if h0 is not None:
        in_specs.append(state_spec)
        args.append(h0)
    out_specs = [pl.BlockSpec((L, GROUP_W), lambda s, g: (s, g))]
    out_shape = [jax.ShapeDtypeStruct((n_seq * L, SSD_INNER), F32)]
    if want_state:
        out_specs.append(state_spec)
        out_shape.append(jax.ShapeDtypeStruct((n_seq, 2, SSD_STATE, SSD_INNER), F32))
    nc = L // CHUNK
    conv_pad = (L // grid_rows + SUBLANES) if grid_rows > 1 else SUBLANES
    body = functools.partial(_ssd_body, seq_len=L, grid_rows=grid_rows,
                             has_h0=h0 is not None, want_state=want_state)
    return pl.pallas_call(
        body,
        grid=(n_seq, SSD_GROUPS),
        in_specs=in_specs,
        out_specs=out_specs,
        out_shape=out_shape,
        scratch_shapes=[pltpu.VMEM((L + 2 * conv_pad, GROUP_W), F32),
                        pltpu.VMEM((L, GROUP_W), F32),
                        pltpu.VMEM((L, SSD_STATE), BF16),
                        pltpu.VMEM((L, SSD_STATE), BF16),
                        pltpu.VMEM((L, LANES), F32),
                        pltpu.VMEM((L, 2 * LANES), BF16),
                        pltpu.VMEM((L, GROUP_W), BF16),
                        pltpu.VMEM((L, 2 * GROUP_W), F32),
                        pltpu.VMEM((L, 2 * GROUP_W), BF16),
                        pltpu.VMEM((nc, SUBLANES, 2 * GROUP_W), F32),
                        pltpu.VMEM((nc, SSD_STATE, 2 * GROUP_W), F32),
                        pltpu.VMEM((nc, SSD_STATE, 2 * GROUP_W), BF16),
                        pltpu.VMEM((SSD_STATE, 2 * GROUP_W), F32)],
        compiler_params=_params(2),
        name="ssd",
    )(*args)


def _layer_path(x, mod, lw, h0_gla, h0_ssd, *, n_seq, seq_len, grid_rows, want_state, final_w):
    x1 = _ffn_call(x, mod, lw["norm_ffn1"], lw["ffn1_w_in"], lw["ffn1_w_out"],
                   sub=0, seq_len=seq_len)
    proj = _proj_call(x1, mod, lw["norm_mix"], lw["w_proj"], seq_len=seq_len)
    gla_out = _gla_call(proj, lw["wa_cat"], lw["ba_cat"], lw["gla_norm_w"], h0_gla,
                        n_seq=n_seq, seq_len=seq_len, want_state=want_state)
    ssd_out = _ssd_call(proj, lw["conv_w"], lw["conv_b"], lw["ssd_par"], lw["d_skip_row"], h0_ssd,
                        n_seq=n_seq, seq_len=seq_len, grid_rows=grid_rows, want_state=want_state)
    o, y = gla_out[0], ssd_out[0]
    out = _ffn_call(x1, mod, lw["norm_ffn2"], lw["ffn2_w_in"], lw["ffn2_w_out"],
                    sub=2, seq_len=seq_len, mix=(o, y, proj, lw["ssd_norm_w"], lw["w_out"]),
                    final_w=final_w)
    if want_state:
        return out, gla_out[1], ssd_out[1]
    return out, None, None


def kernel(x_prompt, x_sample, state_gla, state_ssd, c, c_ctx, norm_ffn1, norm_mix, norm_ffn2, w_mod, b_mod, ffn1_w_in, ffn1_w_out, ffn2_w_in, ffn2_w_out, w_in, gla_w_a2, gla_b_a, gla_norm_w, conv_w, conv_b, dt_bias, a_log, d_skip, ssd_norm_w, w_out, final_norm):
    nb, seq, _ = x_prompt.shape
    db, dseq, _ = x_sample.shape
    grid_rows = dseq // GRID_W
    xp = x_prompt.reshape(nb * seq, D_MODEL)
    xs = x_sample.reshape(db * dseq, D_MODEL)
    row = lambda v: v.reshape(1, -1)
    gla_states, ssd_states = [], []
    for i in range(DEPTH):
        last = i == DEPTH - 1
        n_rows = -(-(db + 1) // SUBLANES) * SUBLANES
        cc = jnp.concatenate([c, c_ctx[None, :], jnp.zeros((n_rows - db - 1, D_MODEL), F32)], 0)
        mod = _mod_call(cc, w_mod[i], row(b_mod[i])).reshape(n_rows, N_MOD, D_MODEL)
        mod_lat, mod_ctx = mod[:db], mod[db:db + 1]

        wi = w_in[i]
        o_af = COL_R + GLA_V
        o_z = o_af + 2 * GLA_LOWRANK
        o_xbc = o_z + SSD_INNER
        o_dt = o_xbc + SSD_CONV_DIM
        w_proj = jnp.concatenate(
            [wi[:, :o_af], wi[:, o_z:o_dt], wi[:, o_af:o_z], wi[:, o_dt:],
             jnp.zeros((D_MODEL, LANES - 2 * GLA_LOWRANK - 2 * SSD_HEADS), F32)], axis=1).astype(BF16)
        wa_cat = jnp.zeros((LANES, GLA_HEADS, 2, GLA_DK), F32)
        wa_cat = wa_cat.at[SM_AF:SM_AF + GLA_LOWRANK, :, 0, :].set(
            gla_w_a2[i, 0].reshape(GLA_LOWRANK, GLA_HEADS, GLA_DK))
        wa_cat = wa_cat.at[SM_AB:SM_AB + GLA_LOWRANK, :, 1, :].set(
            gla_w_a2[i, 1].reshape(GLA_LOWRANK, GLA_HEADS, GLA_DK))
        wa_cat = wa_cat.reshape(LANES, 2 * GLA_QK)
        ba_cat = jnp.stack([gla_b_a[i, 0].reshape(GLA_HEADS, GLA_DK),
                            gla_b_a[i, 1].reshape(GLA_HEADS, GLA_DK)], axis=1).reshape(1, 2 * GLA_QK)
        ssd_par = jnp.zeros((SUBLANES, LANES), F32)
        ssd_par = ssd_par.at[0, SM_DTF:SM_DTB + SSD_HEADS].set(dt_bias[i].reshape(-1))
        ssd_par = ssd_par.at[1, SM_DTF:SM_DTB + SSD_HEADS].set(a_log[i].reshape(-1))
        ssd_par = ssd_par.at[2, SM_DTF:SM_DTB + SSD_HEADS].set(1.0)
        lw = {
            "norm_ffn1": row(norm_ffn1[i]), "norm_mix": row(norm_mix[i]),
            "norm_ffn2": row(norm_ffn2[i]),
            "ffn1_w_in": ffn1_w_in[i].astype(BF16), "ffn1_w_out": ffn1_w_out[i].astype(BF16),
            "ffn2_w_in": ffn2_w_in[i].astype(BF16), "ffn2_w_out": ffn2_w_out[i].astype(BF16),
            "w_proj": w_proj, "wa_cat": wa_cat, "ba_cat": ba_cat,
            "gla_norm_w": row(gla_norm_w[i]), "conv_w": conv_w[i], "conv_b": row(conv_b[i]),
            "ssd_par": ssd_par, "d_skip_row": row(jnp.repeat(d_skip[i], SSD_HEAD_DIM)),
            "ssd_norm_w": row(ssd_norm_w[i]), "w_out": w_out[i].astype(BF16),
        }
        fw = row(final_norm) if last else None

        h0_ssd = state_ssd[:, i].transpose(0, 1, 4, 2, 3).reshape(db, 2, SSD_STATE, SSD_INNER)
        xp, sg, ss = _layer_path(xp, mod_ctx, lw, None, None, n_seq=nb, seq_len=seq,
                                 grid_rows=1, want_state=True, final_w=fw)
        xs, _, _ = _layer_path(xs, mod_lat, lw, state_gla[:, i], h0_ssd, n_seq=db, seq_len=dseq,
                               grid_rows=grid_rows, want_state=False, final_w=fw)
        gla_states.append(sg)
        ssd_states.append(ss.reshape(nb, 2, SSD_STATE, SSD_HEADS, SSD_HEAD_DIM)
                          .transpose(0, 1, 3, 4, 2))
    y_prompt = xp.reshape(nb, seq, D_MODEL)
    y_sample = xs.reshape(db, dseq, D_MODEL)
    return (y_prompt, y_sample, jnp.stack(gla_states, axis=1), jnp.stack(ssd_states, axis=1))
```

```python
import functools

import jax
import jax.numpy as jnp
from jax import lax
from jax.experimental import pallas as pl
from jax.experimental.pallas import tpu as pltpu

F32 = jnp.float32
BF16 = jnp.bfloat16

D_MODEL = 1024
DEPTH = 1
GRID_W = 64
CHUNK = 64
EPS = 1e-6
N_MOD = 9
D_FF = 2816
GLA_HEADS = 4
GLA_DK = 128
GLA_DV = 256
GLA_LOWRANK = 16
GLA_TAU = 16.0
GLA_QK = GLA_HEADS * GLA_DK
GLA_V = GLA_HEADS * GLA_DV
SSD_HEADS = 16
SSD_HEAD_DIM = 64
SSD_GROUPS = 2
SSD_STATE = 128
SSD_INNER = SSD_HEADS * SSD_HEAD_DIM
SSD_BC = SSD_GROUPS * SSD_STATE
SSD_CONV_DIM = SSD_INNER + 2 * SSD_BC
CONV_K = 3
D_MIX = GLA_V + SSD_INNER

LANES = 128
SUBLANES = 8
VMEM_LIMIT_BYTES = 56 * 1024 * 1024

COL_Q = 0
COL_K = COL_Q + GLA_QK
COL_V = COL_K + GLA_QK
COL_R = COL_V + GLA_V
COL_Z = COL_R + GLA_V
COL_XBC = COL_Z + SSD_INNER
COL_SMALL = COL_XBC + SSD_CONV_DIM
N_PROJ = COL_SMALL + LANES
SM_AF = 0
SM_AB = SM_AF + GLA_LOWRANK
SM_DTF = SM_AB + GLA_LOWRANK
SM_DTB = SM_DTF + SSD_HEADS
HEADS_PER_GROUP = SSD_HEADS // SSD_GROUPS
GROUP_W = HEADS_PER_GROUP * SSD_HEAD_DIM

TOKEN_TILE = 256
GLA_UNROLL = 16
SSD_UNROLL = 16
MXU_TILE = 256
TERM_STRIDE = 2 * SSD_HEADS
SCAN_BLOCK_TOKENS = 1024


def _dot(a, b):
    return jnp.dot(a, b, preferred_element_type=F32)


def _dot_nt(a, b):
    return lax.dot_general(a, b, (((1,), (1,)), ((), ())), preferred_element_type=F32)


def _dot_tn(a, b):
    return lax.dot_general(a, b, (((0,), (0,)), ((), ())), preferred_element_type=F32)


def _silu(x):
    return x * jax.nn.sigmoid(x)


def _log1p_exp_neg_abs(x):
    return jnp.log(1.0 + jnp.exp(-jnp.abs(x)))


def _softplus(x):
    return jnp.maximum(x, 0.0) + _log1p_exp_neg_abs(x)


def _log_sigmoid(x):
    return jnp.minimum(x, 0.0) - _log1p_exp_neg_abs(x)


def _rmsnorm(x, w):
    ms = jnp.mean(x * x, axis=-1, keepdims=True)
    return x * lax.rsqrt(ms + EPS) * w


def _split3(x):
    hi = x.astype(BF16)
    r1 = x - hi.astype(F32)
    mid = r1.astype(BF16)
    lo = (r1 - mid.astype(F32)).astype(BF16)
    return hi, mid, lo


def _exact_dot(sel_bf16, x):
    hi, mid, lo = _split3(x)
    return (_dot(sel_bf16, lo) + _dot(sel_bf16, mid)) + _dot(sel_bf16, hi)


def _resident(shape):
    nd = len(shape)
    return pl.BlockSpec(shape, lambda *_: (0,) * nd, pipeline_mode=pl.Buffered(1))


def _params(n_axes):
    return pltpu.CompilerParams(dimension_semantics=("arbitrary",) * n_axes,
                                vmem_limit_bytes=VMEM_LIMIT_BYTES)


def _mod_body(c_ref, w_ref, b_ref, out_ref):
    a = _silu(c_ref[...]).astype(BF16)
    out_ref[...] = _dot(a, w_ref[...].astype(BF16)) + b_ref[...]


def _mod_call(cc, w_mod, b_mod):
    n_rows = cc.shape[0]
    tn = D_MODEL
    return pl.pallas_call(
        _mod_body,
        grid=(N_MOD * D_MODEL // tn,),
        in_specs=[pl.BlockSpec((n_rows, D_MODEL), lambda j: (0, 0)),
                  pl.BlockSpec((D_MODEL, tn), lambda j: (0, j)),
                  pl.BlockSpec((1, tn), lambda j: (0, j))],
        out_specs=pl.BlockSpec((n_rows, tn), lambda j: (0, j)),
        out_shape=jax.ShapeDtypeStruct((n_rows, N_MOD * D_MODEL), F32),
        compiler_params=_params(1),
        name="mod",
    )(cc, w_mod, b_mod)


def _ffn_body(*refs, sub, has_mix, has_final):
    it = iter(refs)
    x_ref, mod_ref, nw_ref, win_ref, wout_ref = (next(it) for _ in range(5))
    if has_mix:
        o_ref, y_ref, z_ref, snw_ref, wo_ref = (next(it) for _ in range(5))
    if has_final:
        fn_ref = next(it)
    out_ref = next(it)

    x = x_ref[...]
    if has_mix:
        g2 = mod_ref[0, 5:6, :]
        yn = _rmsnorm(y_ref[...] * _silu(z_ref[...]), snw_ref[...])
        m = _dot(o_ref[...], wo_ref[:GLA_V, :]) + _dot(yn.astype(BF16), wo_ref[GLA_V:, :])
        x = x + g2 * m
    sh = mod_ref[0, 3 * sub:3 * sub + 1, :]
    sc = mod_ref[0, 3 * sub + 1:3 * sub + 2, :]
    gate = mod_ref[0, 3 * sub + 2:3 * sub + 3, :]
    h = (_rmsnorm(x, nw_ref[...]) * (1.0 + sc) + sh).astype(BF16)
    g = _dot(h, win_ref[:, :D_FF])
    u = _dot(h, win_ref[:, D_FF:])
    act = (_silu(g) * u).astype(BF16)
    x = x + (0.5 * gate) * _dot(act, wout_ref[...])
    if has_final:
        x = _rmsnorm(x, fn_ref[...])
    out_ref[...] = x


def _ffn_call(x, mod, norm_w, w_in, w_out, *, sub, seq_len, mix=None, final_w=None):
    m_tok = x.shape[0]
    tm = TOKEN_TILE
    tiles_per_seq = seq_len // tm
    shared_mod = mod.shape[0] == 1
    mod_map = (lambda i: (0, 0, 0)) if shared_mod else (lambda i: (i // tiles_per_seq, 0, 0))
    tok = lambda width: pl.BlockSpec((tm, width), lambda i: (i, 0))
    in_specs = [tok(D_MODEL), pl.BlockSpec((1, N_MOD, D_MODEL), mod_map),
                _resident((1, D_MODEL)), _resident(w_in.shape), _resident(w_out.shape)]
    args = [x, mod, norm_w, w_in, w_out]
    if mix is not None:
        o, y, proj, ssd_norm_w, w_mix_out = mix
        in_specs += [tok(GLA_V), tok(SSD_INNER),
                     pl.BlockSpec((tm, SSD_INNER), lambda i: (i, COL_Z // SSD_INNER)),
                     _resident((1, SSD_INNER)), _resident(w_mix_out.shape)]
        args += [o, y, proj, ssd_norm_w, w_mix_out]
    if final_w is not None:
        in_specs.append(_resident((1, D_MODEL)))
        args.append(final_w)
    body = functools.partial(_ffn_body, sub=sub, has_mix=mix is not None,
                             has_final=final_w is not None)
    return pl.pallas_call(
        body,
        grid=(m_tok // tm,),
        in_specs=in_specs,
        out_specs=tok(D_MODEL),
        out_shape=jax.ShapeDtypeStruct((m_tok, D_MODEL), F32),
        compiler_params=_params(1),
        name="ffn_mix" if mix is not None else "ffn",
    )(*args)


def _proj_body(x_ref, mod_ref, nw_ref, w_ref, out_ref):
    sh = mod_ref[0, 3:4, :]
    sc = mod_ref[0, 4:5, :]
    h = (_rmsnorm(x_ref[...], nw_ref[...]) * (1.0 + sc) + sh).astype(BF16)
    out_ref[...] = _dot(h, w_ref[...])


def _proj_call(x, mod, norm_w, w_proj, *, seq_len):
    m_tok = x.shape[0]
    tm = TOKEN_TILE
    tiles_per_seq = seq_len // tm
    shared_mod = mod.shape[0] == 1
    mod_map = (lambda i: (0, 0, 0)) if shared_mod else (lambda i: (i // tiles_per_seq, 0, 0))
    return pl.pallas_call(
        _proj_body,
        grid=(m_tok // tm,),
        in_specs=[pl.BlockSpec((tm, D_MODEL), lambda i: (i, 0)),
                  pl.BlockSpec((1, N_MOD, D_MODEL), mod_map),
                  _resident((1, D_MODEL)), _resident(w_proj.shape)],
        out_specs=pl.BlockSpec((tm, N_PROJ), lambda i: (i, 0)),
        out_shape=jax.ShapeDtypeStruct((m_tok, N_PROJ), F32),
        compiler_params=_params(1),
        name="proj",
    )(x, mod, norm_w, w_proj)


def _tri_masks():
    row = lax.broadcasted_iota(jnp.int32, (CHUNK, CHUNK), 0)
    col = lax.broadcasted_iota(jnp.int32, (CHUNK, CHUNK), 1)
    return row >= col, row <= col


def _gla_body(*refs, seq_len, n_sub, has_h0, want_state):
    it = iter(refs)
    q_ref, k_ref, v_ref, r_ref, sm_ref, wa_ref, ba_ref, nw_ref = (next(it) for _ in range(8))
    h0_ref = next(it) if has_h0 else None
    o_ref = next(it)
    st_ref = next(it) if want_state else None
    (la_scr, qk_scr, qg_scr, ks_scr, vb_scr, sc_scr, kv_scr, dec_scr, sprev_scr,
     state_scr) = (next(it) for _ in range(10))

    nc = seq_len // CHUNK
    nct = n_sub * nc
    dk = GLA_DK
    lower, upper = _tri_masks()
    lower_b = jnp.where(lower, 1.0, 0.0).astype(BF16)
    upper_b = jnp.where(upper, 1.0, 0.0).astype(BF16)
    unroll = min(GLA_UNROLL, nct)
    rows = lambda c: pl.ds(pl.multiple_of(c * CHUNK, CHUNK), CHUNK)

    pre = _dot(sm_ref[...].astype(BF16), wa_ref[...].astype(BF16)) + ba_ref[...]
    la_scr[...] = _log_sigmoid(pre) * (1.0 / GLA_TAU)

    def prep(c, carry):
        sl = rows(c)
        qc = q_ref[sl, :] * (GLA_DK ** -0.5)
        kc = k_ref[sl, :]
        vb_scr[sl, :] = v_ref[sl, :].astype(BF16)
        la = la_scr[sl, :]
        hi = la.astype(BF16)
        lo = (la - hi.astype(F32)).astype(BF16)

        def cumulative(tri_b, a):
            p = _dot(tri_b, jnp.concatenate([hi[:, a:a + dk], lo[:, a:a + dk]], axis=1))
            return p[:, 0:dk] + p[:, dk:2 * dk]

        g_f = cumulative(lower_b, 0)
        g_b = cumulative(upper_b, dk)
        mid_f = g_f[CHUNK // 2:CHUNK // 2 + 1, :]
        mid_b = g_b[CHUNK - 1 - CHUNK // 2:CHUNK - CHUNK // 2, :]
        end_f = g_f[CHUNK - 1:CHUNK, :]
        end_b = g_b[0:1, :]
        qk_scr[sl, 0:dk] = (qc * jnp.exp(g_f - mid_f)).astype(BF16)
        qk_scr[sl, dk:2 * dk] = (qc * jnp.exp(g_b - mid_b)).astype(BF16)
        qk_scr[sl, 2 * dk:3 * dk] = (kc * jnp.exp(mid_f - g_f)).astype(BF16)
        qk_scr[sl, 3 * dk:4 * dk] = (kc * jnp.exp(mid_b - g_b)).astype(BF16)
        qg_scr[sl, 0:dk] = (qc * jnp.exp(g_f)).astype(BF16)
        qg_scr[sl, dk:2 * dk] = (qc * jnp.exp(g_b)).astype(BF16)
        ks_scr[sl, 0:dk] = (kc * jnp.exp(end_f - g_f)).astype(BF16)
        ks_scr[sl, dk:2 * dk] = (kc * jnp.exp(end_b - g_b)).astype(BF16)
        dec = jnp.exp(jnp.concatenate([end_f, end_b], axis=1))
        dec_scr[c] = jnp.broadcast_to(dec, (SUBLANES, 2 * dk))
        return carry

    lax.fori_loop(0, nct, prep, 0, unroll=unroll)

    def products(c, carry):
        sl = rows(c)
        s_f = _dot_nt(qk_scr[sl, 0:dk], qk_scr[sl, 2 * dk:3 * dk])
        s_b = _dot_nt(qk_scr[sl, dk:2 * dk], qk_scr[sl, 3 * dk:4 * dk])
        sc_scr[sl, :] = (jnp.where(lower, s_f, 0.0) + jnp.where(upper, s_b, 0.0)).astype(BF16)
        kv_scr[c] = _dot_tn(vb_scr[sl, :], ks_scr[sl, :])
        return carry

    lax.fori_loop(0, nct, products, 0, unroll=unroll)

    for b in range(n_sub):
        if has_h0:
            state_scr[:, 0:dk] = h0_ref[b, 0, 0, 0].T
            state_scr[:, dk:2 * dk] = h0_ref[b, 0, 1, 0].T
        else:
            state_scr[...] = jnp.zeros_like(state_scr)

        def recur(i, carry, b=b):
            for c, a in ((b * nc + i, 0), (b * nc + nc - 1 - i, dk)):
                state = state_scr[:, a:a + dk]
                sprev_scr[c, :, a:a + dk] = state.astype(BF16)
                state_scr[:, a:a + dk] = (state * dec_scr[c, 0:1, a:a + dk]
                                          + kv_scr[c, :, a:a + dk])
            return carry

        lax.fori_loop(0, nc, recur, 0)
        if want_state:
            st_ref[b, 0, 0, 0] = state_scr[:, 0:dk].T
            st_ref[b, 0, 1, 0] = state_scr[:, dk:2 * dk].T

    def finish(c, carry):
        sl = rows(c)
        o = _dot(sc_scr[sl, :], vb_scr[sl, :]) + _dot_nt(qg_scr[sl, :], sprev_scr[c])
        o_ref[sl, :] = (_rmsnorm(o, nw_ref[...]) * _silu(r_ref[sl, :])).astype(o_ref.dtype)
        return carry

    lax.fori_loop(0, nct, finish, 0, unroll=unroll)


def _gla_call(proj, wa_cat, ba_cat, norm_w, h0, layer, *, n_seq, seq_len, want_state):
    n_sub = max(1, SCAN_BLOCK_TOKENS // seq_len)
    L = n_sub * seq_len
    nc = L // CHUNK
    blk = lambda width, base: pl.BlockSpec((L, width), lambda s, h: (s, base // width + h))
    in_specs = [blk(GLA_DK, COL_Q), blk(GLA_DK, COL_K), blk(GLA_DV, COL_V), blk(GLA_DV, COL_R),
                pl.BlockSpec((L, LANES), lambda s, h: (s, COL_SMALL // LANES)),
                pl.BlockSpec((LANES, 2 * GLA_DK), lambda s, h: (0, h)),
                pl.BlockSpec((1, 2 * GLA_DK), lambda s, h: (0, h)),
                pl.BlockSpec((1, GLA_DV), lambda s, h: (0, 0))]
    args = [proj, proj, proj, proj, proj, wa_cat, ba_cat, norm_w]
    state_blk = (n_sub, 1, 2, 1, GLA_DK, GLA_DV)
    if h0 is not None:
        in_specs.append(pl.BlockSpec(state_blk, lambda s, h: (s, layer, 0, h, 0, 0)))
        args.append(h0)
    out_specs = [pl.BlockSpec((L, GLA_DV), lambda s, h: (s, h))]
    out_shape = [jax.ShapeDtypeStruct((n_seq * seq_len, GLA_V), BF16)]
    if want_state:
        out_specs.append(pl.BlockSpec(state_blk, lambda s, h: (s, 0, 0, h, 0, 0)))
        out_shape.append(jax.ShapeDtypeStruct((n_seq, 1, 2, GLA_HEADS, GLA_DK, GLA_DV), F32))
    body = functools.partial(_gla_body, seq_len=seq_len, n_sub=n_sub, has_h0=h0 is not None,
                             want_state=want_state)
    return pl.pallas_call(
        body,
        grid=(n_seq // n_sub, GLA_HEADS),
        in_specs=in_specs,
        out_specs=out_specs,
        out_shape=out_shape,
        scratch_shapes=[pltpu.VMEM((L, 2 * GLA_DK), F32),
                        pltpu.VMEM((L, 4 * GLA_DK), BF16),
                        pltpu.VMEM((L, 2 * GLA_DK), BF16),
                        pltpu.VMEM((L, 2 * GLA_DK), BF16),
                        pltpu.VMEM((L, GLA_DV), BF16),
                        pltpu.VMEM((L, CHUNK), BF16),
                        pltpu.VMEM((nc, GLA_DV, 2 * GLA_DK), F32),
                        pltpu.VMEM((nc, SUBLANES, 2 * GLA_DK), F32),
                        pltpu.VMEM((nc, GLA_DV, 2 * GLA_DK), BF16),
                        pltpu.VMEM((GLA_DV, 2 * GLA_DK), F32)],
        compiler_params=_params(2),
        name="gla",
    )(*args)


def _ssd_body(*refs, seq_len, grid_rows, has_h0, want_state):
    it = iter(refs)
    (xs_ref, b_ref, c_ref, sm_ref, cwx_ref, cwb_ref, cwc_ref, cbx_ref, cbb_ref, cbc_ref,
     par_ref, dsk_ref) = (next(it) for _ in range(12))
    h0_ref = next(it) if has_h0 else None
    y_ref = next(it)
    st_ref = next(it) if want_state else None
    (pad_scr, xs_s, b_s, c_s, dt_scr, xy_scr, sc_scr, ed_scr, xw_scr, dec_scr, cs_scr, sprev_scr,
     state_scr) = (next(it) for _ in range(13))

    L = seq_len
    nc = L // CHUNK
    width = L // grid_rows
    pad = pad_scr.shape[0] - L
    pad //= 2
    grp = pl.program_id(1)

    def conv_into(src_ref, cw_ref, cb_ref, dst_ref):
        ch = src_ref.shape[1]
        pad_scr[0:pad, 0:ch] = jnp.zeros((pad, ch), F32)
        pad_scr[pad + L:pad + L + pad, 0:ch] = jnp.zeros((pad, ch), F32)
        pad_scr[pad:pad + L, 0:ch] = src_ref[...]
        rc = min(2 * CHUNK, L)
        col = lax.broadcasted_iota(jnp.int32, (rc, ch), 0) % width
        di_taps = range(CONV_K) if grid_rows > 1 else (CONV_K // 2,)
        for r0 in range(0, L, rc):
            acc = jnp.broadcast_to(cb_ref[...], (rc, ch))
            for dj in range(CONV_K):
                inner = None
                for di in di_taps:
                    off = pad + r0 + (di - 1) * width + (dj - 1)
                    term = cw_ref[di, dj:dj + 1, :] * pad_scr[off:off + rc, 0:ch]
                    inner = term if inner is None else inner + term
                if grid_rows > 1 and dj == 0:
                    inner = jnp.where(col >= 1, inner, 0.0)
                if grid_rows > 1 and dj == CONV_K - 1:
                    inner = jnp.where(col <= width - 2, inner, 0.0)
                acc = acc + inner
            dst_ref[r0:r0 + rc, :] = _silu(acc).astype(dst_ref.dtype)

    conv_into(xs_ref, cwx_ref, cbx_ref, xs_s)
    conv_into(b_ref, cwb_ref, cbb_ref, b_s)
    conv_into(c_ref, cwc_ref, cbc_ref, c_s)

    gw = GROUP_W
    unroll = min(SSD_UNROLL, nc)
    rows = lambda c: pl.ds(pl.multiple_of(c * CHUNK, CHUNK), CHUNK)
    bias_row = par_ref[0:1, :]
    a_row = -jnp.exp(par_ref[1:2, :]) * par_ref[2:3, :]
    lower, upper = _tri_masks()
    lower_b = jnp.where(lower, 1.0, 0.0).astype(BF16)
    upper_b = jnp.where(upper, 1.0, 0.0).astype(BF16)
    lane = lax.broadcasted_iota(jnp.int32, (CHUNK, LANES), 1)
    dt_lanes = (lane >= SM_DTF) & (lane < SM_DTF + TERM_STRIDE)
    src = lax.broadcasted_iota(jnp.int32, (LANES, 2 * gw), 0)
    dst = lax.broadcasted_iota(jnp.int32, (LANES, 2 * gw), 1)
    dst_slot = (grp * HEADS_PER_GROUP + (lax.shift_right_logical(dst, 6) & (HEADS_PER_GROUP - 1))
                + jnp.where(dst >= gw, SSD_HEADS, 0))
    expand = jnp.where((src >= SM_DTF) & ((src & (TERM_STRIDE - 1)) == dst_slot),
                       1.0, 0.0).astype(BF16)
    t_idx = lax.broadcasted_iota(jnp.int32, (CHUNK, gw), 0)
    s_idx = lax.broadcasted_iota(jnp.int32, (CHUNK, gw), 1) & (CHUNK - 1)
    diag = t_idx == s_idx
    blk_r = lax.shift_right_logical(lax.broadcasted_iota(jnp.int32, (MXU_TILE, MXU_TILE), 0), 6)
    blk_c = lax.shift_right_logical(lax.broadcasted_iota(jnp.int32, (MXU_TILE, MXU_TILE), 1), 6)
    same_head = blk_r == blk_c

    dt_scr[...] = _softplus(sm_ref[...] + bias_row)

    def place3(v):
        hi = v.astype(BF16).astype(F32)
        rest = v - hi
        mid = rest.astype(BF16).astype(F32)
        lo = rest - mid
        keep = lambda t: jnp.where(dt_lanes, t, 0.0)
        out = (keep(hi) + pltpu.roll(keep(mid), TERM_STRIDE, axis=1)
               + pltpu.roll(keep(lo), 2 * TERM_STRIDE, axis=1))
        return out.astype(BF16)

    def cumulate(c, carry):
        sl = rows(c)
        dt = dt_scr[sl, :]
        da = dt * a_row
        hi = da.astype(BF16)
        both = jnp.concatenate([hi, (da - hi.astype(F32)).astype(BF16)], axis=1)
        pf = _dot(lower_b, both)
        pb = _dot(upper_b, both)
        cum = jnp.where(lane < SM_DTB, pf[:, 0:LANES] + pf[:, LANES:], pb[:, 0:LANES] + pb[:, LANES:])
        xy_scr[sl, 0:LANES] = place3(cum)
        xy_scr[sl, LANES:2 * LANES] = place3(dt)
        return carry

    lax.fori_loop(0, nc, cumulate, 0, unroll=unroll)

    def weights(c, carry):
        sl = rows(c)
        xc = xs_s[sl, :]
        cum_e = _dot(xy_scr[sl, 0:LANES], expand)
        dt_e = _dot(xy_scr[sl, LANES:2 * LANES], expand)
        cb = _dot_nt(c_s[sl, :], jnp.concatenate([b_s[sl, :]] * HEADS_PER_GROUP, axis=0))
        w = None
        dec_rows = []
        for a, causal, last_i in ((0, t_idx >= s_idx, CHUNK - 1), (gw, t_idx <= s_idx, 0)):
            ce = cum_e[:, a:a + gw]
            de = dt_e[:, a:a + gw]
            cum_row = jnp.sum(jnp.where(diag, ce, 0.0), axis=0, keepdims=True)
            dt_row = jnp.sum(jnp.where(diag, de, 0.0), axis=0, keepdims=True)
            wd = jnp.exp(jnp.where(causal, ce - cum_row, -jnp.inf)) * dt_row
            w = wd if w is None else w + wd
            cum_last = ce[last_i:last_i + 1, :]
            ed_scr[sl, a:a + gw] = jnp.exp(ce)
            xw_scr[sl, a:a + gw] = (xc * (jnp.exp(cum_last - ce) * de)).astype(BF16)
            dec_rows.append(jnp.exp(cum_last))
        sc_scr[sl, :] = (cb * w).astype(BF16)
        dec_scr[c] = jnp.broadcast_to(jnp.concatenate(dec_rows, axis=1), (SUBLANES, 2 * gw))
        return carry

    lax.fori_loop(0, nc, weights, 0, unroll=unroll)

    def products(c, carry):
        sl = rows(c)
        xc = xs_s[sl, :]
        xb = xc.astype(BF16)
        parts = []
        for j in range(gw // MXU_TILE):
            xh = xb[:, j * MXU_TILE:(j + 1) * MXU_TILE]
            rep = jnp.concatenate([xh] * (MXU_TILE // SSD_HEAD_DIM), axis=0)
            bd = jnp.where(same_head, rep, jnp.zeros_like(rep))
            parts.append(_dot(sc_scr[sl, j * MXU_TILE:(j + 1) * MXU_TILE], bd))
        y_ref[sl, :] = jnp.concatenate(parts, axis=1) + xc * dsk_ref[...]
        cs_scr[c] = _dot_tn(b_s[sl, :], xw_scr[sl, :])
        return carry

    lax.fori_loop(0, nc, products, 0, unroll=unroll)

    pair_w = 2 * SSD_HEAD_DIM
    if has_h0:
        for d in range(2):
            for j in range(HEADS_PER_GROUP // 2):
                pair = jnp.concatenate([h0_ref[0, 0, d, 2 * j], h0_ref[0, 0, d, 2 * j + 1]], axis=0)
                state_scr[:, d * gw + j * pair_w:d * gw + (j + 1) * pair_w] = pair.T
    else:
        state_scr[...] = jnp.zeros_like(state_scr)

    def recur(i, carry):
        for c, a in ((i, 0), (nc - 1 - i, gw)):
            state = state_scr[:, a:a + gw]
            sprev_scr[c, :, a:a + gw] = state.astype(BF16)
            state_scr[:, a:a + gw] = state * dec_scr[c, 0:1, a:a + gw] + cs_scr[c, :, a:a + gw]
        return carry

    lax.fori_loop(0, nc, recur, 0)

    def finish(c, carry):
        sl = rows(c)
        yi = _dot(c_s[sl, :], sprev_scr[c]) * ed_scr[sl, :]
        y_ref[sl, :] += yi[:, 0:gw] + yi[:, gw:2 * gw]
        return carry

    lax.fori_loop(0, nc, finish, 0, unroll=unroll)

    if want_state:
        for d in range(2):
            for j in range(HEADS_PER_GROUP // 2):
                pair = state_scr[:, d * gw + j * pair_w:d * gw + (j + 1) * pair_w].T
                st_ref[0, 0, d, 2 * j] = pair[0:SSD_HEAD_DIM, :]
                st_ref[0, 0, d, 2 * j + 1] = pair[SSD_HEAD_DIM:pair_w, :]


def _ssd_call(proj, conv_w, conv_b, par, dsk, h0, layer, *, n_seq, seq_len, grid_rows,
              want_state):
    L = seq_len
    x_base = COL_XBC
    b_base = COL_XBC + SSD_INNER
    c_base = b_base + SSD_BC
    in_specs = [pl.BlockSpec((L, GROUP_W), lambda s, g: (s, x_base // GROUP_W + g)),
                pl.BlockSpec((L, SSD_STATE), lambda s, g: (s, b_base // SSD_STATE + g)),
                pl.BlockSpec((L, SSD_STATE), lambda s, g: (s, c_base // SSD_STATE + g)),
                pl.BlockSpec((L, LANES), lambda s, g: (s, COL_SMALL // LANES)),
                pl.BlockSpec((CONV_K, CONV_K, GROUP_W), lambda s, g: (0, 0, g)),
                pl.BlockSpec((CONV_K, CONV_K, SSD_STATE),
                             lambda s, g: (0, 0, SSD_INNER // SSD_STATE + g)),
                pl.BlockSpec((CONV_K, CONV_K, SSD_STATE),
                             lambda s, g: (0, 0, (SSD_INNER + SSD_BC) // SSD_STATE + g)),
                pl.BlockSpec((1, GROUP_W), lambda s, g: (0, g)),
                pl.BlockSpec((1, SSD_STATE), lambda s, g: (0, SSD_INNER // SSD_STATE + g)),
                pl.BlockSpec((1, SSD_STATE),
                             lambda s, g: (0, (SSD_INNER + SSD_BC) // SSD_STATE + g)),
                pl.BlockSpec((SUBLANES, LANES), lambda s, g: (0, 0)),
                pl.BlockSpec((1, GROUP_W), lambda s, g: (0, g))]
    args = [proj, proj, proj, proj, conv_w, conv_w, conv_w, conv_b, conv_b, conv_b, par, dsk]
    state_blk = (1, 1, 2, HEADS_PER_GROUP, SSD_HEAD_DIM, SSD_STATE)
    if h0 is not None:
        in_specs.append(pl.BlockSpec(state_blk, lambda s, g: (s, layer, 0, g, 0, 0)))
        args.append(h0)
    out_specs = [pl.BlockSpec((L, GROUP_W), lambda s, g: (s, g))]
    out_shape = [jax.ShapeDtypeStruct((n_seq * L, SSD_INNER), F32)]
    if want_state:
        out_specs.append(pl.BlockSpec(state_blk, lambda s, g: (s, 0, 0, g, 0, 0)))
        out_shape.append(jax.ShapeDtypeStruct(
            (n_seq, 1, 2, SSD_HEADS, SSD_HEAD_DIM, SSD_STATE), F32))
    nc = L // CHUNK
    conv_pad = (L // grid_rows + SUBLANES) if grid_rows > 1 else SUBLANES
    body = functools.partial(_ssd_body, seq_len=L, grid_rows=grid_rows,
                             has_h0=h0 is not None, want_state=want_state)
    return pl.pallas_call(
        body,
        grid=(n_seq, SSD_GROUPS),
        in_specs=in_specs,
        out_specs=out_specs,
        out_shape=out_shape,
        scratch_shapes=[pltpu.VMEM((L + 2 * conv_pad, GROUP_W), F32),
                        pltpu.VMEM((L, GROUP_W), F32),
                        pltpu.VMEM((L, SSD_STATE), BF16),
                        pltpu.VMEM((L, SSD_STATE), BF16),
                        pltpu.VMEM((L, LANES), F32),
                        pltpu.VMEM((L, 2 * LANES), BF16),
                        pltpu.VMEM((L, GROUP_W), BF16),
                        pltpu.VMEM((L, 2 * GROUP_W), F32),
                        pltpu.VMEM((L, 2 * GROUP_W), BF16),
                        pltpu.VMEM((nc, SUBLANES, 2 * GROUP_W), F32),
                        pltpu.VMEM((nc, SSD_STATE, 2 * GROUP_W), F32),
                        pltpu.VMEM((nc, SSD_STATE, 2 * GROUP_W), BF16),
                        pltpu.VMEM((SSD_STATE, 2 * GROUP_W), F32)],
        compiler_params=_params(2),
        name="ssd",
    )(*args)


def _layer_path(x, mod, lw, h0_gla, h0_ssd, layer, *, n_seq, seq_len, grid_rows, want_state,
                final_w):
    x1 = _ffn_call(x, mod, lw["norm_ffn1"], lw["ffn1_w_in"], lw["ffn1_w_out"],
                   sub=0, seq_len=seq_len)
    proj = _proj_call(x1, mod, lw["norm_mix"], lw["w_proj"], seq_len=seq_len)
    gla_out = _gla_call(proj, lw["wa_cat"], lw["ba_cat"], lw["gla_norm_w"], h0_gla, layer,
                        n_seq=n_seq, seq_len=seq_len, want_state=want_state)
    ssd_out = _ssd_call(proj, lw["conv_w"], lw["conv_b"], lw["ssd_par"], lw["d_skip_row"], h0_ssd,
                        layer, n_seq=n_seq, seq_len=seq_len, grid_rows=grid_rows,
                        want_state=want_state)
    o, y = gla_out[0], ssd_out[0]
    out = _ffn_call(x1, mod, lw["norm_ffn2"], lw["ffn2_w_in"], lw["ffn2_w_out"],
                    sub=2, seq_len=seq_len, mix=(o, y, proj, lw["ssd_norm_w"], lw["w_out"]),
                    final_w=final_w)
    if want_state:
        return out, gla_out[1], ssd_out[1]
    return out, None, None


def kernel(x_prompt, x_sample, state_gla, state_ssd, c, c_ctx, norm_ffn1, norm_mix, norm_ffn2, w_mod, b_mod, ffn1_w_in, ffn1_w_out, ffn2_w_in, ffn2_w_out, w_in, gla_w_a2, gla_b_a, gla_norm_w, conv_w, conv_b, dt_bias, a_log, d_skip, ssd_norm_w, w_out, final_norm):
    nb, seq, _ = x_prompt.shape
    db, dseq, _ = x_sample.shape
    grid_rows = dseq // GRID_W
    xp = x_prompt.reshape(nb * seq, D_MODEL)
    xs = x_sample.reshape(db * dseq, D_MODEL)
    row = lambda v: v.reshape(1, -1)
    gla_states, ssd_states = [], []
    for i in range(DEPTH):
        last = i == DEPTH - 1
        n_rows = -(-(db + 1) // SUBLANES) * SUBLANES
        cc = jnp.concatenate([c, c_ctx[None, :], jnp.zeros((n_rows - db - 1, D_MODEL), F32)], 0)
        mod = _mod_call(cc, w_mod[i], row(b_mod[i])).reshape(n_rows, N_MOD, D_MODEL)
        mod_lat, mod_ctx = mod[:db], mod[db:db + 1]

        wi = w_in[i]
        o_af = COL_R + GLA_V
        o_z = o_af + 2 * GLA_LOWRANK
        o_xbc = o_z + SSD_INNER
        o_dt = o_xbc + SSD_CONV_DIM
        w_proj = jnp.concatenate(
            [wi[:, :o_af].astype(BF16), wi[:, o_z:o_dt].astype(BF16), wi[:, o_af:o_z].astype(BF16),
             wi[:, o_dt:].astype(BF16),
             jnp.zeros((D_MODEL, LANES - 2 * GLA_LOWRANK - 2 * SSD_HEADS), BF16)], axis=1)
        wa_cat = jnp.stack(
            [jnp.pad(gla_w_a2[i, d].reshape(GLA_LOWRANK, GLA_HEADS, GLA_DK),
                     ((lo, LANES - lo - GLA_LOWRANK), (0, 0), (0, 0)))
             for d, lo in ((0, SM_AF), (1, SM_AB))], axis=2).reshape(LANES, 2 * GLA_QK)
        ba_cat = jnp.stack([gla_b_a[i, 0].reshape(GLA_HEADS, GLA_DK),
                            gla_b_a[i, 1].reshape(GLA_HEADS, GLA_DK)], axis=1).reshape(1, 2 * GLA_QK)
        ssd_par = jnp.pad(
            jnp.stack([dt_bias[i].reshape(-1), a_log[i].reshape(-1), jnp.ones((2 * SSD_HEADS,), F32)]),
            ((0, SUBLANES - 3), (SM_DTF, LANES - SM_DTF - 2 * SSD_HEADS)))
        lw = {
            "norm_ffn1": row(norm_ffn1[i]), "norm_mix": row(norm_mix[i]),
            "norm_ffn2": row(norm_ffn2[i]),
            "ffn1_w_in": ffn1_w_in[i].astype(BF16), "ffn1_w_out": ffn1_w_out[i].astype(BF16),
            "ffn2_w_in": ffn2_w_in[i].astype(BF16), "ffn2_w_out": ffn2_w_out[i].astype(BF16),
            "w_proj": w_proj, "wa_cat": wa_cat, "ba_cat": ba_cat,
            "gla_norm_w": row(gla_norm_w[i]), "conv_w": conv_w[i], "conv_b": row(conv_b[i]),
            "ssd_par": ssd_par, "d_skip_row": row(jnp.repeat(d_skip[i], SSD_HEAD_DIM)),
            "ssd_norm_w": row(ssd_norm_w[i]), "w_out": w_out[i].astype(BF16),
        }
        fw = row(final_norm) if last else None

        xp, sg, ss = _layer_path(xp, mod_ctx, lw, None, None, i, n_seq=nb, seq_len=seq,
                                 grid_rows=1, want_state=True, final_w=fw)
        xs, _, _ = _layer_path(xs, mod_lat, lw, state_gla, state_ssd, i, n_seq=db, seq_len=dseq,
                               grid_rows=grid_rows, want_state=False, final_w=fw)
        gla_states.append(sg)
        ssd_states.append(ss)
    y_prompt = xp.reshape(nb, seq, D_MODEL)
    y_sample = xs.reshape(db, dseq, D_MODEL)
    return (y_prompt, y_sample, jnp.concatenate(gla_states, axis=1),
            jnp.concatenate(ssd_states, axis=1))
```

```python
import functools

import jax
import jax.numpy as jnp
from jax import lax
from jax.experimental import pallas as pl
from jax.experimental.pallas import tpu as pltpu

F32 = jnp.float32
BF16 = jnp.bfloat16

D_MODEL = 1024
DEPTH = 1
GRID_W = 64
CHUNK = 64
EPS = 1e-6
N_MOD = 9
D_FF = 2816
GLA_HEADS = 4
GLA_DK = 128
GLA_DV = 256
GLA_LOWRANK = 16
GLA_TAU = 16.0
GLA_QK = GLA_HEADS * GLA_DK
GLA_V = GLA_HEADS * GLA_DV
SSD_HEADS = 16
SSD_HEAD_DIM = 64
SSD_GROUPS = 2
SSD_STATE = 128
SSD_INNER = SSD_HEADS * SSD_HEAD_DIM
SSD_BC = SSD_GROUPS * SSD_STATE
SSD_CONV_DIM = SSD_INNER + 2 * SSD_BC
CONV_K = 3
D_MIX = GLA_V + SSD_INNER

LOG2_E = 1.4426950408889634
LANES = 128
SUBLANES = 8
VMEM_LIMIT_BYTES = 56 * 1024 * 1024

COL_Q = 0
COL_K = COL_Q + GLA_QK
COL_V = COL_K + GLA_QK
COL_R = COL_V + GLA_V
COL_Z = COL_R + GLA_V
COL_XBC = COL_Z + SSD_INNER
COL_SMALL = COL_XBC + SSD_CONV_DIM
N_PROJ = COL_SMALL + LANES
SM_AF = 0
SM_AB = SM_AF + GLA_LOWRANK
SM_DTF = SM_AB + GLA_LOWRANK
SM_DTB = SM_DTF + SSD_HEADS
HEADS_PER_GROUP = SSD_HEADS // SSD_GROUPS
GROUP_W = HEADS_PER_GROUP * SSD_HEAD_DIM

TOKEN_TILE = 512
GLA_UNROLL = 16
SSD_UNROLL = 16
MXU_TILE = 256
TERM_STRIDE = 2 * SSD_HEADS
SCAN_BLOCK_TOKENS = 1024


def _dot(a, b):
    return jnp.dot(a, b, preferred_element_type=F32)


def _dot_nt(a, b):
    return lax.dot_general(a, b, (((1,), (1,)), ((), ())), preferred_element_type=F32)


def _dot_tn(a, b):
    return lax.dot_general(a, b, (((0,), (0,)), ((), ())), preferred_element_type=F32)


def _silu(x):
    return x * jax.nn.sigmoid(x)


def _log1p_exp_neg_abs(x):
    return jnp.log(1.0 + jnp.exp(-jnp.abs(x)))


def _softplus(x):
    return jnp.maximum(x, 0.0) + _log1p_exp_neg_abs(x)


def _log_sigmoid(x):
    return jnp.minimum(x, 0.0) - _log1p_exp_neg_abs(x)


def _rmsnorm(x, w):
    ms = jnp.mean(x * x, axis=-1, keepdims=True)
    return x * lax.rsqrt(ms + EPS) * w


def _resident(shape):
    nd = len(shape)
    return pl.BlockSpec(shape, lambda *_: (0,) * nd, pipeline_mode=pl.Buffered(1))


def _params(n_axes):
    return pltpu.CompilerParams(dimension_semantics=("arbitrary",) * n_axes,
                                vmem_limit_bytes=VMEM_LIMIT_BYTES)


def _mod_body(c_ref, w_ref, b_ref, out_ref):
    a = _silu(c_ref[...]).astype(BF16)
    out_ref[...] = _dot(a, w_ref[...].astype(BF16)) + b_ref[...]


def _mod_call(cc, w_mod, b_mod):
    n_rows = cc.shape[0]
    tn = D_MODEL
    return pl.pallas_call(
        _mod_body,
        grid=(N_MOD * D_MODEL // tn,),
        in_specs=[pl.BlockSpec((n_rows, D_MODEL), lambda j: (0, 0)),
                  pl.BlockSpec((D_MODEL, tn), lambda j: (0, j)),
                  pl.BlockSpec((1, tn), lambda j: (0, j))],
        out_specs=pl.BlockSpec((n_rows, tn), lambda j: (0, j)),
        out_shape=jax.ShapeDtypeStruct((n_rows, N_MOD * D_MODEL), F32),
        compiler_params=_params(1),
        name="mod",
    )(cc, w_mod, b_mod)


def _ffn_body(*refs, sub, has_mix, has_final):
    it = iter(refs)
    x_ref, mod_ref, nw_ref, win_ref, wout_ref = (next(it) for _ in range(5))
    if has_mix:
        o_ref, y_ref, z_ref, snw_ref, wo_ref = (next(it) for _ in range(5))
    if has_final:
        fn_ref = next(it)
    out_ref = next(it)

    x = x_ref[...]
    if has_mix:
        g2 = mod_ref[0, 5:6, :]
        yn = _rmsnorm(y_ref[...] * _silu(z_ref[...]), snw_ref[...])
        m = _dot(o_ref[...], wo_ref[:GLA_V, :]) + _dot(yn.astype(BF16), wo_ref[GLA_V:, :])
        x = x + g2 * m
    sh = mod_ref[0, 3 * sub:3 * sub + 1, :]
    sc = mod_ref[0, 3 * sub + 1:3 * sub + 2, :]
    gate = mod_ref[0, 3 * sub + 2:3 * sub + 3, :]
    h = (_rmsnorm(x, nw_ref[...]) * (1.0 + sc) + sh).astype(BF16)
    g = _dot(h, win_ref[:, :D_FF])
    u = _dot(h, win_ref[:, D_FF:])
    act = (_silu(g) * u).astype(BF16)
    x = x + (0.5 * gate) * _dot(act, wout_ref[...])
    if has_final:
        x = _rmsnorm(x, fn_ref[...])
    out_ref[...] = x


def _ffn_call(x, mod, norm_w, w_in, w_out, *, sub, seq_len, mix=None, final_w=None):
    m_tok = x.shape[0]
    tm = TOKEN_TILE
    tiles_per_seq = seq_len // tm
    shared_mod = mod.shape[0] == 1
    mod_map = (lambda i: (0, 0, 0)) if shared_mod else (lambda i: (i // tiles_per_seq, 0, 0))
    tok = lambda width: pl.BlockSpec((tm, width), lambda i: (i, 0))
    in_specs = [tok(D_MODEL), pl.BlockSpec((1, N_MOD, D_MODEL), mod_map),
                _resident((1, D_MODEL)), _resident(w_in.shape), _resident(w_out.shape)]
    args = [x, mod, norm_w, w_in, w_out]
    if mix is not None:
        o, y, proj, ssd_norm_w, w_mix_out = mix
        in_specs += [tok(GLA_V), tok(SSD_INNER),
                     pl.BlockSpec((tm, SSD_INNER), lambda i: (i, COL_Z // SSD_INNER)),
                     _resident((1, SSD_INNER)), _resident(w_mix_out.shape)]
        args += [o, y, proj, ssd_norm_w, w_mix_out]
    if final_w is not None:
        in_specs.append(_resident((1, D_MODEL)))
        args.append(final_w)
    body = functools.partial(_ffn_body, sub=sub, has_mix=mix is not None,
                             has_final=final_w is not None)
    return pl.pallas_call(
        body,
        grid=(m_tok // tm,),
        in_specs=in_specs,
        out_specs=tok(D_MODEL),
        out_shape=jax.ShapeDtypeStruct((m_tok, D_MODEL), F32),
        compiler_params=_params(1),
        name="ffn_mix" if mix is not None else "ffn",
    )(*args)


def _proj_body(x_ref, mod_ref, nw_ref, wa_ref, wb_ref, wc_ref, out_ref):
    sh = mod_ref[0, 3:4, :]
    sc = mod_ref[0, 4:5, :]
    h = (_rmsnorm(x_ref[...], nw_ref[...]) * (1.0 + sc) + sh).astype(BF16)
    out_ref[:, COL_Q:COL_Z] = _dot(h, wa_ref[...])
    out_ref[:, COL_Z:COL_SMALL] = _dot(h, wb_ref[...])
    out_ref[:, COL_SMALL:N_PROJ] = _dot(h, wc_ref[...])


def _proj_call(x, mod, norm_w, w_all, w_zx, w_small, *, seq_len):
    m_tok = x.shape[0]
    tm = TOKEN_TILE
    tiles_per_seq = seq_len // tm
    shared_mod = mod.shape[0] == 1
    mod_map = (lambda i: (0, 0, 0)) if shared_mod else (lambda i: (i // tiles_per_seq, 0, 0))
    return pl.pallas_call(
        _proj_body,
        grid=(m_tok // tm,),
        in_specs=[pl.BlockSpec((tm, D_MODEL), lambda i: (i, 0)),
                  pl.BlockSpec((1, N_MOD, D_MODEL), mod_map),
                  _resident((1, D_MODEL)), _resident((D_MODEL, COL_Z)),
                  _resident(w_zx.shape), _resident(w_small.shape)],
        out_specs=pl.BlockSpec((tm, N_PROJ), lambda i: (i, 0)),
        out_shape=jax.ShapeDtypeStruct((m_tok, N_PROJ), F32),
        compiler_params=_params(1),
        name="proj",
    )(x, mod, norm_w, w_all, w_zx, w_small)


def _tri_masks():
    row = lax.broadcasted_iota(jnp.int32, (CHUNK, CHUNK), 0)
    col = lax.broadcasted_iota(jnp.int32, (CHUNK, CHUNK), 1)
    return row >= col, row <= col


def _gla_body(*refs, seq_len, n_sub, has_h0, want_state):
    it = iter(refs)
    q_ref, k_ref, v_ref, r_ref, sm_ref, wa_ref, ba_ref, nw_ref = (next(it) for _ in range(8))
    h0_ref = next(it) if has_h0 else None
    o_ref = next(it)
    st_ref = next(it) if want_state else None
    (la_scr, qk_scr, qg_scr, ks_scr, vb_scr, sc_scr, kv_scr, dec_scr,
     sprev_scr) = (next(it) for _ in range(9))

    nc = seq_len // CHUNK
    nct = n_sub * nc
    dk = GLA_DK
    lower, upper = _tri_masks()
    lower_b = jnp.where(lower, 1.0, 0.0).astype(BF16)
    upper_b = jnp.where(upper, 1.0, 0.0).astype(BF16)
    unroll = min(GLA_UNROLL, nct)
    rows = lambda c: pl.ds(pl.multiple_of(c * CHUNK, CHUNK), CHUNK)

    pre = _dot(sm_ref[...].astype(BF16), wa_ref[...].astype(BF16)) + ba_ref[...]
    la_scr[...] = _log_sigmoid(pre) * (LOG2_E / GLA_TAU)

    def prep(c, carry):
        sl = rows(c)
        qc = q_ref[sl, :] * (GLA_DK ** -0.5)
        kc = k_ref[sl, :]
        vb_scr[sl, :] = v_ref[sl, :].astype(BF16)
        la = la_scr[sl, :]
        hi = la.astype(BF16)
        lo = (la - hi.astype(F32)).astype(BF16)

        def cumulative(tri_b, a):
            p = _dot(tri_b, jnp.concatenate([hi[:, a:a + dk], lo[:, a:a + dk]], axis=1))
            return p[:, 0:dk] + p[:, dk:2 * dk]

        g_f = cumulative(lower_b, 0)
        g_b = cumulative(upper_b, dk)
        mid_f = g_f[CHUNK // 2:CHUNK // 2 + 1, :]
        mid_b = g_b[CHUNK - 1 - CHUNK // 2:CHUNK - CHUNK // 2, :]
        end_f = g_f[CHUNK - 1:CHUNK, :]
        end_b = g_b[0:1, :]
        qk_scr[sl, 0:dk] = (qc * jnp.exp2(g_f - mid_f)).astype(BF16)
        qk_scr[sl, dk:2 * dk] = (qc * jnp.exp2(g_b - mid_b)).astype(BF16)
        qk_scr[sl, 2 * dk:3 * dk] = (kc * jnp.exp2(mid_f - g_f)).astype(BF16)
        qk_scr[sl, 3 * dk:4 * dk] = (kc * jnp.exp2(mid_b - g_b)).astype(BF16)
        qg_scr[sl, 0:dk] = (qc * jnp.exp2(g_f)).astype(BF16)
        qg_scr[sl, dk:2 * dk] = (qc * jnp.exp2(g_b)).astype(BF16)
        ks_scr[sl, 0:dk] = (kc * jnp.exp2(end_f - g_f)).astype(BF16)
        ks_scr[sl, dk:2 * dk] = (kc * jnp.exp2(end_b - g_b)).astype(BF16)
        dec = jnp.exp2(jnp.concatenate([end_f, end_b], axis=1))
        dec_scr[c] = jnp.broadcast_to(dec, (SUBLANES, 2 * dk))
        return carry

    lax.fori_loop(0, nct, prep, 0, unroll=unroll)

    def products(c, carry):
        sl = rows(c)
        s_f = _dot_nt(qk_scr[sl, 0:dk], qk_scr[sl, 2 * dk:3 * dk])
        s_b = _dot_nt(qk_scr[sl, dk:2 * dk], qk_scr[sl, 3 * dk:4 * dk])
        sc_scr[sl, :] = (jnp.where(lower, s_f, 0.0) + jnp.where(upper, s_b, 0.0)).astype(BF16)
        kv_scr[c] = _dot_tn(vb_scr[sl, :], ks_scr[sl, :])
        return carry

    lax.fori_loop(0, nct, products, 0, unroll=unroll)

    for b in range(n_sub):
        for d, a in ((0, 0), (1, dk)):
            state0 = h0_ref[b, 0, d, 0].T if has_h0 else jnp.zeros((GLA_DV, dk), F32)

            def recur(i, state, b=b, d=d, a=a):
                c = b * nc + (i if d == 0 else nc - 1 - i)
                sprev_scr[c, :, a:a + dk] = state.astype(BF16)
                return state * dec_scr[c, 0:1, a:a + dk] + kv_scr[c, :, a:a + dk]

            state = lax.fori_loop(0, nc, recur, state0)
            if want_state:
                st_ref[b, 0, d, 0] = state.T

    def finish(c, carry):
        sl = rows(c)
        o = _dot(sc_scr[sl, :], vb_scr[sl, :]) + _dot_nt(qg_scr[sl, :], sprev_scr[c])
        o_ref[sl, :] = (_rmsnorm(o, nw_ref[...]) * _silu(r_ref[sl, :])).astype(o_ref.dtype)
        return carry

    lax.fori_loop(0, nct, finish, 0, unroll=unroll)


def _gla_call(proj, wa_cat, ba_cat, norm_w, h0, layer, *, n_seq, seq_len, want_state):
    n_sub = max(1, SCAN_BLOCK_TOKENS // seq_len)
    L = n_sub * seq_len
    nc = L // CHUNK
    blk = lambda width, base: pl.BlockSpec((L, width), lambda s, h: (s, base // width + h))
    in_specs = [blk(GLA_DK, COL_Q), blk(GLA_DK, COL_K), blk(GLA_DV, COL_V), blk(GLA_DV, COL_R),
                pl.BlockSpec((L, LANES), lambda s, h: (s, COL_SMALL // LANES)),
                pl.BlockSpec((LANES, 2 * GLA_DK), lambda s, h: (0, h)),
                pl.BlockSpec((1, 2 * GLA_DK), lambda s, h: (0, h)),
                pl.BlockSpec((1, GLA_DV), lambda s, h: (0, 0))]
    args = [proj, proj, proj, proj, proj, wa_cat, ba_cat, norm_w]
    state_blk = (n_sub, 1, 2, 1, GLA_DK, GLA_DV)
    if h0 is not None:
        in_specs.append(pl.BlockSpec(state_blk, lambda s, h: (s, layer, 0, h, 0, 0)))
        args.append(h0)
    out_specs = [pl.BlockSpec((L, GLA_DV), lambda s, h: (s, h))]
    out_shape = [jax.ShapeDtypeStruct((n_seq * seq_len, GLA_V), BF16)]
    if want_state:
        out_specs.append(pl.BlockSpec(state_blk, lambda s, h: (s, 0, 0, h, 0, 0)))
        out_shape.append(jax.ShapeDtypeStruct((n_seq, 1, 2, GLA_HEADS, GLA_DK, GLA_DV), F32))
    body = functools.partial(_gla_body, seq_len=seq_len, n_sub=n_sub, has_h0=h0 is not None,
                             want_state=want_state)
    return pl.pallas_call(
        body,
        grid=(n_seq // n_sub, GLA_HEADS),
        in_specs=in_specs,
        out_specs=out_specs,
        out_shape=out_shape,
        scratch_shapes=[pltpu.VMEM((L, 2 * GLA_DK), F32),
                        pltpu.VMEM((L, 4 * GLA_DK), BF16),
                        pltpu.VMEM((L, 2 * GLA_DK), BF16),
                        pltpu.VMEM((L, 2 * GLA_DK), BF16),
                        pltpu.VMEM((L, GLA_DV), BF16),
                        pltpu.VMEM((L, CHUNK), BF16),
                        pltpu.VMEM((nc, GLA_DV, 2 * GLA_DK), F32),
                        pltpu.VMEM((nc, SUBLANES, 2 * GLA_DK), F32),
                        pltpu.VMEM((nc, GLA_DV, 2 * GLA_DK), BF16)],
        compiler_params=_params(2),
        name="gla",
    )(*args)


def _ssd_body(*refs, seq_len, grid_rows, has_h0, want_state):
    it = iter(refs)
    (xs_ref, b_ref, c_ref, sm_ref, cwx_ref, cwb_ref, cwc_ref, cbx_ref, cbb_ref, cbc_ref,
     par_ref, dsk_ref) = (next(it) for _ in range(12))
    h0_ref = next(it) if has_h0 else None
    y_ref = next(it)
    st_ref = next(it) if want_state else None
    (pad_scr, xs_s, b_s, c_s, dt_scr, xy_scr, tr_scr, rows_scr, sc_scr, ed_scr, xw_scr, dec_scr,
     cs_scr, sprev_scr, state_scr) = (next(it) for _ in range(15))

    L = seq_len
    nc = L // CHUNK
    width = L // grid_rows
    pad = pad_scr.shape[0] - L
    pad //= 2
    grp = pl.program_id(1)

    def conv_into(src_ref, cw_ref, cb_ref, dst_ref):
        ch = src_ref.shape[1]
        pad_scr[0:pad, 0:ch] = jnp.zeros((pad, ch), F32)
        pad_scr[pad + L:pad + L + pad, 0:ch] = jnp.zeros((pad, ch), F32)
        pad_scr[pad:pad + L, 0:ch] = src_ref[...]
        rc = min(2 * CHUNK, L)
        col = lax.broadcasted_iota(jnp.int32, (rc, ch), 0) % width
        di_taps = range(CONV_K) if grid_rows > 1 else (CONV_K // 2,)
        for r0 in range(0, L, rc):
            acc = jnp.broadcast_to(cb_ref[...], (rc, ch))
            for dj in range(CONV_K):
                inner = None
                for di in di_taps:
                    off = pad + r0 + (di - 1) * width + (dj - 1)
                    term = cw_ref[di, dj:dj + 1, :] * pad_scr[off:off + rc, 0:ch]
                    inner = term if inner is None else inner + term
                if grid_rows > 1 and dj == 0:
                    inner = jnp.where(col >= 1, inner, 0.0)
                if grid_rows > 1 and dj == CONV_K - 1:
                    inner = jnp.where(col <= width - 2, inner, 0.0)
                acc = acc + inner
            dst_ref[r0:r0 + rc, :] = _silu(acc).astype(dst_ref.dtype)

    conv_into(xs_ref, cwx_ref, cbx_ref, xs_s)
    conv_into(b_ref, cwb_ref, cbb_ref, b_s)
    conv_into(c_ref, cwc_ref, cbc_ref, c_s)

    gw = GROUP_W
    unroll = min(SSD_UNROLL, nc)
    rows = lambda c: pl.ds(pl.multiple_of(c * CHUNK, CHUNK), CHUNK)
    bias_row = par_ref[0:1, :]
    a_row = -jnp.exp(par_ref[1:2, :]) * par_ref[2:3, :] * LOG2_E
    lower, upper = _tri_masks()
    lower_b = jnp.where(lower, 1.0, 0.0).astype(BF16)
    upper_b = jnp.where(upper, 1.0, 0.0).astype(BF16)
    lane = lax.broadcasted_iota(jnp.int32, (CHUNK, LANES), 1)
    dt_lanes = (lane >= SM_DTF) & (lane < SM_DTF + TERM_STRIDE)
    src = lax.broadcasted_iota(jnp.int32, (LANES, 2 * gw), 0)
    dst = lax.broadcasted_iota(jnp.int32, (LANES, 2 * gw), 1)
    dst_slot = (grp * HEADS_PER_GROUP + (lax.shift_right_logical(dst, 6) & (HEADS_PER_GROUP - 1))
                + jnp.where(dst >= gw, SSD_HEADS, 0))
    expand = jnp.where((src >= SM_DTF) & ((src & (TERM_STRIDE - 1)) == dst_slot),
                       1.0, 0.0).astype(BF16)
    t_idx = lax.broadcasted_iota(jnp.int32, (CHUNK, gw), 0)
    s_idx = lax.broadcasted_iota(jnp.int32, (CHUNK, gw), 1) & (CHUNK - 1)
    diag = t_idx == s_idx
    blk_r = lax.shift_right_logical(lax.broadcasted_iota(jnp.int32, (MXU_TILE, MXU_TILE), 0), 6)
    blk_c = lax.shift_right_logical(lax.broadcasted_iota(jnp.int32, (MXU_TILE, MXU_TILE), 1), 6)
    same_head = blk_r == blk_c

    dt_scr[...] = _softplus(sm_ref[...] + bias_row)

    def place3(v):
        hi = v.astype(BF16).astype(F32)
        rest = v - hi
        mid = rest.astype(BF16).astype(F32)
        lo = rest - mid
        keep = lambda t: jnp.where(dt_lanes, t, 0.0)
        out = (keep(hi) + pltpu.roll(keep(mid), TERM_STRIDE, axis=1)
               + pltpu.roll(keep(lo), 2 * TERM_STRIDE, axis=1))
        return out.astype(BF16)

    def cumulate(c, carry):
        sl = rows(c)
        dt = dt_scr[sl, :]
        da = dt * a_row
        hi = da.astype(BF16)
        both = jnp.concatenate([hi, (da - hi.astype(F32)).astype(BF16)], axis=1)
        pf = _dot(lower_b, both)
        pb = _dot(upper_b, both)
        cum = jnp.where(lane < SM_DTB, pf[:, 0:LANES] + pf[:, LANES:], pb[:, 0:LANES] + pb[:, LANES:])
        xy_scr[sl, 0:LANES] = place3(cum)
        xy_scr[sl, LANES:2 * LANES] = place3(dt)
        tr_scr[c, 0] = cum.T
        tr_scr[c, 1] = dt.T
        head0 = pl.multiple_of(SM_DTF + grp * HEADS_PER_GROUP, SUBLANES)
        for q, (k, d) in enumerate(((0, 0), (0, 1), (1, 0), (1, 1))):
            slab = tr_scr[c, k, pl.ds(head0 + d * SSD_HEADS, HEADS_PER_GROUP), :]
            row = jnp.concatenate(
                [jnp.broadcast_to(slab[h:h + 1, :], (SUBLANES, CHUNK))
                 for h in range(HEADS_PER_GROUP)], axis=1)
            rows_scr[c, :, q * gw:(q + 1) * gw] = row
        return carry

    lax.fori_loop(0, nc, cumulate, 0, unroll=unroll)

    def weights(c, carry):
        sl = rows(c)
        xc = xs_s[sl, :]
        cum_e = _dot(xy_scr[sl, 0:LANES], expand)
        dt_e = _dot(xy_scr[sl, LANES:2 * LANES], expand)
        cb = _dot_nt(c_s[sl, :], jnp.concatenate([b_s[sl, :]] * HEADS_PER_GROUP, axis=0))
        segs, dt_rows, dec_rows = [], [], []
        for d, (a, last_i) in enumerate(((0, CHUNK - 1), (gw, 0))):
            ce = cum_e[:, a:a + gw]
            de = dt_e[:, a:a + gw]
            segs.append(ce - rows_scr[c, 0:1, d * gw:(d + 1) * gw])
            dt_rows.append(rows_scr[c, 0:1, (2 + d) * gw:(3 + d) * gw])
            cum_last = ce[last_i:last_i + 1, :]
            ed_scr[sl, a:a + gw] = jnp.exp2(ce)
            xw_scr[sl, a:a + gw] = (xc * (jnp.exp2(cum_last - ce) * de)).astype(BF16)
            dec_rows.append(jnp.exp2(cum_last))
        fwd = t_idx >= s_idx
        w = (jnp.exp2(jnp.where(fwd, segs[0], segs[1])) * jnp.where(fwd, dt_rows[0], dt_rows[1])
             + jnp.where(diag, dt_rows[1], 0.0))
        sc_scr[sl, :] = (cb * w).astype(BF16)
        dec_scr[c] = jnp.broadcast_to(jnp.concatenate(dec_rows, axis=1), (SUBLANES, 2 * gw))
        return carry

    lax.fori_loop(0, nc, weights, 0, unroll=unroll)

    def products(c, carry):
        sl = rows(c)
        xc = xs_s[sl, :]
        xb = xc.astype(BF16)
        parts = []
        for j in range(gw // MXU_TILE):
            xh = xb[:, j * MXU_TILE:(j + 1) * MXU_TILE]
            rep = jnp.concatenate([xh] * (MXU_TILE // SSD_HEAD_DIM), axis=0)
            bd = jnp.where(same_head, rep, jnp.zeros_like(rep))
            parts.append(_dot(sc_scr[sl, j * MXU_TILE:(j + 1) * MXU_TILE], bd))
        y_ref[sl, :] = jnp.concatenate(parts, axis=1) + xc * dsk_ref[...]
        cs_scr[c] = _dot_tn(b_s[sl, :], xw_scr[sl, :])
        return carry

    lax.fori_loop(0, nc, products, 0, unroll=unroll)

    pair_w = 2 * SSD_HEAD_DIM
    if has_h0:
        for d in range(2):
            for j in range(HEADS_PER_GROUP // 2):
                pair = jnp.concatenate([h0_ref[0, 0, d, 2 * j], h0_ref[0, 0, d, 2 * j + 1]], axis=0)
                state_scr[:, d * gw + j * pair_w:d * gw + (j + 1) * pair_w] = pair.T
    else:
        state_scr[...] = jnp.zeros_like(state_scr)

    def recur(i, carry):
        for c, a in ((i, 0), (nc - 1 - i, gw)):
            state = state_scr[:, a:a + gw]
            sprev_scr[c, :, a:a + gw] = state.astype(BF16)
            state_scr[:, a:a + gw] = state * dec_scr[c, 0:1, a:a + gw] + cs_scr[c, :, a:a + gw]
        return carry

    lax.fori_loop(0, nc, recur, 0)

    def finish(c, carry):
        sl = rows(c)
        yi = _dot(c_s[sl, :], sprev_scr[c]) * ed_scr[sl, :]
        y_ref[sl, :] += yi[:, 0:gw] + yi[:, gw:2 * gw]
        return carry

    lax.fori_loop(0, nc, finish, 0, unroll=unroll)

    if want_state:
        for d in range(2):
            for j in range(HEADS_PER_GROUP // 2):
                pair = state_scr[:, d * gw + j * pair_w:d * gw + (j + 1) * pair_w].T
                st_ref[0, 0, d, 2 * j] = pair[0:SSD_HEAD_DIM, :]
                st_ref[0, 0, d, 2 * j + 1] = pair[SSD_HEAD_DIM:pair_w, :]


def _ssd_call(proj, conv_w, conv_b, par, dsk, h0, layer, *, n_seq, seq_len, grid_rows,
              want_state):
    L = seq_len
    x_base = COL_XBC
    b_base = COL_XBC + SSD_INNER
    c_base = b_base + SSD_BC
    in_specs = [pl.BlockSpec((L, GROUP_W), lambda s, g: (s, x_base // GROUP_W + g)),
                pl.BlockSpec((L, SSD_STATE), lambda s, g: (s, b_base // SSD_STATE + g)),
                pl.BlockSpec((L, SSD_STATE), lambda s, g: (s, c_base // SSD_STATE + g)),
                pl.BlockSpec((L, LANES), lambda s, g: (s, COL_SMALL // LANES)),
                pl.BlockSpec((CONV_K, CONV_K, GROUP_W), lambda s, g: (0, 0, g)),
                pl.BlockSpec((CONV_K, CONV_K, SSD_STATE),
                             lambda s, g: (0, 0, SSD_INNER // SSD_STATE + g)),
                pl.BlockSpec((CONV_K, CONV_K, SSD_STATE),
                             lambda s, g: (0, 0, (SSD_INNER + SSD_BC) // SSD_STATE + g)),
                pl.BlockSpec((1, GROUP_W), lambda s, g: (0, g)),
                pl.BlockSpec((1, SSD_STATE), lambda s, g: (0, SSD_INNER // SSD_STATE + g)),
                pl.BlockSpec((1, SSD_STATE),
                             lambda s, g: (0, (SSD_INNER + SSD_BC) // SSD_STATE + g)),
                pl.BlockSpec((SUBLANES, LANES), lambda s, g: (0, 0)),
                pl.BlockSpec((1, GROUP_W), lambda s, g: (0, g))]
    args = [proj, proj, proj, proj, conv_w, conv_w, conv_w, conv_b, conv_b, conv_b, par, dsk]
    state_blk = (1, 1, 2, HEADS_PER_GROUP, SSD_HEAD_DIM, SSD_STATE)
    if h0 is not None:
        in_specs.append(pl.BlockSpec(state_blk, lambda s, g: (s, layer, 0, g, 0, 0)))
        args.append(h0)
    out_specs = [pl.BlockSpec((L, GROUP_W), lambda s, g: (s, g))]
    out_shape = [jax.ShapeDtypeStruct((n_seq * L, SSD_INNER), F32)]
    if want_state:
        out_specs.append(pl.BlockSpec(state_blk, lambda s, g: (s, 0, 0, g, 0, 0)))
        out_shape.append(jax.ShapeDtypeStruct(
            (n_seq, 1, 2, SSD_HEADS, SSD_HEAD_DIM, SSD_STATE), F32))
    nc = L // CHUNK
    conv_pad = (L // grid_rows + SUBLANES) if grid_rows > 1 else SUBLANES
    body = functools.partial(_ssd_body, seq_len=L, grid_rows=grid_rows,
                             has_h0=h0 is not None, want_state=want_state)
    return pl.pallas_call(
        body,
        grid=(n_seq, SSD_GROUPS),
        in_specs=in_specs,
        out_specs=out_specs,
        out_shape=out_shape,
        scratch_shapes=[pltpu.VMEM((L + 2 * conv_pad, GROUP_W), F32),
                        pltpu.VMEM((L, GROUP_W), F32),
                        pltpu.VMEM((L, SSD_STATE), BF16),
                        pltpu.VMEM((L, SSD_STATE), BF16),
                        pltpu.VMEM((L, LANES), F32),
                        pltpu.VMEM((L, 2 * LANES), BF16),
                        pltpu.VMEM((nc, 2, LANES, CHUNK), F32),
                        pltpu.VMEM((nc, SUBLANES, 4 * GROUP_W), F32),
                        pltpu.VMEM((L, GROUP_W), BF16),
                        pltpu.VMEM((L, 2 * GROUP_W), F32),
                        pltpu.VMEM((L, 2 * GROUP_W), BF16),
                        pltpu.VMEM((nc, SUBLANES, 2 * GROUP_W), F32),
                        pltpu.VMEM((nc, SSD_STATE, 2 * GROUP_W), F32),
                        pltpu.VMEM((nc, SSD_STATE, 2 * GROUP_W), BF16),
                        pltpu.VMEM((SSD_STATE, 2 * GROUP_W), F32)],
        compiler_params=_params(2),
        name="ssd",
    )(*args)


def _layer_path(x, mod, lw, h0_gla, h0_ssd, layer, *, n_seq, seq_len, grid_rows, want_state,
                final_w):
    x1 = _ffn_call(x, mod, lw["norm_ffn1"], lw["ffn1_w_in"], lw["ffn1_w_out"],
                   sub=0, seq_len=seq_len)
    proj = _proj_call(x1, mod, lw["norm_mix"], lw["w_all"], lw["w_zx"], lw["w_small"],
                      seq_len=seq_len)
    gla_out = _gla_call(proj, lw["wa_cat"], lw["ba_cat"], lw["gla_norm_w"], h0_gla, layer,
                        n_seq=n_seq, seq_len=seq_len, want_state=want_state)
    ssd_out = _ssd_call(proj, lw["conv_w"], lw["conv_b"], lw["ssd_par"], lw["d_skip_row"], h0_ssd,
                        layer, n_seq=n_seq, seq_len=seq_len, grid_rows=grid_rows,
                        want_state=want_state)
    o, y = gla_out[0], ssd_out[0]
    out = _ffn_call(x1, mod, lw["norm_ffn2"], lw["ffn2_w_in"], lw["ffn2_w_out"],
                    sub=2, seq_len=seq_len, mix=(o, y, proj, lw["ssd_norm_w"], lw["w_out"]),
                    final_w=final_w)
    if want_state:
        return out, gla_out[1], ssd_out[1]
    return out, None, None


def kernel(x_prompt, x_sample, state_gla, state_ssd, c, c_ctx, norm_ffn1, norm_mix, norm_ffn2, w_mod, b_mod, ffn1_w_in, ffn1_w_out, ffn2_w_in, ffn2_w_out, w_in, gla_w_a2, gla_b_a, gla_norm_w, conv_w, conv_b, dt_bias, a_log, d_skip, ssd_norm_w, w_out, final_norm):
    nb, seq, _ = x_prompt.shape
    db, dseq, _ = x_sample.shape
    grid_rows = dseq // GRID_W
    xp = x_prompt.reshape(nb * seq, D_MODEL)
    xs = x_sample.reshape(db * dseq, D_MODEL)
    row = lambda v: v.reshape(1, -1)
    gla_states, ssd_states = [], []
    for i in range(DEPTH):
        last = i == DEPTH - 1
        n_rows = -(-(db + 1) // SUBLANES) * SUBLANES
        cc = jnp.concatenate([c, c_ctx[None, :], jnp.zeros((n_rows - db - 1, D_MODEL), F32)], 0)
        mod = _mod_call(cc, w_mod[i], row(b_mod[i])).reshape(n_rows, N_MOD, D_MODEL)
        mod_lat, mod_ctx = mod[:db], mod[db:db + 1]

        wi = w_in[i].astype(BF16)
        o_af = COL_Z
        o_z = o_af + 2 * GLA_LOWRANK
        o_dt = o_z + SSD_INNER + SSD_CONV_DIM
        w_zx = wi[:, o_z:o_dt]
        w_small = jnp.pad(jnp.concatenate([wi[:, o_af:o_z], wi[:, o_dt:]], axis=1),
                          ((0, 0), (0, LANES - 2 * GLA_LOWRANK - 2 * SSD_HEADS)))
        wa_cat = jnp.stack(
            [jnp.pad(gla_w_a2[i, d].reshape(GLA_LOWRANK, GLA_HEADS, GLA_DK),
                     ((lo, LANES - lo - GLA_LOWRANK), (0, 0), (0, 0)))
             for d, lo in ((0, SM_AF), (1, SM_AB))], axis=2).reshape(LANES, 2 * GLA_QK)
        ba_cat = jnp.stack([gla_b_a[i, 0].reshape(GLA_HEADS, GLA_DK),
                            gla_b_a[i, 1].reshape(GLA_HEADS, GLA_DK)], axis=1).reshape(1, 2 * GLA_QK)
        ssd_par = jnp.pad(
            jnp.stack([dt_bias[i].reshape(-1), a_log[i].reshape(-1), jnp.ones((2 * SSD_HEADS,), F32)]),
            ((0, SUBLANES - 3), (SM_DTF, LANES - SM_DTF - 2 * SSD_HEADS)))
        lw = {
            "norm_ffn1": row(norm_ffn1[i]), "norm_mix": row(norm_mix[i]),
            "norm_ffn2": row(norm_ffn2[i]),
            "ffn1_w_in": ffn1_w_in[i].astype(BF16), "ffn1_w_out": ffn1_w_out[i].astype(BF16),
            "ffn2_w_in": ffn2_w_in[i].astype(BF16), "ffn2_w_out": ffn2_w_out[i].astype(BF16),
            "w_all": wi, "w_zx": w_zx, "w_small": w_small, "wa_cat": wa_cat, "ba_cat": ba_cat,
            "gla_norm_w": row(gla_norm_w[i]), "conv_w": conv_w[i], "conv_b": row(conv_b[i]),
            "ssd_par": ssd_par, "d_skip_row": row(jnp.repeat(d_skip[i], SSD_HEAD_DIM)),
            "ssd_norm_w": row(ssd_norm_w[i]), "w_out": w_out[i].astype(BF16),
        }
        fw = row(final_norm) if last else None

        xp, sg, ss = _layer_path(xp, mod_ctx, lw, None, None, i, n_seq=nb, seq_len=seq,
                                 grid_rows=1, want_state=True, final_w=fw)
        xs, _, _ = _layer_path(xs, mod_lat, lw, state_gla, state_ssd, i, n_seq=db, seq_len=dseq,
                               grid_rows=grid_rows, want_state=False, final_w=fw)
        gla_states.append(sg)
        ssd_states.append(ss)
    y_prompt = xp.reshape(nb, seq, D_MODEL)
    y_sample = xs.reshape(db, dseq, D_MODEL)
    return (y_prompt, y_sample, jnp.concatenate(gla_states, axis=1),
            jnp.concatenate(ssd_states, axis=1))
```

```python
import functools

import jax
import jax.numpy as jnp
from jax import lax
from jax.experimental import pallas as pl
from jax.experimental.pallas import tpu as pltpu

F32 = jnp.float32
BF16 = jnp.bfloat16

D_MODEL = 1024
DEPTH = 1
GRID_W = 64
CHUNK = 64
EPS = 1e-6
N_MOD = 9
D_FF = 2816
GLA_HEADS = 4
GLA_DK = 128
GLA_DV = 256
GLA_LOWRANK = 16
GLA_TAU = 16.0
GLA_QK = GLA_HEADS * GLA_DK
GLA_V = GLA_HEADS * GLA_DV
SSD_HEADS = 16
SSD_HEAD_DIM = 64
SSD_GROUPS = 2
SSD_STATE = 128
SSD_INNER = SSD_HEADS * SSD_HEAD_DIM
SSD_BC = SSD_GROUPS * SSD_STATE
SSD_CONV_DIM = SSD_INNER + 2 * SSD_BC
CONV_K = 3
D_MIX = GLA_V + SSD_INNER

LOG2_E = 1.4426950408889634
LANES = 128
SUBLANES = 8
BF16_SUBLANES = 16
VMEM_LIMIT_BYTES = 56 * 1024 * 1024

COL_Q = 0
COL_K = COL_Q + GLA_QK
COL_V = COL_K + GLA_QK
COL_R = COL_V + GLA_V
COL_Z = COL_R + GLA_V
COL_XBC = COL_Z + SSD_INNER
COL_SMALL = COL_XBC + SSD_CONV_DIM
N_PROJ = COL_SMALL + LANES
SM_AF = 0
SM_AB = SM_AF + GLA_LOWRANK
SM_DTF = SM_AB + GLA_LOWRANK
SM_DTB = SM_DTF + SSD_HEADS
HEADS_PER_GROUP = SSD_HEADS // SSD_GROUPS
GROUP_W = HEADS_PER_GROUP * SSD_HEAD_DIM

TOKEN_TILE = 512
GLA_UNROLL = 16
SSD_UNROLL = 16
MXU_TILE = 256
TERM_STRIDE = 2 * SSD_HEADS
SCAN_BLOCK_TOKENS = 1024


def _dot(a, b):
    return jnp.dot(a, b, preferred_element_type=F32)


def _dot_nt(a, b):
    return lax.dot_general(a, b, (((1,), (1,)), ((), ())), preferred_element_type=F32)


def _dot_tn(a, b):
    return lax.dot_general(a, b, (((0,), (0,)), ((), ())), preferred_element_type=F32)


def _silu(x):
    return x * jax.nn.sigmoid(x)


def _log1p_exp_neg_abs(x):
    return jnp.log(1.0 + jnp.exp(-jnp.abs(x)))


def _softplus(x):
    return jnp.maximum(x, 0.0) + _log1p_exp_neg_abs(x)


def _log_sigmoid(x):
    return jnp.minimum(x, 0.0) - _log1p_exp_neg_abs(x)


def _rmsnorm(x, w):
    ms = jnp.mean(x * x, axis=-1, keepdims=True)
    return x * lax.rsqrt(ms + EPS) * w


def _resident(shape):
    nd = len(shape)
    return pl.BlockSpec(shape, lambda *_: (0,) * nd, pipeline_mode=pl.Buffered(1))


def _params(n_axes):
    return pltpu.CompilerParams(dimension_semantics=("arbitrary",) * n_axes,
                                vmem_limit_bytes=VMEM_LIMIT_BYTES)


def _cast_plumbing(cast, n_steps, step_of):
    in_specs, out_specs, out_shape = [], [], []
    for w in cast:
        rows, cols = w.shape
        per_step = -(-rows // n_steps)
        per_step = -(-per_step // BF16_SUBLANES) * BF16_SUBLANES
        blk = pl.BlockSpec((per_step, cols), lambda *g: (step_of(*g), 0))
        in_specs.append(blk)
        out_specs.append(blk)
        out_shape.append(jax.ShapeDtypeStruct((rows, cols), BF16))
    return in_specs, out_specs, out_shape


def _cast_blocks(src_refs, dst_refs):
    for src_ref, dst_ref in zip(src_refs, dst_refs):
        dst_ref[...] = src_ref[...].astype(dst_ref.dtype)


def _mod_body(c_ref, w_ref, b_ref, out_ref):
    a = _silu(c_ref[...]).astype(BF16)
    out_ref[...] = _dot(a, w_ref[...].astype(BF16)) + b_ref[...]


def _mod_call(cc, w_mod, b_mod):
    n_rows = cc.shape[0]
    tn = D_MODEL
    return pl.pallas_call(
        _mod_body,
        grid=(N_MOD * D_MODEL // tn,),
        in_specs=[pl.BlockSpec((n_rows, D_MODEL), lambda j: (0, 0)),
                  pl.BlockSpec((D_MODEL, tn), lambda j: (0, j)),
                  pl.BlockSpec((1, tn), lambda j: (0, j))],
        out_specs=pl.BlockSpec((n_rows, tn), lambda j: (0, j)),
        out_shape=jax.ShapeDtypeStruct((n_rows, N_MOD * D_MODEL), F32),
        compiler_params=_params(1),
        name="mod",
    )(cc, w_mod, b_mod)


def _ffn_body(*refs, sub, has_mix, has_final, n_cast):
    it = iter(refs)
    x_ref, mod_ref, nw_ref, win_ref, wout_ref = (next(it) for _ in range(5))
    if has_mix:
        o_ref, y_ref, z_ref, snw_ref, wo_ref = (next(it) for _ in range(5))
    if has_final:
        fn_ref = next(it)
    cast_src = [next(it) for _ in range(n_cast)]
    out_ref = next(it)
    _cast_blocks(cast_src, [next(it) for _ in range(n_cast)])

    x = x_ref[...]
    if has_mix:
        g2 = mod_ref[0, 5:6, :]
        yn = _rmsnorm(y_ref[...] * _silu(z_ref[...]), snw_ref[...])
        m = _dot(o_ref[...], wo_ref[:GLA_V, :]) + _dot(yn.astype(BF16), wo_ref[GLA_V:, :])
        x = x + g2 * m
    sh = mod_ref[0, 3 * sub:3 * sub + 1, :]
    sc = mod_ref[0, 3 * sub + 1:3 * sub + 2, :]
    gate = mod_ref[0, 3 * sub + 2:3 * sub + 3, :]
    h = (_rmsnorm(x, nw_ref[...]) * (1.0 + sc) + sh).astype(BF16)
    g = _dot(h, win_ref[:, :D_FF])
    u = _dot(h, win_ref[:, D_FF:])
    act = (_silu(g) * u).astype(BF16)
    x = x + (0.5 * gate) * _dot(act, wout_ref[...])
    if has_final:
        x = _rmsnorm(x, fn_ref[...])
    out_ref[...] = x


def _ffn_call(x, mod, norm_w, w_in, w_out, *, sub, seq_len, mix=None, final_w=None, cast=()):
    m_tok = x.shape[0]
    tm = TOKEN_TILE
    tiles_per_seq = seq_len // tm
    shared_mod = mod.shape[0] == 1
    mod_map = (lambda i: (0, 0, 0)) if shared_mod else (lambda i: (i // tiles_per_seq, 0, 0))
    tok = lambda width: pl.BlockSpec((tm, width), lambda i: (i, 0))
    in_specs = [tok(D_MODEL), pl.BlockSpec((1, N_MOD, D_MODEL), mod_map),
                _resident((1, D_MODEL)), _resident(w_in.shape), _resident(w_out.shape)]
    args = [x, mod, norm_w, w_in, w_out]
    if mix is not None:
        o, y, proj, ssd_norm_w, w_mix_out = mix
        in_specs += [tok(GLA_V), tok(SSD_INNER),
                     pl.BlockSpec((tm, SSD_INNER), lambda i: (i, COL_Z // SSD_INNER)),
                     _resident((1, SSD_INNER)), _resident(w_mix_out.shape)]
        args += [o, y, proj, ssd_norm_w, w_mix_out]
    if final_w is not None:
        in_specs.append(_resident((1, D_MODEL)))
        args.append(final_w)
    n_steps = m_tok // tm
    cast_in, cast_out, cast_shape = _cast_plumbing(cast, n_steps, lambda i: i)
    in_specs += cast_in
    args += list(cast)
    out_specs = [tok(D_MODEL)] + cast_out
    out_shape = [jax.ShapeDtypeStruct((m_tok, D_MODEL), F32)] + cast_shape
    body = functools.partial(_ffn_body, sub=sub, has_mix=mix is not None,
                             has_final=final_w is not None, n_cast=len(cast))
    outs = pl.pallas_call(
        body,
        grid=(n_steps,),
        in_specs=in_specs,
        out_specs=out_specs,
        out_shape=out_shape,
        compiler_params=_params(1),
        name="ffn_mix" if mix is not None else "ffn",
    )(*args)
    return outs if cast else outs[0]


def _proj_body(x_ref, mod_ref, nw_ref, wa_ref, wb_ref, wc_ref, out_ref):
    sh = mod_ref[0, 3:4, :]
    sc = mod_ref[0, 4:5, :]
    h = (_rmsnorm(x_ref[...], nw_ref[...]) * (1.0 + sc) + sh).astype(BF16)
    out_ref[:, COL_Q:COL_Z] = _dot_nt(h, wa_ref[...])
    out_ref[:, COL_Z:COL_SMALL] = _dot_nt(h, wb_ref[...])
    out_ref[:, COL_SMALL:N_PROJ] = _dot_nt(h, wc_ref[...])


def _proj_call(x, mod, norm_w, w_all, w_zx, w_small, *, seq_len):
    m_tok = x.shape[0]
    tm = TOKEN_TILE
    tiles_per_seq = seq_len // tm
    shared_mod = mod.shape[0] == 1
    mod_map = (lambda i: (0, 0, 0)) if shared_mod else (lambda i: (i // tiles_per_seq, 0, 0))
    return pl.pallas_call(
        _proj_body,
        grid=(m_tok // tm,),
        in_specs=[pl.BlockSpec((tm, D_MODEL), lambda i: (i, 0)),
                  pl.BlockSpec((1, N_MOD, D_MODEL), mod_map),
                  _resident((1, D_MODEL)), _resident((COL_Z, D_MODEL)),
                  _resident(w_zx.shape), _resident(w_small.shape)],
        out_specs=pl.BlockSpec((tm, N_PROJ), lambda i: (i, 0)),
        out_shape=jax.ShapeDtypeStruct((m_tok, N_PROJ), F32),
        compiler_params=_params(1),
        name="proj",
    )(x, mod, norm_w, w_all, w_zx, w_small)


def _tri_masks():
    row = lax.broadcasted_iota(jnp.int32, (CHUNK, CHUNK), 0)
    col = lax.broadcasted_iota(jnp.int32, (CHUNK, CHUNK), 1)
    return row >= col, row <= col


def _gla_body(*refs, seq_len, n_sub, has_h0, want_state, n_cast):
    it = iter(refs)
    q_ref, k_ref, v_ref, r_ref, sm_ref, wa_ref, ba_ref, nw_ref = (next(it) for _ in range(8))
    h0_ref = next(it) if has_h0 else None
    cast_src = [next(it) for _ in range(n_cast)]
    o_ref = next(it)
    st_ref = next(it) if want_state else None
    _cast_blocks(cast_src, [next(it) for _ in range(n_cast)])
    (la_scr, qk_scr, qg_scr, ks_scr, vb_scr, sc_scr, kv_scr, dec_scr,
     sprev_scr) = (next(it) for _ in range(9))

    nc = seq_len // CHUNK
    nct = n_sub * nc
    dk = GLA_DK
    lower, upper = _tri_masks()
    lower_b = jnp.where(lower, 1.0, 0.0).astype(BF16)
    upper_b = jnp.where(upper, 1.0, 0.0).astype(BF16)
    unroll = min(GLA_UNROLL, nct)
    rows = lambda c: pl.ds(pl.multiple_of(c * CHUNK, CHUNK), CHUNK)

    pre = _dot(sm_ref[...].astype(BF16), wa_ref[...].astype(BF16)) + ba_ref[...]
    la_scr[...] = _log_sigmoid(pre) * (LOG2_E / GLA_TAU)

    def prep(c, carry):
        sl = rows(c)
        qc = q_ref[sl, :] * (GLA_DK ** -0.5)
        kc = k_ref[sl, :]
        vb_scr[sl, :] = v_ref[sl, :].astype(BF16)
        la = la_scr[sl, :]
        hi = la.astype(BF16)
        lo = (la - hi.astype(F32)).astype(BF16)

        def cumulative(tri_b, a):
            p = _dot(tri_b, jnp.concatenate([hi[:, a:a + dk], lo[:, a:a + dk]], axis=1))
            return p[:, 0:dk] + p[:, dk:2 * dk]

        g_f = cumulative(lower_b, 0)
        g_b = cumulative(upper_b, dk)
        mid_f = g_f[CHUNK // 2:CHUNK // 2 + 1, :]
        mid_b = g_b[CHUNK - 1 - CHUNK // 2:CHUNK - CHUNK // 2, :]
        end_f = g_f[CHUNK - 1:CHUNK, :]
        end_b = g_b[0:1, :]
        qk_scr[sl, 0:dk] = (qc * jnp.exp2(g_f - mid_f)).astype(BF16)
        qk_scr[sl, dk:2 * dk] = (qc * jnp.exp2(g_b - mid_b)).astype(BF16)
        qk_scr[sl, 2 * dk:3 * dk] = (kc * jnp.exp2(mid_f - g_f)).astype(BF16)
        qk_scr[sl, 3 * dk:4 * dk] = (kc * jnp.exp2(mid_b - g_b)).astype(BF16)
        qg_scr[sl, 0:dk] = (qc * jnp.exp2(g_f)).astype(BF16)
        qg_scr[sl, dk:2 * dk] = (qc * jnp.exp2(g_b)).astype(BF16)
        ks_scr[sl, 0:dk] = (kc * jnp.exp2(end_f - g_f)).astype(BF16)
        ks_scr[sl, dk:2 * dk] = (kc * jnp.exp2(end_b - g_b)).astype(BF16)
        dec = jnp.exp2(jnp.concatenate([end_f, end_b], axis=1))
        dec_scr[c] = jnp.broadcast_to(dec, (SUBLANES, 2 * dk))
        return carry

    lax.fori_loop(0, nct, prep, 0, unroll=unroll)

    def products(c, carry):
        sl = rows(c)
        s_f = _dot_nt(qk_scr[sl, 0:dk], qk_scr[sl, 2 * dk:3 * dk])
        s_b = _dot_nt(qk_scr[sl, dk:2 * dk], qk_scr[sl, 3 * dk:4 * dk])
        sc_scr[sl, :] = (jnp.where(lower, s_f, 0.0) + jnp.where(upper, s_b, 0.0)).astype(BF16)
        kv_scr[c] = _dot_tn(vb_scr[sl, :], ks_scr[sl, :])
        return carry

    lax.fori_loop(0, nct, products, 0, unroll=unroll)

    for b in range(n_sub):
        for d, a in ((0, 0), (1, dk)):
            state0 = h0_ref[b, 0, d, 0].T if has_h0 else jnp.zeros((GLA_DV, dk), F32)

            def recur(i, state, b=b, d=d, a=a):
                c = b * nc + (i if d == 0 else nc - 1 - i)
                sprev_scr[c, :, a:a + dk] = state.astype(BF16)
                return state * dec_scr[c, 0:1, a:a + dk] + kv_scr[c, :, a:a + dk]

            state = lax.fori_loop(0, nc, recur, state0, unroll=min(nc, 4))
            if want_state:
                st_ref[b, 0, d, 0] = state.T

    def finish(c, carry):
        sl = rows(c)
        o = _dot(sc_scr[sl, :], vb_scr[sl, :]) + _dot_nt(qg_scr[sl, :], sprev_scr[c])
        o_ref[sl, :] = (_rmsnorm(o, nw_ref[...]) * _silu(r_ref[sl, :])).astype(o_ref.dtype)
        return carry

    lax.fori_loop(0, nct, finish, 0, unroll=unroll)


def _gla_call(proj, wa_cat, ba_cat, norm_w, h0, layer, *, n_seq, seq_len, want_state, cast=()):
    n_sub = max(1, SCAN_BLOCK_TOKENS // seq_len)
    L = n_sub * seq_len
    nc = L // CHUNK
    blk = lambda width, base: pl.BlockSpec((L, width), lambda s, h: (s, base // width + h))
    in_specs = [blk(GLA_DK, COL_Q), blk(GLA_DK, COL_K), blk(GLA_DV, COL_V), blk(GLA_DV, COL_R),
                pl.BlockSpec((L, LANES), lambda s, h: (s, COL_SMALL // LANES)),
                pl.BlockSpec((LANES, 2 * GLA_DK), lambda s, h: (0, h)),
                pl.BlockSpec((1, 2 * GLA_DK), lambda s, h: (0, h)),
                pl.BlockSpec((1, GLA_DV), lambda s, h: (0, 0))]
    args = [proj, proj, proj, proj, proj, wa_cat, ba_cat, norm_w]
    state_blk = (n_sub, 1, 2, 1, GLA_DK, GLA_DV)
    if h0 is not None:
        in_specs.append(pl.BlockSpec(state_blk, lambda s, h: (s, layer, 0, h, 0, 0)))
        args.append(h0)
    out_specs = [pl.BlockSpec((L, GLA_DV), lambda s, h: (s, h))]
    out_shape = [jax.ShapeDtypeStruct((n_seq * seq_len, GLA_V), BF16)]
    if want_state:
        out_specs.append(pl.BlockSpec(state_blk, lambda s, h: (s, 0, 0, h, 0, 0)))
        out_shape.append(jax.ShapeDtypeStruct((n_seq, 1, 2, GLA_HEADS, GLA_DK, GLA_DV), F32))
    n_blocks = n_seq // n_sub
    cast_in, cast_out, cast_shape = _cast_plumbing(cast, n_blocks * GLA_HEADS,
                                                   lambda s, h: s * GLA_HEADS + h)
    in_specs += cast_in
    args += list(cast)
    out_specs += cast_out
    out_shape += cast_shape
    body = functools.partial(_gla_body, seq_len=seq_len, n_sub=n_sub, has_h0=h0 is not None,
                             want_state=want_state, n_cast=len(cast))
    return pl.pallas_call(
        body,
        grid=(n_blocks, GLA_HEADS),
        in_specs=in_specs,
        out_specs=out_specs,
        out_shape=out_shape,
        scratch_shapes=[pltpu.VMEM((L, 2 * GLA_DK), F32),
                        pltpu.VMEM((L, 4 * GLA_DK), BF16),
                        pltpu.VMEM((L, 2 * GLA_DK), BF16),
                        pltpu.VMEM((L, 2 * GLA_DK), BF16),
                        pltpu.VMEM((L, GLA_DV), BF16),
                        pltpu.VMEM((L, CHUNK), BF16),
                        pltpu.VMEM((nc, GLA_DV, 2 * GLA_DK), F32),
                        pltpu.VMEM((nc, SUBLANES, 2 * GLA_DK), F32),
                        pltpu.VMEM((nc, GLA_DV, 2 * GLA_DK), BF16)],
        compiler_params=_params(2),
        name="gla",
    )(*args)


def _ssd_body(*refs, seq_len, grid_rows, has_h0, want_state):
    it = iter(refs)
    (xs_ref, b_ref, c_ref, sm_ref, cwx_ref, cwb_ref, cwc_ref, cbx_ref, cbb_ref, cbc_ref,
     par_ref, dsk_ref) = (next(it) for _ in range(12))
    h0_ref = next(it) if has_h0 else None
    y_ref = next(it)
    st_ref = next(it) if want_state else None
    (pad_scr, xs_s, b_s, c_s, dt_scr, xy_scr, tr_scr, rows_scr, sc_scr, ed_scr, xw_scr, dec_scr,
     cs_scr, sprev_scr, state_scr) = (next(it) for _ in range(15))

    L = seq_len
    nc = L // CHUNK
    width = L // grid_rows
    pad = pad_scr.shape[0] - L
    pad //= 2
    grp = pl.program_id(1)

    def conv_into(src_ref, cw_ref, cb_ref, dst_ref):
        ch = src_ref.shape[1]
        pad_scr[0:pad, 0:ch] = jnp.zeros((pad, ch), F32)
        pad_scr[pad + L:pad + L + pad, 0:ch] = jnp.zeros((pad, ch), F32)
        pad_scr[pad:pad + L, 0:ch] = src_ref[...]
        rc = min(2 * CHUNK, L)
        col = lax.broadcasted_iota(jnp.int32, (rc, ch), 0) % width
        di_taps = range(CONV_K) if grid_rows > 1 else (CONV_K // 2,)
        for r0 in range(0, L, rc):
            acc = jnp.broadcast_to(cb_ref[...], (rc, ch))
            for dj in range(CONV_K):
                inner = None
                for di in di_taps:
                    off = pad + r0 + (di - 1) * width + (dj - 1)
                    term = cw_ref[di, dj:dj + 1, :] * pad_scr[off:off + rc, 0:ch]
                    inner = term if inner is None else inner + term
                if grid_rows > 1 and dj == 0:
                    inner = jnp.where(col >= 1, inner, 0.0)
                if grid_rows > 1 and dj == CONV_K - 1:
                    inner = jnp.where(col <= width - 2, inner, 0.0)
                acc = acc + inner
            dst_ref[r0:r0 + rc, :] = _silu(acc).astype(dst_ref.dtype)

    conv_into(xs_ref, cwx_ref, cbx_ref, xs_s)
    conv_into(b_ref, cwb_ref, cbb_ref, b_s)
    conv_into(c_ref, cwc_ref, cbc_ref, c_s)

    gw = GROUP_W
    unroll = min(SSD_UNROLL, nc)
    rows = lambda c: pl.ds(pl.multiple_of(c * CHUNK, CHUNK), CHUNK)
    bias_row = par_ref[0:1, :]
    a_row = -jnp.exp(par_ref[1:2, :]) * par_ref[2:3, :] * LOG2_E
    lower, upper = _tri_masks()
    lower_b = jnp.where(lower, 1.0, 0.0).astype(BF16)
    upper_b = jnp.where(upper, 1.0, 0.0).astype(BF16)
    lane = lax.broadcasted_iota(jnp.int32, (CHUNK, LANES), 1)
    dt_lanes = (lane >= SM_DTF) & (lane < SM_DTF + TERM_STRIDE)
    src = lax.broadcasted_iota(jnp.int32, (LANES, 2 * gw), 0)
    dst = lax.broadcasted_iota(jnp.int32, (LANES, 2 * gw), 1)
    dst_slot = (grp * HEADS_PER_GROUP + (lax.shift_right_logical(dst, 6) & (HEADS_PER_GROUP - 1))
                + jnp.where(dst >= gw, SSD_HEADS, 0))
    expand = jnp.where((src >= SM_DTF) & ((src & (TERM_STRIDE - 1)) == dst_slot),
                       1.0, 0.0).astype(BF16)
    t_idx = lax.broadcasted_iota(jnp.int32, (CHUNK, gw), 0)
    s_idx = lax.broadcasted_iota(jnp.int32, (CHUNK, gw), 1) & (CHUNK - 1)
    diag = t_idx == s_idx
    blk_r = lax.shift_right_logical(lax.broadcasted_iota(jnp.int32, (MXU_TILE, MXU_TILE), 0), 6)
    blk_c = lax.shift_right_logical(lax.broadcasted_iota(jnp.int32, (MXU_TILE, MXU_TILE), 1), 6)
    same_head = blk_r == blk_c

    dt_scr[...] = _softplus(sm_ref[...] + bias_row)

    def place3(v):
        hi = v.astype(BF16).astype(F32)
        rest = v - hi
        mid = rest.astype(BF16).astype(F32)
        lo = rest - mid
        keep = lambda t: jnp.where(dt_lanes, t, 0.0)
        out = (keep(hi) + pltpu.roll(keep(mid), TERM_STRIDE, axis=1)
               + pltpu.roll(keep(lo), 2 * TERM_STRIDE, axis=1))
        return out.astype(BF16)

    def cumulate(c, carry):
        sl = rows(c)
        dt = dt_scr[sl, :]
        da = dt * a_row
        hi = da.astype(BF16)
        both = jnp.concatenate([hi, (da - hi.astype(F32)).astype(BF16)], axis=1)
        pf = _dot(lower_b, both)
        pb = _dot(upper_b, both)
        cum = jnp.where(lane < SM_DTB, pf[:, 0:LANES] + pf[:, LANES:], pb[:, 0:LANES] + pb[:, LANES:])
        xy_scr[sl, 0:LANES] = place3(cum)
        xy_scr[sl, LANES:2 * LANES] = place3(dt)
        tr_scr[c, 0] = cum.T
        tr_scr[c, 1] = dt.T
        head0 = pl.multiple_of(SM_DTF + grp * HEADS_PER_GROUP, SUBLANES)
        for q, (k, d) in enumerate(((0, 0), (0, 1), (1, 0), (1, 1))):
            slab = tr_scr[c, k, pl.ds(head0 + d * SSD_HEADS, HEADS_PER_GROUP), :]
            row = jnp.concatenate(
                [jnp.broadcast_to(slab[h:h + 1, :], (SUBLANES, CHUNK))
                 for h in range(HEADS_PER_GROUP)], axis=1)
            rows_scr[c, :, q * gw:(q + 1) * gw] = row
        return carry

    lax.fori_loop(0, nc, cumulate, 0, unroll=unroll)

    def weights(c, carry):
        sl = rows(c)
        xc = xs_s[sl, :]
        cum_e = _dot(xy_scr[sl, 0:LANES], expand)
        dt_e = _dot(xy_scr[sl, LANES:2 * LANES], expand)
        cb = _dot_nt(c_s[sl, :], jnp.concatenate([b_s[sl, :]] * HEADS_PER_GROUP, axis=0))
        segs, dt_rows, dec_rows = [], [], []
        for d, (a, last_i) in enumerate(((0, CHUNK - 1), (gw, 0))):
            ce = cum_e[:, a:a + gw]
            de = dt_e[:, a:a + gw]
            segs.append(ce - rows_scr[c, 0:1, d * gw:(d + 1) * gw])
            dt_rows.append(rows_scr[c, 0:1, (2 + d) * gw:(3 + d) * gw])
            cum_last = ce[last_i:last_i + 1, :]
            ed_scr[sl, a:a + gw] = jnp.exp2(ce)
            xw_scr[sl, a:a + gw] = (xc * (jnp.exp2(cum_last - ce) * de)).astype(BF16)
            dec_rows.append(jnp.exp2(cum_last))
        fwd = t_idx >= s_idx
        w = (jnp.exp2(jnp.where(fwd, segs[0], segs[1])) * jnp.where(fwd, dt_rows[0], dt_rows[1])
             + jnp.where(diag, dt_rows[1], 0.0))
        sc_scr[sl, :] = (cb * w).astype(BF16)
        dec_scr[c] = jnp.broadcast_to(jnp.concatenate(dec_rows, axis=1), (SUBLANES, 2 * gw))
        return carry

    lax.fori_loop(0, nc, weights, 0, unroll=unroll)

    def products(c, carry):
        sl = rows(c)
        xc = xs_s[sl, :]
        xb = xc.astype(BF16)
        parts = []
        for j in range(gw // MXU_TILE):
            xh = xb[:, j * MXU_TILE:(j + 1) * MXU_TILE]
            rep = jnp.concatenate([xh] * (MXU_TILE // SSD_HEAD_DIM), axis=0)
            bd = jnp.where(same_head, rep, jnp.zeros_like(rep))
            parts.append(_dot(sc_scr[sl, j * MXU_TILE:(j + 1) * MXU_TILE], bd))
        y_ref[sl, :] = jnp.concatenate(parts, axis=1) + xc * dsk_ref[...]
        cs_scr[c] = _dot_tn(b_s[sl, :], xw_scr[sl, :])
        return carry

    lax.fori_loop(0, nc, products, 0, unroll=unroll)

    pair_w = 2 * SSD_HEAD_DIM
    if has_h0:
        for d in range(2):
            for j in range(HEADS_PER_GROUP // 2):
                pair = jnp.concatenate([h0_ref[0, 0, d, 2 * j], h0_ref[0, 0, d, 2 * j + 1]], axis=0)
                state_scr[:, d * gw + j * pair_w:d * gw + (j + 1) * pair_w] = pair.T
    else:
        state_scr[...] = jnp.zeros_like(state_scr)

    def recur(i, carry):
        for c, a in ((i, 0), (nc - 1 - i, gw)):
            state = state_scr[:, a:a + gw]
            sprev_scr[c, :, a:a + gw] = state.astype(BF16)
            state_scr[:, a:a + gw] = state * dec_scr[c, 0:1, a:a + gw] + cs_scr[c, :, a:a + gw]
        return carry

    lax.fori_loop(0, nc, recur, 0)

    def finish(c, carry):
        sl = rows(c)
        yi = _dot(c_s[sl, :], sprev_scr[c]) * ed_scr[sl, :]
        y_ref[sl, :] += yi[:, 0:gw] + yi[:, gw:2 * gw]
        return carry

    lax.fori_loop(0, nc, finish, 0, unroll=unroll)

    if want_state:
        for d in range(2):
            for j in range(HEADS_PER_GROUP // 2):
                pair = state_scr[:, d * gw + j * pair_w:d * gw + (j + 1) * pair_w].T
                st_ref[0, 0, d, 2 * j] = pair[0:SSD_HEAD_DIM, :]
                st_ref[0, 0, d, 2 * j + 1] = pair[SSD_HEAD_DIM:pair_w, :]


def _ssd_call(proj, conv_w, conv_b, par, dsk, h0, layer, *, n_seq, seq_len, grid_rows,
              want_state):
    L = seq_len
    x_base = COL_XBC
    b_base = COL_XBC + SSD_INNER
    c_base = b_base + SSD_BC
    in_specs = [pl.BlockSpec((L, GROUP_W), lambda s, g: (s, x_base // GROUP_W + g)),
                pl.BlockSpec((L, SSD_STATE), lambda s, g: (s, b_base // SSD_STATE + g)),
                pl.BlockSpec((L, SSD_STATE), lambda s, g: (s, c_base // SSD_STATE + g)),
                pl.BlockSpec((L, LANES), lambda s, g: (s, COL_SMALL // LANES)),
                pl.BlockSpec((CONV_K, CONV_K, GROUP_W), lambda s, g: (0, 0, g)),
                pl.BlockSpec((CONV_K, CONV_K, SSD_STATE),
                             lambda s, g: (0, 0, SSD_INNER // SSD_STATE + g)),
                pl.BlockSpec((CONV_K, CONV_K, SSD_STATE),
                             lambda s, g: (0, 0, (SSD_INNER + SSD_BC) // SSD_STATE + g)),
                pl.BlockSpec((1, GROUP_W), lambda s, g: (0, g)),
                pl.BlockSpec((1, SSD_STATE), lambda s, g: (0, SSD_INNER // SSD_STATE + g)),
                pl.BlockSpec((1, SSD_STATE),
                             lambda s, g: (0, (SSD_INNER + SSD_BC) // SSD_STATE + g)),
                pl.BlockSpec((SUBLANES, LANES), lambda s, g: (0, 0)),
                pl.BlockSpec((1, GROUP_W), lambda s, g: (0, g))]
    args = [proj, proj, proj, proj, conv_w, conv_w, conv_w, conv_b, conv_b, conv_b, par, dsk]
    state_blk = (1, 1, 2, HEADS_PER_GROUP, SSD_HEAD_DIM, SSD_STATE)
    if h0 is not None:
        in_specs.append(pl.BlockSpec(state_blk, lambda s, g: (s, layer, 0, g, 0, 0)))
        args.append(h0)
    out_specs = [pl.BlockSpec((L, GROUP_W), lambda s, g: (s, g))]
    out_shape = [jax.ShapeDtypeStruct((n_seq * L, SSD_INNER), F32)]
    if want_state:
        out_specs.append(pl.BlockSpec(state_blk, lambda s, g: (s, 0, 0, g, 0, 0)))
        out_shape.append(jax.ShapeDtypeStruct(
            (n_seq, 1, 2, SSD_HEADS, SSD_HEAD_DIM, SSD_STATE), F32))
    nc = L // CHUNK
    conv_pad = (L // grid_rows + SUBLANES) if grid_rows > 1 else SUBLANES
    body = functools.partial(_ssd_body, seq_len=L, grid_rows=grid_rows,
                             has_h0=h0 is not None, want_state=want_state)
    return pl.pallas_call(
        body,
        grid=(n_seq, SSD_GROUPS),
        in_specs=in_specs,
        out_specs=out_specs,
        out_shape=out_shape,
        scratch_shapes=[pltpu.VMEM((L + 2 * conv_pad, GROUP_W), F32),
                        pltpu.VMEM((L, GROUP_W), F32),
                        pltpu.VMEM((L, SSD_STATE), BF16),
                        pltpu.VMEM((L, SSD_STATE), BF16),
                        pltpu.VMEM((L, LANES), F32),
                        pltpu.VMEM((L, 2 * LANES), BF16),
                        pltpu.VMEM((nc, 2, LANES, CHUNK), F32),
                        pltpu.VMEM((nc, SUBLANES, 4 * GROUP_W), F32),
                        pltpu.VMEM((L, GROUP_W), BF16),
                        pltpu.VMEM((L, 2 * GROUP_W), F32),
                        pltpu.VMEM((L, 2 * GROUP_W), BF16),
                        pltpu.VMEM((nc, SUBLANES, 2 * GROUP_W), F32),
                        pltpu.VMEM((nc, SSD_STATE, 2 * GROUP_W), F32),
                        pltpu.VMEM((nc, SSD_STATE, 2 * GROUP_W), BF16),
                        pltpu.VMEM((SSD_STATE, 2 * GROUP_W), F32)],
        compiler_params=_params(2),
        name="ssd",
    )(*args)


def _proj_pieces(w_all):
    o_z = COL_Z + 2 * GLA_LOWRANK
    o_dt = o_z + SSD_INNER + SSD_CONV_DIM
    w_small = jnp.pad(jnp.concatenate([w_all[COL_Z:o_z], w_all[o_dt:]], axis=0),
                      ((0, LANES - 2 * GLA_LOWRANK - 2 * SSD_HEADS), (0, 0)))
    return {"w_all": w_all, "w_zx": w_all[o_z:o_dt], "w_small": w_small}


def _layer_path(x, mod, lw, raw, h0_gla, h0_ssd, layer, *, n_seq, seq_len, grid_rows,
                want_state, final_w):
    if raw is not None:
        x1, w_all = _ffn_call(x, mod, lw["norm_ffn1"], lw["ffn1_w_in"], lw["ffn1_w_out"],
                              sub=0, seq_len=seq_len, cast=(raw["w_in"],))
        lw.update(_proj_pieces(w_all))
    else:
        x1 = _ffn_call(x, mod, lw["norm_ffn1"], lw["ffn1_w_in"], lw["ffn1_w_out"],
                       sub=0, seq_len=seq_len)
    proj = _proj_call(x1, mod, lw["norm_mix"], lw["w_all"], lw["w_zx"], lw["w_small"],
                      seq_len=seq_len)
    late = ("ffn2_w_in", "ffn2_w_out", "w_out")
    gla_out = _gla_call(proj, lw["wa_cat"], lw["ba_cat"], lw["gla_norm_w"], h0_gla, layer,
                        n_seq=n_seq, seq_len=seq_len, want_state=want_state,
                        cast=tuple(raw[k] for k in late) if raw is not None else ())
    if raw is not None:
        lw.update(zip(late, gla_out[-len(late):]))
    ssd_out = _ssd_call(proj, lw["conv_w"], lw["conv_b"], lw["ssd_par"], lw["d_skip_row"], h0_ssd,
                        layer, n_seq=n_seq, seq_len=seq_len, grid_rows=grid_rows,
                        want_state=want_state)
    o, y = gla_out[0], ssd_out[0]
    out = _ffn_call(x1, mod, lw["norm_ffn2"], lw["ffn2_w_in"], lw["ffn2_w_out"],
                    sub=2, seq_len=seq_len, mix=(o, y, proj, lw["ssd_norm_w"], lw["w_out"]),
                    final_w=final_w)
    if want_state:
        return out, gla_out[1], ssd_out[1]
    return out, None, None


def kernel(x_prompt, x_sample, state_gla, state_ssd, c, c_ctx, norm_ffn1, norm_mix, norm_ffn2, w_mod, b_mod, ffn1_w_in, ffn1_w_out, ffn2_w_in, ffn2_w_out, w_in, gla_w_a2, gla_b_a, gla_norm_w, conv_w, conv_b, dt_bias, a_log, d_skip, ssd_norm_w, w_out, final_norm):
    nb, seq, _ = x_prompt.shape
    db, dseq, _ = x_sample.shape
    grid_rows = dseq // GRID_W
    xp = x_prompt.reshape(nb * seq, D_MODEL)
    xs = x_sample.reshape(db * dseq, D_MODEL)
    row = lambda v: v.reshape(1, -1)
    gla_states, ssd_states = [], []
    for i in range(DEPTH):
        last = i == DEPTH - 1
        n_rows = -(-(db + 1) // SUBLANES) * SUBLANES
        cc = jnp.concatenate([c, c_ctx[None, :], jnp.zeros((n_rows - db - 1, D_MODEL), F32)], 0)
        mod = _mod_call(cc, w_mod[i], row(b_mod[i])).reshape(n_rows, N_MOD, D_MODEL)
        mod_lat, mod_ctx = mod[:db], mod[db:db + 1]

        wa_cat = jnp.stack(
            [jnp.pad(gla_w_a2[i, d].reshape(GLA_LOWRANK, GLA_HEADS, GLA_DK),
                     ((lo, LANES - lo - GLA_LOWRANK), (0, 0), (0, 0)))
             for d, lo in ((0, SM_AF), (1, SM_AB))], axis=2).reshape(LANES, 2 * GLA_QK)
        ba_cat = jnp.stack([gla_b_a[i, 0].reshape(GLA_HEADS, GLA_DK),
                            gla_b_a[i, 1].reshape(GLA_HEADS, GLA_DK)], axis=1).reshape(1, 2 * GLA_QK)
        ssd_par = jnp.pad(
            jnp.stack([dt_bias[i].reshape(-1), a_log[i].reshape(-1), jnp.ones((2 * SSD_HEADS,), F32)]),
            ((0, SUBLANES - 3), (SM_DTF, LANES - SM_DTF - 2 * SSD_HEADS)))
        lw = {
            "norm_ffn1": row(norm_ffn1[i]), "norm_mix": row(norm_mix[i]),
            "norm_ffn2": row(norm_ffn2[i]),
            "ffn1_w_in": ffn1_w_in[i].astype(BF16), "ffn1_w_out": ffn1_w_out[i].astype(BF16),
            "wa_cat": wa_cat, "ba_cat": ba_cat,
            "gla_norm_w": row(gla_norm_w[i]), "conv_w": conv_w[i], "conv_b": row(conv_b[i]),
            "ssd_par": ssd_par, "d_skip_row": row(jnp.repeat(d_skip[i], SSD_HEAD_DIM)),
            "ssd_norm_w": row(ssd_norm_w[i]),
        }
        raw = {"w_in": jnp.swapaxes(w_in[i], 0, 1), "ffn2_w_in": ffn2_w_in[i], "ffn2_w_out": ffn2_w_out[i],
               "w_out": w_out[i]}
        fw = row(final_norm) if last else None

        xp, sg, ss = _layer_path(xp, mod_ctx, lw, raw, None, None, i, n_seq=nb, seq_len=seq,
                                 grid_rows=1, want_state=True, final_w=fw)
        xs, _, _ = _layer_path(xs, mod_lat, lw, None, state_gla, state_ssd, i, n_seq=db,
                               seq_len=dseq, grid_rows=grid_rows, want_state=False, final_w=fw)
        gla_states.append(sg)
        ssd_states.append(ss)
    y_prompt = xp.reshape(nb, seq, D_MODEL)
    y_sample = xs.reshape(db, dseq, D_MODEL)
    return (y_prompt, y_sample, jnp.concatenate(gla_states, axis=1),
            jnp.concatenate(ssd_states, axis=1))
```

```python
import functools

import jax
import jax.numpy as jnp
from jax import lax
from jax.experimental import pallas as pl
from jax.experimental.pallas import tpu as pltpu

F32 = jnp.float32
BF16 = jnp.bfloat16

D_MODEL = 1024
DEPTH = 1
GRID_W = 64
CHUNK = 64
EPS = 1e-6
N_MOD = 9
D_FF = 2816
GLA_HEADS = 4
GLA_DK = 128
GLA_DV = 256
GLA_LOWRANK = 16
GLA_TAU = 16.0
GLA_QK = GLA_HEADS * GLA_DK
GLA_V = GLA_HEADS * GLA_DV
SSD_HEADS = 16
SSD_HEAD_DIM = 64
SSD_GROUPS = 2
SSD_STATE = 128
SSD_INNER = SSD_HEADS * SSD_HEAD_DIM
SSD_BC = SSD_GROUPS * SSD_STATE
SSD_CONV_DIM = SSD_INNER + 2 * SSD_BC
CONV_K = 3
D_MIX = GLA_V + SSD_INNER

LOG2_E = 1.4426950408889634
LANES = 128
SUBLANES = 8
BF16_SUBLANES = 16
VMEM_LIMIT_BYTES = 56 * 1024 * 1024

COL_Q = 0
COL_K = COL_Q + GLA_QK
COL_V = COL_K + GLA_QK
COL_R = COL_V + GLA_V
COL_Z = COL_R + GLA_V
COL_XBC = COL_Z + SSD_INNER
COL_SMALL = COL_XBC + SSD_CONV_DIM
N_PROJ = COL_SMALL + LANES
SM_AF = 0
SM_AB = SM_AF + GLA_LOWRANK
SM_DTF = SM_AB + GLA_LOWRANK
SM_DTB = SM_DTF + SSD_HEADS
HEADS_PER_GROUP = SSD_HEADS // SSD_GROUPS
GROUP_W = HEADS_PER_GROUP * SSD_HEAD_DIM

TOKEN_TILE = 512
GLA_UNROLL = 16
SSD_UNROLL = 16
MXU_TILE = 256
TERM_STRIDE = 2 * SSD_HEADS
SCAN_BLOCK_TOKENS = 1024


def _dot(a, b):
    return jnp.dot(a, b, preferred_element_type=F32)


def _dot_nt(a, b):
    return lax.dot_general(a, b, (((1,), (1,)), ((), ())), preferred_element_type=F32)


def _dot_tn(a, b):
    return lax.dot_general(a, b, (((0,), (0,)), ((), ())), preferred_element_type=F32)


def _silu(x):
    return x * jax.nn.sigmoid(x)


def _log1p_exp_neg_abs(x):
    return jnp.log(1.0 + jnp.exp(-jnp.abs(x)))


def _softplus(x):
    return jnp.maximum(x, 0.0) + _log1p_exp_neg_abs(x)


def _log_sigmoid(x):
    return jnp.minimum(x, 0.0) - _log1p_exp_neg_abs(x)


def _rmsnorm(x, w):
    ms = jnp.mean(x * x, axis=-1, keepdims=True)
    return x * lax.rsqrt(ms + EPS) * w


def _resident(shape):
    nd = len(shape)
    return pl.BlockSpec(shape, lambda *_: (0,) * nd, pipeline_mode=pl.Buffered(1))


def _params(n_axes):
    return pltpu.CompilerParams(dimension_semantics=("arbitrary",) * n_axes,
                                vmem_limit_bytes=VMEM_LIMIT_BYTES)


def _cast_plumbing(cast, n_steps, step_of):
    in_specs, out_specs, out_shape = [], [], []
    for w in cast:
        rows, cols = w.shape
        per_step = -(-rows // n_steps)
        per_step = -(-per_step // BF16_SUBLANES) * BF16_SUBLANES
        blk = pl.BlockSpec((per_step, cols), lambda *g: (step_of(*g), 0))
        in_specs.append(blk)
        out_specs.append(blk)
        out_shape.append(jax.ShapeDtypeStruct((rows, cols), BF16))
    return in_specs, out_specs, out_shape


def _cast_blocks(src_refs, dst_refs):
    for src_ref, dst_ref in zip(src_refs, dst_refs):
        dst_ref[...] = src_ref[...].astype(dst_ref.dtype)


def _mod_body(c_ref, w_ref, b_ref, out_ref):
    a = _silu(c_ref[...]).astype(BF16)
    out_ref[...] = _dot(a, w_ref[...].astype(BF16)) + b_ref[...]


def _mod_call(cc, w_mod, b_mod):
    n_rows = cc.shape[0]
    tn = D_MODEL
    return pl.pallas_call(
        _mod_body,
        grid=(N_MOD * D_MODEL // tn,),
        in_specs=[pl.BlockSpec((n_rows, D_MODEL), lambda j: (0, 0)),
                  pl.BlockSpec((D_MODEL, tn), lambda j: (0, j)),
                  pl.BlockSpec((1, tn), lambda j: (0, j))],
        out_specs=pl.BlockSpec((n_rows, tn), lambda j: (0, j)),
        out_shape=jax.ShapeDtypeStruct((n_rows, N_MOD * D_MODEL), F32),
        compiler_params=_params(1),
        name="mod",
    )(cc, w_mod, b_mod)


def _ffn_body(*refs, sub, has_mix, has_final, n_cast):
    it = iter(refs)
    x_ref, mod_ref, nw_ref, win_ref, wout_ref = (next(it) for _ in range(5))
    if has_mix:
        o_ref, y_ref, z_ref, snw_ref, wo_ref = (next(it) for _ in range(5))
    if has_final:
        fn_ref = next(it)
    cast_src = [next(it) for _ in range(n_cast)]
    out_ref = next(it)
    _cast_blocks(cast_src, [next(it) for _ in range(n_cast)])

    x = x_ref[...]
    if has_mix:
        g2 = mod_ref[0, 5:6, :]
        yn = _rmsnorm(y_ref[...] * _silu(z_ref[...]), snw_ref[...])
        m = _dot(o_ref[...], wo_ref[:GLA_V, :]) + _dot(yn.astype(BF16), wo_ref[GLA_V:, :])
        x = x + g2 * m
    sh = mod_ref[0, 3 * sub:3 * sub + 1, :]
    sc = mod_ref[0, 3 * sub + 1:3 * sub + 2, :]
    gate = mod_ref[0, 3 * sub + 2:3 * sub + 3, :]
    h = (_rmsnorm(x, nw_ref[...]) * (1.0 + sc) + sh).astype(BF16)
    g = _dot(h, win_ref[:, :D_FF])
    u = _dot(h, win_ref[:, D_FF:])
    act = (_silu(g) * u).astype(BF16)
    x = x + (0.5 * gate) * _dot(act, wout_ref[...])
    if has_final:
        x = _rmsnorm(x, fn_ref[...])
    out_ref[...] = x


def _ffn_call(x, mod, norm_w, w_in, w_out, *, sub, seq_len, mix=None, final_w=None, cast=()):
    m_tok = x.shape[0]
    tm = TOKEN_TILE
    tiles_per_seq = seq_len // tm
    shared_mod = mod.shape[0] == 1
    mod_map = (lambda i: (0, 0, 0)) if shared_mod else (lambda i: (i // tiles_per_seq, 0, 0))
    tok = lambda width: pl.BlockSpec((tm, width), lambda i: (i, 0))
    in_specs = [tok(D_MODEL), pl.BlockSpec((1, N_MOD, D_MODEL), mod_map),
                _resident((1, D_MODEL)), _resident(w_in.shape), _resident(w_out.shape)]
    args = [x, mod, norm_w, w_in, w_out]
    if mix is not None:
        o, y, proj, ssd_norm_w, w_mix_out = mix
        in_specs += [tok(GLA_V), tok(SSD_INNER),
                     pl.BlockSpec((tm, SSD_INNER), lambda i: (i, COL_Z // SSD_INNER)),
                     _resident((1, SSD_INNER)), _resident(w_mix_out.shape)]
        args += [o, y, proj, ssd_norm_w, w_mix_out]
    if final_w is not None:
        in_specs.append(_resident((1, D_MODEL)))
        args.append(final_w)
    n_steps = m_tok // tm
    cast_in, cast_out, cast_shape = _cast_plumbing(cast, n_steps, lambda i: i)
    in_specs += cast_in
    args += list(cast)
    out_specs = [tok(D_MODEL)] + cast_out
    out_shape = [jax.ShapeDtypeStruct((m_tok, D_MODEL), F32)] + cast_shape
    body = functools.partial(_ffn_body, sub=sub, has_mix=mix is not None,
                             has_final=final_w is not None, n_cast=len(cast))
    outs = pl.pallas_call(
        body,
        grid=(n_steps,),
        in_specs=in_specs,
        out_specs=out_specs,
        out_shape=out_shape,
        compiler_params=_params(1),
        name="ffn_mix" if mix is not None else "ffn",
    )(*args)
    return outs if cast else outs[0]


def _proj_body(x_ref, mod_ref, nw_ref, wa_ref, wb_ref, wc_ref, out_ref):
    sh = mod_ref[0, 3:4, :]
    sc = mod_ref[0, 4:5, :]
    h = (_rmsnorm(x_ref[...], nw_ref[...]) * (1.0 + sc) + sh).astype(BF16)
    out_ref[:, COL_Q:COL_Z] = _dot_nt(h, wa_ref[...])
    out_ref[:, COL_Z:COL_SMALL] = _dot_nt(h, wb_ref[...])
    out_ref[:, COL_SMALL:N_PROJ] = _dot_nt(h, wc_ref[...])


def _proj_call(x, mod, norm_w, w_all, w_zx, w_small, *, seq_len):
    m_tok = x.shape[0]
    tm = TOKEN_TILE
    tiles_per_seq = seq_len // tm
    shared_mod = mod.shape[0] == 1
    mod_map = (lambda i: (0, 0, 0)) if shared_mod else (lambda i: (i // tiles_per_seq, 0, 0))
    return pl.pallas_call(
        _proj_body,
        grid=(m_tok // tm,),
        in_specs=[pl.BlockSpec((tm, D_MODEL), lambda i: (i, 0)),
                  pl.BlockSpec((1, N_MOD, D_MODEL), mod_map),
                  _resident((1, D_MODEL)), _resident((COL_Z, D_MODEL)),
                  _resident(w_zx.shape), _resident(w_small.shape)],
        out_specs=pl.BlockSpec((tm, N_PROJ), lambda i: (i, 0)),
        out_shape=jax.ShapeDtypeStruct((m_tok, N_PROJ), F32),
        compiler_params=_params(1),
        name="proj",
    )(x, mod, norm_w, w_all, w_zx, w_small)


def _tri_masks():
    row = lax.broadcasted_iota(jnp.int32, (CHUNK, CHUNK), 0)
    col = lax.broadcasted_iota(jnp.int32, (CHUNK, CHUNK), 1)
    return row >= col, row <= col


def _gla_body(*refs, seq_len, n_sub, has_h0, want_state, n_cast):
    it = iter(refs)
    q_ref, k_ref, v_ref, r_ref, sm_ref, wa_ref, ba_ref, nw_ref = (next(it) for _ in range(8))
    h0_ref = next(it) if has_h0 else None
    cast_src = [next(it) for _ in range(n_cast)]
    o_ref = next(it)
    st_ref = next(it) if want_state else None
    _cast_blocks(cast_src, [next(it) for _ in range(n_cast)])
    (la_scr, qk_scr, qg_scr, ks_scr, vb_scr, sc_scr, kv_scr, dec_scr,
     sprev_scr) = (next(it) for _ in range(9))

    nc = seq_len // CHUNK
    nct = n_sub * nc
    dk = GLA_DK
    lower, upper = _tri_masks()
    lower_b = jnp.where(lower, 1.0, 0.0).astype(BF16)
    upper_b = jnp.where(upper, 1.0, 0.0).astype(BF16)
    unroll = min(GLA_UNROLL, nct)
    rows = lambda c: pl.ds(pl.multiple_of(c * CHUNK, CHUNK), CHUNK)

    pre = _dot(sm_ref[...].astype(BF16), wa_ref[...].astype(BF16)) + ba_ref[...]
    la_scr[...] = _log_sigmoid(pre) * (LOG2_E / GLA_TAU)

    def prep(c, carry):
        sl = rows(c)
        qc = q_ref[sl, :] * (GLA_DK ** -0.5)
        kc = k_ref[sl, :]
        vb_scr[sl, :] = v_ref[sl, :].astype(BF16)
        la = la_scr[sl, :]
        hi = la.astype(BF16)
        lo = (la - hi.astype(F32)).astype(BF16)

        def cumulative(tri_b, a):
            p = _dot(tri_b, jnp.concatenate([hi[:, a:a + dk], lo[:, a:a + dk]], axis=1))
            return p[:, 0:dk] + p[:, dk:2 * dk]

        g_f = cumulative(lower_b, 0)
        g_b = cumulative(upper_b, dk)
        mid_f = g_f[CHUNK // 2:CHUNK // 2 + 1, :]
        mid_b = g_b[CHUNK - 1 - CHUNK // 2:CHUNK - CHUNK // 2, :]
        end_f = g_f[CHUNK - 1:CHUNK, :]
        end_b = g_b[0:1, :]
        qk_scr[sl, 0:dk] = (qc * jnp.exp2(g_f - mid_f)).astype(BF16)
        qk_scr[sl, dk:2 * dk] = (qc * jnp.exp2(g_b - mid_b)).astype(BF16)
        qk_scr[sl, 2 * dk:3 * dk] = (kc * jnp.exp2(mid_f - g_f)).astype(BF16)
        qk_scr[sl, 3 * dk:4 * dk] = (kc * jnp.exp2(mid_b - g_b)).astype(BF16)
        qg_scr[sl, 0:dk] = (qc * jnp.exp2(g_f)).astype(BF16)
        qg_scr[sl, dk:2 * dk] = (qc * jnp.exp2(g_b)).astype(BF16)
        ks_scr[sl, 0:dk] = (kc * jnp.exp2(end_f - g_f)).astype(BF16)
        ks_scr[sl, dk:2 * dk] = (kc * jnp.exp2(end_b - g_b)).astype(BF16)
        dec = jnp.exp2(jnp.concatenate([end_f, end_b], axis=1))
        dec_scr[c] = jnp.broadcast_to(dec, (SUBLANES, 2 * dk))
        return carry

    lax.fori_loop(0, nct, prep, 0, unroll=unroll)

    def products(c, carry):
        sl = rows(c)
        s_f = _dot_nt(qk_scr[sl, 0:dk], qk_scr[sl, 2 * dk:3 * dk])
        s_b = _dot_nt(qk_scr[sl, dk:2 * dk], qk_scr[sl, 3 * dk:4 * dk])
        sc_scr[sl, :] = (jnp.where(lower, s_f, 0.0) + jnp.where(upper, s_b, 0.0)).astype(BF16)
        kv_scr[c] = _dot_tn(vb_scr[sl, :], ks_scr[sl, :])
        return carry

    lax.fori_loop(0, nct, products, 0, unroll=unroll)

    for b in range(n_sub):
        for d, a in ((0, 0), (1, dk)):
            state0 = h0_ref[b, 0, d, 0].T if has_h0 else jnp.zeros((GLA_DV, dk), F32)

            def recur(i, state, b=b, d=d, a=a):
                c = b * nc + (i if d == 0 else nc - 1 - i)
                sprev_scr[c, :, a:a + dk] = state.astype(BF16)
                return state * dec_scr[c, 0:1, a:a + dk] + kv_scr[c, :, a:a + dk]

            state = lax.fori_loop(0, nc, recur, state0, unroll=min(nc, 4))
            if want_state:
                st_ref[b, 0, d, 0] = state.T

    def finish(c, carry):
        sl = rows(c)
        o = _dot(sc_scr[sl, :], vb_scr[sl, :]) + _dot_nt(qg_scr[sl, :], sprev_scr[c])
        o_ref[sl, :] = (_rmsnorm(o, nw_ref[...]) * _silu(r_ref[sl, :])).astype(o_ref.dtype)
        return carry

    lax.fori_loop(0, nct, finish, 0, unroll=unroll)


def _gla_call(proj, wa_cat, ba_cat, norm_w, h0, layer, *, n_seq, seq_len, want_state, cast=()):
    n_sub = max(1, SCAN_BLOCK_TOKENS // seq_len)
    L = n_sub * seq_len
    nc = L // CHUNK
    blk = lambda width, base: pl.BlockSpec((L, width), lambda s, h: (s, base // width + h))
    in_specs = [blk(GLA_DK, COL_Q), blk(GLA_DK, COL_K), blk(GLA_DV, COL_V), blk(GLA_DV, COL_R),
                pl.BlockSpec((L, LANES), lambda s, h: (s, COL_SMALL // LANES)),
                pl.BlockSpec((LANES, 2 * GLA_DK), lambda s, h: (0, h)),
                pl.BlockSpec((1, 2 * GLA_DK), lambda s, h: (0, h)),
                pl.BlockSpec((1, GLA_DV), lambda s, h: (0, 0))]
    args = [proj, proj, proj, proj, proj, wa_cat, ba_cat, norm_w]
    state_blk = (n_sub, 1, 2, 1, GLA_DK, GLA_DV)
    if h0 is not None:
        in_specs.append(pl.BlockSpec(state_blk, lambda s, h: (s, layer, 0, h, 0, 0)))
        args.append(h0)
    out_specs = [pl.BlockSpec((L, GLA_DV), lambda s, h: (s, h))]
    out_shape = [jax.ShapeDtypeStruct((n_seq * seq_len, GLA_V), BF16)]
    if want_state:
        out_specs.append(pl.BlockSpec(state_blk, lambda s, h: (s, 0, 0, h, 0, 0)))
        out_shape.append(jax.ShapeDtypeStruct((n_seq, 1, 2, GLA_HEADS, GLA_DK, GLA_DV), F32))
    n_blocks = n_seq // n_sub
    cast_in, cast_out, cast_shape = _cast_plumbing(cast, n_blocks * GLA_HEADS,
                                                   lambda s, h: s * GLA_HEADS + h)
    in_specs += cast_in
    args += list(cast)
    out_specs += cast_out
    out_shape += cast_shape
    body = functools.partial(_gla_body, seq_len=seq_len, n_sub=n_sub, has_h0=h0 is not None,
                             want_state=want_state, n_cast=len(cast))
    return pl.pallas_call(
        body,
        grid=(n_blocks, GLA_HEADS),
        in_specs=in_specs,
        out_specs=out_specs,
        out_shape=out_shape,
        scratch_shapes=[pltpu.VMEM((L, 2 * GLA_DK), F32),
                        pltpu.VMEM((L, 4 * GLA_DK), BF16),
                        pltpu.VMEM((L, 2 * GLA_DK), BF16),
                        pltpu.VMEM((L, 2 * GLA_DK), BF16),
                        pltpu.VMEM((L, GLA_DV), BF16),
                        pltpu.VMEM((L, CHUNK), BF16),
                        pltpu.VMEM((nc, GLA_DV, 2 * GLA_DK), F32),
                        pltpu.VMEM((nc, SUBLANES, 2 * GLA_DK), F32),
                        pltpu.VMEM((nc, GLA_DV, 2 * GLA_DK), BF16)],
        compiler_params=_params(2),
        name="gla",
    )(*args)


def _ssd_body(*refs, seq_len, grid_rows, has_h0, want_state):
    it = iter(refs)
    (xs_ref, b_ref, c_ref, sm_ref, cwx_ref, cwb_ref, cwc_ref, cbx_ref, cbb_ref, cbc_ref,
     par_ref, dsk_ref) = (next(it) for _ in range(12))
    h0_ref = next(it) if has_h0 else None
    y_ref = next(it)
    st_ref = next(it) if want_state else None
    (pad_scr, xs_s, b_s, c_s, dt_scr, xy_scr, tr_scr, rows_scr, sc_scr, ed_scr, xw_scr, dec_scr,
     cs_scr, sprev_scr, state_scr) = (next(it) for _ in range(15))

    L = seq_len
    nc = L // CHUNK
    width = L // grid_rows
    pad = pad_scr.shape[0] - L
    pad //= 2
    grp = pl.program_id(1)

    def conv_into(src_ref, cw_ref, cb_ref, dst_ref):
        ch = src_ref.shape[1]
        pad_scr[0:pad, 0:ch] = jnp.zeros((pad, ch), F32)
        pad_scr[pad + L:pad + L + pad, 0:ch] = jnp.zeros((pad, ch), F32)
        pad_scr[pad:pad + L, 0:ch] = src_ref[...]
        rc = min(2 * CHUNK, L)
        col = lax.broadcasted_iota(jnp.int32, (rc, ch), 0) % width
        di_taps = range(CONV_K) if grid_rows > 1 else (CONV_K // 2,)
        for r0 in range(0, L, rc):
            acc = jnp.broadcast_to(cb_ref[...], (rc, ch))
            for dj in range(CONV_K):
                inner = None
                for di in di_taps:
                    off = pad + r0 + (di - 1) * width + (dj - 1)
                    term = cw_ref[di, dj:dj + 1, :] * pad_scr[off:off + rc, 0:ch]
                    inner = term if inner is None else inner + term
                if grid_rows > 1 and dj == 0:
                    inner = jnp.where(col >= 1, inner, 0.0)
                if grid_rows > 1 and dj == CONV_K - 1:
                    inner = jnp.where(col <= width - 2, inner, 0.0)
                acc = acc + inner
            dst_ref[r0:r0 + rc, :] = _silu(acc).astype(dst_ref.dtype)

    conv_into(xs_ref, cwx_ref, cbx_ref, xs_s)
    conv_into(b_ref, cwb_ref, cbb_ref, b_s)
    conv_into(c_ref, cwc_ref, cbc_ref, c_s)

    gw = GROUP_W
    unroll = min(SSD_UNROLL, nc)
    rows = lambda c: pl.ds(pl.multiple_of(c * CHUNK, CHUNK), CHUNK)
    bias_row = par_ref[0:1, :]
    a_row = -jnp.exp(par_ref[1:2, :]) * par_ref[2:3, :] * LOG2_E
    lower, upper = _tri_masks()
    lower_b = jnp.where(lower, 1.0, 0.0).astype(BF16)
    upper_b = jnp.where(upper, 1.0, 0.0).astype(BF16)
    lane = lax.broadcasted_iota(jnp.int32, (CHUNK, LANES), 1)
    dt_lanes = (lane >= SM_DTF) & (lane < SM_DTF + TERM_STRIDE)
    src = lax.broadcasted_iota(jnp.int32, (LANES, 2 * gw), 0)
    dst = lax.broadcasted_iota(jnp.int32, (LANES, 2 * gw), 1)
    dst_slot = (grp * HEADS_PER_GROUP + (lax.shift_right_logical(dst, 6) & (HEADS_PER_GROUP - 1))
                + jnp.where(dst >= gw, SSD_HEADS, 0))
    expand = jnp.where((src >= SM_DTF) & ((src & (TERM_STRIDE - 1)) == dst_slot),
                       1.0, 0.0).astype(BF16)
    t_idx = lax.broadcasted_iota(jnp.int32, (CHUNK, gw), 0)
    s_idx = lax.broadcasted_iota(jnp.int32, (CHUNK, gw), 1) & (CHUNK - 1)
    diag = t_idx == s_idx
    blk_r = lax.shift_right_logical(lax.broadcasted_iota(jnp.int32, (MXU_TILE, MXU_TILE), 0), 6)
    blk_c = lax.shift_right_logical(lax.broadcasted_iota(jnp.int32, (MXU_TILE, MXU_TILE), 1), 6)
    same_head = blk_r == blk_c

    dt_scr[...] = _softplus(sm_ref[...] + bias_row)

    def place3(v):
        hi = v.astype(BF16).astype(F32)
        rest = v - hi
        mid = rest.astype(BF16).astype(F32)
        lo = rest - mid
        keep = lambda t: jnp.where(dt_lanes, t, 0.0)
        out = (keep(hi) + pltpu.roll(keep(mid), TERM_STRIDE, axis=1)
               + pltpu.roll(keep(lo), 2 * TERM_STRIDE, axis=1))
        return out.astype(BF16)

    def cumulate(c, carry):
        sl = rows(c)
        dt = dt_scr[sl, :]
        da = dt * a_row
        hi = da.astype(BF16)
        both = jnp.concatenate([hi, (da - hi.astype(F32)).astype(BF16)], axis=1)
        pf = _dot(lower_b, both)
        pb = _dot(upper_b, both)
        cum = jnp.where(lane < SM_DTB, pf[:, 0:LANES] + pf[:, LANES:], pb[:, 0:LANES] + pb[:, LANES:])
        xy_scr[sl, 0:LANES] = place3(cum)
        xy_scr[sl, LANES:2 * LANES] = place3(dt)
        tr_scr[c, 0] = cum.T
        tr_scr[c, 1] = dt.T
        head0 = pl.multiple_of(SM_DTF + grp * HEADS_PER_GROUP, SUBLANES)
        for q, (k, d) in enumerate(((0, 0), (0, 1), (1, 0), (1, 1))):
            slab = tr_scr[c, k, pl.ds(head0 + d * SSD_HEADS, HEADS_PER_GROUP), :]
            row = jnp.concatenate(
                [jnp.broadcast_to(slab[h:h + 1, :], (SUBLANES, CHUNK))
                 for h in range(HEADS_PER_GROUP)], axis=1)
            rows_scr[c, :, q * gw:(q + 1) * gw] = row
        return carry

    lax.fori_loop(0, nc, cumulate, 0, unroll=unroll)

    def weights(c, carry):
        sl = rows(c)
        xc = xs_s[sl, :]
        cum_e = _dot(xy_scr[sl, 0:LANES], expand)
        dt_e = _dot(xy_scr[sl, LANES:2 * LANES], expand)
        cb = _dot_nt(c_s[sl, :], jnp.concatenate([b_s[sl, :]] * HEADS_PER_GROUP, axis=0))
        segs, dt_rows, dec_rows = [], [], []
        for d, (a, last_i) in enumerate(((0, CHUNK - 1), (gw, 0))):
            ce = cum_e[:, a:a + gw]
            de = dt_e[:, a:a + gw]
            segs.append(ce - rows_scr[c, 0:1, d * gw:(d + 1) * gw])
            dt_rows.append(rows_scr[c, 0:1, (2 + d) * gw:(3 + d) * gw])
            cum_last = ce[last_i:last_i + 1, :]
            ed_scr[sl, a:a + gw] = jnp.exp2(ce)
            xw_scr[sl, a:a + gw] = (xc * (jnp.exp2(cum_last - ce) * de)).astype(BF16)
            dec_rows.append(jnp.exp2(cum_last))
        fwd = t_idx >= s_idx
        w = (jnp.exp2(jnp.where(fwd, segs[0], segs[1])) * jnp.where(fwd, dt_rows[0], dt_rows[1])
             + jnp.where(diag, dt_rows[1], 0.0))
        sc_scr[sl, :] = (cb * w).astype(BF16)
        dec_scr[c] = jnp.broadcast_to(jnp.concatenate(dec_rows, axis=1), (SUBLANES, 2 * gw))
        return carry

    lax.fori_loop(0, nc, weights, 0, unroll=unroll)

    def products(c, carry):
        sl = rows(c)
        xc = xs_s[sl, :]
        xb = xc.astype(BF16)
        parts = []
        for j in range(gw // MXU_TILE):
            xh = xb[:, j * MXU_TILE:(j + 1) * MXU_TILE]
            rep = jnp.concatenate([xh] * (MXU_TILE // SSD_HEAD_DIM), axis=0)
            bd = jnp.where(same_head, rep, jnp.zeros_like(rep))
            parts.append(_dot(sc_scr[sl, j * MXU_TILE:(j + 1) * MXU_TILE], bd))
        y_ref[sl, :] = jnp.concatenate(parts, axis=1) + xc * dsk_ref[...]
        cs_scr[c] = _dot_tn(b_s[sl, :], xw_scr[sl, :])
        return carry

    lax.fori_loop(0, nc, products, 0, unroll=unroll)

    pair_w = 2 * SSD_HEAD_DIM
    if has_h0:
        for d in range(2):
            for j in range(HEADS_PER_GROUP // 2):
                pair = jnp.concatenate([h0_ref[0, 0, d, 2 * j], h0_ref[0, 0, d, 2 * j + 1]], axis=0)
                state_scr[:, d * gw + j * pair_w:d * gw + (j + 1) * pair_w] = pair.T
    else:
        state_scr[...] = jnp.zeros_like(state_scr)

    for d in range(2):
        for a in range(d * gw, (d + 1) * gw, MXU_TILE):

            def recur(i, state, d=d, a=a):
                c = i if d == 0 else nc - 1 - i
                sprev_scr[c, :, a:a + MXU_TILE] = state.astype(BF16)
                return (state * dec_scr[c, 0:1, a:a + MXU_TILE]
                        + cs_scr[c, :, a:a + MXU_TILE])

            state_scr[:, a:a + MXU_TILE] = lax.fori_loop(
                0, nc, recur, state_scr[:, a:a + MXU_TILE], unroll=min(nc, 4))

    def finish(c, carry):
        sl = rows(c)
        yi = _dot(c_s[sl, :], sprev_scr[c]) * ed_scr[sl, :]
        y_ref[sl, :] += yi[:, 0:gw] + yi[:, gw:2 * gw]
        return carry

    lax.fori_loop(0, nc, finish, 0, unroll=unroll)

    if want_state:
        for d in range(2):
            for j in range(HEADS_PER_GROUP // 2):
                pair = state_scr[:, d * gw + j * pair_w:d * gw + (j + 1) * pair_w].T
                st_ref[0, 0, d, 2 * j] = pair[0:SSD_HEAD_DIM, :]
                st_ref[0, 0, d, 2 * j + 1] = pair[SSD_HEAD_DIM:pair_w, :]


def _ssd_call(proj, conv_w, conv_b, par, dsk, h0, layer, *, n_seq, seq_len, grid_rows,
              want_state):
    L = seq_len
    x_base = COL_XBC
    b_base = COL_XBC + SSD_INNER
    c_base = b_base + SSD_BC
    in_specs = [pl.BlockSpec((L, GROUP_W), lambda s, g: (s, x_base // GROUP_W + g)),
                pl.BlockSpec((L, SSD_STATE), lambda s, g: (s, b_base // SSD_STATE + g)),
                pl.BlockSpec((L, SSD_STATE), lambda s, g: (s, c_base // SSD_STATE + g)),
                pl.BlockSpec((L, LANES), lambda s, g: (s, COL_SMALL // LANES)),
                pl.BlockSpec((CONV_K, CONV_K, GROUP_W), lambda s, g: (0, 0, g)),
                pl.BlockSpec((CONV_K, CONV_K, SSD_STATE),
                             lambda s, g: (0, 0, SSD_INNER // SSD_STATE + g)),
                pl.BlockSpec((CONV_K, CONV_K, SSD_STATE),
                             lambda s, g: (0, 0, (SSD_INNER + SSD_BC) // SSD_STATE + g)),
                pl.BlockSpec((1, GROUP_W), lambda s, g: (0, g)),
                pl.BlockSpec((1, SSD_STATE), lambda s, g: (0, SSD_INNER // SSD_STATE + g)),
                pl.BlockSpec((1, SSD_STATE),
                             lambda s, g: (0, (SSD_INNER + SSD_BC) // SSD_STATE + g)),
                pl.BlockSpec((SUBLANES, LANES), lambda s, g: (0, 0)),
                pl.BlockSpec((1, GROUP_W), lambda s, g: (0, g))]
    args = [proj, proj, proj, proj, conv_w, conv_w, conv_w, conv_b, conv_b, conv_b, par, dsk]
    state_blk = (1, 1, 2, HEADS_PER_GROUP, SSD_HEAD_DIM, SSD_STATE)
    if h0 is not None:
        in_specs.append(pl.BlockSpec(state_blk, lambda s, g: (s, layer, 0, g, 0, 0)))
        args.append(h0)
    out_specs = [pl.BlockSpec((L, GROUP_W), lambda s, g: (s, g))]
    out_shape = [jax.ShapeDtypeStruct((n_seq * L, SSD_INNER), F32)]
    if want_state:
        out_specs.append(pl.BlockSpec(state_blk, lambda s, g: (s, 0, 0, g, 0, 0)))
        out_shape.append(jax.ShapeDtypeStruct(
            (n_seq, 1, 2, SSD_HEADS, SSD_HEAD_DIM, SSD_STATE), F32))
    nc = L // CHUNK
    conv_pad = (L // grid_rows + SUBLANES) if grid_rows > 1 else SUBLANES
    body = functools.partial(_ssd_body, seq_len=L, grid_rows=grid_rows,
                             has_h0=h0 is not None, want_state=want_state)
    return pl.pallas_call(
        body,
        grid=(n_seq, SSD_GROUPS),
        in_specs=in_specs,
        out_specs=out_specs,
        out_shape=out_shape,
        scratch_shapes=[pltpu.VMEM((L + 2 * conv_pad, GROUP_W), F32),
                        pltpu.VMEM((L, GROUP_W), F32),
                        pltpu.VMEM((L, SSD_STATE), BF16),
                        pltpu.VMEM((L, SSD_STATE), BF16),
                        pltpu.VMEM((L, LANES), F32),
                        pltpu.VMEM((L, 2 * LANES), BF16),
                        pltpu.VMEM((nc, 2, LANES, CHUNK), F32),
                        pltpu.VMEM((nc, SUBLANES, 4 * GROUP_W), F32),
                        pltpu.VMEM((L, GROUP_W), BF16),
                        pltpu.VMEM((L, 2 * GROUP_W), F32),
                        pltpu.VMEM((L, 2 * GROUP_W), BF16),
                        pltpu.VMEM((nc, SUBLANES, 2 * GROUP_W), F32),
                        pltpu.VMEM((nc, SSD_STATE, 2 * GROUP_W), F32),
                        pltpu.VMEM((nc, SSD_STATE, 2 * GROUP_W), BF16),
                        pltpu.VMEM((SSD_STATE, 2 * GROUP_W), F32)],
        compiler_params=_params(2),
        name="ssd",
    )(*args)


def _proj_pieces(w_all):
    o_z = COL_Z + 2 * GLA_LOWRANK
    o_dt = o_z + SSD_INNER + SSD_CONV_DIM
    w_small = jnp.pad(jnp.concatenate([w_all[COL_Z:o_z], w_all[o_dt:]], axis=0),
                      ((0, LANES - 2 * GLA_LOWRANK - 2 * SSD_HEADS), (0, 0)))
    return {"w_all": w_all, "w_zx": w_all[o_z:o_dt], "w_small": w_small}


def _layer_path(x, mod, lw, raw, h0_gla, h0_ssd, layer, *, n_seq, seq_len, grid_rows,
                want_state, final_w):
    if raw is not None:
        x1, w_all = _ffn_call(x, mod, lw["norm_ffn1"], lw["ffn1_w_in"], lw["ffn1_w_out"],
                              sub=0, seq_len=seq_len, cast=(raw["w_in"],))
        lw.update(_proj_pieces(w_all))
    else:
        x1 = _ffn_call(x, mod, lw["norm_ffn1"], lw["ffn1_w_in"], lw["ffn1_w_out"],
                       sub=0, seq_len=seq_len)
    proj = _proj_call(x1, mod, lw["norm_mix"], lw["w_all"], lw["w_zx"], lw["w_small"],
                      seq_len=seq_len)
    late = ("ffn2_w_in", "ffn2_w_out", "w_out")
    gla_out = _gla_call(proj, lw["wa_cat"], lw["ba_cat"], lw["gla_norm_w"], h0_gla, layer,
                        n_seq=n_seq, seq_len=seq_len, want_state=want_state,
                        cast=tuple(raw[k] for k in late) if raw is not None else ())
    if raw is not None:
        lw.update(zip(late, gla_out[-len(late):]))
    ssd_out = _ssd_call(proj, lw["conv_w"], lw["conv_b"], lw["ssd_par"], lw["d_skip_row"], h0_ssd,
                        layer, n_seq=n_seq, seq_len=seq_len, grid_rows=grid_rows,
                        want_state=want_state)
    o, y = gla_out[0], ssd_out[0]
    out = _ffn_call(x1, mod, lw["norm_ffn2"], lw["ffn2_w_in"], lw["ffn2_w_out"],
                    sub=2, seq_len=seq_len, mix=(o, y, proj, lw["ssd_norm_w"], lw["w_out"]),
                    final_w=final_w)
    if want_state:
        return out, gla_out[1], ssd_out[1]
    return out, None, None


def kernel(x_prompt, x_sample, state_gla, state_ssd, c, c_ctx, norm_ffn1, norm_mix, norm_ffn2, w_mod, b_mod, ffn1_w_in, ffn1_w_out, ffn2_w_in, ffn2_w_out, w_in, gla_w_a2, gla_b_a, gla_norm_w, conv_w, conv_b, dt_bias, a_log, d_skip, ssd_norm_w, w_out, final_norm):
    nb, seq, _ = x_prompt.shape
    db, dseq, _ = x_sample.shape
    grid_rows = dseq // GRID_W
    xp = x_prompt.reshape(nb * seq, D_MODEL)
    xs = x_sample.reshape(db * dseq, D_MODEL)
    row = lambda v: v.reshape(1, -1)
    gla_states, ssd_states = [], []
    for i in range(DEPTH):
        last = i == DEPTH - 1
        n_rows = -(-(db + 1) // SUBLANES) * SUBLANES
        cc = jnp.concatenate([c, c_ctx[None, :], jnp.zeros((n_rows - db - 1, D_MODEL), F32)], 0)
        mod = _mod_call(cc, w_mod[i], row(b_mod[i])).reshape(n_rows, N_MOD, D_MODEL)
        mod_lat, mod_ctx = mod[:db], mod[db:db + 1]

        wa_cat = jnp.stack(
            [jnp.pad(gla_w_a2[i, d].reshape(GLA_LOWRANK, GLA_HEADS, GLA_DK),
                     ((lo, LANES - lo - GLA_LOWRANK), (0, 0), (0, 0)))
             for d, lo in ((0, SM_AF), (1, SM_AB))], axis=2).reshape(LANES, 2 * GLA_QK)
        ba_cat = jnp.stack([gla_b_a[i, 0].reshape(GLA_HEADS, GLA_DK),
                            gla_b_a[i, 1].reshape(GLA_HEADS, GLA_DK)], axis=1).reshape(1, 2 * GLA_QK)
        ssd_par = jnp.pad(
            jnp.stack([dt_bias[i].reshape(-1), a_log[i].reshape(-1), jnp.ones((2 * SSD_HEADS,), F32)]),
            ((0, SUBLANES - 3), (SM_DTF, LANES - SM_DTF - 2 * SSD_HEADS)))
        lw = {
            "norm_ffn1": row(norm_ffn1[i]), "norm_mix": row(norm_mix[i]),
            "norm_ffn2": row(norm_ffn2[i]),
            "ffn1_w_in": ffn1_w_in[i].astype(BF16), "ffn1_w_out": ffn1_w_out[i].astype(BF16),
            "wa_cat": wa_cat, "ba_cat": ba_cat,
            "gla_norm_w": row(gla_norm_w[i]), "conv_w": conv_w[i], "conv_b": row(conv_b[i]),
            "ssd_par": ssd_par, "d_skip_row": row(jnp.repeat(d_skip[i], SSD_HEAD_DIM)),
            "ssd_norm_w": row(ssd_norm_w[i]),
        }
        raw = {"w_in": jnp.swapaxes(w_in[i], 0, 1), "ffn2_w_in": ffn2_w_in[i], "ffn2_w_out": ffn2_w_out[i],
               "w_out": w_out[i]}
        fw = row(final_norm) if last else None

        xp, sg, ss = _layer_path(xp, mod_ctx, lw, raw, None, None, i, n_seq=nb, seq_len=seq,
                                 grid_rows=1, want_state=True, final_w=fw)
        xs, _, _ = _layer_path(xs, mod_lat, lw, None, state_gla, state_ssd, i, n_seq=db,
                               seq_len=dseq, grid_rows=grid_rows, want_state=False, final_w=fw)
        gla_states.append(sg)
        ssd_states.append(ss)
    y_prompt = xp.reshape(nb, seq, D_MODEL)
    y_sample = xs.reshape(db, dseq, D_MODEL)
    return (y_prompt, y_sample, jnp.concatenate(gla_states, axis=1),
            jnp.concatenate(ssd_states, axis=1))
```

```python
import functools

import jax
import jax.numpy as jnp
from jax import lax
from jax.experimental import pallas as pl
from jax.experimental.pallas import tpu as pltpu

F32 = jnp.float32
BF16 = jnp.bfloat16

D_MODEL = 1024
DEPTH = 1
GRID_W = 64
CHUNK = 64
EPS = 1e-6
N_MOD = 9
D_FF = 2816
GLA_HEADS = 4
GLA_DK = 128
GLA_DV = 256
GLA_LOWRANK = 16
GLA_TAU = 16.0
GLA_QK = GLA_HEADS * GLA_DK
GLA_V = GLA_HEADS * GLA_DV
SSD_HEADS = 16
SSD_HEAD_DIM = 64
SSD_GROUPS = 2
SSD_STATE = 128
SSD_INNER = SSD_HEADS * SSD_HEAD_DIM
SSD_BC = SSD_GROUPS * SSD_STATE
SSD_CONV_DIM = SSD_INNER + 2 * SSD_BC
CONV_K = 3
D_MIX = GLA_V + SSD_INNER

LOG2_E = 1.4426950408889634
LANES = 128
SUBLANES = 8
BF16_SUBLANES = 16
VMEM_LIMIT_BYTES = 56 * 1024 * 1024

COL_Q = 0
COL_K = COL_Q + GLA_QK
COL_V = COL_K + GLA_QK
COL_R = COL_V + GLA_V
COL_Z = COL_R + GLA_V
HEAD_K = GLA_DK
HEAD_V = 2 * GLA_DK
HEAD_R = HEAD_V + GLA_DV
HEAD_W = HEAD_R + GLA_DV
GRP_B = SSD_INNER // SSD_GROUPS
GRP_C = GRP_B + SSD_STATE
GRP_W = GRP_C + SSD_STATE
ZS_SMALL = SSD_INNER
ZS_W = ZS_SMALL + LANES
SM_AF = 0
SM_AB = SM_AF + GLA_LOWRANK
SM_DTF = SM_AB + GLA_LOWRANK
SM_DTB = SM_DTF + SSD_HEADS
HEADS_PER_GROUP = SSD_HEADS // SSD_GROUPS
GROUP_W = HEADS_PER_GROUP * SSD_HEAD_DIM

TOKEN_TILE = 512
GLA_UNROLL = 16
SSD_UNROLL = 16
MXU_TILE = 256
TERM_STRIDE = 2 * SSD_HEADS
SCAN_BLOCK_TOKENS = 1024


def _dot(a, b):
    return jnp.dot(a, b, preferred_element_type=F32)


def _dot_nt(a, b):
    return lax.dot_general(a, b, (((1,), (1,)), ((), ())), preferred_element_type=F32)


def _dot_tn(a, b):
    return lax.dot_general(a, b, (((0,), (0,)), ((), ())), preferred_element_type=F32)


def _silu(x):
    return x * jax.nn.sigmoid(x)


def _log1p_exp_neg_abs(x):
    return jnp.log(1.0 + jnp.exp(-jnp.abs(x)))


def _softplus(x):
    return jnp.maximum(x, 0.0) + _log1p_exp_neg_abs(x)


def _log_sigmoid(x):
    return jnp.minimum(x, 0.0) - _log1p_exp_neg_abs(x)


def _rmsnorm(x, w):
    ms = jnp.mean(x * x, axis=-1, keepdims=True)
    return x * lax.rsqrt(ms + EPS) * w


def _resident(shape):
    nd = len(shape)
    return pl.BlockSpec(shape, lambda *_: (0,) * nd, pipeline_mode=pl.Buffered(1))


def _params(n_axes):
    return pltpu.CompilerParams(dimension_semantics=("arbitrary",) * n_axes,
                                vmem_limit_bytes=VMEM_LIMIT_BYTES)


def _cast_plumbing(cast, n_steps, step_of):
    in_specs, out_specs, out_shape = [], [], []
    for w in cast:
        rows, cols = w.shape
        per_step = -(-rows // n_steps)
        per_step = -(-per_step // BF16_SUBLANES) * BF16_SUBLANES
        blk = pl.BlockSpec((per_step, cols), lambda *g: (step_of(*g), 0))
        in_specs.append(blk)
        out_specs.append(blk)
        out_shape.append(jax.ShapeDtypeStruct((rows, cols), BF16))
    return in_specs, out_specs, out_shape


def _cast_blocks(src_refs, dst_refs):
    for src_ref, dst_ref in zip(src_refs, dst_refs):
        dst_ref[...] = src_ref[...].astype(dst_ref.dtype)


def _mod_body(c_ref, w_ref, b_ref, out_ref):
    a = _silu(c_ref[...]).astype(BF16)
    out_ref[...] = _dot(a, w_ref[...].astype(BF16)) + b_ref[...]


def _mod_call(cc, w_mod, b_mod):
    n_rows = cc.shape[0]
    tn = D_MODEL
    return pl.pallas_call(
        _mod_body,
        grid=(N_MOD * D_MODEL // tn,),
        in_specs=[pl.BlockSpec((n_rows, D_MODEL), lambda j: (0, 0)),
                  pl.BlockSpec((D_MODEL, tn), lambda j: (0, j)),
                  pl.BlockSpec((1, tn), lambda j: (0, j))],
        out_specs=pl.BlockSpec((n_rows, tn), lambda j: (0, j)),
        out_shape=jax.ShapeDtypeStruct((n_rows, N_MOD * D_MODEL), F32),
        compiler_params=_params(1),
        name="mod",
    )(cc, w_mod, b_mod)


def _ffn_body(*refs, sub, has_mix, has_final, n_cast):
    it = iter(refs)
    x_ref, mod_ref, nw_ref, win_ref, wout_ref = (next(it) for _ in range(5))
    if has_mix:
        o_ref, y_ref, z_ref, snw_ref, wo_ref = (next(it) for _ in range(5))
    if has_final:
        fn_ref = next(it)
    cast_src = [next(it) for _ in range(n_cast)]
    out_ref = next(it)
    _cast_blocks(cast_src, [next(it) for _ in range(n_cast)])

    x = x_ref[...]
    if has_mix:
        g2 = mod_ref[0, 5:6, :]
        yn = _rmsnorm(y_ref[...] * _silu(z_ref[...]), snw_ref[...])
        m = _dot(o_ref[...], wo_ref[:GLA_V, :]) + _dot(yn.astype(BF16), wo_ref[GLA_V:, :])
        x = x + g2 * m
    sh = mod_ref[0, 3 * sub:3 * sub + 1, :]
    sc = mod_ref[0, 3 * sub + 1:3 * sub + 2, :]
    gate = mod_ref[0, 3 * sub + 2:3 * sub + 3, :]
    h = (_rmsnorm(x, nw_ref[...]) * (1.0 + sc) + sh).astype(BF16)
    g = _dot(h, win_ref[:, :D_FF])
    u = _dot(h, win_ref[:, D_FF:])
    act = (_silu(g) * u).astype(BF16)
    x = x + (0.5 * gate) * _dot(act, wout_ref[...])
    if has_final:
        x = _rmsnorm(x, fn_ref[...])
    out_ref[...] = x


def _ffn_call(x, mod, norm_w, w_in, w_out, *, sub, seq_len, mix=None, final_w=None, cast=()):
    m_tok = x.shape[0]
    tm = TOKEN_TILE
    tiles_per_seq = seq_len // tm
    shared_mod = mod.shape[0] == 1
    mod_map = (lambda i: (0, 0, 0)) if shared_mod else (lambda i: (i // tiles_per_seq, 0, 0))
    tok = lambda width: pl.BlockSpec((tm, width), lambda i: (i, 0))
    in_specs = [tok(D_MODEL), pl.BlockSpec((1, N_MOD, D_MODEL), mod_map),
                _resident((1, D_MODEL)), _resident(w_in.shape), _resident(w_out.shape)]
    args = [x, mod, norm_w, w_in, w_out]
    if mix is not None:
        o, y, proj, ssd_norm_w, w_mix_out = mix
        in_specs += [tok(GLA_V), tok(SSD_INNER),
                     pl.BlockSpec((tm, SSD_INNER), lambda i: (i, 0)),
                     _resident((1, SSD_INNER)), _resident(w_mix_out.shape)]
        args += [o, y, proj, ssd_norm_w, w_mix_out]
    if final_w is not None:
        in_specs.append(_resident((1, D_MODEL)))
        args.append(final_w)
    n_steps = m_tok // tm
    cast_in, cast_out, cast_shape = _cast_plumbing(cast, n_steps, lambda i: i)
    in_specs += cast_in
    args += list(cast)
    out_specs = [tok(D_MODEL)] + cast_out
    out_shape = [jax.ShapeDtypeStruct((m_tok, D_MODEL), F32)] + cast_shape
    body = functools.partial(_ffn_body, sub=sub, has_mix=mix is not None,
                             has_final=final_w is not None, n_cast=len(cast))
    outs = pl.pallas_call(
        body,
        grid=(n_steps,),
        in_specs=in_specs,
        out_specs=out_specs,
        out_shape=out_shape,
        compiler_params=_params(1),
        name="ffn_mix" if mix is not None else "ffn",
    )(*args)
    return outs if cast else outs[0]


def _proj_body(x_ref, mod_ref, nw_ref, wa_ref, wb_ref, wc_ref, heads_ref, groups_ref, zs_ref):
    sh = mod_ref[0, 3:4, :]
    sc = mod_ref[0, 4:5, :]
    h = (_rmsnorm(x_ref[...], nw_ref[...]) * (1.0 + sc) + sh).astype(BF16)
    qkvr = _dot_nt(h, wa_ref[...])
    for hd in range(GLA_HEADS):
        for dst, src, wid in ((0, COL_Q + hd * GLA_DK, GLA_DK), (HEAD_K, COL_K + hd * GLA_DK, GLA_DK),
                              (HEAD_V, COL_V + hd * GLA_DV, GLA_DV), (HEAD_R, COL_R + hd * GLA_DV, GLA_DV)):
            heads_ref[hd, :, dst:dst + wid] = qkvr[:, src:src + wid]
    zx = _dot_nt(h, wb_ref[...])
    zs_ref[:, 0:ZS_SMALL] = zx[:, 0:SSD_INNER]
    for g in range(SSD_GROUPS):
        for dst, src, wid in ((0, g * GRP_B, GRP_B), (GRP_B, SSD_INNER + g * SSD_STATE, SSD_STATE),
                              (GRP_C, SSD_INNER + SSD_BC + g * SSD_STATE, SSD_STATE)):
            groups_ref[g, :, dst:dst + wid] = zx[:, SSD_INNER + src:SSD_INNER + src + wid]
    zs_ref[:, ZS_SMALL:ZS_W] = _dot_nt(h, wc_ref[...])


def _proj_call(x, mod, norm_w, w_all, w_zx, w_small, *, seq_len):
    m_tok = x.shape[0]
    tm = TOKEN_TILE
    tiles_per_seq = seq_len // tm
    shared_mod = mod.shape[0] == 1
    mod_map = (lambda i: (0, 0, 0)) if shared_mod else (lambda i: (i // tiles_per_seq, 0, 0))
    return pl.pallas_call(
        _proj_body,
        grid=(m_tok // tm,),
        in_specs=[pl.BlockSpec((tm, D_MODEL), lambda i: (i, 0)),
                  pl.BlockSpec((1, N_MOD, D_MODEL), mod_map),
                  _resident((1, D_MODEL)), _resident((COL_Z, D_MODEL)),
                  _resident(w_zx.shape), _resident(w_small.shape)],
        out_specs=[pl.BlockSpec((GLA_HEADS, tm, HEAD_W), lambda i: (0, i, 0)),
                   pl.BlockSpec((SSD_GROUPS, tm, GRP_W), lambda i: (0, i, 0)),
                   pl.BlockSpec((tm, ZS_W), lambda i: (i, 0))],
        out_shape=[jax.ShapeDtypeStruct((GLA_HEADS, m_tok, HEAD_W), F32),
                   jax.ShapeDtypeStruct((SSD_GROUPS, m_tok, GRP_W), F32),
                   jax.ShapeDtypeStruct((m_tok, ZS_W), F32)],
        compiler_params=_params(1),
        name="proj",
    )(x, mod, norm_w, w_all, w_zx, w_small)


def _tri_masks():
    row = lax.broadcasted_iota(jnp.int32, (CHUNK, CHUNK), 0)
    col = lax.broadcasted_iota(jnp.int32, (CHUNK, CHUNK), 1)
    return row >= col, row <= col


def _gla_body(*refs, seq_len, n_sub, has_h0, want_state, n_cast):
    it = iter(refs)
    hd_ref, sm_ref, wa_ref, ba_ref, nw_ref = (next(it) for _ in range(5))
    h0_ref = next(it) if has_h0 else None
    cast_src = [next(it) for _ in range(n_cast)]
    o_ref = next(it)
    st_ref = next(it) if want_state else None
    _cast_blocks(cast_src, [next(it) for _ in range(n_cast)])
    (la_scr, qk_scr, qg_scr, ks_scr, vb_scr, sc_scr, kv_scr, dec_scr,
     sprev_scr) = (next(it) for _ in range(9))

    nc = seq_len // CHUNK
    nct = n_sub * nc
    dk = GLA_DK
    lower, upper = _tri_masks()
    lower_b = jnp.where(lower, 1.0, 0.0).astype(BF16)
    upper_b = jnp.where(upper, 1.0, 0.0).astype(BF16)
    unroll = min(GLA_UNROLL, nct)
    rows = lambda c: pl.ds(pl.multiple_of(c * CHUNK, CHUNK), CHUNK)

    pre = _dot(sm_ref[...].astype(BF16), wa_ref[...].astype(BF16)) + ba_ref[...]
    la_scr[...] = _log_sigmoid(pre) * (LOG2_E / GLA_TAU)

    def prep(c, carry):
        sl = rows(c)
        qc = hd_ref[0, sl, 0:HEAD_K] * (GLA_DK ** -0.5)
        kc = hd_ref[0, sl, HEAD_K:HEAD_V]
        vb_scr[sl, :] = hd_ref[0, sl, HEAD_V:HEAD_R].astype(BF16)
        la = la_scr[sl, :]
        hi = la.astype(BF16)
        lo = (la - hi.astype(F32)).astype(BF16)

        def cumulative(tri_b, a):
            p = _dot(tri_b, jnp.concatenate([hi[:, a:a + dk], lo[:, a:a + dk]], axis=1))
            return p[:, 0:dk] + p[:, dk:2 * dk]

        g_f = cumulative(lower_b, 0)
        g_b = cumulative(upper_b, dk)
        mid_f = g_f[CHUNK // 2:CHUNK // 2 + 1, :]
        mid_b = g_b[CHUNK - 1 - CHUNK // 2:CHUNK - CHUNK // 2, :]
        end_f = g_f[CHUNK - 1:CHUNK, :]
        end_b = g_b[0:1, :]
        qk_scr[sl, 0:dk] = (qc * jnp.exp2(g_f - mid_f)).astype(BF16)
        qk_scr[sl, dk:2 * dk] = (qc * jnp.exp2(g_b - mid_b)).astype(BF16)
        qk_scr[sl, 2 * dk:3 * dk] = (kc * jnp.exp2(mid_f - g_f)).astype(BF16)
        qk_scr[sl, 3 * dk:4 * dk] = (kc * jnp.exp2(mid_b - g_b)).astype(BF16)
        qg_scr[sl, 0:dk] = (qc * jnp.exp2(g_f)).astype(BF16)
        qg_scr[sl, dk:2 * dk] = (qc * jnp.exp2(g_b)).astype(BF16)
        ks_scr[sl, 0:dk] = (kc * jnp.exp2(end_f - g_f)).astype(BF16)
        ks_scr[sl, dk:2 * dk] = (kc * jnp.exp2(end_b - g_b)).astype(BF16)
        dec = jnp.exp2(jnp.concatenate([end_f, end_b], axis=1))
        dec_scr[c] = jnp.broadcast_to(dec, (SUBLANES, 2 * dk))
        return carry

    lax.fori_loop(0, nct, prep, 0, unroll=unroll)

    def products(c, carry):
        sl = rows(c)
        s_f = _dot_nt(qk_scr[sl, 0:dk], qk_scr[sl, 2 * dk:3 * dk])
        s_b = _dot_nt(qk_scr[sl, dk:2 * dk], qk_scr[sl, 3 * dk:4 * dk])
        sc_scr[sl, :] = (jnp.where(lower, s_f, 0.0) + jnp.where(upper, s_b, 0.0)).astype(BF16)
        kv_scr[c] = _dot_tn(vb_scr[sl, :], ks_scr[sl, :])
        return carry

    lax.fori_loop(0, nct, products, 0, unroll=unroll)

    for b in range(n_sub):
        for d, a in ((0, 0), (1, dk)):
            state0 = h0_ref[b, 0, d, 0].T if has_h0 else jnp.zeros((GLA_DV, dk), F32)

            def recur(i, state, b=b, d=d, a=a):
                c = b * nc + (i if d == 0 else nc - 1 - i)
                sprev_scr[c, :, a:a + dk] = state.astype(BF16)
                return state * dec_scr[c, 0:1, a:a + dk] + kv_scr[c, :, a:a + dk]

            state = lax.fori_loop(0, nc, recur, state0, unroll=min(nc, 4))
            if want_state:
                st_ref[b, 0, d, 0] = state.T

    def finish(c, carry):
        sl = rows(c)
        o = _dot(sc_scr[sl, :], vb_scr[sl, :]) + _dot_nt(qg_scr[sl, :], sprev_scr[c])
        o_ref[sl, :] = (_rmsnorm(o, nw_ref[...]) * _silu(hd_ref[0, sl, HEAD_R:HEAD_W])).astype(o_ref.dtype)
        return carry

    lax.fori_loop(0, nct, finish, 0, unroll=unroll)


def _gla_call(heads, zs, wa_cat, ba_cat, norm_w, h0, layer, *, n_seq, seq_len, want_state,
              cast=()):
    n_sub = max(1, SCAN_BLOCK_TOKENS // seq_len)
    L = n_sub * seq_len
    nc = L // CHUNK
    in_specs = [pl.BlockSpec((1, L, HEAD_W), lambda s, h: (h, s, 0)),
                pl.BlockSpec((L, LANES), lambda s, h: (s, ZS_SMALL // LANES)),
                pl.BlockSpec((LANES, 2 * GLA_DK), lambda s, h: (0, h)),
                pl.BlockSpec((1, 2 * GLA_DK), lambda s, h: (0, h)),
                pl.BlockSpec((1, GLA_DV), lambda s, h: (0, 0))]
    args = [heads, zs, wa_cat, ba_cat, norm_w]
    state_blk = (n_sub, 1, 2, 1, GLA_DK, GLA_DV)
    if h0 is not None:
        in_specs.append(pl.BlockSpec(state_blk, lambda s, h: (s, layer, 0, h, 0, 0)))
        args.append(h0)
    out_specs = [pl.BlockSpec((L, GLA_DV), lambda s, h: (s, h))]
    out_shape = [jax.ShapeDtypeStruct((n_seq * seq_len, GLA_V), BF16)]
    if want_state:
        out_specs.append(pl.BlockSpec(state_blk, lambda s, h: (s, 0, 0, h, 0, 0)))
        out_shape.append(jax.ShapeDtypeStruct((n_seq, 1, 2, GLA_HEADS, GLA_DK, GLA_DV), F32))
    n_blocks = n_seq // n_sub
    cast_in, cast_out, cast_shape = _cast_plumbing(cast, n_blocks * GLA_HEADS,
                                                   lambda s, h: s * GLA_HEADS + h)
    in_specs += cast_in
    args += list(cast)
    out_specs += cast_out
    out_shape += cast_shape
    body = functools.partial(_gla_body, seq_len=seq_len, n_sub=n_sub, has_h0=h0 is not None,
                             want_state=want_state, n_cast=len(cast))
    return pl.pallas_call(
        body,
        grid=(n_blocks, GLA_HEADS),
        in_specs=in_specs,
        out_specs=out_specs,
        out_shape=out_shape,
        scratch_shapes=[pltpu.VMEM((L, 2 * GLA_DK), F32),
                        pltpu.VMEM((L, 4 * GLA_DK), BF16),
                        pltpu.VMEM((L, 2 * GLA_DK), BF16),
                        pltpu.VMEM((L, 2 * GLA_DK), BF16),
                        pltpu.VMEM((L, GLA_DV), BF16),
                        pltpu.VMEM((L, CHUNK), BF16),
                        pltpu.VMEM((nc, GLA_DV, 2 * GLA_DK), F32),
                        pltpu.VMEM((nc, SUBLANES, 2 * GLA_DK), F32),
                        pltpu.VMEM((nc, GLA_DV, 2 * GLA_DK), BF16)],
        compiler_params=_params(2),
        name="gla",
    )(*args)


def _ssd_body(*refs, seq_len, grid_rows, has_h0, want_state):
    it = iter(refs)
    (grp_ref, sm_ref, cwx_ref, cwb_ref, cwc_ref, cbx_ref, cbb_ref, cbc_ref,
     par_ref, dsk_ref) = (next(it) for _ in range(10))
    h0_ref = next(it) if has_h0 else None
    y_ref = next(it)
    st_ref = next(it) if want_state else None
    (pad_scr, xs_s, b_s, c_s, dt_scr, xy_scr, tr_scr, rows_scr, sc_scr, ed_scr, xw_scr, dec_scr,
     cs_scr, sprev_scr, state_scr) = (next(it) for _ in range(15))

    L = seq_len
    nc = L // CHUNK
    width = L // grid_rows
    pad = pad_scr.shape[0] - L
    pad //= 2
    grp = pl.program_id(1)

    def conv_into(c0, cw_ref, cb_ref, dst_ref):
        ch = dst_ref.shape[1]
        pad_scr[0:pad, 0:ch] = jnp.zeros((pad, ch), F32)
        pad_scr[pad + L:pad + L + pad, 0:ch] = jnp.zeros((pad, ch), F32)
        pad_scr[pad:pad + L, 0:ch] = grp_ref[0, :, c0:c0 + ch]
        rc = min(2 * CHUNK, L)
        col = lax.broadcasted_iota(jnp.int32, (rc, ch), 0) % width
        di_taps = range(CONV_K) if grid_rows > 1 else (CONV_K // 2,)
        for r0 in range(0, L, rc):
            acc = jnp.broadcast_to(cb_ref[...], (rc, ch))
            for dj in range(CONV_K):
                inner = None
                for di in di_taps:
                    off = pad + r0 + (di - 1) * width + (dj - 1)
                    term = cw_ref[di, dj:dj + 1, :] * pad_scr[off:off + rc, 0:ch]
                    inner = term if inner is None else inner + term
                if grid_rows > 1 and dj == 0:
                    inner = jnp.where(col >= 1, inner, 0.0)
                if grid_rows > 1 and dj == CONV_K - 1:
                    inner = jnp.where(col <= width - 2, inner, 0.0)
                acc = acc + inner
            dst_ref[r0:r0 + rc, :] = _silu(acc).astype(dst_ref.dtype)

    conv_into(0, cwx_ref, cbx_ref, xs_s)
    conv_into(GRP_B, cwb_ref, cbb_ref, b_s)
    conv_into(GRP_C, cwc_ref, cbc_ref, c_s)

    gw = GROUP_W
    unroll = min(SSD_UNROLL, nc)
    rows = lambda c: pl.ds(pl.multiple_of(c * CHUNK, CHUNK), CHUNK)
    bias_row = par_ref[0:1, :]
    a_row = -jnp.exp(par_ref[1:2, :]) * par_ref[2:3, :] * LOG2_E
    lower, upper = _tri_masks()
    lower_b = jnp.where(lower, 1.0, 0.0).astype(BF16)
    upper_b = jnp.where(upper, 1.0, 0.0).astype(BF16)
    lane = lax.broadcasted_iota(jnp.int32, (CHUNK, LANES), 1)
    dt_lanes = (lane >= SM_DTF) & (lane < SM_DTF + TERM_STRIDE)
    src = lax.broadcasted_iota(jnp.int32, (LANES, 2 * gw), 0)
    dst = lax.broadcasted_iota(jnp.int32, (LANES, 2 * gw), 1)
    dst_slot = (grp * HEADS_PER_GROUP + (lax.shift_right_logical(dst, 6) & (HEADS_PER_GROUP - 1))
                + jnp.where(dst >= gw, SSD_HEADS, 0))
    expand = jnp.where((src >= SM_DTF) & ((src & (TERM_STRIDE - 1)) == dst_slot),
                       1.0, 0.0).astype(BF16)
    t_idx = lax.broadcasted_iota(jnp.int32, (CHUNK, gw), 0)
    s_idx = lax.broadcasted_iota(jnp.int32, (CHUNK, gw), 1) & (CHUNK - 1)
    diag = t_idx == s_idx
    blk_r = lax.shift_right_logical(lax.broadcasted_iota(jnp.int32, (MXU_TILE, MXU_TILE), 0), 6)
    blk_c = lax.shift_right_logical(lax.broadcasted_iota(jnp.int32, (MXU_TILE, MXU_TILE), 1), 6)
    same_head = blk_r == blk_c

    dt_scr[...] = _softplus(sm_ref[...] + bias_row)

    def place3(v):
        hi = v.astype(BF16).astype(F32)
        rest = v - hi
        mid = rest.astype(BF16).astype(F32)
        lo = rest - mid
        keep = lambda t: jnp.where(dt_lanes, t, 0.0)
        out = (keep(hi) + pltpu.roll(keep(mid), TERM_STRIDE, axis=1)
               + pltpu.roll(keep(lo), 2 * TERM_STRIDE, axis=1))
        return out.astype(BF16)

    def cumulate(c, carry):
        sl = rows(c)
        dt = dt_scr[sl, :]
        da = dt * a_row
        hi = da.astype(BF16)
        both = jnp.concatenate([hi, (da - hi.astype(F32)).astype(BF16)], axis=1)
        pf = _dot(lower_b, both)
        pb = _dot(upper_b, both)
        cum = jnp.where(lane < SM_DTB, pf[:, 0:LANES] + pf[:, LANES:], pb[:, 0:LANES] + pb[:, LANES:])
        xy_scr[sl, 0:LANES] = place3(cum)
        xy_scr[sl, LANES:2 * LANES] = place3(dt)
        tr_scr[c, 0] = cum.T
        tr_scr[c, 1] = dt.T
        head0 = pl.multiple_of(SM_DTF + grp * HEADS_PER_GROUP, SUBLANES)
        for q, (k, d) in enumerate(((0, 0), (0, 1), (1, 0), (1, 1))):
            slab = tr_scr[c, k, pl.ds(head0 + d * SSD_HEADS, HEADS_PER_GROUP), :]
            row = jnp.concatenate(
                [jnp.broadcast_to(slab[h:h + 1, :], (SUBLANES, CHUNK))
                 for h in range(HEADS_PER_GROUP)], axis=1)
            rows_scr[c, :, q * gw:(q + 1) * gw] = row
        return carry

    lax.fori_loop(0, nc, cumulate, 0, unroll=unroll)

    def weights(c, carry):
        sl = rows(c)
        xc = xs_s[sl, :]
        cum_e = _dot(xy_scr[sl, 0:LANES], expand)
        dt_e = _dot(xy_scr[sl, LANES:2 * LANES], expand)
        cb = _dot_nt(c_s[sl, :], jnp.concatenate([b_s[sl, :]] * HEADS_PER_GROUP, axis=0))
        segs, dt_rows, dec_rows = [], [], []
        for d, (a, last_i) in enumerate(((0, CHUNK - 1), (gw, 0))):
            ce = cum_e[:, a:a + gw]
            de = dt_e[:, a:a + gw]
            segs.append(ce - rows_scr[c, 0:1, d * gw:(d + 1) * gw])
            dt_rows.append(rows_scr[c, 0:1, (2 + d) * gw:(3 + d) * gw])
            cum_last = ce[last_i:last_i + 1, :]
            ed_scr[sl, a:a + gw] = jnp.exp2(ce)
            xw_scr[sl, a:a + gw] = (xc * (jnp.exp2(cum_last - ce) * de)).astype(BF16)
            dec_rows.append(jnp.exp2(cum_last))
        fwd = t_idx >= s_idx
        w = (jnp.exp2(jnp.where(fwd, segs[0], segs[1])) * jnp.where(fwd, dt_rows[0], dt_rows[1])
             + jnp.where(diag, dt_rows[1], 0.0))
        sc_scr[sl, :] = (cb * w).astype(BF16)
        dec_scr[c] = jnp.broadcast_to(jnp.concatenate(dec_rows, axis=1), (SUBLANES, 2 * gw))
        return carry

    lax.fori_loop(0, nc, weights, 0, unroll=unroll)

    def products(c, carry):
        sl = rows(c)
        xc = xs_s[sl, :]
        xb = xc.astype(BF16)
        parts = []
        for j in range(gw // MXU_TILE):
            xh = xb[:, j * MXU_TILE:(j + 1) * MXU_TILE]
            rep = jnp.concatenate([xh] * (MXU_TILE // SSD_HEAD_DIM), axis=0)
            bd = jnp.where(same_head, rep, jnp.zeros_like(rep))
            parts.append(_dot(sc_scr[sl, j * MXU_TILE:(j + 1) * MXU_TILE], bd))
        y_ref[sl, :] = jnp.concatenate(parts, axis=1) + xc * dsk_ref[...]
        cs_scr[c] = _dot_tn(b_s[sl, :], xw_scr[sl, :])
        return carry

    lax.fori_loop(0, nc, products, 0, unroll=unroll)

    pair_w = 2 * SSD_HEAD_DIM
    if has_h0:
        for d in range(2):
            for j in range(HEADS_PER_GROUP // 2):
                pair = jnp.concatenate([h0_ref[0, 0, d, 2 * j], h0_ref[0, 0, d, 2 * j + 1]], axis=0)
                state_scr[:, d * gw + j * pair_w:d * gw + (j + 1) * pair_w] = pair.T
    else:
        state_scr[...] = jnp.zeros_like(state_scr)

    for d in range(2):
        for a in range(d * gw, (d + 1) * gw, MXU_TILE):

            def recur(i, state, d=d, a=a):
                c = i if d == 0 else nc - 1 - i
                sprev_scr[c, :, a:a + MXU_TILE] = state.astype(BF16)
                return (state * dec_scr[c, 0:1, a:a + MXU_TILE]
                        + cs_scr[c, :, a:a + MXU_TILE])

            state_scr[:, a:a + MXU_TILE] = lax.fori_loop(
                0, nc, recur, state_scr[:, a:a + MXU_TILE], unroll=min(nc, 4))

    def finish(c, carry):
        sl = rows(c)
        yi = _dot(c_s[sl, :], sprev_scr[c]) * ed_scr[sl, :]
        y_ref[sl, :] += yi[:, 0:gw] + yi[:, gw:2 * gw]
        return carry

    lax.fori_loop(0, nc, finish, 0, unroll=unroll)

    if want_state:
        for d in range(2):
            for j in range(HEADS_PER_GROUP // 2):
                pair = state_scr[:, d * gw + j * pair_w:d * gw + (j + 1) * pair_w].T
                st_ref[0, 0, d, 2 * j] = pair[0:SSD_HEAD_DIM, :]
                st_ref[0, 0, d, 2 * j + 1] = pair[SSD_HEAD_DIM:pair_w, :]


def _ssd_call(groups, zs, conv_w, conv_b, par, dsk, h0, layer, *, n_seq, seq_len, grid_rows,
              want_state):
    L = seq_len
    in_specs = [pl.BlockSpec((1, L, GRP_W), lambda s, g: (g, s, 0)),
                pl.BlockSpec((L, LANES), lambda s, g: (s, ZS_SMALL // LANES)),
                pl.BlockSpec((CONV_K, CONV_K, GROUP_W), lambda s, g: (0, 0, g)),
                pl.BlockSpec((CONV_K, CONV_K, SSD_STATE),
                             lambda s, g: (0, 0, SSD_INNER // SSD_STATE + g)),
                pl.BlockSpec((CONV_K, CONV_K, SSD_STATE),
                             lambda s, g: (0, 0, (SSD_INNER + SSD_BC) // SSD_STATE + g)),
                pl.BlockSpec((1, GROUP_W), lambda s, g: (0, g)),
                pl.BlockSpec((1, SSD_STATE), lambda s, g: (0, SSD_INNER // SSD_STATE + g)),
                pl.BlockSpec((1, SSD_STATE),
                             lambda s, g: (0, (SSD_INNER + SSD_BC) // SSD_STATE + g)),
                pl.BlockSpec((SUBLANES, LANES), lambda s, g: (0, 0)),
                pl.BlockSpec((1, GROUP_W), lambda s, g: (0, g))]
    args = [groups, zs, conv_w, conv_w, conv_w, conv_b, conv_b, conv_b, par, dsk]
    state_blk = (1, 1, 2, HEADS_PER_GROUP, SSD_HEAD_DIM, SSD_STATE)
    if h0 is not None:
        in_specs.append(pl.BlockSpec(state_blk, lambda s, g: (s, layer, 0, g, 0, 0)))
        args.append(h0)
    out_specs = [pl.BlockSpec((L, GROUP_W), lambda s, g: (s, g))]
    out_shape = [jax.ShapeDtypeStruct((n_seq * L, SSD_INNER), F32)]
    if want_state:
        out_specs.append(pl.BlockSpec(state_blk, lambda s, g: (s, 0, 0, g, 0, 0)))
        out_shape.append(jax.ShapeDtypeStruct(
            (n_seq, 1, 2, SSD_HEADS, SSD_HEAD_DIM, SSD_STATE), F32))
    nc = L // CHUNK
    conv_pad = (L // grid_rows + SUBLANES) if grid_rows > 1 else SUBLANES
    body = functools.partial(_ssd_body, seq_len=L, grid_rows=grid_rows,
                             has_h0=h0 is not None, want_state=want_state)
    return pl.pallas_call(
        body,
        grid=(n_seq, SSD_GROUPS),
        in_specs=in_specs,
        out_specs=out_specs,
        out_shape=out_shape,
        scratch_shapes=[pltpu.VMEM((L + 2 * conv_pad, GROUP_W), F32),
                        pltpu.VMEM((L, GROUP_W), F32),
                        pltpu.VMEM((L, SSD_STATE), BF16),
                        pltpu.VMEM((L, SSD_STATE), BF16),
                        pltpu.VMEM((L, LANES), F32),
                        pltpu.VMEM((L, 2 * LANES), BF16),
                        pltpu.VMEM((nc, 2, LANES, CHUNK), F32),
                        pltpu.VMEM((nc, SUBLANES, 4 * GROUP_W), F32),
                        pltpu.VMEM((L, GROUP_W), BF16),
                        pltpu.VMEM((L, 2 * GROUP_W), F32),
                        pltpu.VMEM((L, 2 * GROUP_W), BF16),
                        pltpu.VMEM((nc, SUBLANES, 2 * GROUP_W), F32),
                        pltpu.VMEM((nc, SSD_STATE, 2 * GROUP_W), F32),
                        pltpu.VMEM((nc, SSD_STATE, 2 * GROUP_W), BF16),
                        pltpu.VMEM((SSD_STATE, 2 * GROUP_W), F32)],
        compiler_params=_params(2),
        name="ssd",
    )(*args)


def _proj_pieces(w_all):
    o_z = COL_Z + 2 * GLA_LOWRANK
    o_dt = o_z + SSD_INNER + SSD_CONV_DIM
    w_small = jnp.pad(jnp.concatenate([w_all[COL_Z:o_z], w_all[o_dt:]], axis=0),
                      ((0, LANES - 2 * GLA_LOWRANK - 2 * SSD_HEADS), (0, 0)))
    return {"w_all": w_all, "w_zx": w_all[o_z:o_dt], "w_small": w_small}


def _layer_path(x, mod, lw, raw, h0_gla, h0_ssd, layer, *, n_seq, seq_len, grid_rows,
                want_state, final_w):
    if raw is not None:
        x1, w_all = _ffn_call(x, mod, lw["norm_ffn1"], lw["ffn1_w_in"], lw["ffn1_w_out"],
                              sub=0, seq_len=seq_len, cast=(raw["w_in"],))
        lw.update(_proj_pieces(w_all))
    else:
        x1 = _ffn_call(x, mod, lw["norm_ffn1"], lw["ffn1_w_in"], lw["ffn1_w_out"],
                       sub=0, seq_len=seq_len)
    heads, groups, zs = _proj_call(x1, mod, lw["norm_mix"], lw["w_all"], lw["w_zx"],
                                   lw["w_small"], seq_len=seq_len)
    late = ("ffn2_w_in", "ffn2_w_out", "w_out")
    gla_out = _gla_call(heads, zs, lw["wa_cat"], lw["ba_cat"], lw["gla_norm_w"], h0_gla, layer,
                        n_seq=n_seq, seq_len=seq_len, want_state=want_state,
                        cast=tuple(raw[k] for k in late) if raw is not None else ())
    if raw is not None:
        lw.update(zip(late, gla_out[-len(late):]))
    ssd_out = _ssd_call(groups, zs, lw["conv_w"], lw["conv_b"], lw["ssd_par"], lw["d_skip_row"], h0_ssd,
                        layer, n_seq=n_seq, seq_len=seq_len, grid_rows=grid_rows,
                        want_state=want_state)
    o, y = gla_out[0], ssd_out[0]
    out = _ffn_call(x1, mod, lw["norm_ffn2"], lw["ffn2_w_in"], lw["ffn2_w_out"],
                    sub=2, seq_len=seq_len, mix=(o, y, zs, lw["ssd_norm_w"], lw["w_out"]),
                    final_w=final_w)
    if want_state:
        return out, gla_out[1], ssd_out[1]
    return out, None, None


def kernel(x_prompt, x_sample, state_gla, state_ssd, c, c_ctx, norm_ffn1, norm_mix, norm_ffn2, w_mod, b_mod, ffn1_w_in, ffn1_w_out, ffn2_w_in, ffn2_w_out, w_in, gla_w_a2, gla_b_a, gla_norm_w, conv_w, conv_b, dt_bias, a_log, d_skip, ssd_norm_w, w_out, final_norm):
    nb, seq, _ = x_prompt.shape
    db, dseq, _ = x_sample.shape
    grid_rows = dseq // GRID_W
    xp = x_prompt.reshape(nb * seq, D_MODEL)
    xs = x_sample.reshape(db * dseq, D_MODEL)
    row = lambda v: v.reshape(1, -1)
    gla_states, ssd_states = [], []
    for i in range(DEPTH):
        last = i == DEPTH - 1
        n_rows = -(-(db + 1) // SUBLANES) * SUBLANES
        cc = jnp.concatenate([c, c_ctx[None, :], jnp.zeros((n_rows - db - 1, D_MODEL), F32)], 0)
        mod = _mod_call(cc, w_mod[i], row(b_mod[i])).reshape(n_rows, N_MOD, D_MODEL)
        mod_lat, mod_ctx = mod[:db], mod[db:db + 1]

        wa_cat = jnp.stack(
            [jnp.pad(gla_w_a2[i, d].reshape(GLA_LOWRANK, GLA_HEADS, GLA_DK),
                     ((lo, LANES - lo - GLA_LOWRANK), (0, 0), (0, 0)))
             for d, lo in ((0, SM_AF), (1, SM_AB))], axis=2).reshape(LANES, 2 * GLA_QK)
        ba_cat = jnp.stack([gla_b_a[i, 0].reshape(GLA_HEADS, GLA_DK),
                            gla_b_a[i, 1].reshape(GLA_HEADS, GLA_DK)], axis=1).reshape(1, 2 * GLA_QK)
        ssd_par = jnp.pad(
            jnp.stack([dt_bias[i].reshape(-1), a_log[i].reshape(-1), jnp.ones((2 * SSD_HEADS,), F32)]),
            ((0, SUBLANES - 3), (SM_DTF, LANES - SM_DTF - 2 * SSD_HEADS)))
        lw = {
            "norm_ffn1": row(norm_ffn1[i]), "norm_mix": row(norm_mix[i]),
            "norm_ffn2": row(norm_ffn2[i]),
            "ffn1_w_in": ffn1_w_in[i].astype(BF16), "ffn1_w_out": ffn1_w_out[i].astype(BF16),
            "wa_cat": wa_cat, "ba_cat": ba_cat,
            "gla_norm_w": row(gla_norm_w[i]), "conv_w": conv_w[i], "conv_b": row(conv_b[i]),
            "ssd_par": ssd_par, "d_skip_row": row(jnp.repeat(d_skip[i], SSD_HEAD_DIM)),
            "ssd_norm_w": row(ssd_norm_w[i]),
        }
        raw = {"w_in": jnp.swapaxes(w_in[i], 0, 1), "ffn2_w_in": ffn2_w_in[i], "ffn2_w_out": ffn2_w_out[i],
               "w_out": w_out[i]}
        fw = row(final_norm) if last else None

        xp, sg, ss = _layer_path(xp, mod_ctx, lw, raw, None, None, i, n_seq=nb, seq_len=seq,
                                 grid_rows=1, want_state=True, final_w=fw)
        xs, _, _ = _layer_path(xs, mod_lat, lw, None, state_gla, state_ssd, i, n_seq=db,
                               seq_len=dseq, grid_rows=grid_rows, want_state=False, final_w=fw)
        gla_states.append(sg)
        ssd_states.append(ss)
    y_prompt = xp.reshape(nb, seq, D_MODEL)
    y_sample = xs.reshape(db, dseq, D_MODEL)
    return (y_prompt, y_sample, jnp.concatenate(gla_states, axis=1),
            jnp.concatenate(ssd_states, axis=1))
```

```python
import functools

import jax
import jax.numpy as jnp
from jax import lax
from jax.experimental import pallas as pl
from jax.experimental.pallas import tpu as pltpu

F32 = jnp.float32
BF16 = jnp.bfloat16

D_MODEL = 1024
DEPTH = 1
GRID_W = 64
CHUNK = 64
EPS = 1e-6
N_MOD = 9
D_FF = 2816
GLA_HEADS = 4
GLA_DK = 128
GLA_DV = 256
GLA_LOWRANK = 16
GLA_TAU = 16.0
GLA_QK = GLA_HEADS * GLA_DK
GLA_V = GLA_HEADS * GLA_DV
SSD_HEADS = 16
SSD_HEAD_DIM = 64
SSD_GROUPS = 2
SSD_STATE = 128
SSD_INNER = SSD_HEADS * SSD_HEAD_DIM
SSD_BC = SSD_GROUPS * SSD_STATE
SSD_CONV_DIM = SSD_INNER + 2 * SSD_BC
CONV_K = 3
D_MIX = GLA_V + SSD_INNER

LOG2_E = 1.4426950408889634
LANES = 128
SUBLANES = 8
BF16_SUBLANES = 16
VMEM_LIMIT_BYTES = 56 * 1024 * 1024

COL_Q = 0
COL_K = COL_Q + GLA_QK
COL_V = COL_K + GLA_QK
COL_R = COL_V + GLA_V
COL_Z = COL_R + GLA_V
HEAD_K = GLA_DK
HEAD_V = 2 * GLA_DK
HEAD_R = HEAD_V + GLA_DV
HEAD_W = HEAD_R + GLA_DV
GRP_B = SSD_INNER // SSD_GROUPS
GRP_C = GRP_B + SSD_STATE
GRP_W = GRP_C + SSD_STATE
ZS_SMALL = SSD_INNER
ZS_W = ZS_SMALL + LANES
SM_AF = 0
SM_AB = SM_AF + GLA_LOWRANK
SM_DTF = SM_AB + GLA_LOWRANK
SM_DTB = SM_DTF + SSD_HEADS
HEADS_PER_GROUP = SSD_HEADS // SSD_GROUPS
GROUP_W = HEADS_PER_GROUP * SSD_HEAD_DIM

TOKEN_TILE = 512
GLA_UNROLL = 16
SSD_UNROLL = 16
MXU_TILE = 256
TERM_STRIDE = 2 * SSD_HEADS
SCAN_BLOCK_TOKENS = 1024


def _dot(a, b):
    return jnp.dot(a, b, preferred_element_type=F32)


def _dot_nt(a, b):
    return lax.dot_general(a, b, (((1,), (1,)), ((), ())), preferred_element_type=F32)


def _dot_tn(a, b):
    return lax.dot_general(a, b, (((0,), (0,)), ((), ())), preferred_element_type=F32)


def _silu(x):
    return x * jax.nn.sigmoid(x)


def _log1p_exp_neg_abs(x):
    return jnp.log(1.0 + jnp.exp(-jnp.abs(x)))


def _softplus(x):
    return jnp.maximum(x, 0.0) + _log1p_exp_neg_abs(x)


def _log_sigmoid(x):
    return jnp.minimum(x, 0.0) - _log1p_exp_neg_abs(x)


def _rmsnorm(x, w):
    ms = jnp.mean(x * x, axis=-1, keepdims=True)
    return x * lax.rsqrt(ms + EPS) * w


def _resident(shape):
    nd = len(shape)
    return pl.BlockSpec(shape, lambda *_: (0,) * nd, pipeline_mode=pl.Buffered(1))


def _params(n_axes):
    return pltpu.CompilerParams(dimension_semantics=("arbitrary",) * n_axes,
                                vmem_limit_bytes=VMEM_LIMIT_BYTES)


def _cast_plumbing(cast, n_steps, step_of):
    in_specs, out_specs, out_shape = [], [], []
    for w in cast:
        rows, cols = w.shape
        per_step = -(-rows // n_steps)
        per_step = -(-per_step // BF16_SUBLANES) * BF16_SUBLANES
        last = -(-rows // per_step) - 1
        blk = pl.BlockSpec((per_step, cols),
                           lambda *g, last=last: (jnp.minimum(step_of(*g), last), 0))
        in_specs.append(blk)
        out_specs.append(blk)
        out_shape.append(jax.ShapeDtypeStruct((rows, cols), BF16))
    return in_specs, out_specs, out_shape


def _cast_blocks(src_refs, dst_refs):
    for src_ref, dst_ref in zip(src_refs, dst_refs):
        dst_ref[...] = src_ref[...].astype(dst_ref.dtype)


def _mod_body(c_ref, w_ref, b_ref, out_ref):
    a = _silu(c_ref[...]).astype(BF16)
    out_ref[...] = _dot(a, w_ref[...].astype(BF16)) + b_ref[...]


def _mod_call(cc, w_mod, b_mod):
    n_rows = cc.shape[0]
    tn = D_MODEL
    return pl.pallas_call(
        _mod_body,
        grid=(N_MOD * D_MODEL // tn,),
        in_specs=[pl.BlockSpec((n_rows, D_MODEL), lambda j: (0, 0)),
                  pl.BlockSpec((D_MODEL, tn), lambda j: (0, j)),
                  pl.BlockSpec((1, tn), lambda j: (0, j))],
        out_specs=pl.BlockSpec((n_rows, tn), lambda j: (0, j)),
        out_shape=jax.ShapeDtypeStruct((n_rows, N_MOD * D_MODEL), F32),
        compiler_params=_params(1),
        name="mod",
    )(cc, w_mod, b_mod)


def _ffn_body(*refs, sub, has_mix, has_final, n_cast):
    it = iter(refs)
    x_ref, mod_ref, nw_ref, win_ref, wout_ref = (next(it) for _ in range(5))
    if has_mix:
        o_ref, y_ref, z_ref, snw_ref, wo_ref = (next(it) for _ in range(5))
    if has_final:
        fn_ref = next(it)
    cast_src = [next(it) for _ in range(n_cast)]
    out_ref = next(it)
    _cast_blocks(cast_src, [next(it) for _ in range(n_cast)])

    x = x_ref[...]
    if has_mix:
        g2 = mod_ref[0, 5:6, :]
        yn = _rmsnorm(y_ref[...] * _silu(z_ref[...]), snw_ref[...])
        m = _dot(o_ref[...], wo_ref[:GLA_V, :]) + _dot(yn.astype(BF16), wo_ref[GLA_V:, :])
        x = x + g2 * m
    sh = mod_ref[0, 3 * sub:3 * sub + 1, :]
    sc = mod_ref[0, 3 * sub + 1:3 * sub + 2, :]
    gate = mod_ref[0, 3 * sub + 2:3 * sub + 3, :]
    h = (_rmsnorm(x, nw_ref[...]) * (1.0 + sc) + sh).astype(BF16)
    g = _dot(h, win_ref[:, :D_FF])
    u = _dot(h, win_ref[:, D_FF:])
    act = (_silu(g) * u).astype(BF16)
    x = x + (0.5 * gate) * _dot(act, wout_ref[...])
    if has_final:
        x = _rmsnorm(x, fn_ref[...])
    out_ref[...] = x


def _ffn_call(x, mod, norm_w, w_in, w_out, *, sub, seq_len, mix=None, final_w=None, cast=()):
    m_tok = x.shape[0]
    tm = TOKEN_TILE
    tiles_per_seq = seq_len // tm
    shared_mod = mod.shape[0] == 1
    mod_map = (lambda i: (0, 0, 0)) if shared_mod else (lambda i: (i // tiles_per_seq, 0, 0))
    tok = lambda width: pl.BlockSpec((tm, width), lambda i: (i, 0))
    in_specs = [tok(D_MODEL), pl.BlockSpec((1, N_MOD, D_MODEL), mod_map),
                _resident((1, D_MODEL)), _resident(w_in.shape), _resident(w_out.shape)]
    args = [x, mod, norm_w, w_in, w_out]
    if mix is not None:
        o, y, proj, ssd_norm_w, w_mix_out = mix
        in_specs += [tok(GLA_V), tok(SSD_INNER),
                     pl.BlockSpec((tm, SSD_INNER), lambda i: (i, 0)),
                     _resident((1, SSD_INNER)), _resident(w_mix_out.shape)]
        args += [o, y, proj, ssd_norm_w, w_mix_out]
    if final_w is not None:
        in_specs.append(_resident((1, D_MODEL)))
        args.append(final_w)
    n_steps = m_tok // tm
    cast_in, cast_out, cast_shape = _cast_plumbing(cast, n_steps, lambda i: i)
    in_specs += cast_in
    args += list(cast)
    out_specs = [tok(D_MODEL)] + cast_out
    out_shape = [jax.ShapeDtypeStruct((m_tok, D_MODEL), F32)] + cast_shape
    body = functools.partial(_ffn_body, sub=sub, has_mix=mix is not None,
                             has_final=final_w is not None, n_cast=len(cast))
    outs = pl.pallas_call(
        body,
        grid=(n_steps,),
        in_specs=in_specs,
        out_specs=out_specs,
        out_shape=out_shape,
        compiler_params=_params(1),
        name="ffn_mix" if mix is not None else "ffn",
    )(*args)
    return outs if cast else outs[0]


def _proj_body(x_ref, mod_ref, nw_ref, wa_ref, wb_ref, wc_ref, heads_ref, groups_ref, zs_ref):
    sh = mod_ref[0, 3:4, :]
    sc = mod_ref[0, 4:5, :]
    h = (_rmsnorm(x_ref[...], nw_ref[...]) * (1.0 + sc) + sh).astype(BF16)
    qkvr = _dot_nt(h, wa_ref[...])
    for hd in range(GLA_HEADS):
        for dst, src, wid in ((0, COL_Q + hd * GLA_DK, GLA_DK), (HEAD_K, COL_K + hd * GLA_DK, GLA_DK),
                              (HEAD_V, COL_V + hd * GLA_DV, GLA_DV), (HEAD_R, COL_R + hd * GLA_DV, GLA_DV)):
            heads_ref[hd, :, dst:dst + wid] = qkvr[:, src:src + wid]
    zx = _dot_nt(h, wb_ref[...])
    zs_ref[:, 0:ZS_SMALL] = zx[:, 0:SSD_INNER]
    for g in range(SSD_GROUPS):
        for dst, src, wid in ((0, g * GRP_B, GRP_B), (GRP_B, SSD_INNER + g * SSD_STATE, SSD_STATE),
                              (GRP_C, SSD_INNER + SSD_BC + g * SSD_STATE, SSD_STATE)):
            groups_ref[g, :, dst:dst + wid] = zx[:, SSD_INNER + src:SSD_INNER + src + wid]
    zs_ref[:, ZS_SMALL:ZS_W] = _dot_nt(h, wc_ref[...])


def _proj_call(x, mod, norm_w, w_all, w_zx, w_small, *, seq_len):
    m_tok = x.shape[0]
    tm = TOKEN_TILE
    tiles_per_seq = seq_len // tm
    shared_mod = mod.shape[0] == 1
    mod_map = (lambda i: (0, 0, 0)) if shared_mod else (lambda i: (i // tiles_per_seq, 0, 0))
    return pl.pallas_call(
        _proj_body,
        grid=(m_tok // tm,),
        in_specs=[pl.BlockSpec((tm, D_MODEL), lambda i: (i, 0)),
                  pl.BlockSpec((1, N_MOD, D_MODEL), mod_map),
                  _resident((1, D_MODEL)), _resident((COL_Z, D_MODEL)),
                  _resident(w_zx.shape), _resident(w_small.shape)],
        out_specs=[pl.BlockSpec((GLA_HEADS, tm, HEAD_W), lambda i: (0, i, 0)),
                   pl.BlockSpec((SSD_GROUPS, tm, GRP_W), lambda i: (0, i, 0)),
                   pl.BlockSpec((tm, ZS_W), lambda i: (i, 0))],
        out_shape=[jax.ShapeDtypeStruct((GLA_HEADS, m_tok, HEAD_W), F32),
                   jax.ShapeDtypeStruct((SSD_GROUPS, m_tok, GRP_W), F32),
                   jax.ShapeDtypeStruct((m_tok, ZS_W), F32)],
        compiler_params=_params(1),
        name="proj",
    )(x, mod, norm_w, w_all, w_zx, w_small)


def _tri_masks():
    row = lax.broadcasted_iota(jnp.int32, (CHUNK, CHUNK), 0)
    col = lax.broadcasted_iota(jnp.int32, (CHUNK, CHUNK), 1)
    return row >= col, row <= col


def _gla_body(*refs, seq_len, n_sub, has_h0, want_state):
    it = iter(refs)
    hd_ref, sm_ref, wa_ref, ba_ref, nw_ref = (next(it) for _ in range(5))
    h0_ref = next(it) if has_h0 else None
    o_ref = next(it)
    st_ref = next(it) if want_state else None
    (la_scr, qk_scr, qg_scr, ks_scr, vb_scr, sc_scr, kv_scr, dec_scr,
     sprev_scr) = (next(it) for _ in range(9))

    nc = seq_len // CHUNK
    nct = n_sub * nc
    dk = GLA_DK
    lower, upper = _tri_masks()
    lower_b = jnp.where(lower, 1.0, 0.0).astype(BF16)
    upper_b = jnp.where(upper, 1.0, 0.0).astype(BF16)
    unroll = min(GLA_UNROLL, nct)
    rows = lambda c: pl.ds(pl.multiple_of(c * CHUNK, CHUNK), CHUNK)

    pre = _dot(sm_ref[...].astype(BF16), wa_ref[...].astype(BF16)) + ba_ref[...]
    la_scr[...] = _log_sigmoid(pre) * (LOG2_E / GLA_TAU)

    def prep(c, carry):
        sl = rows(c)
        qc = hd_ref[0, sl, 0:HEAD_K] * (GLA_DK ** -0.5)
        kc = hd_ref[0, sl, HEAD_K:HEAD_V]
        vb_scr[sl, :] = hd_ref[0, sl, HEAD_V:HEAD_R].astype(BF16)
        la = la_scr[sl, :]
        hi = la.astype(BF16)
        lo = (la - hi.astype(F32)).astype(BF16)

        def cumulative(tri_b, a):
            p = _dot(tri_b, jnp.concatenate([hi[:, a:a + dk], lo[:, a:a + dk]], axis=1))
            return p[:, 0:dk] + p[:, dk:2 * dk]

        g_f = cumulative(lower_b, 0)
        g_b = cumulative(upper_b, dk)
        mid_f = g_f[CHUNK // 2:CHUNK // 2 + 1, :]
        mid_b = g_b[CHUNK - 1 - CHUNK // 2:CHUNK - CHUNK // 2, :]
        end_f = g_f[CHUNK - 1:CHUNK, :]
        end_b = g_b[0:1, :]
        qk_scr[sl, 0:dk] = (qc * jnp.exp2(g_f - mid_f)).astype(BF16)
        qk_scr[sl, dk:2 * dk] = (qc * jnp.exp2(g_b - mid_b)).astype(BF16)
        qk_scr[sl, 2 * dk:3 * dk] = (kc * jnp.exp2(mid_f - g_f)).astype(BF16)
        qk_scr[sl, 3 * dk:4 * dk] = (kc * jnp.exp2(mid_b - g_b)).astype(BF16)
        qg_scr[sl, 0:dk] = (qc * jnp.exp2(g_f)).astype(BF16)
        qg_scr[sl, dk:2 * dk] = (qc * jnp.exp2(g_b)).astype(BF16)
        ks_scr[sl, 0:dk] = (kc * jnp.exp2(end_f - g_f)).astype(BF16)
        ks_scr[sl, dk:2 * dk] = (kc * jnp.exp2(end_b - g_b)).astype(BF16)
        dec = jnp.exp2(jnp.concatenate([end_f, end_b], axis=1))
        dec_scr[c] = jnp.broadcast_to(dec, (SUBLANES, 2 * dk))
        return carry

    lax.fori_loop(0, nct, prep, 0, unroll=unroll)

    def products(c, carry):
        sl = rows(c)
        s_f = _dot_nt(qk_scr[sl, 0:dk], qk_scr[sl, 2 * dk:3 * dk])
        s_b = _dot_nt(qk_scr[sl, dk:2 * dk], qk_scr[sl, 3 * dk:4 * dk])
        sc_scr[sl, :] = (jnp.where(lower, s_f, 0.0) + jnp.where(upper, s_b, 0.0)).astype(BF16)
        kv_scr[c] = _dot_tn(vb_scr[sl, :], ks_scr[sl, :])
        return carry

    lax.fori_loop(0, nct, products, 0, unroll=unroll)

    for b in range(n_sub):
        for d, a in ((0, 0), (1, dk)):
            state0 = h0_ref[b, 0, d, 0].T if has_h0 else jnp.zeros((GLA_DV, dk), F32)

            def recur(i, state, b=b, d=d, a=a):
                c = b * nc + (i if d == 0 else nc - 1 - i)
                sprev_scr[c, :, a:a + dk] = state.astype(BF16)
                return state * dec_scr[c, 0:1, a:a + dk] + kv_scr[c, :, a:a + dk]

            state = lax.fori_loop(0, nc, recur, state0, unroll=min(nc, 4))
            if want_state:
                st_ref[b, 0, d, 0] = state.T

    def finish(c, carry):
        sl = rows(c)
        o = _dot(sc_scr[sl, :], vb_scr[sl, :]) + _dot_nt(qg_scr[sl, :], sprev_scr[c])
        o_ref[sl, :] = (_rmsnorm(o, nw_ref[...]) * _silu(hd_ref[0, sl, HEAD_R:HEAD_W])).astype(o_ref.dtype)
        return carry

    lax.fori_loop(0, nct, finish, 0, unroll=unroll)


def _gla_call(heads, zs, wa_cat, ba_cat, norm_w, h0, layer, *, n_seq, seq_len, want_state):
    n_sub = max(1, SCAN_BLOCK_TOKENS // seq_len)
    L = n_sub * seq_len
    nc = L // CHUNK
    in_specs = [pl.BlockSpec((1, L, HEAD_W), lambda s, h: (h, s, 0)),
                pl.BlockSpec((L, LANES), lambda s, h: (s, ZS_SMALL // LANES)),
                pl.BlockSpec((LANES, 2 * GLA_DK), lambda s, h: (0, h)),
                pl.BlockSpec((1, 2 * GLA_DK), lambda s, h: (0, h)),
                pl.BlockSpec((1, GLA_DV), lambda s, h: (0, 0))]
    args = [heads, zs, wa_cat, ba_cat, norm_w]
    state_blk = (n_sub, 1, 2, 1, GLA_DK, GLA_DV)
    if h0 is not None:
        in_specs.append(pl.BlockSpec(state_blk, lambda s, h: (s, layer, 0, h, 0, 0)))
        args.append(h0)
    out_specs = [pl.BlockSpec((L, GLA_DV), lambda s, h: (s, h))]
    out_shape = [jax.ShapeDtypeStruct((n_seq * seq_len, GLA_V), BF16)]
    if want_state:
        out_specs.append(pl.BlockSpec(state_blk, lambda s, h: (s, 0, 0, h, 0, 0)))
        out_shape.append(jax.ShapeDtypeStruct((n_seq, 1, 2, GLA_HEADS, GLA_DK, GLA_DV), F32))
    body = functools.partial(_gla_body, seq_len=seq_len, n_sub=n_sub, has_h0=h0 is not None,
                             want_state=want_state)
    return pl.pallas_call(
        body,
        grid=(n_seq // n_sub, GLA_HEADS),
        in_specs=in_specs,
        out_specs=out_specs,
        out_shape=out_shape,
        scratch_shapes=[pltpu.VMEM((L, 2 * GLA_DK), F32),
                        pltpu.VMEM((L, 4 * GLA_DK), BF16),
                        pltpu.VMEM((L, 2 * GLA_DK), BF16),
                        pltpu.VMEM((L, 2 * GLA_DK), BF16),
                        pltpu.VMEM((L, GLA_DV), BF16),
                        pltpu.VMEM((L, CHUNK), BF16),
                        pltpu.VMEM((nc, GLA_DV, 2 * GLA_DK), F32),
                        pltpu.VMEM((nc, SUBLANES, 2 * GLA_DK), F32),
                        pltpu.VMEM((nc, GLA_DV, 2 * GLA_DK), BF16)],
        compiler_params=_params(2),
        name="gla",
    )(*args)


def _ssd_body(*refs, seq_len, grid_rows, has_h0, want_state, n_cast):
    it = iter(refs)
    (grp_ref, sm_ref, cwx_ref, cwb_ref, cwc_ref, cbx_ref, cbb_ref, cbc_ref,
     par_ref, dsk_ref) = (next(it) for _ in range(10))
    h0_ref = next(it) if has_h0 else None
    cast_src = [next(it) for _ in range(n_cast)]
    y_ref = next(it)
    st_ref = next(it) if want_state else None
    _cast_blocks(cast_src, [next(it) for _ in range(n_cast)])
    (pad_scr, xs_s, b_s, c_s, dt_scr, xy_scr, tr_scr, rows_scr, sc_scr, ed_scr, xw_scr, dec_scr,
     cs_scr, sprev_scr, state_scr) = (next(it) for _ in range(15))

    L = seq_len
    nc = L // CHUNK
    width = L // grid_rows
    pad = pad_scr.shape[0] - L
    pad //= 2
    grp = pl.program_id(1)

    def conv_into(c0, cw_ref, cb_ref, dst_ref):
        ch = dst_ref.shape[1]
        pad_scr[0:pad, 0:ch] = jnp.zeros((pad, ch), F32)
        pad_scr[pad + L:pad + L + pad, 0:ch] = jnp.zeros((pad, ch), F32)
        pad_scr[pad:pad + L, 0:ch] = grp_ref[0, :, c0:c0 + ch]
        rc = min(2 * CHUNK, L)
        col = lax.broadcasted_iota(jnp.int32, (rc, ch), 0) % width
        di_taps = range(CONV_K) if grid_rows > 1 else (CONV_K // 2,)
        for r0 in range(0, L, rc):
            acc = jnp.broadcast_to(cb_ref[...], (rc, ch))
            for dj in range(CONV_K):
                inner = None
                for di in di_taps:
                    off = pad + r0 + (di - 1) * width + (dj - 1)
                    term = cw_ref[di, dj:dj + 1, :] * pad_scr[off:off + rc, 0:ch]
                    inner = term if inner is None else inner + term
                if grid_rows > 1 and dj == 0:
                    inner = jnp.where(col >= 1, inner, 0.0)
                if grid_rows > 1 and dj == CONV_K - 1:
                    inner = jnp.where(col <= width - 2, inner, 0.0)
                acc = acc + inner
            dst_ref[r0:r0 + rc, :] = _silu(acc).astype(dst_ref.dtype)

    conv_into(0, cwx_ref, cbx_ref, xs_s)
    conv_into(GRP_B, cwb_ref, cbb_ref, b_s)
    conv_into(GRP_C, cwc_ref, cbc_ref, c_s)

    gw = GROUP_W
    unroll = min(SSD_UNROLL, nc)
    rows = lambda c: pl.ds(pl.multiple_of(c * CHUNK, CHUNK), CHUNK)
    bias_row = par_ref[0:1, :]
    a_row = -jnp.exp(par_ref[1:2, :]) * par_ref[2:3, :] * LOG2_E
    lower, upper = _tri_masks()
    lower_b = jnp.where(lower, 1.0, 0.0).astype(BF16)
    upper_b = jnp.where(upper, 1.0, 0.0).astype(BF16)
    lane = lax.broadcasted_iota(jnp.int32, (CHUNK, LANES), 1)
    dt_lanes = (lane >= SM_DTF) & (lane < SM_DTF + TERM_STRIDE)
    src = lax.broadcasted_iota(jnp.int32, (LANES, 2 * gw), 0)
    dst = lax.broadcasted_iota(jnp.int32, (LANES, 2 * gw), 1)
    dst_slot = (grp * HEADS_PER_GROUP + (lax.shift_right_logical(dst, 6) & (HEADS_PER_GROUP - 1))
                + jnp.where(dst >= gw, SSD_HEADS, 0))
    expand = jnp.where((src >= SM_DTF) & ((src & (TERM_STRIDE - 1)) == dst_slot),
                       1.0, 0.0).astype(BF16)
    t_idx = lax.broadcasted_iota(jnp.int32, (CHUNK, gw), 0)
    s_idx = lax.broadcasted_iota(jnp.int32, (CHUNK, gw), 1) & (CHUNK - 1)
    diag = t_idx == s_idx
    blk_r = lax.shift_right_logical(lax.broadcasted_iota(jnp.int32, (MXU_TILE, MXU_TILE), 0), 6)
    blk_c = lax.shift_right_logical(lax.broadcasted_iota(jnp.int32, (MXU_TILE, MXU_TILE), 1), 6)
    same_head = blk_r == blk_c

    dt_scr[...] = _softplus(sm_ref[...] + bias_row)

    def place3(v):
        hi = v.astype(BF16).astype(F32)
        rest = v - hi
        mid = rest.astype(BF16).astype(F32)
        lo = rest - mid
        keep = lambda t: jnp.where(dt_lanes, t, 0.0)
        out = (keep(hi) + pltpu.roll(keep(mid), TERM_STRIDE, axis=1)
               + pltpu.roll(keep(lo), 2 * TERM_STRIDE, axis=1))
        return out.astype(BF16)

    def cumulate(c, carry):
        sl = rows(c)
        dt = dt_scr[sl, :]
        da = dt * a_row
        hi = da.astype(BF16)
        both = jnp.concatenate([hi, (da - hi.astype(F32)).astype(BF16)], axis=1)
        pf = _dot(lower_b, both)
        pb = _dot(upper_b, both)
        cum = jnp.where(lane < SM_DTB, pf[:, 0:LANES] + pf[:, LANES:], pb[:, 0:LANES] + pb[:, LANES:])
        xy_scr[sl, 0:LANES] = place3(cum)
        xy_scr[sl, LANES:2 * LANES] = place3(dt)
        tr_scr[c, 0] = cum.T
        tr_scr[c, 1] = dt.T
        head0 = pl.multiple_of(SM_DTF + grp * HEADS_PER_GROUP, SUBLANES)
        for q, (k, d) in enumerate(((0, 0), (0, 1), (1, 0), (1, 1))):
            slab = tr_scr[c, k, pl.ds(head0 + d * SSD_HEADS, HEADS_PER_GROUP), :]
            row = jnp.concatenate(
                [jnp.broadcast_to(slab[h:h + 1, :], (SUBLANES, CHUNK))
                 for h in range(HEADS_PER_GROUP)], axis=1)
            rows_scr[c, :, q * gw:(q + 1) * gw] = row
        return carry

    lax.fori_loop(0, nc, cumulate, 0, unroll=unroll)

    def weights(c, carry):
        sl = rows(c)
        xc = xs_s[sl, :]
        cum_e = _dot(xy_scr[sl, 0:LANES], expand)
        dt_e = _dot(xy_scr[sl, LANES:2 * LANES], expand)
        cb = _dot_nt(c_s[sl, :], jnp.concatenate([b_s[sl, :]] * HEADS_PER_GROUP, axis=0))
        segs, dt_rows, dec_rows = [], [], []
        for d, (a, last_i) in enumerate(((0, CHUNK - 1), (gw, 0))):
            ce = cum_e[:, a:a + gw]
            de = dt_e[:, a:a + gw]
            segs.append(ce - rows_scr[c, 0:1, d * gw:(d + 1) * gw])
            dt_rows.append(rows_scr[c, 0:1, (2 + d) * gw:(3 + d) * gw])
            cum_last = ce[last_i:last_i + 1, :]
            ed_scr[sl, a:a + gw] = jnp.exp2(ce)
            xw_scr[sl, a:a + gw] = (xc * (jnp.exp2(cum_last - ce) * de)).astype(BF16)
            dec_rows.append(jnp.exp2(cum_last))
        fwd = t_idx >= s_idx
        w = (jnp.exp2(jnp.where(fwd, segs[0], segs[1])) * jnp.where(fwd, dt_rows[0], dt_rows[1])
             + jnp.where(diag, dt_rows[1], 0.0))
        sc_scr[sl, :] = (cb * w).astype(BF16)
        dec_scr[c] = jnp.broadcast_to(jnp.concatenate(dec_rows, axis=1), (SUBLANES, 2 * gw))
        return carry

    lax.fori_loop(0, nc, weights, 0, unroll=unroll)

    def products(c, carry):
        sl = rows(c)
        xc = xs_s[sl, :]
        xb = xc.astype(BF16)
        parts = []
        for j in range(gw // MXU_TILE):
            xh = xb[:, j * MXU_TILE:(j + 1) * MXU_TILE]
            rep = jnp.concatenate([xh] * (MXU_TILE // SSD_HEAD_DIM), axis=0)
            bd = jnp.where(same_head, rep, jnp.zeros_like(rep))
            parts.append(_dot(sc_scr[sl, j * MXU_TILE:(j + 1) * MXU_TILE], bd))
        y_ref[sl, :] = jnp.concatenate(parts, axis=1) + xc * dsk_ref[...]
        cs_scr[c] = _dot_tn(b_s[sl, :], xw_scr[sl, :])
        return carry

    lax.fori_loop(0, nc, products, 0, unroll=unroll)

    pair_w = 2 * SSD_HEAD_DIM
    if has_h0:
        for d in range(2):
            for j in range(HEADS_PER_GROUP // 2):
                pair = jnp.concatenate([h0_ref[0, 0, d, 2 * j], h0_ref[0, 0, d, 2 * j + 1]], axis=0)
                state_scr[:, d * gw + j * pair_w:d * gw + (j + 1) * pair_w] = pair.T
    else:
        state_scr[...] = jnp.zeros_like(state_scr)

    for d in range(2):
        for a in range(d * gw, (d + 1) * gw, MXU_TILE):

            def recur(i, state, d=d, a=a):
                c = i if d == 0 else nc - 1 - i
                sprev_scr[c, :, a:a + MXU_TILE] = state.astype(BF16)
                return (state * dec_scr[c, 0:1, a:a + MXU_TILE]
                        + cs_scr[c, :, a:a + MXU_TILE])

            state_scr[:, a:a + MXU_TILE] = lax.fori_loop(
                0, nc, recur, state_scr[:, a:a + MXU_TILE], unroll=min(nc, 4))

    def finish(c, carry):
        sl = rows(c)
        yi = _dot(c_s[sl, :], sprev_scr[c]) * ed_scr[sl, :]
        y_ref[sl, :] += yi[:, 0:gw] + yi[:, gw:2 * gw]
        return carry

    lax.fori_loop(0, nc, finish, 0, unroll=unroll)

    if want_state:
        for d in range(2):
            for j in range(HEADS_PER_GROUP // 2):
                pair = state_scr[:, d * gw + j * pair_w:d * gw + (j + 1) * pair_w].T
                st_ref[0, 0, d, 2 * j] = pair[0:SSD_HEAD_DIM, :]
                st_ref[0, 0, d, 2 * j + 1] = pair[SSD_HEAD_DIM:pair_w, :]


def _ssd_call(groups, zs, conv_w, conv_b, par, dsk, h0, layer, *, n_seq, seq_len, grid_rows,
              want_state, cast=()):
    L = seq_len
    in_specs = [pl.BlockSpec((1, L, GRP_W), lambda s, g: (g, s, 0)),
                pl.BlockSpec((L, LANES), lambda s, g: (s, ZS_SMALL // LANES)),
                pl.BlockSpec((CONV_K, CONV_K, GROUP_W), lambda s, g: (0, 0, g)),
                pl.BlockSpec((CONV_K, CONV_K, SSD_STATE),
                             lambda s, g: (0, 0, SSD_INNER // SSD_STATE + g)),
                pl.BlockSpec((CONV_K, CONV_K, SSD_STATE),
                             lambda s, g: (0, 0, (SSD_INNER + SSD_BC) // SSD_STATE + g)),
                pl.BlockSpec((1, GROUP_W), lambda s, g: (0, g)),
                pl.BlockSpec((1, SSD_STATE), lambda s, g: (0, SSD_INNER // SSD_STATE + g)),
                pl.BlockSpec((1, SSD_STATE),
                             lambda s, g: (0, (SSD_INNER + SSD_BC) // SSD_STATE + g)),
                pl.BlockSpec((SUBLANES, LANES), lambda s, g: (0, 0)),
                pl.BlockSpec((1, GROUP_W), lambda s, g: (0, g))]
    args = [groups, zs, conv_w, conv_w, conv_w, conv_b, conv_b, conv_b, par, dsk]
    state_blk = (1, 1, 2, HEADS_PER_GROUP, SSD_HEAD_DIM, SSD_STATE)
    if h0 is not None:
        in_specs.append(pl.BlockSpec(state_blk, lambda s, g: (s, layer, 0, g, 0, 0)))
        args.append(h0)
    out_specs = [pl.BlockSpec((L, GROUP_W), lambda s, g: (s, g))]
    out_shape = [jax.ShapeDtypeStruct((n_seq * L, SSD_INNER), F32)]
    if want_state:
        out_specs.append(pl.BlockSpec(state_blk, lambda s, g: (s, 0, 0, g, 0, 0)))
        out_shape.append(jax.ShapeDtypeStruct(
            (n_seq, 1, 2, SSD_HEADS, SSD_HEAD_DIM, SSD_STATE), F32))
    cast_in, cast_out, cast_shape = _cast_plumbing(cast, n_seq * SSD_GROUPS,
                                                   lambda s, g: s * SSD_GROUPS + g)
    in_specs += cast_in
    args += list(cast)
    out_specs += cast_out
    out_shape += cast_shape
    nc = L // CHUNK
    conv_pad = (L // grid_rows + SUBLANES) if grid_rows > 1 else SUBLANES
    body = functools.partial(_ssd_body, seq_len=L, grid_rows=grid_rows,
                             has_h0=h0 is not None, want_state=want_state, n_cast=len(cast))
    return pl.pallas_call(
        body,
        grid=(n_seq, SSD_GROUPS),
        in_specs=in_specs,
        out_specs=out_specs,
        out_shape=out_shape,
        scratch_shapes=[pltpu.VMEM((L + 2 * conv_pad, GROUP_W), F32),
                        pltpu.VMEM((L, GROUP_W), F32),
                        pltpu.VMEM((L, SSD_STATE), BF16),
                        pltpu.VMEM((L, SSD_STATE), BF16),
                        pltpu.VMEM((L, LANES), F32),
                        pltpu.VMEM((L, 2 * LANES), BF16),
                        pltpu.VMEM((nc, 2, LANES, CHUNK), F32),
                        pltpu.VMEM((nc, SUBLANES, 4 * GROUP_W), F32),
                        pltpu.VMEM((L, GROUP_W), BF16),
                        pltpu.VMEM((L, 2 * GROUP_W), F32),
                        pltpu.VMEM((L, 2 * GROUP_W), BF16),
                        pltpu.VMEM((nc, SUBLANES, 2 * GROUP_W), F32),
                        pltpu.VMEM((nc, SSD_STATE, 2 * GROUP_W), F32),
                        pltpu.VMEM((nc, SSD_STATE, 2 * GROUP_W), BF16),
                        pltpu.VMEM((SSD_STATE, 2 * GROUP_W), F32)],
        compiler_params=_params(2),
        name="ssd",
    )(*args)


def _proj_pieces(w_all):
    o_z = COL_Z + 2 * GLA_LOWRANK
    o_dt = o_z + SSD_INNER + SSD_CONV_DIM
    w_small = jnp.pad(jnp.concatenate([w_all[COL_Z:o_z], w_all[o_dt:]], axis=0),
                      ((0, LANES - 2 * GLA_LOWRANK - 2 * SSD_HEADS), (0, 0)))
    return {"w_all": w_all, "w_zx": w_all[o_z:o_dt], "w_small": w_small}


def _layer_path(x, mod, lw, raw, h0_gla, h0_ssd, layer, *, n_seq, seq_len, grid_rows,
                want_state, final_w):
    if raw is not None:
        x1, w_all = _ffn_call(x, mod, lw["norm_ffn1"], lw["ffn1_w_in"], lw["ffn1_w_out"],
                              sub=0, seq_len=seq_len, cast=(raw["w_in"],))
        lw.update(_proj_pieces(w_all))
    else:
        x1 = _ffn_call(x, mod, lw["norm_ffn1"], lw["ffn1_w_in"], lw["ffn1_w_out"],
                       sub=0, seq_len=seq_len)
    heads, groups, zs = _proj_call(x1, mod, lw["norm_mix"], lw["w_all"], lw["w_zx"],
                                   lw["w_small"], seq_len=seq_len)
    gla_out = _gla_call(heads, zs, lw["wa_cat"], lw["ba_cat"], lw["gla_norm_w"], h0_gla, layer,
                        n_seq=n_seq, seq_len=seq_len, want_state=want_state)
    late = ("ffn2_w_in", "ffn2_w_out", "w_out")
    ssd_out = _ssd_call(groups, zs, lw["conv_w"], lw["conv_b"], lw["ssd_par"], lw["d_skip_row"],
                        h0_ssd, layer, n_seq=n_seq, seq_len=seq_len, grid_rows=grid_rows,
                        want_state=want_state,
                        cast=tuple(raw[k] for k in late) if raw is not None else ())
    if raw is not None:
        lw.update(zip(late, ssd_out[-len(late):]))
    o, y = gla_out[0], ssd_out[0]
    out = _ffn_call(x1, mod, lw["norm_ffn2"], lw["ffn2_w_in"], lw["ffn2_w_out"],
                    sub=2, seq_len=seq_len, mix=(o, y, zs, lw["ssd_norm_w"], lw["w_out"]),
                    final_w=final_w)
    if want_state:
        return out, gla_out[1], ssd_out[1]
    return out, None, None


def kernel(x_prompt, x_sample, state_gla, state_ssd, c, c_ctx, norm_ffn1, norm_mix, norm_ffn2, w_mod, b_mod, ffn1_w_in, ffn1_w_out, ffn2_w_in, ffn2_w_out, w_in, gla_w_a2, gla_b_a, gla_norm_w, conv_w, conv_b, dt_bias, a_log, d_skip, ssd_norm_w, w_out, final_norm):
    nb, seq, _ = x_prompt.shape
    db, dseq, _ = x_sample.shape
    grid_rows = dseq // GRID_W
    xp = x_prompt.reshape(nb * seq, D_MODEL)
    xs = x_sample.reshape(db * dseq, D_MODEL)
    row = lambda v: v.reshape(1, -1)
    gla_states, ssd_states = [], []
    for i in range(DEPTH):
        last = i == DEPTH - 1
        n_rows = -(-(db + 1) // SUBLANES) * SUBLANES
        cc = jnp.concatenate([c, c_ctx[None, :], jnp.zeros((n_rows - db - 1, D_MODEL), F32)], 0)
        mod = _mod_call(cc, w_mod[i], row(b_mod[i])).reshape(n_rows, N_MOD, D_MODEL)
        mod_lat, mod_ctx = mod[:db], mod[db:db + 1]

        wa_cat = jnp.stack(
            [jnp.pad(gla_w_a2[i, d].reshape(GLA_LOWRANK, GLA_HEADS, GLA_DK),
                     ((lo, LANES - lo - GLA_LOWRANK), (0, 0), (0, 0)))
             for d, lo in ((0, SM_AF), (1, SM_AB))], axis=2).reshape(LANES, 2 * GLA_QK)
        ba_cat = jnp.stack([gla_b_a[i, 0].reshape(GLA_HEADS, GLA_DK),
                            gla_b_a[i, 1].reshape(GLA_HEADS, GLA_DK)], axis=1).reshape(1, 2 * GLA_QK)
        ssd_par = jnp.pad(
            jnp.stack([dt_bias[i].reshape(-1), a_log[i].reshape(-1), jnp.ones((2 * SSD_HEADS,), F32)]),
            ((0, SUBLANES - 3), (SM_DTF, LANES - SM_DTF - 2 * SSD_HEADS)))
        lw = {
            "norm_ffn1": row(norm_ffn1[i]), "norm_mix": row(norm_mix[i]),
            "norm_ffn2": row(norm_ffn2[i]),
            "ffn1_w_in": ffn1_w_in[i].astype(BF16), "ffn1_w_out": ffn1_w_out[i].astype(BF16),
            "wa_cat": wa_cat, "ba_cat": ba_cat,
            "gla_norm_w": row(gla_norm_w[i]), "conv_w": conv_w[i], "conv_b": row(conv_b[i]),
            "ssd_par": ssd_par, "d_skip_row": row(jnp.repeat(d_skip[i], SSD_HEAD_DIM)),
            "ssd_norm_w": row(ssd_norm_w[i]),
        }
        raw = {"w_in": jnp.swapaxes(w_in[i], 0, 1), "ffn2_w_in": ffn2_w_in[i], "ffn2_w_out": ffn2_w_out[i],
               "w_out": w_out[i]}
        fw = row(final_norm) if last else None

        xp, sg, ss = _layer_path(xp, mod_ctx, lw, raw, None, None, i, n_seq=nb, seq_len=seq,
                                 grid_rows=1, want_state=True, final_w=fw)
        xs, _, _ = _layer_path(xs, mod_lat, lw, None, state_gla, state_ssd, i, n_seq=db,
                               seq_len=dseq, grid_rows=grid_rows, want_state=False, final_w=fw)
        gla_states.append(sg)
        ssd_states.append(ss)
    y_prompt = xp.reshape(nb, seq, D_MODEL)
    y_sample = xs.reshape(db, dseq, D_MODEL)
    return (y_prompt, y_sample, jnp.concatenate(gla_states, axis=1),
            jnp.concatenate(ssd_states, axis=1))
```

```python
import functools

import jax
import jax.numpy as jnp
from jax import lax
from jax.experimental import pallas as pl
from jax.experimental.pallas import tpu as pltpu

F32 = jnp.float32
BF16 = jnp.bfloat16

D_MODEL = 1024
DEPTH = 1
GRID_W = 64
CHUNK = 64
EPS = 1e-6
N_MOD = 9
D_FF = 2816
GLA_HEADS = 4
GLA_DK = 128
GLA_DV = 256
GLA_LOWRANK = 16
GLA_TAU = 16.0
GLA_QK = GLA_HEADS * GLA_DK
GLA_V = GLA_HEADS * GLA_DV
SSD_HEADS = 16
SSD_HEAD_DIM = 64
SSD_GROUPS = 2
SSD_STATE = 128
SSD_INNER = SSD_HEADS * SSD_HEAD_DIM
SSD_BC = SSD_GROUPS * SSD_STATE
SSD_CONV_DIM = SSD_INNER + 2 * SSD_BC
CONV_K = 3
D_MIX = GLA_V + SSD_INNER

LOG2_E = 1.4426950408889634
LANES = 128
SUBLANES = 8
BF16_SUBLANES = 16
VMEM_LIMIT_BYTES = 56 * 1024 * 1024

COL_Q = 0
COL_K = COL_Q + GLA_QK
COL_V = COL_K + GLA_QK
COL_R = COL_V + GLA_V
COL_Z = COL_R + GLA_V
HEAD_K = GLA_DK
HEAD_V = 2 * GLA_DK
HEAD_R = HEAD_V + GLA_DV
HEAD_W = HEAD_R + GLA_DV
GRP_B = SSD_INNER // SSD_GROUPS
GRP_C = GRP_B + SSD_STATE
GRP_W = GRP_C + SSD_STATE
ZS_SMALL = SSD_INNER
ZS_W = ZS_SMALL + LANES
SM_AF = 0
SM_AB = SM_AF + GLA_LOWRANK
SM_DTF = SM_AB + GLA_LOWRANK
SM_DTB = SM_DTF + SSD_HEADS
HEADS_PER_GROUP = SSD_HEADS // SSD_GROUPS
GROUP_W = HEADS_PER_GROUP * SSD_HEAD_DIM

TOKEN_TILE = 512
GLA_UNROLL = 16
SSD_UNROLL = 16
MXU_TILE = 256
TERM_STRIDE = 2 * SSD_HEADS
SCAN_BLOCK_TOKENS = 1024


def _dot(a, b):
    return jnp.dot(a, b, preferred_element_type=F32)


def _dot_nt(a, b):
    return lax.dot_general(a, b, (((1,), (1,)), ((), ())), preferred_element_type=F32)


def _dot_tn(a, b):
    return lax.dot_general(a, b, (((0,), (0,)), ((), ())), preferred_element_type=F32)


def _silu(x):
    return x * jax.nn.sigmoid(x)


def _log1p_exp_neg_abs(x):
    return jnp.log(1.0 + jnp.exp(-jnp.abs(x)))


def _softplus(x):
    return jnp.maximum(x, 0.0) + _log1p_exp_neg_abs(x)


def _log_sigmoid(x):
    return jnp.minimum(x, 0.0) - _log1p_exp_neg_abs(x)


def _rmsnorm(x, w):
    ms = jnp.mean(x * x, axis=-1, keepdims=True)
    return x * lax.rsqrt(ms + EPS) * w


def _resident(shape):
    nd = len(shape)
    return pl.BlockSpec(shape, lambda *_: (0,) * nd, pipeline_mode=pl.Buffered(1))


def _params(n_axes):
    return pltpu.CompilerParams(dimension_semantics=("arbitrary",) * n_axes,
                                vmem_limit_bytes=VMEM_LIMIT_BYTES)


def _cast_plumbing(cast, n_steps, step_of):
    in_specs, out_specs, out_shape = [], [], []
    for w in cast:
        rows, cols = w.shape
        per_step = -(-rows // n_steps)
        per_step = -(-per_step // BF16_SUBLANES) * BF16_SUBLANES
        last = -(-rows // per_step) - 1
        blk = pl.BlockSpec((per_step, cols),
                           lambda *g, last=last: (jnp.minimum(step_of(*g), last), 0))
        in_specs.append(blk)
        out_specs.append(blk)
        out_shape.append(jax.ShapeDtypeStruct((rows, cols), BF16))
    return in_specs, out_specs, out_shape


def _cast_blocks(src_refs, dst_refs):
    for src_ref, dst_ref in zip(src_refs, dst_refs):
        dst_ref[...] = src_ref[...].astype(dst_ref.dtype)


def _mod_body(c_ref, w_ref, b_ref, out_ref):
    a = _silu(c_ref[...]).astype(BF16)
    out_ref[...] = _dot(a, w_ref[...].astype(BF16)) + b_ref[...]


def _mod_call(cc, w_mod, b_mod):
    n_rows = cc.shape[0]
    tn = D_MODEL
    return pl.pallas_call(
        _mod_body,
        grid=(N_MOD * D_MODEL // tn,),
        in_specs=[pl.BlockSpec((n_rows, D_MODEL), lambda j: (0, 0)),
                  pl.BlockSpec((D_MODEL, tn), lambda j: (0, j)),
                  pl.BlockSpec((1, tn), lambda j: (0, j))],
        out_specs=pl.BlockSpec((n_rows, tn), lambda j: (0, j)),
        out_shape=jax.ShapeDtypeStruct((n_rows, N_MOD * D_MODEL), F32),
        compiler_params=_params(1),
        name="mod",
    )(cc, w_mod, b_mod)


def _ffn_body(*refs, sub, has_mix, has_final, n_cast):
    it = iter(refs)
    x_ref, mod_ref, nw_ref, win_ref, wout_ref = (next(it) for _ in range(5))
    if has_mix:
        o_ref, y_ref, z_ref, snw_ref, wo_ref = (next(it) for _ in range(5))
    if has_final:
        fn_ref = next(it)
    cast_src = [next(it) for _ in range(n_cast)]
    out_ref = next(it)
    _cast_blocks(cast_src, [next(it) for _ in range(n_cast)])

    x = x_ref[...]
    if has_mix:
        g2 = mod_ref[0, 5:6, :]
        yn = _rmsnorm(y_ref[...] * _silu(z_ref[...]), snw_ref[...])
        m = _dot(o_ref[...], wo_ref[:GLA_V, :]) + _dot(yn.astype(BF16), wo_ref[GLA_V:, :])
        x = x + g2 * m
    sh = mod_ref[0, 3 * sub:3 * sub + 1, :]
    sc = mod_ref[0, 3 * sub + 1:3 * sub + 2, :]
    gate = mod_ref[0, 3 * sub + 2:3 * sub + 3, :]
    h = (_rmsnorm(x, nw_ref[...]) * (1.0 + sc) + sh).astype(BF16)
    g = _dot(h, win_ref[:, :D_FF])
    u = _dot(h, win_ref[:, D_FF:])
    act = (_silu(g) * u).astype(BF16)
    x = x + (0.5 * gate) * _dot(act, wout_ref[...])
    if has_final:
        x = _rmsnorm(x, fn_ref[...])
    out_ref[...] = x


def _ffn_call(x, mod, norm_w, w_in, w_out, *, sub, seq_len, mix=None, final_w=None, cast=()):
    m_tok = x.shape[0]
    tm = TOKEN_TILE
    tiles_per_seq = seq_len // tm
    shared_mod = mod.shape[0] == 1
    mod_map = (lambda i: (0, 0, 0)) if shared_mod else (lambda i: (i // tiles_per_seq, 0, 0))
    tok = lambda width: pl.BlockSpec((tm, width), lambda i: (i, 0))
    in_specs = [tok(D_MODEL), pl.BlockSpec((1, N_MOD, D_MODEL), mod_map),
                _resident((1, D_MODEL)), _resident(w_in.shape), _resident(w_out.shape)]
    args = [x, mod, norm_w, w_in, w_out]
    if mix is not None:
        o, y, proj, ssd_norm_w, w_mix_out = mix
        in_specs += [tok(GLA_V), tok(SSD_INNER),
                     pl.BlockSpec((tm, SSD_INNER), lambda i: (i, 0)),
                     _resident((1, SSD_INNER)), _resident(w_mix_out.shape)]
        args += [o, y, proj, ssd_norm_w, w_mix_out]
    if final_w is not None:
        in_specs.append(_resident((1, D_MODEL)))
        args.append(final_w)
    n_steps = m_tok // tm
    cast_in, cast_out, cast_shape = _cast_plumbing(cast, n_steps, lambda i: i)
    in_specs += cast_in
    args += list(cast)
    out_specs = [tok(D_MODEL)] + cast_out
    out_shape = [jax.ShapeDtypeStruct((m_tok, D_MODEL), F32)] + cast_shape
    body = functools.partial(_ffn_body, sub=sub, has_mix=mix is not None,
                             has_final=final_w is not None, n_cast=len(cast))
    outs = pl.pallas_call(
        body,
        grid=(n_steps,),
        in_specs=in_specs,
        out_specs=out_specs,
        out_shape=out_shape,
        compiler_params=_params(1),
        name="ffn_mix" if mix is not None else "ffn",
    )(*args)
    return outs if cast else outs[0]


def _proj_body(x_ref, mod_ref, nw_ref, wa_ref, wb_ref, wc_ref, heads_ref, groups_ref, zs_ref):
    sh = mod_ref[0, 3:4, :]
    sc = mod_ref[0, 4:5, :]
    h = (_rmsnorm(x_ref[...], nw_ref[...]) * (1.0 + sc) + sh).astype(BF16)
    qkvr = _dot_nt(h, wa_ref[...])
    for hd in range(GLA_HEADS):
        for dst, src, wid in ((0, COL_Q + hd * GLA_DK, GLA_DK), (HEAD_K, COL_K + hd * GLA_DK, GLA_DK),
                              (HEAD_V, COL_V + hd * GLA_DV, GLA_DV), (HEAD_R, COL_R + hd * GLA_DV, GLA_DV)):
            heads_ref[hd, :, dst:dst + wid] = qkvr[:, src:src + wid]
    zx = _dot_nt(h, wb_ref[...])
    zs_ref[:, 0:ZS_SMALL] = zx[:, 0:SSD_INNER]
    for g in range(SSD_GROUPS):
        for dst, src, wid in ((0, g * GRP_B, GRP_B), (GRP_B, SSD_INNER + g * SSD_STATE, SSD_STATE),
                              (GRP_C, SSD_INNER + SSD_BC + g * SSD_STATE, SSD_STATE)):
            groups_ref[g, :, dst:dst + wid] = zx[:, SSD_INNER + src:SSD_INNER + src + wid]
    zs_ref[:, ZS_SMALL:ZS_W] = _dot_nt(h, wc_ref[...])


def _proj_call(x, mod, norm_w, w_all, w_zx, w_small, *, seq_len):
    m_tok = x.shape[0]
    tm = TOKEN_TILE
    tiles_per_seq = seq_len // tm
    shared_mod = mod.shape[0] == 1
    mod_map = (lambda i: (0, 0, 0)) if shared_mod else (lambda i: (i // tiles_per_seq, 0, 0))
    return pl.pallas_call(
        _proj_body,
        grid=(m_tok // tm,),
        in_specs=[pl.BlockSpec((tm, D_MODEL), lambda i: (i, 0)),
                  pl.BlockSpec((1, N_MOD, D_MODEL), mod_map),
                  _resident((1, D_MODEL)), _resident((COL_Z, D_MODEL)),
                  _resident(w_zx.shape), _resident(w_small.shape)],
        out_specs=[pl.BlockSpec((GLA_HEADS, tm, HEAD_W), lambda i: (0, i, 0)),
                   pl.BlockSpec((SSD_GROUPS, tm, GRP_W), lambda i: (0, i, 0)),
                   pl.BlockSpec((tm, ZS_W), lambda i: (i, 0))],
        out_shape=[jax.ShapeDtypeStruct((GLA_HEADS, m_tok, HEAD_W), F32),
                   jax.ShapeDtypeStruct((SSD_GROUPS, m_tok, GRP_W), F32),
                   jax.ShapeDtypeStruct((m_tok, ZS_W), F32)],
        compiler_params=_params(1),
        name="proj",
    )(x, mod, norm_w, w_all, w_zx, w_small)


def _tri_masks():
    row = lax.broadcasted_iota(jnp.int32, (CHUNK, CHUNK), 0)
    col = lax.broadcasted_iota(jnp.int32, (CHUNK, CHUNK), 1)
    return row >= col, row <= col


def _gla_body(*refs, seq_len, n_sub, has_h0, want_state):
    it = iter(refs)
    hd_ref, sm_ref, wa_ref, ba_ref, nw_ref = (next(it) for _ in range(5))
    h0_ref = next(it) if has_h0 else None
    o_ref = next(it)
    st_ref = next(it) if want_state else None
    (la_scr, qk_scr, qg_scr, ks_scr, vb_scr, sc_scr, kv_scr, dec_scr,
     sprev_scr) = (next(it) for _ in range(9))

    nc = seq_len // CHUNK
    nct = n_sub * nc
    dk = GLA_DK
    lower, upper = _tri_masks()
    lower_b = jnp.where(lower, 1.0, 0.0).astype(BF16)
    upper_b = jnp.where(upper, 1.0, 0.0).astype(BF16)
    unroll = min(GLA_UNROLL, nct)
    rows = lambda c: pl.ds(pl.multiple_of(c * CHUNK, CHUNK), CHUNK)

    pre = _dot(sm_ref[...].astype(BF16), wa_ref[...].astype(BF16)) + ba_ref[...]
    la_scr[...] = _log_sigmoid(pre) * (LOG2_E / GLA_TAU)

    def prep(c, carry):
        sl = rows(c)
        qc = hd_ref[0, sl, 0:HEAD_K] * (GLA_DK ** -0.5)
        kc = hd_ref[0, sl, HEAD_K:HEAD_V]
        vb_scr[sl, :] = hd_ref[0, sl, HEAD_V:HEAD_R].astype(BF16)
        la = la_scr[sl, :]
        hi = la.astype(BF16)
        lo = (la - hi.astype(F32)).astype(BF16)

        def cumulative(tri_b, a):
            p = _dot(tri_b, jnp.concatenate([hi[:, a:a + dk], lo[:, a:a + dk]], axis=1))
            return p[:, 0:dk] + p[:, dk:2 * dk]

        g_f = cumulative(lower_b, 0)
        g_b = cumulative(upper_b, dk)
        mid_f = g_f[CHUNK // 2:CHUNK // 2 + 1, :]
        mid_b = g_b[CHUNK - 1 - CHUNK // 2:CHUNK - CHUNK // 2, :]
        end_f = g_f[CHUNK - 1:CHUNK, :]
        end_b = g_b[0:1, :]
        qk_scr[sl, 0:dk] = (qc * jnp.exp2(g_f - mid_f)).astype(BF16)
        qk_scr[sl, dk:2 * dk] = (qc * jnp.exp2(g_b - mid_b)).astype(BF16)
        qk_scr[sl, 2 * dk:3 * dk] = (kc * jnp.exp2(mid_f - g_f)).astype(BF16)
        qk_scr[sl, 3 * dk:4 * dk] = (kc * jnp.exp2(mid_b - g_b)).astype(BF16)
        qg_scr[sl, 0:dk] = (qc * jnp.exp2(g_f)).astype(BF16)
        qg_scr[sl, dk:2 * dk] = (qc * jnp.exp2(g_b)).astype(BF16)
        ks_scr[sl, 0:dk] = (kc * jnp.exp2(end_f - g_f)).astype(BF16)
        ks_scr[sl, dk:2 * dk] = (kc * jnp.exp2(end_b - g_b)).astype(BF16)
        dec = jnp.exp2(jnp.concatenate([end_f, end_b], axis=1))
        dec_scr[c] = jnp.broadcast_to(dec, (SUBLANES, 2 * dk)).T
        return carry

    lax.fori_loop(0, nct, prep, 0, unroll=unroll)

    def products(c, carry):
        sl = rows(c)
        s_f = _dot_nt(qk_scr[sl, 0:dk], qk_scr[sl, 2 * dk:3 * dk])
        s_b = _dot_nt(qk_scr[sl, dk:2 * dk], qk_scr[sl, 3 * dk:4 * dk])
        sc_scr[sl, :] = (jnp.where(lower, s_f, 0.0) + jnp.where(upper, s_b, 0.0)).astype(BF16)
        kv_scr[c] = _dot_tn(ks_scr[sl, :], vb_scr[sl, :])
        return carry

    lax.fori_loop(0, nct, products, 0, unroll=unroll)

    for b in range(n_sub):
        for d, a in ((0, 0), (1, dk)):
            state0 = h0_ref[b, 0, d, 0] if has_h0 else jnp.zeros((dk, GLA_DV), F32)

            def recur(i, state, b=b, d=d, a=a):
                c = b * nc + (i if d == 0 else nc - 1 - i)
                sprev_scr[c, a:a + dk, :] = state.astype(BF16)
                return state * dec_scr[c, a:a + dk, 0:1] + kv_scr[c, a:a + dk, :]

            state = lax.fori_loop(0, nc, recur, state0, unroll=min(nc, 4))
            if want_state:
                st_ref[b, 0, d, 0] = state

    def finish(c, carry):
        sl = rows(c)
        o = _dot(sc_scr[sl, :], vb_scr[sl, :]) + _dot(qg_scr[sl, :], sprev_scr[c])
        o_ref[sl, :] = (_rmsnorm(o, nw_ref[...]) * _silu(hd_ref[0, sl, HEAD_R:HEAD_W])).astype(o_ref.dtype)
        return carry

    lax.fori_loop(0, nct, finish, 0, unroll=unroll)


def _gla_call(heads, zs, wa_cat, ba_cat, norm_w, h0, layer, *, n_seq, seq_len, want_state):
    n_sub = max(1, SCAN_BLOCK_TOKENS // seq_len)
    L = n_sub * seq_len
    nc = L // CHUNK
    in_specs = [pl.BlockSpec((1, L, HEAD_W), lambda s, h: (h, s, 0)),
                pl.BlockSpec((L, LANES), lambda s, h: (s, ZS_SMALL // LANES)),
                pl.BlockSpec((LANES, 2 * GLA_DK), lambda s, h: (0, h)),
                pl.BlockSpec((1, 2 * GLA_DK), lambda s, h: (0, h)),
                pl.BlockSpec((1, GLA_DV), lambda s, h: (0, 0))]
    args = [heads, zs, wa_cat, ba_cat, norm_w]
    state_blk = (n_sub, 1, 2, 1, GLA_DK, GLA_DV)
    if h0 is not None:
        in_specs.append(pl.BlockSpec(state_blk, lambda s, h: (s, layer, 0, h, 0, 0)))
        args.append(h0)
    out_specs = [pl.BlockSpec((L, GLA_DV), lambda s, h: (s, h))]
    out_shape = [jax.ShapeDtypeStruct((n_seq * seq_len, GLA_V), BF16)]
    if want_state:
        out_specs.append(pl.BlockSpec(state_blk, lambda s, h: (s, 0, 0, h, 0, 0)))
        out_shape.append(jax.ShapeDtypeStruct((n_seq, 1, 2, GLA_HEADS, GLA_DK, GLA_DV), F32))
    body = functools.partial(_gla_body, seq_len=seq_len, n_sub=n_sub, has_h0=h0 is not None,
                             want_state=want_state)
    return pl.pallas_call(
        body,
        grid=(n_seq // n_sub, GLA_HEADS),
        in_specs=in_specs,
        out_specs=out_specs,
        out_shape=out_shape,
        scratch_shapes=[pltpu.VMEM((L, 2 * GLA_DK), F32),
                        pltpu.VMEM((L, 4 * GLA_DK), BF16),
                        pltpu.VMEM((L, 2 * GLA_DK), BF16),
                        pltpu.VMEM((L, 2 * GLA_DK), BF16),
                        pltpu.VMEM((L, GLA_DV), BF16),
                        pltpu.VMEM((L, CHUNK), BF16),
                        pltpu.VMEM((nc, 2 * GLA_DK, GLA_DV), F32),
                        pltpu.VMEM((nc, 2 * GLA_DK, SUBLANES), F32),
                        pltpu.VMEM((nc, 2 * GLA_DK, GLA_DV), BF16)],
        compiler_params=_params(2),
        name="gla",
    )(*args)


def _ssd_body(*refs, seq_len, grid_rows, has_h0, want_state, n_cast):
    it = iter(refs)
    (grp_ref, sm_ref, cwx_ref, cwb_ref, cwc_ref, cbx_ref, cbb_ref, cbc_ref,
     par_ref, dsk_ref) = (next(it) for _ in range(10))
    h0_ref = next(it) if has_h0 else None
    cast_src = [next(it) for _ in range(n_cast)]
    y_ref = next(it)
    st_ref = next(it) if want_state else None
    _cast_blocks(cast_src, [next(it) for _ in range(n_cast)])
    (pad_scr, xs_s, b_s, c_s, dt_scr, xy_scr, tr_scr, rows_scr, sc_scr, ed_scr, xw_scr, dec_scr,
     cs_scr, sprev_scr, state_scr) = (next(it) for _ in range(15))

    L = seq_len
    nc = L // CHUNK
    width = L // grid_rows
    pad = pad_scr.shape[0] - L
    pad //= 2
    grp = pl.program_id(1)

    def conv_into(c0, cw_ref, cb_ref, dst_ref):
        ch = dst_ref.shape[1]
        pad_scr[0:pad, 0:ch] = jnp.zeros((pad, ch), F32)
        pad_scr[pad + L:pad + L + pad, 0:ch] = jnp.zeros((pad, ch), F32)
        pad_scr[pad:pad + L, 0:ch] = grp_ref[0, :, c0:c0 + ch]
        rc = min(2 * CHUNK, L)
        col = lax.broadcasted_iota(jnp.int32, (rc, ch), 0) % width
        di_taps = range(CONV_K) if grid_rows > 1 else (CONV_K // 2,)
        for r0 in range(0, L, rc):
            acc = jnp.broadcast_to(cb_ref[...], (rc, ch))
            for dj in range(CONV_K):
                inner = None
                for di in di_taps:
                    off = pad + r0 + (di - 1) * width + (dj - 1)
                    term = cw_ref[di, dj:dj + 1, :] * pad_scr[off:off + rc, 0:ch]
                    inner = term if inner is None else inner + term
                if grid_rows > 1 and dj == 0:
                    inner = jnp.where(col >= 1, inner, 0.0)
                if grid_rows > 1 and dj == CONV_K - 1:
                    inner = jnp.where(col <= width - 2, inner, 0.0)
                acc = acc + inner
            dst_ref[r0:r0 + rc, :] = _silu(acc).astype(dst_ref.dtype)

    conv_into(0, cwx_ref, cbx_ref, xs_s)
    conv_into(GRP_B, cwb_ref, cbb_ref, b_s)
    conv_into(GRP_C, cwc_ref, cbc_ref, c_s)

    gw = GROUP_W
    unroll = min(SSD_UNROLL, nc)
    rows = lambda c: pl.ds(pl.multiple_of(c * CHUNK, CHUNK), CHUNK)
    bias_row = par_ref[0:1, :]
    a_row = -jnp.exp(par_ref[1:2, :]) * par_ref[2:3, :] * LOG2_E
    lower, upper = _tri_masks()
    lower_b = jnp.where(lower, 1.0, 0.0).astype(BF16)
    upper_b = jnp.where(upper, 1.0, 0.0).astype(BF16)
    lane = lax.broadcasted_iota(jnp.int32, (CHUNK, LANES), 1)
    dt_lanes = (lane >= SM_DTF) & (lane < SM_DTF + TERM_STRIDE)
    src = lax.broadcasted_iota(jnp.int32, (LANES, 2 * gw), 0)
    dst = lax.broadcasted_iota(jnp.int32, (LANES, 2 * gw), 1)
    dst_slot = (grp * HEADS_PER_GROUP + (lax.shift_right_logical(dst, 6) & (HEADS_PER_GROUP - 1))
                + jnp.where(dst >= gw, SSD_HEADS, 0))
    expand = jnp.where((src >= SM_DTF) & ((src & (TERM_STRIDE - 1)) == dst_slot),
                       1.0, 0.0).astype(BF16)
    t_idx = lax.broadcasted_iota(jnp.int32, (CHUNK, gw), 0)
    s_idx = lax.broadcasted_iota(jnp.int32, (CHUNK, gw), 1) & (CHUNK - 1)
    diag = t_idx == s_idx
    blk_r = lax.shift_right_logical(lax.broadcasted_iota(jnp.int32, (MXU_TILE, MXU_TILE), 0), 6)
    blk_c = lax.shift_right_logical(lax.broadcasted_iota(jnp.int32, (MXU_TILE, MXU_TILE), 1), 6)
    same_head = blk_r == blk_c

    dt_scr[...] = _softplus(sm_ref[...] + bias_row)

    def place3(v):
        hi = v.astype(BF16).astype(F32)
        rest = v - hi
        mid = rest.astype(BF16).astype(F32)
        lo = rest - mid
        keep = lambda t: jnp.where(dt_lanes, t, 0.0)
        out = (keep(hi) + pltpu.roll(keep(mid), TERM_STRIDE, axis=1)
               + pltpu.roll(keep(lo), 2 * TERM_STRIDE, axis=1))
        return out.astype(BF16)

    def cumulate(c, carry):
        sl = rows(c)
        dt = dt_scr[sl, :]
        da = dt * a_row
        hi = da.astype(BF16)
        both = jnp.concatenate([hi, (da - hi.astype(F32)).astype(BF16)], axis=1)
        pf = _dot(lower_b, both)
        pb = _dot(upper_b, both)
        cum = jnp.where(lane < SM_DTB, pf[:, 0:LANES] + pf[:, LANES:], pb[:, 0:LANES] + pb[:, LANES:])
        xy_scr[sl, 0:LANES] = place3(cum)
        xy_scr[sl, LANES:2 * LANES] = place3(dt)
        tr_scr[c, 0] = cum.T
        tr_scr[c, 1] = dt.T
        head0 = pl.multiple_of(SM_DTF + grp * HEADS_PER_GROUP, SUBLANES)
        for q, (k, d) in enumerate(((0, 0), (0, 1), (1, 0), (1, 1))):
            slab = tr_scr[c, k, pl.ds(head0 + d * SSD_HEADS, HEADS_PER_GROUP), :]
            row = jnp.concatenate(
                [jnp.broadcast_to(slab[h:h + 1, :], (SUBLANES, CHUNK))
                 for h in range(HEADS_PER_GROUP)], axis=1)
            rows_scr[c, :, q * gw:(q + 1) * gw] = row
        return carry

    lax.fori_loop(0, nc, cumulate, 0, unroll=unroll)

    def weights(c, carry):
        sl = rows(c)
        xc = xs_s[sl, :]
        cum_e = _dot(xy_scr[sl, 0:LANES], expand)
        dt_e = _dot(xy_scr[sl, LANES:2 * LANES], expand)
        cb = _dot_nt(c_s[sl, :], jnp.concatenate([b_s[sl, :]] * HEADS_PER_GROUP, axis=0))
        segs, dt_rows, dec_rows = [], [], []
        for d, (a, last_i) in enumerate(((0, CHUNK - 1), (gw, 0))):
            ce = cum_e[:, a:a + gw]
            de = dt_e[:, a:a + gw]
            segs.append(ce - rows_scr[c, 0:1, d * gw:(d + 1) * gw])
            dt_rows.append(rows_scr[c, 0:1, (2 + d) * gw:(3 + d) * gw])
            cum_last = ce[last_i:last_i + 1, :]
            ed_scr[sl, a:a + gw] = jnp.exp2(ce)
            xw_scr[sl, a:a + gw] = (xc * (jnp.exp2(cum_last - ce) * de)).astype(BF16)
            dec_rows.append(jnp.exp2(cum_last))
        fwd = t_idx >= s_idx
        w = (jnp.exp2(jnp.where(fwd, segs[0], segs[1])) * jnp.where(fwd, dt_rows[0], dt_rows[1])
             + jnp.where(diag, dt_rows[1], 0.0))
        sc_scr[sl, :] = (cb * w).astype(BF16)
        dec_scr[c] = jnp.broadcast_to(jnp.concatenate(dec_rows, axis=1), (SUBLANES, 2 * gw))
        return carry

    lax.fori_loop(0, nc, weights, 0, unroll=unroll)

    def products(c, carry):
        sl = rows(c)
        xc = xs_s[sl, :]
        xb = xc.astype(BF16)
        parts = []
        for j in range(gw // MXU_TILE):
            xh = xb[:, j * MXU_TILE:(j + 1) * MXU_TILE]
            rep = jnp.concatenate([xh] * (MXU_TILE // SSD_HEAD_DIM), axis=0)
            bd = jnp.where(same_head, rep, jnp.zeros_like(rep))
            parts.append(_dot(sc_scr[sl, j * MXU_TILE:(j + 1) * MXU_TILE], bd))
        y_ref[sl, :] = jnp.concatenate(parts, axis=1) + xc * dsk_ref[...]
        cs_scr[c] = _dot_tn(b_s[sl, :], xw_scr[sl, :])
        return carry

    lax.fori_loop(0, nc, products, 0, unroll=unroll)

    pair_w = 2 * SSD_HEAD_DIM
    if has_h0:
        for d in range(2):
            for j in range(HEADS_PER_GROUP // 2):
                pair = jnp.concatenate([h0_ref[0, 0, d, 2 * j], h0_ref[0, 0, d, 2 * j + 1]], axis=0)
                state_scr[:, d * gw + j * pair_w:d * gw + (j + 1) * pair_w] = pair.T
    else:
        state_scr[...] = jnp.zeros_like(state_scr)

    for d in range(2):
        for a in range(d * gw, (d + 1) * gw, MXU_TILE):

            def recur(i, state, d=d, a=a):
                c = i if d == 0 else nc - 1 - i
                sprev_scr[c, :, a:a + MXU_TILE] = state.astype(BF16)
                return (state * dec_scr[c, 0:1, a:a + MXU_TILE]
                        + cs_scr[c, :, a:a + MXU_TILE])

            state_scr[:, a:a + MXU_TILE] = lax.fori_loop(
                0, nc, recur, state_scr[:, a:a + MXU_TILE], unroll=min(nc, 4))

    def finish(c, carry):
        sl = rows(c)
        yi = _dot(c_s[sl, :], sprev_scr[c]) * ed_scr[sl, :]
        y_ref[sl, :] += yi[:, 0:gw] + yi[:, gw:2 * gw]
        return carry

    lax.fori_loop(0, nc, finish, 0, unroll=unroll)

    if want_state:
        for d in range(2):
            for j in range(HEADS_PER_GROUP // 2):
                pair = state_scr[:, d * gw + j * pair_w:d * gw + (j + 1) * pair_w].T
                st_ref[0, 0, d, 2 * j] = pair[0:SSD_HEAD_DIM, :]
                st_ref[0, 0, d, 2 * j + 1] = pair[SSD_HEAD_DIM:pair_w, :]


def _ssd_call(groups, zs, conv_w, conv_b, par, dsk, h0, layer, *, n_seq, seq_len, grid_rows,
              want_state, cast=()):
    L = seq_len
    in_specs = [pl.BlockSpec((1, L, GRP_W), lambda s, g: (g, s, 0)),
                pl.BlockSpec((L, LANES), lambda s, g: (s, ZS_SMALL // LANES)),
                pl.BlockSpec((CONV_K, CONV_K, GROUP_W), lambda s, g: (0, 0, g)),
                pl.BlockSpec((CONV_K, CONV_K, SSD_STATE),
                             lambda s, g: (0, 0, SSD_INNER // SSD_STATE + g)),
                pl.BlockSpec((CONV_K, CONV_K, SSD_STATE),
                             lambda s, g: (0, 0, (SSD_INNER + SSD_BC) // SSD_STATE + g)),
                pl.BlockSpec((1, GROUP_W), lambda s, g: (0, g)),
                pl.BlockSpec((1, SSD_STATE), lambda s, g: (0, SSD_INNER // SSD_STATE + g)),
                pl.BlockSpec((1, SSD_STATE),
                             lambda s, g: (0, (SSD_INNER + SSD_BC) // SSD_STATE + g)),
                pl.BlockSpec((SUBLANES, LANES), lambda s, g: (0, 0)),
                pl.BlockSpec((1, GROUP_W), lambda s, g: (0, g))]
    args = [groups, zs, conv_w, conv_w, conv_w, conv_b, conv_b, conv_b, par, dsk]
    state_blk = (1, 1, 2, HEADS_PER_GROUP, SSD_HEAD_DIM, SSD_STATE)
    if h0 is not None:
        in_specs.append(pl.BlockSpec(state_blk, lambda s, g: (s, layer, 0, g, 0, 0)))
        args.append(h0)
    out_specs = [pl.BlockSpec((L, GROUP_W), lambda s, g: (s, g))]
    out_shape = [jax.ShapeDtypeStruct((n_seq * L, SSD_INNER), F32)]
    if want_state:
        out_specs.append(pl.BlockSpec(state_blk, lambda s, g: (s, 0, 0, g, 0, 0)))
        out_shape.append(jax.ShapeDtypeStruct(
            (n_seq, 1, 2, SSD_HEADS, SSD_HEAD_DIM, SSD_STATE), F32))
    cast_in, cast_out, cast_shape = _cast_plumbing(cast, n_seq * SSD_GROUPS,
                                                   lambda s, g: s * SSD_GROUPS + g)
    in_specs += cast_in
    args += list(cast)
    out_specs += cast_out
    out_shape += cast_shape
    nc = L // CHUNK
    conv_pad = (L // grid_rows + SUBLANES) if grid_rows > 1 else SUBLANES
    body = functools.partial(_ssd_body, seq_len=L, grid_rows=grid_rows,
                             has_h0=h0 is not None, want_state=want_state, n_cast=len(cast))
    return pl.pallas_call(
        body,
        grid=(n_seq, SSD_GROUPS),
        in_specs=in_specs,
        out_specs=out_specs,
        out_shape=out_shape,
        scratch_shapes=[pltpu.VMEM((L + 2 * conv_pad, GROUP_W), F32),
                        pltpu.VMEM((L, GROUP_W), F32),
                        pltpu.VMEM((L, SSD_STATE), BF16),
                        pltpu.VMEM((L, SSD_STATE), BF16),
                        pltpu.VMEM((L, LANES), F32),
                        pltpu.VMEM((L, 2 * LANES), BF16),
                        pltpu.VMEM((nc, 2, LANES, CHUNK), F32),
                        pltpu.VMEM((nc, SUBLANES, 4 * GROUP_W), F32),
                        pltpu.VMEM((L, GROUP_W), BF16),
                        pltpu.VMEM((L, 2 * GROUP_W), F32),
                        pltpu.VMEM((L, 2 * GROUP_W), BF16),
                        pltpu.VMEM((nc, SUBLANES, 2 * GROUP_W), F32),
                        pltpu.VMEM((nc, SSD_STATE, 2 * GROUP_W), F32),
                        pltpu.VMEM((nc, SSD_STATE, 2 * GROUP_W), BF16),
                        pltpu.VMEM((SSD_STATE, 2 * GROUP_W), F32)],
        compiler_params=_params(2),
        name="ssd",
    )(*args)


def _proj_pieces(w_all):
    o_z = COL_Z + 2 * GLA_LOWRANK
    o_dt = o_z + SSD_INNER + SSD_CONV_DIM
    w_small = jnp.pad(jnp.concatenate([w_all[COL_Z:o_z], w_all[o_dt:]], axis=0),
                      ((0, LANES - 2 * GLA_LOWRANK - 2 * SSD_HEADS), (0, 0)))
    return {"w_all": w_all, "w_zx": w_all[o_z:o_dt], "w_small": w_small}


def _layer_path(x, mod, lw, raw, h0_gla, h0_ssd, layer, *, n_seq, seq_len, grid_rows,
                want_state, final_w):
    if raw is not None:
        x1, w_all = _ffn_call(x, mod, lw["norm_ffn1"], lw["ffn1_w_in"], lw["ffn1_w_out"],
                              sub=0, seq_len=seq_len, cast=(raw["w_in"],))
        lw.update(_proj_pieces(w_all))
    else:
        x1 = _ffn_call(x, mod, lw["norm_ffn1"], lw["ffn1_w_in"], lw["ffn1_w_out"],
                       sub=0, seq_len=seq_len)
    heads, groups, zs = _proj_call(x1, mod, lw["norm_mix"], lw["w_all"], lw["w_zx"],
                                   lw["w_small"], seq_len=seq_len)
    gla_out = _gla_call(heads, zs, lw["wa_cat"], lw["ba_cat"], lw["gla_norm_w"], h0_gla, layer,
                        n_seq=n_seq, seq_len=seq_len, want_state=want_state)
    late = ("ffn2_w_in", "ffn2_w_out", "w_out")
    ssd_out = _ssd_call(groups, zs, lw["conv_w"], lw["conv_b"], lw["ssd_par"], lw["d_skip_row"],
                        h0_ssd, layer, n_seq=n_seq, seq_len=seq_len, grid_rows=grid_rows,
                        want_state=want_state,
                        cast=tuple(raw[k] for k in late) if raw is not None else ())
    if raw is not None:
        lw.update(zip(late, ssd_out[-len(late):]))
    o, y = gla_out[0], ssd_out[0]
    out = _ffn_call(x1, mod, lw["norm_ffn2"], lw["ffn2_w_in"], lw["ffn2_w_out"],
                    sub=2, seq_len=seq_len, mix=(o, y, zs, lw["ssd_norm_w"], lw["w_out"]),
                    final_w=final_w)
    if want_state:
        return out, gla_out[1], ssd_out[1]
    return out, None, None


def kernel(x_prompt, x_sample, state_gla, state_ssd, c, c_ctx, norm_ffn1, norm_mix, norm_ffn2, w_mod, b_mod, ffn1_w_in, ffn1_w_out, ffn2_w_in, ffn2_w_out, w_in, gla_w_a2, gla_b_a, gla_norm_w, conv_w, conv_b, dt_bias, a_log, d_skip, ssd_norm_w, w_out, final_norm):
    nb, seq, _ = x_prompt.shape
    db, dseq, _ = x_sample.shape
    grid_rows = dseq // GRID_W
    xp = x_prompt.reshape(nb * seq, D_MODEL)
    xs = x_sample.reshape(db * dseq, D_MODEL)
    row = lambda v: v.reshape(1, -1)
    gla_states, ssd_states = [], []
    for i in range(DEPTH):
        last = i == DEPTH - 1
        n_rows = -(-(db + 1) // SUBLANES) * SUBLANES
        cc = jnp.concatenate([c, c_ctx[None, :], jnp.zeros((n_rows - db - 1, D_MODEL), F32)], 0)
        mod = _mod_call(cc, w_mod[i], row(b_mod[i])).reshape(n_rows, N_MOD, D_MODEL)
        mod_lat, mod_ctx = mod[:db], mod[db:db + 1]

        wa_cat = jnp.stack(
            [jnp.pad(gla_w_a2[i, d].reshape(GLA_LOWRANK, GLA_HEADS, GLA_DK),
                     ((lo, LANES - lo - GLA_LOWRANK), (0, 0), (0, 0)))
             for d, lo in ((0, SM_AF), (1, SM_AB))], axis=2).reshape(LANES, 2 * GLA_QK)
        ba_cat = jnp.stack([gla_b_a[i, 0].reshape(GLA_HEADS, GLA_DK),
                            gla_b_a[i, 1].reshape(GLA_HEADS, GLA_DK)], axis=1).reshape(1, 2 * GLA_QK)
        ssd_par = jnp.pad(
            jnp.stack([dt_bias[i].reshape(-1), a_log[i].reshape(-1), jnp.ones((2 * SSD_HEADS,), F32)]),
            ((0, SUBLANES - 3), (SM_DTF, LANES - SM_DTF - 2 * SSD_HEADS)))
        lw = {
            "norm_ffn1": row(norm_ffn1[i]), "norm_mix": row(norm_mix[i]),
            "norm_ffn2": row(norm_ffn2[i]),
            "ffn1_w_in": ffn1_w_in[i].astype(BF16), "ffn1_w_out": ffn1_w_out[i].astype(BF16),
            "wa_cat": wa_cat, "ba_cat": ba_cat,
            "gla_norm_w": row(gla_norm_w[i]), "conv_w": conv_w[i], "conv_b": row(conv_b[i]),
            "ssd_par": ssd_par, "d_skip_row": row(jnp.repeat(d_skip[i], SSD_HEAD_DIM)),
            "ssd_norm_w": row(ssd_norm_w[i]),
        }
        raw = {"w_in": jnp.swapaxes(w_in[i], 0, 1), "ffn2_w_in": ffn2_w_in[i], "ffn2_w_out": ffn2_w_out[i],
               "w_out": w_out[i]}
        fw = row(final_norm) if last else None

        xp, sg, ss = _layer_path(xp, mod_ctx, lw, raw, None, None, i, n_seq=nb, seq_len=seq,
                                 grid_rows=1, want_state=True, final_w=fw)
        xs, _, _ = _layer_path(xs, mod_lat, lw, None, state_gla, state_ssd, i, n_seq=db,
                               seq_len=dseq, grid_rows=grid_rows, want_state=False, final_w=fw)
        gla_states.append(sg)
        ssd_states.append(ss)
    y_prompt = xp.reshape(nb, seq, D_MODEL)
    y_sample = xs.reshape(db, dseq, D_MODEL)
    return (y_prompt, y_sample, jnp.concatenate(gla_states, axis=1),
            jnp.concatenate(ssd_states, axis=1))
```

```python
import functools

import jax
import jax.numpy as jnp
from jax import lax
from jax.experimental import pallas as pl
from jax.experimental.pallas import tpu as pltpu

F32 = jnp.float32
BF16 = jnp.bfloat16

D_MODEL = 1024
DEPTH = 1
GRID_W = 64
CHUNK = 64
EPS = 1e-6
N_MOD = 9
D_FF = 2816
GLA_HEADS = 4
GLA_DK = 128
GLA_DV = 256
GLA_LOWRANK = 16
GLA_TAU = 16.0
GLA_QK = GLA_HEADS * GLA_DK
GLA_V = GLA_HEADS * GLA_DV
SSD_HEADS = 16
SSD_HEAD_DIM = 64
SSD_GROUPS = 2
SSD_STATE = 128
SSD_INNER = SSD_HEADS * SSD_HEAD_DIM
SSD_BC = SSD_GROUPS * SSD_STATE
SSD_CONV_DIM = SSD_INNER + 2 * SSD_BC
CONV_K = 3
D_MIX = GLA_V + SSD_INNER

LOG2_E = 1.4426950408889634
LANES = 128
SUBLANES = 8
BF16_SUBLANES = 16
VMEM_LIMIT_BYTES = 56 * 1024 * 1024

COL_Q = 0
COL_K = COL_Q + GLA_QK
COL_V = COL_K + GLA_QK
COL_R = COL_V + GLA_V
COL_Z = COL_R + GLA_V
HEAD_K = GLA_DK
HEAD_V = 2 * GLA_DK
HEAD_R = HEAD_V + GLA_DV
HEAD_W = HEAD_R + GLA_DV
GRP_B = SSD_INNER // SSD_GROUPS
GRP_C = GRP_B + SSD_STATE
GRP_W = GRP_C + SSD_STATE
ZS_SMALL = SSD_INNER
ZS_W = ZS_SMALL + LANES
SM_AF = 0
SM_AB = SM_AF + GLA_LOWRANK
SM_DTF = SM_AB + GLA_LOWRANK
SM_DTB = SM_DTF + SSD_HEADS
HEADS_PER_GROUP = SSD_HEADS // SSD_GROUPS
GROUP_W = HEADS_PER_GROUP * SSD_HEAD_DIM

TOKEN_TILE = 512
GLA_UNROLL = 16
SSD_UNROLL = 16
MXU_TILE = 256
TERM_STRIDE = 2 * SSD_HEADS
SCAN_BLOCK_TOKENS = 1024


def _dot(a, b):
    return jnp.dot(a, b, preferred_element_type=F32)


def _dot_nt(a, b):
    return lax.dot_general(a, b, (((1,), (1,)), ((), ())), preferred_element_type=F32)


def _dot_tn(a, b):
    return lax.dot_general(a, b, (((0,), (0,)), ((), ())), preferred_element_type=F32)


def _silu(x):
    return x * jax.nn.sigmoid(x)


def _log1p_exp_neg_abs(x):
    return jnp.log(1.0 + jnp.exp(-jnp.abs(x)))


def _softplus(x):
    return jnp.maximum(x, 0.0) + _log1p_exp_neg_abs(x)


def _log_sigmoid(x):
    return jnp.minimum(x, 0.0) - _log1p_exp_neg_abs(x)


def _rmsnorm(x, w):
    ms = jnp.mean(x * x, axis=-1, keepdims=True)
    return x * lax.rsqrt(ms + EPS) * w


def _resident(shape):
    nd = len(shape)
    return pl.BlockSpec(shape, lambda *_: (0,) * nd, pipeline_mode=pl.Buffered(1))


def _params(n_axes):
    return pltpu.CompilerParams(dimension_semantics=("arbitrary",) * n_axes,
                                vmem_limit_bytes=VMEM_LIMIT_BYTES)


def _cast_plumbing(cast, n_steps, step_of):
    in_specs, out_specs, out_shape = [], [], []
    for w in cast:
        rows, cols = w.shape
        per_step = -(-rows // n_steps)
        per_step = -(-per_step // BF16_SUBLANES) * BF16_SUBLANES
        last = -(-rows // per_step) - 1
        blk = pl.BlockSpec((per_step, cols),
                           lambda *g, last=last: (jnp.minimum(step_of(*g), last), 0))
        in_specs.append(blk)
        out_specs.append(blk)
        out_shape.append(jax.ShapeDtypeStruct((rows, cols), BF16))
    return in_specs, out_specs, out_shape


def _cast_blocks(src_refs, dst_refs):
    for src_ref, dst_ref in zip(src_refs, dst_refs):
        dst_ref[...] = src_ref[...].astype(dst_ref.dtype)


def _mod_body(c_ref, w_ref, b_ref, out_ref):
    a = _silu(c_ref[...]).astype(BF16)
    out_ref[...] = _dot(a, w_ref[...].astype(BF16)) + b_ref[...]


def _mod_call(cc, w_mod, b_mod):
    n_rows = cc.shape[0]
    tn = D_MODEL
    return pl.pallas_call(
        _mod_body,
        grid=(N_MOD * D_MODEL // tn,),
        in_specs=[pl.BlockSpec((n_rows, D_MODEL), lambda j: (0, 0)),
                  pl.BlockSpec((D_MODEL, tn), lambda j: (0, j)),
                  pl.BlockSpec((1, tn), lambda j: (0, j))],
        out_specs=pl.BlockSpec((n_rows, tn), lambda j: (0, j)),
        out_shape=jax.ShapeDtypeStruct((n_rows, N_MOD * D_MODEL), F32),
        compiler_params=_params(1),
        name="mod",
    )(cc, w_mod, b_mod)


def _ffn_body(*refs, sub, has_mix, has_final, n_cast):
    it = iter(refs)
    x_ref, mod_ref, nw_ref, win_ref, wout_ref = (next(it) for _ in range(5))
    if has_mix:
        o_ref, y_ref, z_ref, snw_ref, wo_ref = (next(it) for _ in range(5))
    if has_final:
        fn_ref = next(it)
    cast_src = [next(it) for _ in range(n_cast)]
    out_ref = next(it)
    _cast_blocks(cast_src, [next(it) for _ in range(n_cast)])

    x = x_ref[...]
    if has_mix:
        g2 = mod_ref[0, 5:6, :]
        yn = _rmsnorm(y_ref[...] * _silu(z_ref[...]), snw_ref[...])
        m = _dot(o_ref[...], wo_ref[:GLA_V, :]) + _dot(yn.astype(BF16), wo_ref[GLA_V:, :])
        x = x + g2 * m
    sh = mod_ref[0, 3 * sub:3 * sub + 1, :]
    sc = mod_ref[0, 3 * sub + 1:3 * sub + 2, :]
    gate = mod_ref[0, 3 * sub + 2:3 * sub + 3, :]
    h = (_rmsnorm(x, nw_ref[...]) * (1.0 + sc) + sh).astype(BF16)
    g = _dot(h, win_ref[:, :D_FF])
    u = _dot(h, win_ref[:, D_FF:])
    act = (_silu(g) * u).astype(BF16)
    x = x + (0.5 * gate) * _dot(act, wout_ref[...])
    if has_final:
        x = _rmsnorm(x, fn_ref[...])
    out_ref[...] = x


def _ffn_call(x, mod, norm_w, w_in, w_out, *, sub, seq_len, mix=None, final_w=None, cast=()):
    m_tok = x.shape[0]
    tm = TOKEN_TILE
    tiles_per_seq = seq_len // tm
    shared_mod = mod.shape[0] == 1
    mod_map = (lambda i: (0, 0, 0)) if shared_mod else (lambda i: (i // tiles_per_seq, 0, 0))
    tok = lambda width: pl.BlockSpec((tm, width), lambda i: (i, 0))
    in_specs = [tok(D_MODEL), pl.BlockSpec((1, N_MOD, D_MODEL), mod_map),
                _resident((1, D_MODEL)), _resident(w_in.shape), _resident(w_out.shape)]
    args = [x, mod, norm_w, w_in, w_out]
    if mix is not None:
        o, y, proj, ssd_norm_w, w_mix_out = mix
        in_specs += [tok(GLA_V), tok(SSD_INNER),
                     pl.BlockSpec((tm, SSD_INNER), lambda i: (i, 0)),
                     _resident((1, SSD_INNER)), _resident(w_mix_out.shape)]
        args += [o, y, proj, ssd_norm_w, w_mix_out]
    if final_w is not None:
        in_specs.append(_resident((1, D_MODEL)))
        args.append(final_w)
    n_steps = m_tok // tm
    cast_in, cast_out, cast_shape = _cast_plumbing(cast, n_steps, lambda i: i)
    in_specs += cast_in
    args += list(cast)
    out_specs = [tok(D_MODEL)] + cast_out
    out_shape = [jax.ShapeDtypeStruct((m_tok, D_MODEL), F32)] + cast_shape
    body = functools.partial(_ffn_body, sub=sub, has_mix=mix is not None,
                             has_final=final_w is not None, n_cast=len(cast))
    outs = pl.pallas_call(
        body,
        grid=(n_steps,),
        in_specs=in_specs,
        out_specs=out_specs,
        out_shape=out_shape,
        compiler_params=_params(1),
        name="ffn_mix" if mix is not None else "ffn",
    )(*args)
    return outs if cast else outs[0]


def _proj_body(x_ref, mod_ref, nw_ref, w_ref, heads_ref, groups_ref, zs_ref):
    sh = mod_ref[0, 3:4, :]
    sc = mod_ref[0, 4:5, :]
    h = (_rmsnorm(x_ref[...], nw_ref[...]) * (1.0 + sc) + sh).astype(BF16)
    o_z = COL_Z + 2 * GLA_LOWRANK
    o_dt = o_z + SSD_INNER + SSD_CONV_DIM
    qkvr = _dot_nt(h, w_ref[0:COL_Z, :])
    for hd in range(GLA_HEADS):
        for dst, src, wid in ((0, COL_Q + hd * GLA_DK, GLA_DK), (HEAD_K, COL_K + hd * GLA_DK, GLA_DK),
                              (HEAD_V, COL_V + hd * GLA_DV, GLA_DV), (HEAD_R, COL_R + hd * GLA_DV, GLA_DV)):
            heads_ref[hd, :, dst:dst + wid] = qkvr[:, src:src + wid]
    zx = _dot_nt(h, w_ref[o_z:o_dt, :])
    zs_ref[:, 0:ZS_SMALL] = zx[:, 0:SSD_INNER]
    for g in range(SSD_GROUPS):
        for dst, src, wid in ((0, g * GRP_B, GRP_B), (GRP_B, SSD_INNER + g * SSD_STATE, SSD_STATE),
                              (GRP_C, SSD_INNER + SSD_BC + g * SSD_STATE, SSD_STATE)):
            groups_ref[g, :, dst:dst + wid] = zx[:, SSD_INNER + src:SSD_INNER + src + wid]
    w_small = jnp.concatenate(
        [w_ref[COL_Z:o_z, :], w_ref[o_dt:, :],
         jnp.zeros((LANES - 2 * GLA_LOWRANK - 2 * SSD_HEADS, D_MODEL), BF16)], axis=0)
    zs_ref[:, ZS_SMALL:ZS_W] = _dot_nt(h, w_small)


def _proj_call(x, mod, norm_w, w_all, *, seq_len):
    m_tok = x.shape[0]
    tm = TOKEN_TILE
    tiles_per_seq = seq_len // tm
    shared_mod = mod.shape[0] == 1
    mod_map = (lambda i: (0, 0, 0)) if shared_mod else (lambda i: (i // tiles_per_seq, 0, 0))
    return pl.pallas_call(
        _proj_body,
        grid=(m_tok // tm,),
        in_specs=[pl.BlockSpec((tm, D_MODEL), lambda i: (i, 0)),
                  pl.BlockSpec((1, N_MOD, D_MODEL), mod_map),
                  _resident((1, D_MODEL)), _resident(w_all.shape)],
        out_specs=[pl.BlockSpec((GLA_HEADS, tm, HEAD_W), lambda i: (0, i, 0)),
                   pl.BlockSpec((SSD_GROUPS, tm, GRP_W), lambda i: (0, i, 0)),
                   pl.BlockSpec((tm, ZS_W), lambda i: (i, 0))],
        out_shape=[jax.ShapeDtypeStruct((GLA_HEADS, m_tok, HEAD_W), F32),
                   jax.ShapeDtypeStruct((SSD_GROUPS, m_tok, GRP_W), F32),
                   jax.ShapeDtypeStruct((m_tok, ZS_W), F32)],
        compiler_params=_params(1),
        name="proj",
    )(x, mod, norm_w, w_all)


def _tri_masks():
    row = lax.broadcasted_iota(jnp.int32, (CHUNK, CHUNK), 0)
    col = lax.broadcasted_iota(jnp.int32, (CHUNK, CHUNK), 1)
    return row >= col, row <= col


def _gla_body(*refs, seq_len, n_sub, has_h0, want_state):
    it = iter(refs)
    hd_ref, sm_ref, wa_ref, ba_ref, nw_ref = (next(it) for _ in range(5))
    h0_ref = next(it) if has_h0 else None
    o_ref = next(it)
    st_ref = next(it) if want_state else None
    (la_scr, qk_scr, qg_scr, ks_scr, vb_scr, sc_scr, kv_scr, dec_scr,
     sprev_scr) = (next(it) for _ in range(9))

    nc = seq_len // CHUNK
    nct = n_sub * nc
    dk = GLA_DK
    lower, upper = _tri_masks()
    lower_b = jnp.where(lower, 1.0, 0.0).astype(BF16)
    upper_b = jnp.where(upper, 1.0, 0.0).astype(BF16)
    unroll = min(GLA_UNROLL, nct)
    rows = lambda c: pl.ds(pl.multiple_of(c * CHUNK, CHUNK), CHUNK)

    pre = _dot(sm_ref[...].astype(BF16), wa_ref[...].astype(BF16)) + ba_ref[...]
    la_scr[...] = _log_sigmoid(pre) * (LOG2_E / GLA_TAU)

    def prep(c, carry):
        sl = rows(c)
        qc = hd_ref[0, sl, 0:HEAD_K] * (GLA_DK ** -0.5)
        kc = hd_ref[0, sl, HEAD_K:HEAD_V]
        vb_scr[sl, :] = hd_ref[0, sl, HEAD_V:HEAD_R].astype(BF16)
        la = la_scr[sl, :]
        hi = la.astype(BF16)
        lo = (la - hi.astype(F32)).astype(BF16)

        def cumulative(tri_b, a):
            p = _dot(tri_b, jnp.concatenate([hi[:, a:a + dk], lo[:, a:a + dk]], axis=1))
            return p[:, 0:dk] + p[:, dk:2 * dk]

        g_f = cumulative(lower_b, 0)
        g_b = cumulative(upper_b, dk)
        mid_f = g_f[CHUNK // 2:CHUNK // 2 + 1, :]
        mid_b = g_b[CHUNK - 1 - CHUNK // 2:CHUNK - CHUNK // 2, :]
        end_f = g_f[CHUNK - 1:CHUNK, :]
        end_b = g_b[0:1, :]
        qk_scr[sl, 0:dk] = (qc * jnp.exp2(g_f - mid_f)).astype(BF16)
        qk_scr[sl, dk:2 * dk] = (qc * jnp.exp2(g_b - mid_b)).astype(BF16)
        qk_scr[sl, 2 * dk:3 * dk] = (kc * jnp.exp2(mid_f - g_f)).astype(BF16)
        qk_scr[sl, 3 * dk:4 * dk] = (kc * jnp.exp2(mid_b - g_b)).astype(BF16)
        qg_scr[sl, 0:dk] = (qc * jnp.exp2(g_f)).astype(BF16)
        qg_scr[sl, dk:2 * dk] = (qc * jnp.exp2(g_b)).astype(BF16)
        ks_scr[sl, 0:dk] = (kc * jnp.exp2(end_f - g_f)).astype(BF16)
        ks_scr[sl, dk:2 * dk] = (kc * jnp.exp2(end_b - g_b)).astype(BF16)
        dec = jnp.exp2(jnp.concatenate([end_f, end_b], axis=1))
        dec_scr[c] = jnp.broadcast_to(dec, (SUBLANES, 2 * dk))
        return carry

    lax.fori_loop(0, nct, prep, 0, unroll=unroll)

    def products(c, carry):
        sl = rows(c)
        s_f = _dot_nt(qk_scr[sl, 0:dk], qk_scr[sl, 2 * dk:3 * dk])
        s_b = _dot_nt(qk_scr[sl, dk:2 * dk], qk_scr[sl, 3 * dk:4 * dk])
        sc_scr[sl, :] = (jnp.where(lower, s_f, 0.0) + jnp.where(upper, s_b, 0.0)).astype(BF16)
        kv_scr[c] = _dot_tn(vb_scr[sl, :], ks_scr[sl, :])
        return carry

    lax.fori_loop(0, nct, products, 0, unroll=unroll)

    for b in range(n_sub):
        for d, a in ((0, 0), (1, dk)):
            state0 = h0_ref[b, 0, d, 0].T if has_h0 else jnp.zeros((GLA_DV, dk), F32)

            def recur(i, state, b=b, d=d, a=a):
                c = b * nc + (i if d == 0 else nc - 1 - i)
                sprev_scr[c, :, a:a + dk] = state.astype(BF16)
                return state * dec_scr[c, 0:1, a:a + dk] + kv_scr[c, :, a:a + dk]

            state = lax.fori_loop(0, nc, recur, state0, unroll=min(nc, 4))
            if want_state:
                st_ref[b, 0, d, 0] = state.T

    def finish(c, carry):
        sl = rows(c)
        o = _dot(sc_scr[sl, :], vb_scr[sl, :]) + _dot_nt(qg_scr[sl, :], sprev_scr[c])
        o_ref[sl, :] = (_rmsnorm(o, nw_ref[...]) * _silu(hd_ref[0, sl, HEAD_R:HEAD_W])).astype(o_ref.dtype)
        return carry

    lax.fori_loop(0, nct, finish, 0, unroll=unroll)


def _gla_call(heads, zs, wa_cat, ba_cat, norm_w, h0, layer, *, n_seq, seq_len, want_state):
    n_sub = max(1, SCAN_BLOCK_TOKENS // seq_len)
    L = n_sub * seq_len
    nc = L // CHUNK
    in_specs = [pl.BlockSpec((1, L, HEAD_W), lambda s, h: (h, s, 0)),
                pl.BlockSpec((L, LANES), lambda s, h: (s, ZS_SMALL // LANES)),
                pl.BlockSpec((LANES, 2 * GLA_DK), lambda s, h: (0, h)),
                pl.BlockSpec((1, 2 * GLA_DK), lambda s, h: (0, h)),
                pl.BlockSpec((1, GLA_DV), lambda s, h: (0, 0))]
    args = [heads, zs, wa_cat, ba_cat, norm_w]
    state_blk = (n_sub, 1, 2, 1, GLA_DK, GLA_DV)
    if h0 is not None:
        in_specs.append(pl.BlockSpec(state_blk, lambda s, h: (s, layer, 0, h, 0, 0)))
        args.append(h0)
    out_specs = [pl.BlockSpec((L, GLA_DV), lambda s, h: (s, h))]
    out_shape = [jax.ShapeDtypeStruct((n_seq * seq_len, GLA_V), BF16)]
    if want_state:
        out_specs.append(pl.BlockSpec(state_blk, lambda s, h: (s, 0, 0, h, 0, 0)))
        out_shape.append(jax.ShapeDtypeStruct((n_seq, 1, 2, GLA_HEADS, GLA_DK, GLA_DV), F32))
    body = functools.partial(_gla_body, seq_len=seq_len, n_sub=n_sub, has_h0=h0 is not None,
                             want_state=want_state)
    return pl.pallas_call(
        body,
        grid=(n_seq // n_sub, GLA_HEADS),
        in_specs=in_specs,
        out_specs=out_specs,
        out_shape=out_shape,
        scratch_shapes=[pltpu.VMEM((L, 2 * GLA_DK), F32),
                        pltpu.VMEM((L, 4 * GLA_DK), BF16),
                        pltpu.VMEM((L, 2 * GLA_DK), BF16),
                        pltpu.VMEM((L, 2 * GLA_DK), BF16),
                        pltpu.VMEM((L, GLA_DV), BF16),
                        pltpu.VMEM((L, CHUNK), BF16),
                        pltpu.VMEM((nc, GLA_DV, 2 * GLA_DK), F32),
                        pltpu.VMEM((nc, SUBLANES, 2 * GLA_DK), F32),
                        pltpu.VMEM((nc, GLA_DV, 2 * GLA_DK), BF16)],
        compiler_params=_params(2),
        name="gla",
    )(*args)


def _ssd_body(*refs, seq_len, grid_rows, has_h0, want_state, n_cast):
    it = iter(refs)
    (grp_ref, sm_ref, cwx_ref, cwb_ref, cwc_ref, cbx_ref, cbb_ref, cbc_ref,
     par_ref, dsk_ref) = (next(it) for _ in range(10))
    h0_ref = next(it) if has_h0 else None
    cast_src = [next(it) for _ in range(n_cast)]
    y_ref = next(it)
    st_ref = next(it) if want_state else None
    _cast_blocks(cast_src, [next(it) for _ in range(n_cast)])
    (pad_scr, xs_s, b_s, c_s, dt_scr, xy_scr, tr_scr, rows_scr, sc_scr, ed_scr, xw_scr, dec_scr,
     cs_scr, sprev_scr, state_scr) = (next(it) for _ in range(15))

    L = seq_len
    nc = L // CHUNK
    width = L // grid_rows
    pad = pad_scr.shape[0] - L
    pad //= 2
    grp = pl.program_id(1)

    def conv_into(c0, cw_ref, cb_ref, dst_ref):
        ch = dst_ref.shape[1]
        pad_scr[0:pad, 0:ch] = jnp.zeros((pad, ch), F32)
        pad_scr[pad + L:pad + L + pad, 0:ch] = jnp.zeros((pad, ch), F32)
        pad_scr[pad:pad + L, 0:ch] = grp_ref[0, :, c0:c0 + ch]
        rc = min(2 * CHUNK, L)
        col = lax.broadcasted_iota(jnp.int32, (rc, ch), 0) % width
        di_taps = range(CONV_K) if grid_rows > 1 else (CONV_K // 2,)
        for r0 in range(0, L, rc):
            acc = jnp.broadcast_to(cb_ref[...], (rc, ch))
            for dj in range(CONV_K):
                inner = None
                for di in di_taps:
                    off = pad + r0 + (di - 1) * width + (dj - 1)
                    term = cw_ref[di, dj:dj + 1, :] * pad_scr[off:off + rc, 0:ch]
                    inner = term if inner is None else inner + term
                if grid_rows > 1 and dj == 0:
                    inner = jnp.where(col >= 1, inner, 0.0)
                if grid_rows > 1 and dj == CONV_K - 1:
                    inner = jnp.where(col <= width - 2, inner, 0.0)
                acc = acc + inner
            dst_ref[r0:r0 + rc, :] = _silu(acc).astype(dst_ref.dtype)

    conv_into(0, cwx_ref, cbx_ref, xs_s)
    conv_into(GRP_B, cwb_ref, cbb_ref, b_s)
    conv_into(GRP_C, cwc_ref, cbc_ref, c_s)

    gw = GROUP_W
    unroll = min(SSD_UNROLL, nc)
    rows = lambda c: pl.ds(pl.multiple_of(c * CHUNK, CHUNK), CHUNK)
    bias_row = par_ref[0:1, :]
    a_row = -jnp.exp(par_ref[1:2, :]) * par_ref[2:3, :] * LOG2_E
    lower, upper = _tri_masks()
    lower_b = jnp.where(lower, 1.0, 0.0).astype(BF16)
    upper_b = jnp.where(upper, 1.0, 0.0).astype(BF16)
    lane = lax.broadcasted_iota(jnp.int32, (CHUNK, LANES), 1)
    dt_lanes = (lane >= SM_DTF) & (lane < SM_DTF + TERM_STRIDE)
    src = lax.broadcasted_iota(jnp.int32, (LANES, 2 * gw), 0)
    dst = lax.broadcasted_iota(jnp.int32, (LANES, 2 * gw), 1)
    dst_slot = (grp * HEADS_PER_GROUP + (lax.shift_right_logical(dst, 6) & (HEADS_PER_GROUP - 1))
                + jnp.where(dst >= gw, SSD_HEADS, 0))
    expand = jnp.where((src >= SM_DTF) & ((src & (TERM_STRIDE - 1)) == dst_slot),
                       1.0, 0.0).astype(BF16)
    t_idx = lax.broadcasted_iota(jnp.int32, (CHUNK, gw), 0)
    s_idx = lax.broadcasted_iota(jnp.int32, (CHUNK, gw), 1) & (CHUNK - 1)
    diag = t_idx == s_idx
    blk_r = lax.shift_right_logical(lax.broadcasted_iota(jnp.int32, (MXU_TILE, MXU_TILE), 0), 6)
    blk_c = lax.shift_right_logical(lax.broadcasted_iota(jnp.int32, (MXU_TILE, MXU_TILE), 1), 6)
    same_head = blk_r == blk_c

    dt_scr[...] = _softplus(sm_ref[...] + bias_row)

    def place3(v):
        hi = v.astype(BF16).astype(F32)
        rest = v - hi
        mid = rest.astype(BF16).astype(F32)
        lo = rest - mid
        keep = lambda t: jnp.where(dt_lanes, t, 0.0)
        out = (keep(hi) + pltpu.roll(keep(mid), TERM_STRIDE, axis=1)
               + pltpu.roll(keep(lo), 2 * TERM_STRIDE, axis=1))
        return out.astype(BF16)

    def cumulate(c, carry):
        sl = rows(c)
        dt = dt_scr[sl, :]
        da = dt * a_row
        hi = da.astype(BF16)
        both = jnp.concatenate([hi, (da - hi.astype(F32)).astype(BF16)], axis=1)
        pf = _dot(lower_b, both)
        pb = _dot(upper_b, both)
        cum = jnp.where(lane < SM_DTB, pf[:, 0:LANES] + pf[:, LANES:], pb[:, 0:LANES] + pb[:, LANES:])
        xy_scr[sl, 0:LANES] = place3(cum)
        xy_scr[sl, LANES:2 * LANES] = place3(dt)
        tr_scr[c, 0] = cum.T
        tr_scr[c, 1] = dt.T
        head0 = pl.multiple_of(SM_DTF + grp * HEADS_PER_GROUP, SUBLANES)
        for q, (k, d) in enumerate(((0, 0), (0, 1), (1, 0), (1, 1))):
            slab = tr_scr[c, k, pl.ds(head0 + d * SSD_HEADS, HEADS_PER_GROUP), :]
            row = jnp.concatenate(
                [jnp.broadcast_to(slab[h:h + 1, :], (SUBLANES, CHUNK))
                 for h in range(HEADS_PER_GROUP)], axis=1)
            rows_scr[c, :, q * gw:(q + 1) * gw] = row
        return carry

    lax.fori_loop(0, nc, cumulate, 0, unroll=unroll)

    def weights(c, carry):
        sl = rows(c)
        xc = xs_s[sl, :]
        cum_e = _dot(xy_scr[sl, 0:LANES], expand)
        dt_e = _dot(xy_scr[sl, LANES:2 * LANES], expand)
        cb = _dot_nt(c_s[sl, :], jnp.concatenate([b_s[sl, :]] * HEADS_PER_GROUP, axis=0))
        segs, dt_rows, dec_rows = [], [], []
        for d, (a, last_i) in enumerate(((0, CHUNK - 1), (gw, 0))):
            ce = cum_e[:, a:a + gw]
            de = dt_e[:, a:a + gw]
            segs.append(ce - rows_scr[c, 0:1, d * gw:(d + 1) * gw])
            dt_rows.append(rows_scr[c, 0:1, (2 + d) * gw:(3 + d) * gw])
            cum_last = ce[last_i:last_i + 1, :]
            ed_scr[sl, a:a + gw] = jnp.exp2(ce)
            xw_scr[sl, a:a + gw] = (xc * (jnp.exp2(cum_last - ce) * de)).astype(BF16)
            dec_rows.append(jnp.exp2(cum_last))
        fwd = t_idx >= s_idx
        w = (jnp.exp2(jnp.where(fwd, segs[0], segs[1])) * jnp.where(fwd, dt_rows[0], dt_rows[1])
             + jnp.where(diag, dt_rows[1], 0.0))
        sc_scr[sl, :] = (cb * w).astype(BF16)
        dec_scr[c] = jnp.broadcast_to(jnp.concatenate(dec_rows, axis=1), (SUBLANES, 2 * gw))
        return carry

    lax.fori_loop(0, nc, weights, 0, unroll=unroll)

    def products(c, carry):
        sl = rows(c)
        xc = xs_s[sl, :]
        xb = xc.astype(BF16)
        parts = []
        for j in range(gw // MXU_TILE):
            xh = xb[:, j * MXU_TILE:(j + 1) * MXU_TILE]
            rep = jnp.concatenate([xh] * (MXU_TILE // SSD_HEAD_DIM), axis=0)
            bd = jnp.where(same_head, rep, jnp.zeros_like(rep))
            parts.append(_dot(sc_scr[sl, j * MXU_TILE:(j + 1) * MXU_TILE], bd))
        y_ref[sl, :] = jnp.concatenate(parts, axis=1) + xc * dsk_ref[...]
        cs_scr[c] = _dot_tn(b_s[sl, :], xw_scr[sl, :])
        return carry

    lax.fori_loop(0, nc, products, 0, unroll=unroll)

    pair_w = 2 * SSD_HEAD_DIM
    if has_h0:
        for d in range(2):
            for j in range(HEADS_PER_GROUP // 2):
                pair = jnp.concatenate([h0_ref[0, 0, d, 2 * j], h0_ref[0, 0, d, 2 * j + 1]], axis=0)
                state_scr[:, d * gw + j * pair_w:d * gw + (j + 1) * pair_w] = pair.T
    else:
        state_scr[...] = jnp.zeros_like(state_scr)

    for d in range(2):
        for a in range(d * gw, (d + 1) * gw, MXU_TILE):

            def recur(i, state, d=d, a=a):
                c = i if d == 0 else nc - 1 - i
                sprev_scr[c, :, a:a + MXU_TILE] = state.astype(BF16)
                return (state * dec_scr[c, 0:1, a:a + MXU_TILE]
                        + cs_scr[c, :, a:a + MXU_TILE])

            state_scr[:, a:a + MXU_TILE] = lax.fori_loop(
                0, nc, recur, state_scr[:, a:a + MXU_TILE], unroll=min(nc, 4))

    def finish(c, carry):
        sl = rows(c)
        yi = _dot(c_s[sl, :], sprev_scr[c]) * ed_scr[sl, :]
        y_ref[sl, :] += yi[:, 0:gw] + yi[:, gw:2 * gw]
        return carry

    lax.fori_loop(0, nc, finish, 0, unroll=unroll)

    if want_state:
        for d in range(2):
            for j in range(HEADS_PER_GROUP // 2):
                pair = state_scr[:, d * gw + j * pair_w:d * gw + (j + 1) * pair_w].T
                st_ref[0, 0, d, 2 * j] = pair[0:SSD_HEAD_DIM, :]
                st_ref[0, 0, d, 2 * j + 1] = pair[SSD_HEAD_DIM:pair_w, :]


def _ssd_call(groups, zs, conv_w, conv_b, par, dsk, h0, layer, *, n_seq, seq_len, grid_rows,
              want_state, cast=()):
    L = seq_len
    in_specs = [pl.BlockSpec((1, L, GRP_W), lambda s, g: (g, s, 0)),
                pl.BlockSpec((L, LANES), lambda s, g: (s, ZS_SMALL // LANES)),
                pl.BlockSpec((CONV_K, CONV_K, GROUP_W), lambda s, g: (0, 0, g)),
                pl.BlockSpec((CONV_K, CONV_K, SSD_STATE),
                             lambda s, g: (0, 0, SSD_INNER // SSD_STATE + g)),
                pl.BlockSpec((CONV_K, CONV_K, SSD_STATE),
                             lambda s, g: (0, 0, (SSD_INNER + SSD_BC) // SSD_STATE + g)),
                pl.BlockSpec((1, GROUP_W), lambda s, g: (0, g)),
                pl.BlockSpec((1, SSD_STATE), lambda s, g: (0, SSD_INNER // SSD_STATE + g)),
                pl.BlockSpec((1, SSD_STATE),
                             lambda s, g: (0, (SSD_INNER + SSD_BC) // SSD_STATE + g)),
                pl.BlockSpec((SUBLANES, LANES), lambda s, g: (0, 0)),
                pl.BlockSpec((1, GROUP_W), lambda s, g: (0, g))]
    args = [groups, zs, conv_w, conv_w, conv_w, conv_b, conv_b, conv_b, par, dsk]
    state_blk = (1, 1, 2, HEADS_PER_GROUP, SSD_HEAD_DIM, SSD_STATE)
    if h0 is not None:
        in_specs.append(pl.BlockSpec(state_blk, lambda s, g: (s, layer, 0, g, 0, 0)))
        args.append(h0)
    out_specs = [pl.BlockSpec((L, GROUP_W), lambda s, g: (s, g))]
    out_shape = [jax.ShapeDtypeStruct((n_seq * L, SSD_INNER), F32)]
    if want_state:
        out_specs.append(pl.BlockSpec(state_blk, lambda s, g: (s, 0, 0, g, 0, 0)))
        out_shape.append(jax.ShapeDtypeStruct(
            (n_seq, 1, 2, SSD_HEADS, SSD_HEAD_DIM, SSD_STATE), F32))
    cast_in, cast_out, cast_shape = _cast_plumbing(cast, n_seq * SSD_GROUPS,
                                                   lambda s, g: s * SSD_GROUPS + g)
    in_specs += cast_in
    args += list(cast)
    out_specs += cast_out
    out_shape += cast_shape
    nc = L // CHUNK
    conv_pad = (L // grid_rows + SUBLANES) if grid_rows > 1 else SUBLANES
    body = functools.partial(_ssd_body, seq_len=L, grid_rows=grid_rows,
                             has_h0=h0 is not None, want_state=want_state, n_cast=len(cast))
    return pl.pallas_call(
        body,
        grid=(n_seq, SSD_GROUPS),
        in_specs=in_specs,
        out_specs=out_specs,
        out_shape=out_shape,
        scratch_shapes=[pltpu.VMEM((L + 2 * conv_pad, GROUP_W), F32),
                        pltpu.VMEM((L, GROUP_W), F32),
                        pltpu.VMEM((L, SSD_STATE), BF16),
                        pltpu.VMEM((L, SSD_STATE), BF16),
                        pltpu.VMEM((L, LANES), F32),
                        pltpu.VMEM((L, 2 * LANES), BF16),
                        pltpu.VMEM((nc, 2, LANES, CHUNK), F32),
                        pltpu.VMEM((nc, SUBLANES, 4 * GROUP_W), F32),
                        pltpu.VMEM((L, GROUP_W), BF16),
                        pltpu.VMEM((L, 2 * GROUP_W), F32),
                        pltpu.VMEM((L, 2 * GROUP_W), BF16),
                        pltpu.VMEM((nc, SUBLANES, 2 * GROUP_W), F32),
                        pltpu.VMEM((nc, SSD_STATE, 2 * GROUP_W), F32),
                        pltpu.VMEM((nc, SSD_STATE, 2 * GROUP_W), BF16),
                        pltpu.VMEM((SSD_STATE, 2 * GROUP_W), F32)],
        compiler_params=_params(2),
        name="ssd",
    )(*args)


def _layer_path(x, mod, lw, raw, h0_gla, h0_ssd, layer, *, n_seq, seq_len, grid_rows,
                want_state, final_w):
    if raw is not None:
        x1, lw["w_all"] = _ffn_call(x, mod, lw["norm_ffn1"], lw["ffn1_w_in"], lw["ffn1_w_out"],
                                    sub=0, seq_len=seq_len, cast=(raw["w_in"],))
    else:
        x1 = _ffn_call(x, mod, lw["norm_ffn1"], lw["ffn1_w_in"], lw["ffn1_w_out"],
                       sub=0, seq_len=seq_len)
    heads, groups, zs = _proj_call(x1, mod, lw["norm_mix"], lw["w_all"], seq_len=seq_len)
    gla_out = _gla_call(heads, zs, lw["wa_cat"], lw["ba_cat"], lw["gla_norm_w"], h0_gla, layer,
                        n_seq=n_seq, seq_len=seq_len, want_state=want_state)
    late = ("ffn2_w_in", "ffn2_w_out", "w_out")
    ssd_out = _ssd_call(groups, zs, lw["conv_w"], lw["conv_b"], lw["ssd_par"], lw["d_skip_row"],
                        h0_ssd, layer, n_seq=n_seq, seq_len=seq_len, grid_rows=grid_rows,
                        want_state=want_state,
                        cast=tuple(raw[k] for k in late) if raw is not None else ())
    if raw is not None:
        lw.update(zip(late, ssd_out[-len(late):]))
    o, y = gla_out[0], ssd_out[0]
    out = _ffn_call(x1, mod, lw["norm_ffn2"], lw["ffn2_w_in"], lw["ffn2_w_out"],
                    sub=2, seq_len=seq_len, mix=(o, y, zs, lw["ssd_norm_w"], lw["w_out"]),
                    final_w=final_w)
    if want_state:
        return out, gla_out[1], ssd_out[1]
    return out, None, None


def kernel(x_prompt, x_sample, state_gla, state_ssd, c, c_ctx, norm_ffn1, norm_mix, norm_ffn2, w_mod, b_mod, ffn1_w_in, ffn1_w_out, ffn2_w_in, ffn2_w_out, w_in, gla_w_a2, gla_b_a, gla_norm_w, conv_w, conv_b, dt_bias, a_log, d_skip, ssd_norm_w, w_out, final_norm):
    nb, seq, _ = x_prompt.shape
    db, dseq, _ = x_sample.shape
    grid_rows = dseq // GRID_W
    xp = x_prompt.reshape(nb * seq, D_MODEL)
    xs = x_sample.reshape(db * dseq, D_MODEL)
    row = lambda v: v.reshape(1, -1)
    gla_states, ssd_states = [], []
    for i in range(DEPTH):
        last = i == DEPTH - 1
        n_rows = -(-(db + 1) // SUBLANES) * SUBLANES
        cc = jnp.concatenate([c, c_ctx[None, :], jnp.zeros((n_rows - db - 1, D_MODEL), F32)], 0)
        mod = _mod_call(cc, w_mod[i], row(b_mod[i])).reshape(n_rows, N_MOD, D_MODEL)
        mod_lat, mod_ctx = mod[:db], mod[db:db + 1]

        wa_cat = jnp.stack(
            [jnp.pad(gla_w_a2[i, d].reshape(GLA_LOWRANK, GLA_HEADS, GLA_DK),
                     ((lo, LANES - lo - GLA_LOWRANK), (0, 0), (0, 0)))
             for d, lo in ((0, SM_AF), (1, SM_AB))], axis=2).reshape(LANES, 2 * GLA_QK)
        ba_cat = jnp.stack([gla_b_a[i, 0].reshape(GLA_HEADS, GLA_DK),
                            gla_b_a[i, 1].reshape(GLA_HEADS, GLA_DK)], axis=1).reshape(1, 2 * GLA_QK)
        ssd_par = jnp.pad(
            jnp.stack([dt_bias[i].reshape(-1), a_log[i].reshape(-1), jnp.ones((2 * SSD_HEADS,), F32)]),
            ((0, SUBLANES - 3), (SM_DTF, LANES - SM_DTF - 2 * SSD_HEADS)))
        lw = {
            "norm_ffn1": row(norm_ffn1[i]), "norm_mix": row(norm_mix[i]),
            "norm_ffn2": row(norm_ffn2[i]),
            "ffn1_w_in": ffn1_w_in[i].astype(BF16), "ffn1_w_out": ffn1_w_out[i].astype(BF16),
            "wa_cat": wa_cat, "ba_cat": ba_cat,
            "gla_norm_w": row(gla_norm_w[i]), "conv_w": conv_w[i], "conv_b": row(conv_b[i]),
            "ssd_par": ssd_par, "d_skip_row": row(jnp.repeat(d_skip[i], SSD_HEAD_DIM)),
            "ssd_norm_w": row(ssd_norm_w[i]),
        }
        raw = {"w_in": jnp.swapaxes(w_in[i], 0, 1), "ffn2_w_in": ffn2_w_in[i], "ffn2_w_out": ffn2_w_out[i],
               "w_out": w_out[i]}
        fw = row(final_norm) if last else None

        xp, sg, ss = _layer_path(xp, mod_ctx, lw, raw, None, None, i, n_seq=nb, seq_len=seq,
                                 grid_rows=1, want_state=True, final_w=fw)
        xs, _, _ = _layer_path(xs, mod_lat, lw, None, state_gla, state_ssd, i, n_seq=db,
                               seq_len=dseq, grid_rows=grid_rows, want_state=False, final_w=fw)
        gla_states.append(sg)
        ssd_states.append(ss)
    y_prompt = xp.reshape(nb, seq, D_MODEL)
    y_sample = xs.reshape(db, dseq, D_MODEL)
    return (y_prompt, y_sample, jnp.concatenate(gla_states, axis=1),
            jnp.concatenate(ssd_states, axis=1))
```

```python
import functools

import jax
import jax.numpy as jnp
from jax import lax
from jax.experimental import pallas as pl
from jax.experimental.pallas import tpu as pltpu

F32 = jnp.float32
BF16 = jnp.bfloat16

D_MODEL = 1024
DEPTH = 1
GRID_W = 64
CHUNK = 64
EPS = 1e-6
N_MOD = 9
D_FF = 2816
GLA_HEADS = 4
GLA_DK = 128
GLA_DV = 256
GLA_LOWRANK = 16
GLA_TAU = 16.0
GLA_QK = GLA_HEADS * GLA_DK
GLA_V = GLA_HEADS * GLA_DV
SSD_HEADS = 16
SSD_HEAD_DIM = 64
SSD_GROUPS = 2
SSD_STATE = 128
SSD_INNER = SSD_HEADS * SSD_HEAD_DIM
SSD_BC = SSD_GROUPS * SSD_STATE
SSD_CONV_DIM = SSD_INNER + 2 * SSD_BC
CONV_K = 3
D_MIX = GLA_V + SSD_INNER

LOG2_E = 1.4426950408889634
LANES = 128
SUBLANES = 8
BF16_SUBLANES = 16
VMEM_LIMIT_BYTES = 56 * 1024 * 1024

COL_Q = 0
COL_K = COL_Q + GLA_QK
COL_V = COL_K + GLA_QK
COL_R = COL_V + GLA_V
COL_Z = COL_R + GLA_V
HEAD_K = GLA_DK
HEAD_V = 2 * GLA_DK
HEAD_R = HEAD_V + GLA_DV
HEAD_W = HEAD_R + GLA_DV
GRP_B = SSD_INNER // SSD_GROUPS
GRP_C = GRP_B + SSD_STATE
GRP_W = GRP_C + SSD_STATE
ZS_SMALL = SSD_INNER
ZS_W = ZS_SMALL + LANES
SM_AF = 0
SM_AB = SM_AF + GLA_LOWRANK
SM_DTF = SM_AB + GLA_LOWRANK
SM_DTB = SM_DTF + SSD_HEADS
HEADS_PER_GROUP = SSD_HEADS // SSD_GROUPS
GROUP_W = GRP_B

TOKEN_TILE = 512
GLA_UNROLL = 16
SSD_UNROLL = 16
MXU_TILE = 256
TERM_STRIDE = 2 * SSD_HEADS
HEAD_DIM_SHIFT = SSD_HEAD_DIM.bit_length() - 1
assert 1 << HEAD_DIM_SHIFT == SSD_HEAD_DIM == CHUNK
SCAN_BLOCK_TOKENS = 1024


def _dot(a, b):
    return jnp.dot(a, b, preferred_element_type=F32)


def _dot_nt(a, b):
    return lax.dot_general(a, b, (((1,), (1,)), ((), ())), preferred_element_type=F32)


def _dot_tn(a, b):
    return lax.dot_general(a, b, (((0,), (0,)), ((), ())), preferred_element_type=F32)


def _silu(x):
    return x * jax.nn.sigmoid(x)


def _log1p_exp_neg_abs(x):
    return jnp.log(1.0 + jnp.exp(-jnp.abs(x)))


def _softplus(x):
    return jnp.maximum(x, 0.0) + _log1p_exp_neg_abs(x)


def _log_sigmoid(x):
    return jnp.minimum(x, 0.0) - _log1p_exp_neg_abs(x)


def _rmsnorm(x, w):
    ms = jnp.mean(x * x, axis=-1, keepdims=True)
    return x * lax.rsqrt(ms + EPS) * w


def _resident(shape):
    nd = len(shape)
    return pl.BlockSpec(shape, lambda *_: (0,) * nd, pipeline_mode=pl.Buffered(1))


def _params(n_axes):
    return pltpu.CompilerParams(dimension_semantics=("arbitrary",) * n_axes,
                                vmem_limit_bytes=VMEM_LIMIT_BYTES)


def _cast_plumbing(cast, n_steps, step_of):
    in_specs, out_specs, out_shape = [], [], []
    for w in cast:
        rows, cols = w.shape
        per_step = -(-rows // n_steps)
        per_step = -(-per_step // BF16_SUBLANES) * BF16_SUBLANES
        last = -(-rows // per_step) - 1
        blk = pl.BlockSpec((per_step, cols),
                           lambda *g, last=last: (jnp.minimum(step_of(*g), last), 0))
        in_specs.append(blk)
        out_specs.append(blk)
        out_shape.append(jax.ShapeDtypeStruct((rows, cols), BF16))
    return in_specs, out_specs, out_shape


def _cast_blocks(src_refs, dst_refs):
    for src_ref, dst_ref in zip(src_refs, dst_refs):
        dst_ref[...] = src_ref[...].astype(dst_ref.dtype)


def _mod_body(c_ref, w_ref, b_ref, out_ref):
    a = _silu(c_ref[...]).astype(BF16)
    out_ref[...] = _dot(a, w_ref[...].astype(BF16)) + b_ref[...]


def _mod_call(cc, w_mod, b_mod):
    n_rows = cc.shape[0]
    tn = D_MODEL
    return pl.pallas_call(
        _mod_body,
        grid=(N_MOD * D_MODEL // tn,),
        in_specs=[pl.BlockSpec((n_rows, D_MODEL), lambda j: (0, 0)),
                  pl.BlockSpec((D_MODEL, tn), lambda j: (0, j)),
                  pl.BlockSpec((1, tn), lambda j: (0, j))],
        out_specs=pl.BlockSpec((n_rows, tn), lambda j: (0, j)),
        out_shape=jax.ShapeDtypeStruct((n_rows, N_MOD * D_MODEL), F32),
        compiler_params=_params(1),
        name="mod",
    )(cc, w_mod, b_mod)


def _ffn_body(*refs, sub, has_mix, has_final, n_cast):
    it = iter(refs)
    x_ref, mod_ref, nw_ref, win_ref, wout_ref = (next(it) for _ in range(5))
    if has_mix:
        o_ref, y_ref, z_ref, snw_ref, wo_ref = (next(it) for _ in range(5))
    if has_final:
        fn_ref = next(it)
    cast_src = [next(it) for _ in range(n_cast)]
    out_ref = next(it)
    _cast_blocks(cast_src, [next(it) for _ in range(n_cast)])

    x = x_ref[...]
    if has_mix:
        g2 = mod_ref[0, 5:6, :]
        yn = _rmsnorm(y_ref[...] * _silu(z_ref[...]), snw_ref[...])
        m = _dot(jnp.concatenate([o_ref[...], yn.astype(BF16)], axis=1), wo_ref[...])
        x = x + g2 * m
    sh = mod_ref[0, 3 * sub:3 * sub + 1, :]
    sc = mod_ref[0, 3 * sub + 1:3 * sub + 2, :]
    gate = mod_ref[0, 3 * sub + 2:3 * sub + 3, :]
    h = (_rmsnorm(x, nw_ref[...]) * (1.0 + sc) + sh).astype(BF16)
    g = _dot(h, win_ref[:, :D_FF])
    u = _dot(h, win_ref[:, D_FF:])
    act = (_silu(g) * u).astype(BF16)
    x = x + (0.5 * gate) * _dot(act, wout_ref[...])
    if has_final:
        x = _rmsnorm(x, fn_ref[...])
    out_ref[...] = x


def _ffn_call(x, mod, norm_w, w_in, w_out, *, sub, seq_len, mix=None, final_w=None, cast=()):
    m_tok = x.shape[0]
    tm = TOKEN_TILE
    tiles_per_seq = seq_len // tm
    shared_mod = mod.shape[0] == 1
    mod_map = (lambda i: (0, 0, 0)) if shared_mod else (lambda i: (i // tiles_per_seq, 0, 0))
    tok = lambda width: pl.BlockSpec((tm, width), lambda i: (i, 0))
    in_specs = [tok(D_MODEL), pl.BlockSpec((1, N_MOD, D_MODEL), mod_map),
                _resident((1, D_MODEL)), _resident(w_in.shape), _resident(w_out.shape)]
    args = [x, mod, norm_w, w_in, w_out]
    if mix is not None:
        o, y, proj, ssd_norm_w, w_mix_out = mix
        in_specs += [tok(GLA_V), tok(SSD_INNER),
                     pl.BlockSpec((tm, SSD_INNER), lambda i: (i, 0)),
                     _resident((1, SSD_INNER)), _resident(w_mix_out.shape)]
        args += [o, y, proj, ssd_norm_w, w_mix_out]
    if final_w is not None:
        in_specs.append(_resident((1, D_MODEL)))
        args.append(final_w)
    n_steps = m_tok // tm
    cast_in, cast_out, cast_shape = _cast_plumbing(cast, n_steps, lambda i: i)
    in_specs += cast_in
    args += list(cast)
    out_specs = [tok(D_MODEL)] + cast_out
    out_shape = [jax.ShapeDtypeStruct((m_tok, D_MODEL), F32)] + cast_shape
    body = functools.partial(_ffn_body, sub=sub, has_mix=mix is not None,
                             has_final=final_w is not None, n_cast=len(cast))
    outs = pl.pallas_call(
        body,
        grid=(n_steps,),
        in_specs=in_specs,
        out_specs=out_specs,
        out_shape=out_shape,
        compiler_params=_params(1),
        name="ffn_mix" if mix is not None else "ffn",
    )(*args)
    return outs if cast else outs[0]


def _proj_body(x_ref, mod_ref, nw_ref, w_ref, heads_ref, groups_ref, zs_ref):
    sh = mod_ref[0, 3:4, :]
    sc = mod_ref[0, 4:5, :]
    h = (_rmsnorm(x_ref[...], nw_ref[...]) * (1.0 + sc) + sh).astype(BF16)
    o_z = COL_Z + 2 * GLA_LOWRANK
    o_dt = o_z + SSD_INNER + SSD_CONV_DIM
    qkvr = _dot_nt(h, w_ref[0:COL_Z, :])
    for hd in range(GLA_HEADS):
        for dst, src, wid in ((0, COL_Q + hd * GLA_DK, GLA_DK), (HEAD_K, COL_K + hd * GLA_DK, GLA_DK),
                              (HEAD_V, COL_V + hd * GLA_DV, GLA_DV), (HEAD_R, COL_R + hd * GLA_DV, GLA_DV)):
            heads_ref[hd, :, dst:dst + wid] = qkvr[:, src:src + wid]
    zx = _dot_nt(h, w_ref[o_z:o_dt, :])
    zs_ref[:, 0:ZS_SMALL] = zx[:, 0:SSD_INNER]
    for g in range(SSD_GROUPS):
        for dst, src, wid in ((0, g * GRP_B, GRP_B), (GRP_B, SSD_INNER + g * SSD_STATE, SSD_STATE),
                              (GRP_C, SSD_INNER + SSD_BC + g * SSD_STATE, SSD_STATE)):
            groups_ref[g, :, dst:dst + wid] = zx[:, SSD_INNER + src:SSD_INNER + src + wid]
    w_small = jnp.concatenate(
        [w_ref[COL_Z:o_z, :], w_ref[o_dt:, :],
         jnp.zeros((LANES - 2 * GLA_LOWRANK - 2 * SSD_HEADS, D_MODEL), BF16)], axis=0)
    zs_ref[:, ZS_SMALL:ZS_W] = _dot_nt(h, w_small)


def _proj_call(x, mod, norm_w, w_all, *, seq_len):
    m_tok = x.shape[0]
    tm = TOKEN_TILE
    tiles_per_seq = seq_len // tm
    shared_mod = mod.shape[0] == 1
    mod_map = (lambda i: (0, 0, 0)) if shared_mod else (lambda i: (i // tiles_per_seq, 0, 0))
    return pl.pallas_call(
        _proj_body,
        grid=(m_tok // tm,),
        in_specs=[pl.BlockSpec((tm, D_MODEL), lambda i: (i, 0)),
                  pl.BlockSpec((1, N_MOD, D_MODEL), mod_map),
                  _resident((1, D_MODEL)), _resident(w_all.shape)],
        out_specs=[pl.BlockSpec((GLA_HEADS, tm, HEAD_W), lambda i: (0, i, 0)),
                   pl.BlockSpec((SSD_GROUPS, tm, GRP_W), lambda i: (0, i, 0)),
                   pl.BlockSpec((tm, ZS_W), lambda i: (i, 0))],
        out_shape=[jax.ShapeDtypeStruct((GLA_HEADS, m_tok, HEAD_W), F32),
                   jax.ShapeDtypeStruct((SSD_GROUPS, m_tok, GRP_W), F32),
                   jax.ShapeDtypeStruct((m_tok, ZS_W), F32)],
        compiler_params=_params(1),
        name="proj",
    )(x, mod, norm_w, w_all)


def _tri_masks():
    row = lax.broadcasted_iota(jnp.int32, (CHUNK, CHUNK), 0)
    col = lax.broadcasted_iota(jnp.int32, (CHUNK, CHUNK), 1)
    return row >= col, row <= col


def _gla_body(*refs, seq_len, n_sub, has_h0, want_state):
    it = iter(refs)
    hd_ref, sm_ref, wa_ref, ba_ref, nw_ref = (next(it) for _ in range(5))
    h0_ref = next(it) if has_h0 else None
    o_ref = next(it)
    st_ref = next(it) if want_state else None
    (la_scr, qk_scr, qg_scr, ks_scr, vb_scr, sc_scr, kv_scr, dec_scr,
     sprev_scr) = (next(it) for _ in range(9))

    nc = seq_len // CHUNK
    nct = n_sub * nc
    dk = GLA_DK
    lower, upper = _tri_masks()
    lower_b = jnp.where(lower, 1.0, 0.0).astype(BF16)
    upper_b = jnp.where(upper, 1.0, 0.0).astype(BF16)
    unroll = min(GLA_UNROLL, nct)
    rows = lambda c: pl.ds(pl.multiple_of(c * CHUNK, CHUNK), CHUNK)

    pre = _dot(sm_ref[...].astype(BF16), wa_ref[...].astype(BF16)) + ba_ref[...]
    la_scr[...] = _log_sigmoid(pre) * (LOG2_E / GLA_TAU)

    def prep(c, carry):
        sl = rows(c)
        qc = hd_ref[0, sl, 0:HEAD_K] * (GLA_DK ** -0.5)
        kc = hd_ref[0, sl, HEAD_K:HEAD_V]
        vb_scr[sl, :] = hd_ref[0, sl, HEAD_V:HEAD_R].astype(BF16)
        la = la_scr[sl, :]
        hi = la.astype(BF16)
        lo = (la - hi.astype(F32)).astype(BF16)

        def cumulative(tri_b, a):
            p = _dot(tri_b, jnp.concatenate([hi[:, a:a + dk], lo[:, a:a + dk]], axis=1))
            return p[:, 0:dk] + p[:, dk:2 * dk]

        g_f = cumulative(lower_b, 0)
        g_b = cumulative(upper_b, dk)
        mid_f = g_f[CHUNK // 2:CHUNK // 2 + 1, :]
        mid_b = g_b[CHUNK - 1 - CHUNK // 2:CHUNK - CHUNK // 2, :]
        end_f = g_f[CHUNK - 1:CHUNK, :]
        end_b = g_b[0:1, :]
        qk_scr[sl, 0:dk] = (qc * jnp.exp2(g_f - mid_f)).astype(BF16)
        qk_scr[sl, dk:2 * dk] = (qc * jnp.exp2(g_b - mid_b)).astype(BF16)
        qk_scr[sl, 2 * dk:3 * dk] = (kc * jnp.exp2(mid_f - g_f)).astype(BF16)
        qk_scr[sl, 3 * dk:4 * dk] = (kc * jnp.exp2(mid_b - g_b)).astype(BF16)
        qg_scr[sl, 0:dk] = (qc * jnp.exp2(g_f)).astype(BF16)
        qg_scr[sl, dk:2 * dk] = (qc * jnp.exp2(g_b)).astype(BF16)
        ks_scr[sl, 0:dk] = (kc * jnp.exp2(end_f - g_f)).astype(BF16)
        ks_scr[sl, dk:2 * dk] = (kc * jnp.exp2(end_b - g_b)).astype(BF16)
        dec = jnp.exp2(jnp.concatenate([end_f, end_b], axis=1))
        dec_scr[c] = jnp.broadcast_to(dec, (SUBLANES, 2 * dk))
        return carry

    lax.fori_loop(0, nct, prep, 0, unroll=unroll)

    def products(c, carry):
        sl = rows(c)
        s_f = _dot_nt(qk_scr[sl, 0:dk], qk_scr[sl, 2 * dk:3 * dk])
        s_b = _dot_nt(qk_scr[sl, dk:2 * dk], qk_scr[sl, 3 * dk:4 * dk])
        sc_scr[sl, :] = (jnp.where(lower, s_f, 0.0) + jnp.where(upper, s_b, 0.0)).astype(BF16)
        kv_scr[c] = _dot_tn(vb_scr[sl, :], ks_scr[sl, :])
        return carry

    lax.fori_loop(0, nct, products, 0, unroll=unroll)

    for b in range(n_sub):
        for d, a in ((0, 0), (1, dk)):
            state0 = h0_ref[b, 0, d, 0].T if has_h0 else jnp.zeros((GLA_DV, dk), F32)

            def recur(i, state, b=b, d=d, a=a):
                c = b * nc + (i if d == 0 else nc - 1 - i)
                sprev_scr[c, :, a:a + dk] = state.astype(BF16)
                return state * dec_scr[c, 0:1, a:a + dk] + kv_scr[c, :, a:a + dk]

            state = lax.fori_loop(0, nc, recur, state0, unroll=min(nc, 4))
            if want_state:
                st_ref[b, 0, d, 0] = state.T

    def finish(c, carry):
        sl = rows(c)
        o = _dot(sc_scr[sl, :], vb_scr[sl, :]) + _dot_nt(qg_scr[sl, :], sprev_scr[c])
        o_ref[sl, :] = (_rmsnorm(o, nw_ref[...]) * _silu(hd_ref[0, sl, HEAD_R:HEAD_W])).astype(o_ref.dtype)
        return carry

    lax.fori_loop(0, nct, finish, 0, unroll=unroll)


def _gla_call(heads, zs, wa_cat, ba_cat, norm_w, h0, layer, *, n_seq, seq_len, want_state):
    n_sub = max(1, SCAN_BLOCK_TOKENS // seq_len)
    L = n_sub * seq_len
    nc = L // CHUNK
    in_specs = [pl.BlockSpec((1, L, HEAD_W), lambda s, h: (h, s, 0)),
                pl.BlockSpec((L, LANES), lambda s, h: (s, ZS_SMALL // LANES)),
                pl.BlockSpec((LANES, 2 * GLA_DK), lambda s, h: (0, h)),
                pl.BlockSpec((1, 2 * GLA_DK), lambda s, h: (0, h)),
                pl.BlockSpec((1, GLA_DV), lambda s, h: (0, 0))]
    args = [heads, zs, wa_cat, ba_cat, norm_w]
    state_blk = (n_sub, 1, 2, 1, GLA_DK, GLA_DV)
    if h0 is not None:
        in_specs.append(pl.BlockSpec(state_blk, lambda s, h: (s, layer, 0, h, 0, 0)))
        args.append(h0)
    out_specs = [pl.BlockSpec((L, GLA_DV), lambda s, h: (s, h))]
    out_shape = [jax.ShapeDtypeStruct((n_seq * seq_len, GLA_V), BF16)]
    if want_state:
        out_specs.append(pl.BlockSpec(state_blk, lambda s, h: (s, 0, 0, h, 0, 0)))
        out_shape.append(jax.ShapeDtypeStruct((n_seq, 1, 2, GLA_HEADS, GLA_DK, GLA_DV), F32))
    body = functools.partial(_gla_body, seq_len=seq_len, n_sub=n_sub, has_h0=h0 is not None,
                             want_state=want_state)
    return pl.pallas_call(
        body,
        grid=(n_seq // n_sub, GLA_HEADS),
        in_specs=in_specs,
        out_specs=out_specs,
        out_shape=out_shape,
        scratch_shapes=[pltpu.VMEM((L, 2 * GLA_DK), F32),
                        pltpu.VMEM((L, 4 * GLA_DK), BF16),
                        pltpu.VMEM((L, 2 * GLA_DK), BF16),
                        pltpu.VMEM((L, 2 * GLA_DK), BF16),
                        pltpu.VMEM((L, GLA_DV), BF16),
                        pltpu.VMEM((L, CHUNK), BF16),
                        pltpu.VMEM((nc, GLA_DV, 2 * GLA_DK), F32),
                        pltpu.VMEM((nc, SUBLANES, 2 * GLA_DK), F32),
                        pltpu.VMEM((nc, GLA_DV, 2 * GLA_DK), BF16)],
        compiler_params=_params(2),
        name="gla",
    )(*args)


def _ssd_body(*refs, seq_len, grid_rows, has_h0, want_state, n_cast):
    it = iter(refs)
    (grp_ref, sm_ref, cwx_ref, cwb_ref, cwc_ref, cbx_ref, cbb_ref, cbc_ref,
     par_ref, dsk_ref) = (next(it) for _ in range(10))
    h0_ref = next(it) if has_h0 else None
    cast_src = [next(it) for _ in range(n_cast)]
    y_ref = next(it)
    st_ref = next(it) if want_state else None
    _cast_blocks(cast_src, [next(it) for _ in range(n_cast)])
    (pad_scr, xs_s, b_s, c_s, dt_scr, xy_scr, tr_scr, rows_scr, sc_scr, ed_scr, xw_scr, dec_scr,
     cs_scr, sprev_scr, state_scr) = (next(it) for _ in range(15))

    L = seq_len
    nc = L // CHUNK
    width = L // grid_rows
    pad = pad_scr.shape[0] - L
    pad //= 2
    grp = pl.program_id(1)

    def conv_into(c0, cw_ref, cb_ref, dst_ref):
        ch = dst_ref.shape[1]
        pad_scr[0:pad, 0:ch] = jnp.zeros((pad, ch), F32)
        pad_scr[pad + L:pad + L + pad, 0:ch] = jnp.zeros((pad, ch), F32)
        pad_scr[pad:pad + L, 0:ch] = grp_ref[0, :, c0:c0 + ch]
        rc = min(2 * CHUNK, L)
        col = lax.broadcasted_iota(jnp.int32, (rc, ch), 0) % width
        di_taps = range(CONV_K) if grid_rows > 1 else (CONV_K // 2,)
        for r0 in range(0, L, rc):
            acc = jnp.broadcast_to(cb_ref[...], (rc, ch))
            for dj in range(CONV_K):
                inner = None
                for di in di_taps:
                    off = pad + r0 + (di - 1) * width + (dj - 1)
                    term = cw_ref[di, dj:dj + 1, :] * pad_scr[off:off + rc, 0:ch]
                    inner = term if inner is None else inner + term
                if grid_rows > 1 and dj == 0:
                    inner = jnp.where(col >= 1, inner, 0.0)
                if grid_rows > 1 and dj == CONV_K - 1:
                    inner = jnp.where(col <= width - 2, inner, 0.0)
                acc = acc + inner
            dst_ref[r0:r0 + rc, :] = _silu(acc).astype(dst_ref.dtype)

    conv_into(0, cwx_ref, cbx_ref, xs_s)
    conv_into(GRP_B, cwb_ref, cbb_ref, b_s)
    conv_into(GRP_C, cwc_ref, cbc_ref, c_s)

    gw = GROUP_W
    unroll = min(SSD_UNROLL, nc)
    rows = lambda c: pl.ds(pl.multiple_of(c * CHUNK, CHUNK), CHUNK)
    bias_row = par_ref[0:1, :]
    a_row = -jnp.exp(par_ref[1:2, :]) * par_ref[2:3, :] * LOG2_E
    lower, upper = _tri_masks()
    lower_b = jnp.where(lower, 1.0, 0.0).astype(BF16)
    upper_b = jnp.where(upper, 1.0, 0.0).astype(BF16)
    lane = lax.broadcasted_iota(jnp.int32, (CHUNK, LANES), 1)
    dt_lanes = (lane >= SM_DTF) & (lane < SM_DTF + TERM_STRIDE)
    src = lax.broadcasted_iota(jnp.int32, (LANES, 2 * gw), 0)
    dst = lax.broadcasted_iota(jnp.int32, (LANES, 2 * gw), 1)
    dst_slot = (grp * HEADS_PER_GROUP
                + (lax.shift_right_logical(dst, HEAD_DIM_SHIFT) & (HEADS_PER_GROUP - 1))
                + jnp.where(dst >= gw, SSD_HEADS, 0))
    expand = jnp.where((src >= SM_DTF) & ((src & (TERM_STRIDE - 1)) == dst_slot),
                       1.0, 0.0).astype(BF16)
    t_idx = lax.broadcasted_iota(jnp.int32, (CHUNK, gw), 0)
    s_idx = lax.broadcasted_iota(jnp.int32, (CHUNK, gw), 1) & (CHUNK - 1)
    diag = t_idx == s_idx
    blk_r = lax.shift_right_logical(lax.broadcasted_iota(jnp.int32, (MXU_TILE, MXU_TILE), 0),
                                    HEAD_DIM_SHIFT)
    blk_c = lax.shift_right_logical(lax.broadcasted_iota(jnp.int32, (MXU_TILE, MXU_TILE), 1),
                                    HEAD_DIM_SHIFT)
    same_head = blk_r == blk_c

    dt_scr[...] = _softplus(sm_ref[...] + bias_row)

    def place3(v):
        hi = v.astype(BF16).astype(F32)
        rest = v - hi
        mid = rest.astype(BF16).astype(F32)
        lo = rest - mid
        keep = lambda t: jnp.where(dt_lanes, t, 0.0)
        out = (keep(hi) + pltpu.roll(keep(mid), TERM_STRIDE, axis=1)
               + pltpu.roll(keep(lo), 2 * TERM_STRIDE, axis=1))
        return out.astype(BF16)

    def cumulate(c, carry):
        sl = rows(c)
        dt = dt_scr[sl, :]
        da = dt * a_row
        hi = da.astype(BF16)
        both = jnp.concatenate([hi, (da - hi.astype(F32)).astype(BF16)], axis=1)
        pf = _dot(lower_b, both)
        pb = _dot(upper_b, both)
        cum = jnp.where(lane < SM_DTB, pf[:, 0:LANES] + pf[:, LANES:], pb[:, 0:LANES] + pb[:, LANES:])
        xy_scr[sl, 0:LANES] = place3(cum)
        xy_scr[sl, LANES:2 * LANES] = place3(dt)
        tr_scr[c, 0] = cum.T
        tr_scr[c, 1] = dt.T
        head0 = pl.multiple_of(SM_DTF + grp * HEADS_PER_GROUP, SUBLANES)
        for q, (k, d) in enumerate(((0, 0), (0, 1), (1, 0), (1, 1))):
            slab = tr_scr[c, k, pl.ds(head0 + d * SSD_HEADS, HEADS_PER_GROUP), :]
            row = jnp.concatenate(
                [jnp.broadcast_to(slab[h:h + 1, :], (SUBLANES, CHUNK))
                 for h in range(HEADS_PER_GROUP)], axis=1)
            rows_scr[c, :, q * gw:(q + 1) * gw] = row
        return carry

    lax.fori_loop(0, nc, cumulate, 0, unroll=unroll)

    def weights(c, carry):
        sl = rows(c)
        xc = xs_s[sl, :]
        cum_e = _dot(xy_scr[sl, 0:LANES], expand)
        dt_e = _dot(xy_scr[sl, LANES:2 * LANES], expand)
        cb = _dot_nt(c_s[sl, :], jnp.concatenate([b_s[sl, :]] * HEADS_PER_GROUP, axis=0))
        segs, dt_rows, dec_rows = [], [], []
        for d, (a, last_i) in enumerate(((0, CHUNK - 1), (gw, 0))):
            ce = cum_e[:, a:a + gw]
            de = dt_e[:, a:a + gw]
            segs.append(ce - rows_scr[c, 0:1, d * gw:(d + 1) * gw])
            dt_rows.append(rows_scr[c, 0:1, (2 + d) * gw:(3 + d) * gw])
            cum_last = ce[last_i:last_i + 1, :]
            ed_scr[sl, a:a + gw] = jnp.exp2(ce)
            xw_scr[sl, a:a + gw] = (xc * (jnp.exp2(cum_last - ce) * de)).astype(BF16)
            dec_rows.append(jnp.exp2(cum_last))
        fwd = t_idx >= s_idx
        w = (jnp.exp2(jnp.where(fwd, segs[0], segs[1])) * jnp.where(fwd, dt_rows[0], dt_rows[1])
             + jnp.where(diag, dt_rows[1], 0.0))
        sc_scr[sl, :] = (cb * w).astype(BF16)
        dec_scr[c] = jnp.broadcast_to(jnp.concatenate(dec_rows, axis=1), (SUBLANES, 2 * gw))
        return carry

    lax.fori_loop(0, nc, weights, 0, unroll=unroll)

    def products(c, carry):
        sl = rows(c)
        xc = xs_s[sl, :]
        xb = xc.astype(BF16)
        parts = []
        for j in range(gw // MXU_TILE):
            xh = xb[:, j * MXU_TILE:(j + 1) * MXU_TILE]
            rep = jnp.concatenate([xh] * (MXU_TILE // SSD_HEAD_DIM), axis=0)
            bd = jnp.where(same_head, rep, jnp.zeros_like(rep))
            parts.append(_dot(sc_scr[sl, j * MXU_TILE:(j + 1) * MXU_TILE], bd))
        y_ref[sl, :] = jnp.concatenate(parts, axis=1) + xc * dsk_ref[...]
        cs_scr[c] = _dot_tn(b_s[sl, :], xw_scr[sl, :])
        return carry

    lax.fori_loop(0, nc, products, 0, unroll=unroll)

    pair_w = 2 * SSD_HEAD_DIM
    if has_h0:
        for d in range(2):
            for j in range(HEADS_PER_GROUP // 2):
                pair = jnp.concatenate([h0_ref[0, 0, d, 2 * j], h0_ref[0, 0, d, 2 * j + 1]], axis=0)
                state_scr[:, d * gw + j * pair_w:d * gw + (j + 1) * pair_w] = pair.T
    else:
        state_scr[...] = jnp.zeros_like(state_scr)

    for d in range(2):
        for a in range(d * gw, (d + 1) * gw, MXU_TILE):

            def recur(i, state, d=d, a=a):
                c = i if d == 0 else nc - 1 - i
                sprev_scr[c, :, a:a + MXU_TILE] = state.astype(BF16)
                return (state * dec_scr[c, 0:1, a:a + MXU_TILE]
                        + cs_scr[c, :, a:a + MXU_TILE])

            state_scr[:, a:a + MXU_TILE] = lax.fori_loop(
                0, nc, recur, state_scr[:, a:a + MXU_TILE], unroll=min(nc, 4))

    def finish(c, carry):
        sl = rows(c)
        yi = _dot(c_s[sl, :], sprev_scr[c]) * ed_scr[sl, :]
        y_ref[sl, :] += yi[:, 0:gw] + yi[:, gw:2 * gw]
        return carry

    lax.fori_loop(0, nc, finish, 0, unroll=unroll)

    if want_state:
        for d in range(2):
            for j in range(HEADS_PER_GROUP // 2):
                pair = state_scr[:, d * gw + j * pair_w:d * gw + (j + 1) * pair_w].T
                st_ref[0, 0, d, 2 * j] = pair[0:SSD_HEAD_DIM, :]
                st_ref[0, 0, d, 2 * j + 1] = pair[SSD_HEAD_DIM:pair_w, :]


def _ssd_call(groups, zs, conv_w, conv_b, par, dsk, h0, layer, *, n_seq, seq_len, grid_rows,
              want_state, cast=()):
    L = seq_len
    in_specs = [pl.BlockSpec((1, L, GRP_W), lambda s, g: (g, s, 0)),
                pl.BlockSpec((L, LANES), lambda s, g: (s, ZS_SMALL // LANES)),
                pl.BlockSpec((CONV_K, CONV_K, GROUP_W), lambda s, g: (0, 0, g)),
                pl.BlockSpec((CONV_K, CONV_K, SSD_STATE),
                             lambda s, g: (0, 0, SSD_INNER // SSD_STATE + g)),
                pl.BlockSpec((CONV_K, CONV_K, SSD_STATE),
                             lambda s, g: (0, 0, (SSD_INNER + SSD_BC) // SSD_STATE + g)),
                pl.BlockSpec((1, GROUP_W), lambda s, g: (0, g)),
                pl.BlockSpec((1, SSD_STATE), lambda s, g: (0, SSD_INNER // SSD_STATE + g)),
                pl.BlockSpec((1, SSD_STATE),
                             lambda s, g: (0, (SSD_INNER + SSD_BC) // SSD_STATE + g)),
                pl.BlockSpec((SUBLANES, LANES), lambda s, g: (0, 0)),
                pl.BlockSpec((1, GROUP_W), lambda s, g: (0, g))]
    args = [groups, zs, conv_w, conv_w, conv_w, conv_b, conv_b, conv_b, par, dsk]
    state_blk = (1, 1, 2, HEADS_PER_GROUP, SSD_HEAD_DIM, SSD_STATE)
    if h0 is not None:
        in_specs.append(pl.BlockSpec(state_blk, lambda s, g: (s, layer, 0, g, 0, 0)))
        args.append(h0)
    out_specs = [pl.BlockSpec((L, GROUP_W), lambda s, g: (s, g))]
    out_shape = [jax.ShapeDtypeStruct((n_seq * L, SSD_INNER), F32)]
    if want_state:
        out_specs.append(pl.BlockSpec(state_blk, lambda s, g: (s, 0, 0, g, 0, 0)))
        out_shape.append(jax.ShapeDtypeStruct(
            (n_seq, 1, 2, SSD_HEADS, SSD_HEAD_DIM, SSD_STATE), F32))
    cast_in, cast_out, cast_shape = _cast_plumbing(cast, n_seq * SSD_GROUPS,
                                                   lambda s, g: s * SSD_GROUPS + g)
    in_specs += cast_in
    args += list(cast)
    out_specs += cast_out
    out_shape += cast_shape
    nc = L // CHUNK
    conv_pad = (L // grid_rows + SUBLANES) if grid_rows > 1 else SUBLANES
    body = functools.partial(_ssd_body, seq_len=L, grid_rows=grid_rows,
                             has_h0=h0 is not None, want_state=want_state, n_cast=len(cast))
    return pl.pallas_call(
        body,
        grid=(n_seq, SSD_GROUPS),
        in_specs=in_specs,
        out_specs=out_specs,
        out_shape=out_shape,
        scratch_shapes=[pltpu.VMEM((L + 2 * conv_pad, GROUP_W), F32),
                        pltpu.VMEM((L, GROUP_W), F32),
                        pltpu.VMEM((L, SSD_STATE), BF16),
                        pltpu.VMEM((L, SSD_STATE), BF16),
                        pltpu.VMEM((L, LANES), F32),
                        pltpu.VMEM((L, 2 * LANES), BF16),
                        pltpu.VMEM((nc, 2, LANES, CHUNK), F32),
                        pltpu.VMEM((nc, SUBLANES, 4 * GROUP_W), F32),
                        pltpu.VMEM((L, GROUP_W), BF16),
                        pltpu.VMEM((L, 2 * GROUP_W), F32),
                        pltpu.VMEM((L, 2 * GROUP_W), BF16),
                        pltpu.VMEM((nc, SUBLANES, 2 * GROUP_W), F32),
                        pltpu.VMEM((nc, SSD_STATE, 2 * GROUP_W), F32),
                        pltpu.VMEM((nc, SSD_STATE, 2 * GROUP_W), BF16),
                        pltpu.VMEM((SSD_STATE, 2 * GROUP_W), F32)],
        compiler_params=_params(2),
        name="ssd",
    )(*args)


def _layer_path(x, mod, lw, raw, h0_gla, h0_ssd, layer, *, n_seq, seq_len, grid_rows,
                want_state, final_w):
    if raw is not None:
        x1, lw["w_all"] = _ffn_call(x, mod, lw["norm_ffn1"], lw["ffn1_w_in"], lw["ffn1_w_out"],
                                    sub=0, seq_len=seq_len, cast=(raw["w_in"],))
    else:
        x1 = _ffn_call(x, mod, lw["norm_ffn1"], lw["ffn1_w_in"], lw["ffn1_w_out"],
                       sub=0, seq_len=seq_len)
    heads, groups, zs = _proj_call(x1, mod, lw["norm_mix"], lw["w_all"], seq_len=seq_len)
    gla_out = _gla_call(heads, zs, lw["wa_cat"], lw["ba_cat"], lw["gla_norm_w"], h0_gla, layer,
                        n_seq=n_seq, seq_len=seq_len, want_state=want_state)
    late = ("ffn2_w_in", "ffn2_w_out", "w_out")
    ssd_out = _ssd_call(groups, zs, lw["conv_w"], lw["conv_b"], lw["ssd_par"], lw["d_skip_row"],
                        h0_ssd, layer, n_seq=n_seq, seq_len=seq_len, grid_rows=grid_rows,
                        want_state=want_state,
                        cast=tuple(raw[k] for k in late) if raw is not None else ())
    if raw is not None:
        lw.update(zip(late, ssd_out[-len(late):]))
    o, y = gla_out[0], ssd_out[0]
    out = _ffn_call(x1, mod, lw["norm_ffn2"], lw["ffn2_w_in"], lw["ffn2_w_out"],
                    sub=2, seq_len=seq_len, mix=(o, y, zs, lw["ssd_norm_w"], lw["w_out"]),
                    final_w=final_w)
    if want_state:
        return out, gla_out[1], ssd_out[1]
    return out, None, None


def kernel(x_prompt, x_sample, state_gla, state_ssd, c, c_ctx, norm_ffn1, norm_mix, norm_ffn2, w_mod, b_mod, ffn1_w_in, ffn1_w_out, ffn2_w_in, ffn2_w_out, w_in, gla_w_a2, gla_b_a, gla_norm_w, conv_w, conv_b, dt_bias, a_log, d_skip, ssd_norm_w, w_out, final_norm):
    nb, seq, _ = x_prompt.shape
    db, dseq, _ = x_sample.shape
    grid_rows = dseq // GRID_W
    xp = x_prompt.reshape(nb * seq, D_MODEL)
    xs = x_sample.reshape(db * dseq, D_MODEL)
    row = lambda v: v.reshape(1, -1)
    gla_states, ssd_states = [], []
    for i in range(DEPTH):
        last = i == DEPTH - 1
        n_rows = -(-(db + 1) // SUBLANES) * SUBLANES
        cc = jnp.concatenate([c, c_ctx[None, :], jnp.zeros((n_rows - db - 1, D_MODEL), F32)], 0)
        mod = _mod_call(cc, w_mod[i], row(b_mod[i])).reshape(n_rows, N_MOD, D_MODEL)
        mod_lat, mod_ctx = mod[:db], mod[db:db + 1]

        wa_cat = jnp.stack(
            [jnp.pad(gla_w_a2[i, d].reshape(GLA_LOWRANK, GLA_HEADS, GLA_DK),
                     ((lo, LANES - lo - GLA_LOWRANK), (0, 0), (0, 0)))
             for d, lo in ((0, SM_AF), (1, SM_AB))], axis=2).reshape(LANES, 2 * GLA_QK)
        ba_cat = jnp.stack([gla_b_a[i, 0].reshape(GLA_HEADS, GLA_DK),
                            gla_b_a[i, 1].reshape(GLA_HEADS, GLA_DK)], axis=1).reshape(1, 2 * GLA_QK)
        ssd_par = jnp.pad(
            jnp.stack([dt_bias[i].reshape(-1), a_log[i].reshape(-1), jnp.ones((2 * SSD_HEADS,), F32)]),
            ((0, SUBLANES - 3), (SM_DTF, LANES - SM_DTF - 2 * SSD_HEADS)))
        lw = {
            "norm_ffn1": row(norm_ffn1[i]), "norm_mix": row(norm_mix[i]),
            "norm_ffn2": row(norm_ffn2[i]),
            "ffn1_w_in": ffn1_w_in[i].astype(BF16), "ffn1_w_out": ffn1_w_out[i].astype(BF16),
            "wa_cat": wa_cat, "ba_cat": ba_cat,
            "gla_norm_w": row(gla_norm_w[i]), "conv_w": conv_w[i], "conv_b": row(conv_b[i]),
            "ssd_par": ssd_par, "d_skip_row": row(jnp.repeat(d_skip[i], SSD_HEAD_DIM)),
            "ssd_norm_w": row(ssd_norm_w[i]),
        }
        raw = {"w_in": jnp.swapaxes(w_in[i], 0, 1), "ffn2_w_in": ffn2_w_in[i], "ffn2_w_out": ffn2_w_out[i],
               "w_out": w_out[i]}
        fw = row(final_norm) if last else None

        xp, sg, ss = _layer_path(xp, mod_ctx, lw, raw, None, None, i, n_seq=nb, seq_len=seq,
                                 grid_rows=1, want_state=True, final_w=fw)
        xs, _, _ = _layer_path(xs, mod_lat, lw, None, state_gla, state_ssd, i, n_seq=db,
                               seq_len=dseq, grid_rows=grid_rows, want_state=False, final_w=fw)
        gla_states.append(sg)
        ssd_states.append(ss)
    y_prompt = xp.reshape(nb, seq, D_MODEL)
    y_sample = xs.reshape(db, dseq, D_MODEL)
    return (y_prompt, y_sample, jnp.concatenate(gla_states, axis=1),
            jnp.concatenate(ssd_states, axis=1))
```

```python
import functools

import jax
import jax.numpy as jnp
from jax import lax
from jax.experimental import pallas as pl
from jax.experimental.pallas import tpu as pltpu

F32 = jnp.float32
BF16 = jnp.bfloat16

D_MODEL = 1024
DEPTH = 1
GRID_W = 64
CHUNK = 64
EPS = 1e-6
N_MOD = 9
D_FF = 2816
GLA_HEADS = 4
GLA_DK = 128
GLA_DV = 256
GLA_LOWRANK = 16
GLA_TAU = 16.0
GLA_QK = GLA_HEADS * GLA_DK
GLA_V = GLA_HEADS * GLA_DV
SSD_HEADS = 16
SSD_HEAD_DIM = 64
SSD_GROUPS = 2
SSD_STATE = 128
SSD_INNER = SSD_HEADS * SSD_HEAD_DIM
SSD_BC = SSD_GROUPS * SSD_STATE
SSD_CONV_DIM = SSD_INNER + 2 * SSD_BC
CONV_K = 3
D_MIX = GLA_V + SSD_INNER

LOG2_E = 1.4426950408889634
LANES = 128
SUBLANES = 8
BF16_SUBLANES = 16
VMEM_LIMIT_BYTES = 56 * 1024 * 1024

COL_Q = 0
COL_K = COL_Q + GLA_QK
COL_V = COL_K + GLA_QK
COL_R = COL_V + GLA_V
COL_Z = COL_R + GLA_V
HEAD_K = GLA_DK
HEAD_V = 2 * GLA_DK
HEAD_R = HEAD_V + GLA_DV
HEAD_W = HEAD_R + GLA_DV
GRP_B = SSD_INNER // SSD_GROUPS
GRP_C = GRP_B + SSD_STATE
GRP_W = GRP_C + SSD_STATE
ZS_SMALL = SSD_INNER
ZS_W = ZS_SMALL + LANES
SM_AF = 0
SM_AB = SM_AF + GLA_LOWRANK
SM_DTF = SM_AB + GLA_LOWRANK
SM_DTB = SM_DTF + SSD_HEADS
HEADS_PER_GROUP = SSD_HEADS // SSD_GROUPS
GROUP_W = GRP_B

TOKEN_TILE = 512
GLA_UNROLL = 32
SSD_UNROLL = 16
MXU_TILE = 256
TERM_STRIDE = 2 * SSD_HEADS
HEAD_DIM_SHIFT = SSD_HEAD_DIM.bit_length() - 1
assert 1 << HEAD_DIM_SHIFT == SSD_HEAD_DIM == CHUNK
SCAN_BLOCK_TOKENS = 2048


def _dot(a, b):
    return jnp.dot(a, b, preferred_element_type=F32)


def _dot_nt(a, b):
    return lax.dot_general(a, b, (((1,), (1,)), ((), ())), preferred_element_type=F32)


def _dot_tn(a, b):
    return lax.dot_general(a, b, (((0,), (0,)), ((), ())), preferred_element_type=F32)


def _silu(x):
    return x * jax.nn.sigmoid(x)


def _log1p_exp_neg_abs(x):
    return jnp.log(1.0 + jnp.exp(-jnp.abs(x)))


def _softplus(x):
    return jnp.maximum(x, 0.0) + _log1p_exp_neg_abs(x)


def _log_sigmoid(x):
    return jnp.minimum(x, 0.0) - _log1p_exp_neg_abs(x)


def _rmsnorm(x, w):
    ms = jnp.mean(x * x, axis=-1, keepdims=True)
    return x * lax.rsqrt(ms + EPS) * w


def _resident(shape):
    nd = len(shape)
    return pl.BlockSpec(shape, lambda *_: (0,) * nd, pipeline_mode=pl.Buffered(1))


def _params(n_axes):
    return pltpu.CompilerParams(dimension_semantics=("arbitrary",) * n_axes,
                                vmem_limit_bytes=VMEM_LIMIT_BYTES)


def _cast_plumbing(cast, n_steps, step_of):
    in_specs, out_specs, out_shape = [], [], []
    for w in cast:
        rows, cols = w.shape
        per_step = -(-rows // n_steps)
        per_step = -(-per_step // BF16_SUBLANES) * BF16_SUBLANES
        last = -(-rows // per_step) - 1
        blk = pl.BlockSpec((per_step, cols),
                           lambda *g, last=last: (jnp.minimum(step_of(*g), last), 0))
        in_specs.append(blk)
        out_specs.append(blk)
        out_shape.append(jax.ShapeDtypeStruct((rows, cols), BF16))
    return in_specs, out_specs, out_shape


def _cast_blocks(src_refs, dst_refs):
    for src_ref, dst_ref in zip(src_refs, dst_refs):
        dst_ref[...] = src_ref[...].astype(dst_ref.dtype)


def _mod_body(c_ref, w_ref, b_ref, out_ref):
    a = _silu(c_ref[...]).astype(BF16)
    out_ref[...] = _dot(a, w_ref[...].astype(BF16)) + b_ref[...]


def _mod_call(cc, w_mod, b_mod):
    n_rows = cc.shape[0]
    tn = D_MODEL
    return pl.pallas_call(
        _mod_body,
        grid=(N_MOD * D_MODEL // tn,),
        in_specs=[pl.BlockSpec((n_rows, D_MODEL), lambda j: (0, 0)),
                  pl.BlockSpec((D_MODEL, tn), lambda j: (0, j)),
                  pl.BlockSpec((1, tn), lambda j: (0, j))],
        out_specs=pl.BlockSpec((n_rows, tn), lambda j: (0, j)),
        out_shape=jax.ShapeDtypeStruct((n_rows, N_MOD * D_MODEL), F32),
        compiler_params=_params(1),
        name="mod",
    )(cc, w_mod, b_mod)


def _ffn_body(*refs, sub, has_mix, has_final, n_cast):
    it = iter(refs)
    x_ref, mod_ref, nw_ref, win_ref, wout_ref = (next(it) for _ in range(5))
    if has_mix:
        o_ref, y_ref, z_ref, snw_ref, wo_ref = (next(it) for _ in range(5))
    if has_final:
        fn_ref = next(it)
    cast_src = [next(it) for _ in range(n_cast)]
    out_ref = next(it)
    _cast_blocks(cast_src, [next(it) for _ in range(n_cast)])

    x = x_ref[...]
    if has_mix:
        g2 = mod_ref[0, 5:6, :]
        yn = _rmsnorm(y_ref[...] * _silu(z_ref[...]), snw_ref[...])
        m = _dot(jnp.concatenate([o_ref[...], yn.astype(BF16)], axis=1), wo_ref[...])
        x = x + g2 * m
    sh = mod_ref[0, 3 * sub:3 * sub + 1, :]
    sc = mod_ref[0, 3 * sub + 1:3 * sub + 2, :]
    gate = mod_ref[0, 3 * sub + 2:3 * sub + 3, :]
    h = (_rmsnorm(x, nw_ref[...]) * (1.0 + sc) + sh).astype(BF16)
    g = _dot(h, win_ref[:, :D_FF])
    u = _dot(h, win_ref[:, D_FF:])
    act = (_silu(g) * u).astype(BF16)
    x = x + (0.5 * gate) * _dot(act, wout_ref[...])
    if has_final:
        x = _rmsnorm(x, fn_ref[...])
    out_ref[...] = x


def _ffn_call(x, mod, norm_w, w_in, w_out, *, sub, seq_len, mix=None, final_w=None, cast=()):
    m_tok = x.shape[0]
    tm = TOKEN_TILE
    tiles_per_seq = seq_len // tm
    shared_mod = mod.shape[0] == 1
    mod_map = (lambda i: (0, 0, 0)) if shared_mod else (lambda i: (i // tiles_per_seq, 0, 0))
    tok = lambda width: pl.BlockSpec((tm, width), lambda i: (i, 0))
    in_specs = [tok(D_MODEL), pl.BlockSpec((1, N_MOD, D_MODEL), mod_map),
                _resident((1, D_MODEL)), _resident(w_in.shape), _resident(w_out.shape)]
    args = [x, mod, norm_w, w_in, w_out]
    if mix is not None:
        o, y, proj, ssd_norm_w, w_mix_out = mix
        in_specs += [tok(GLA_V), tok(SSD_INNER),
                     pl.BlockSpec((tm, SSD_INNER), lambda i: (i, 0)),
                     _resident((1, SSD_INNER)), _resident(w_mix_out.shape)]
        args += [o, y, proj, ssd_norm_w, w_mix_out]
    if final_w is not None:
        in_specs.append(_resident((1, D_MODEL)))
        args.append(final_w)
    n_steps = m_tok // tm
    cast_in, cast_out, cast_shape = _cast_plumbing(cast, n_steps, lambda i: i)
    in_specs += cast_in
    args += list(cast)
    out_specs = [tok(D_MODEL)] + cast_out
    out_shape = [jax.ShapeDtypeStruct((m_tok, D_MODEL), F32)] + cast_shape
    body = functools.partial(_ffn_body, sub=sub, has_mix=mix is not None,
                             has_final=final_w is not None, n_cast=len(cast))
    outs = pl.pallas_call(
        body,
        grid=(n_steps,),
        in_specs=in_specs,
        out_specs=out_specs,
        out_shape=out_shape,
        compiler_params=_params(1),
        name="ffn_mix" if mix is not None else "ffn",
    )(*args)
    return outs if cast else outs[0]


def _proj_body(x_ref, mod_ref, nw_ref, w_ref, heads_ref, groups_ref, zs_ref):
    sh = mod_ref[0, 3:4, :]
    sc = mod_ref[0, 4:5, :]
    h = (_rmsnorm(x_ref[...], nw_ref[...]) * (1.0 + sc) + sh).astype(BF16)
    o_z = COL_Z + 2 * GLA_LOWRANK
    o_dt = o_z + SSD_INNER + SSD_CONV_DIM
    qkvr = _dot_nt(h, w_ref[0:COL_Z, :])
    for hd in range(GLA_HEADS):
        for dst, src, wid in ((0, COL_Q + hd * GLA_DK, GLA_DK), (HEAD_K, COL_K + hd * GLA_DK, GLA_DK),
                              (HEAD_V, COL_V + hd * GLA_DV, GLA_DV), (HEAD_R, COL_R + hd * GLA_DV, GLA_DV)):
            heads_ref[hd, :, dst:dst + wid] = qkvr[:, src:src + wid]
    zx = _dot_nt(h, w_ref[o_z:o_dt, :])
    zs_ref[:, 0:ZS_SMALL] = zx[:, 0:SSD_INNER]
    for g in range(SSD_GROUPS):
        for dst, src, wid in ((0, g * GRP_B, GRP_B), (GRP_B, SSD_INNER + g * SSD_STATE, SSD_STATE),
                              (GRP_C, SSD_INNER + SSD_BC + g * SSD_STATE, SSD_STATE)):
            groups_ref[g, :, dst:dst + wid] = zx[:, SSD_INNER + src:SSD_INNER + src + wid]
    w_small = jnp.concatenate(
        [w_ref[COL_Z:o_z, :], w_ref[o_dt:, :],
         jnp.zeros((LANES - 2 * GLA_LOWRANK - 2 * SSD_HEADS, D_MODEL), BF16)], axis=0)
    zs_ref[:, ZS_SMALL:ZS_W] = _dot_nt(h, w_small)


def _proj_call(x, mod, norm_w, w_all, *, seq_len):
    m_tok = x.shape[0]
    tm = TOKEN_TILE
    tiles_per_seq = seq_len // tm
    shared_mod = mod.shape[0] == 1
    mod_map = (lambda i: (0, 0, 0)) if shared_mod else (lambda i: (i // tiles_per_seq, 0, 0))
    return pl.pallas_call(
        _proj_body,
        grid=(m_tok // tm,),
        in_specs=[pl.BlockSpec((tm, D_MODEL), lambda i: (i, 0)),
                  pl.BlockSpec((1, N_MOD, D_MODEL), mod_map),
                  _resident((1, D_MODEL)), _resident(w_all.shape)],
        out_specs=[pl.BlockSpec((GLA_HEADS, tm, HEAD_W), lambda i: (0, i, 0)),
                   pl.BlockSpec((SSD_GROUPS, tm, GRP_W), lambda i: (0, i, 0)),
                   pl.BlockSpec((tm, ZS_W), lambda i: (i, 0))],
        out_shape=[jax.ShapeDtypeStruct((GLA_HEADS, m_tok, HEAD_W), F32),
                   jax.ShapeDtypeStruct((SSD_GROUPS, m_tok, GRP_W), F32),
                   jax.ShapeDtypeStruct((m_tok, ZS_W), F32)],
        compiler_params=_params(1),
        name="proj",
    )(x, mod, norm_w, w_all)


def _tri_masks():
    row = lax.broadcasted_iota(jnp.int32, (CHUNK, CHUNK), 0)
    col = lax.broadcasted_iota(jnp.int32, (CHUNK, CHUNK), 1)
    return row >= col, row <= col


def _gla_body(*refs, seq_len, n_sub, has_h0, want_state):
    it = iter(refs)
    hd_ref, sm_ref, wa_ref, ba_ref, nw_ref = (next(it) for _ in range(5))
    h0_ref = next(it) if has_h0 else None
    o_ref = next(it)
    st_ref = next(it) if want_state else None
    (la_scr, qk_scr, qg_scr, ks_scr, vb_scr, sc_scr, kv_scr, dec_scr,
     sprev_scr) = (next(it) for _ in range(9))

    nc = seq_len // CHUNK
    nct = n_sub * nc
    dk = GLA_DK
    lower, upper = _tri_masks()
    lower_b = jnp.where(lower, 1.0, 0.0).astype(BF16)
    upper_b = jnp.where(upper, 1.0, 0.0).astype(BF16)
    unroll = min(GLA_UNROLL, nct)
    rows = lambda c: pl.ds(pl.multiple_of(c * CHUNK, CHUNK), CHUNK)

    pre = _dot(sm_ref[...].astype(BF16), wa_ref[...].astype(BF16)) + ba_ref[...]
    la_scr[...] = _log_sigmoid(pre) * (LOG2_E / GLA_TAU)

    def prep(c, carry):
        sl = rows(c)
        qc = hd_ref[0, sl, 0:HEAD_K] * (GLA_DK ** -0.5)
        kc = hd_ref[0, sl, HEAD_K:HEAD_V]
        vb_scr[sl, :] = hd_ref[0, sl, HEAD_V:HEAD_R].astype(BF16)
        la = la_scr[sl, :]
        hi = la.astype(BF16)
        lo = (la - hi.astype(F32)).astype(BF16)

        def cumulative(tri_b, a):
            p = _dot(tri_b, jnp.concatenate([hi[:, a:a + dk], lo[:, a:a + dk]], axis=1))
            return p[:, 0:dk] + p[:, dk:2 * dk]

        g_f = cumulative(lower_b, 0)
        g_b = cumulative(upper_b, dk)
        mid_f = g_f[CHUNK // 2:CHUNK // 2 + 1, :]
        mid_b = g_b[CHUNK - 1 - CHUNK // 2:CHUNK - CHUNK // 2, :]
        end_f = g_f[CHUNK - 1:CHUNK, :]
        end_b = g_b[0:1, :]
        qk_scr[sl, 0:dk] = (qc * jnp.exp2(g_f - mid_f)).astype(BF16)
        qk_scr[sl, dk:2 * dk] = (qc * jnp.exp2(g_b - mid_b)).astype(BF16)
        qk_scr[sl, 2 * dk:3 * dk] = (kc * jnp.exp2(mid_f - g_f)).astype(BF16)
        qk_scr[sl, 3 * dk:4 * dk] = (kc * jnp.exp2(mid_b - g_b)).astype(BF16)
        qg_scr[sl, 0:dk] = (qc * jnp.exp2(g_f)).astype(BF16)
        qg_scr[sl, dk:2 * dk] = (qc * jnp.exp2(g_b)).astype(BF16)
        ks_scr[sl, 0:dk] = (kc * jnp.exp2(end_f - g_f)).astype(BF16)
        ks_scr[sl, dk:2 * dk] = (kc * jnp.exp2(end_b - g_b)).astype(BF16)
        dec = jnp.exp2(jnp.concatenate([end_f, end_b], axis=1))
        dec_scr[c] = jnp.broadcast_to(dec, (SUBLANES, 2 * dk))
        return carry

    lax.fori_loop(0, nct, prep, 0, unroll=unroll)

    def products(c, carry):
        sl = rows(c)
        s_f = _dot_nt(qk_scr[sl, 0:dk], qk_scr[sl, 2 * dk:3 * dk])
        s_b = _dot_nt(qk_scr[sl, dk:2 * dk], qk_scr[sl, 3 * dk:4 * dk])
        sc_scr[sl, :] = (jnp.where(lower, s_f, 0.0) + jnp.where(upper, s_b, 0.0)).astype(BF16)
        kv_scr[c] = _dot_tn(vb_scr[sl, :], ks_scr[sl, :])
        return carry

    lax.fori_loop(0, nct, products, 0, unroll=unroll)

    for b in range(n_sub):
        for d, a in ((0, 0), (1, dk)):
            state0 = h0_ref[b, 0, d, 0].T if has_h0 else jnp.zeros((GLA_DV, dk), F32)

            def recur(i, state, b=b, d=d, a=a):
                c = b * nc + (i if d == 0 else nc - 1 - i)
                sprev_scr[c, :, a:a + dk] = state.astype(BF16)
                return state * dec_scr[c, 0:1, a:a + dk] + kv_scr[c, :, a:a + dk]

            state = lax.fori_loop(0, nc, recur, state0, unroll=min(nc, 4))
            if want_state:
                st_ref[b, 0, d, 0] = state.T

    def finish(c, carry):
        sl = rows(c)
        o = _dot(sc_scr[sl, :], vb_scr[sl, :]) + _dot_nt(qg_scr[sl, :], sprev_scr[c])
        o_ref[sl, :] = (_rmsnorm(o, nw_ref[...]) * _silu(hd_ref[0, sl, HEAD_R:HEAD_W])).astype(o_ref.dtype)
        return carry

    lax.fori_loop(0, nct, finish, 0, unroll=unroll)


def _gla_call(heads, zs, wa_cat, ba_cat, norm_w, h0, layer, *, n_seq, seq_len, want_state):
    n_sub = max(1, SCAN_BLOCK_TOKENS // seq_len)
    L = n_sub * seq_len
    nc = L // CHUNK
    in_specs = [pl.BlockSpec((1, L, HEAD_W), lambda s, h: (h, s, 0)),
                pl.BlockSpec((L, LANES), lambda s, h: (s, ZS_SMALL // LANES)),
                pl.BlockSpec((LANES, 2 * GLA_DK), lambda s, h: (0, h)),
                pl.BlockSpec((1, 2 * GLA_DK), lambda s, h: (0, h)),
                pl.BlockSpec((1, GLA_DV), lambda s, h: (0, 0))]
    args = [heads, zs, wa_cat, ba_cat, norm_w]
    state_blk = (n_sub, 1, 2, 1, GLA_DK, GLA_DV)
    if h0 is not None:
        in_specs.append(pl.BlockSpec(state_blk, lambda s, h: (s, layer, 0, h, 0, 0)))
        args.append(h0)
    out_specs = [pl.BlockSpec((L, GLA_DV), lambda s, h: (s, h))]
    out_shape = [jax.ShapeDtypeStruct((n_seq * seq_len, GLA_V), BF16)]
    if want_state:
        out_specs.append(pl.BlockSpec(state_blk, lambda s, h: (s, 0, 0, h, 0, 0)))
        out_shape.append(jax.ShapeDtypeStruct((n_seq, 1, 2, GLA_HEADS, GLA_DK, GLA_DV), F32))
    body = functools.partial(_gla_body, seq_len=seq_len, n_sub=n_sub, has_h0=h0 is not None,
                             want_state=want_state)
    return pl.pallas_call(
        body,
        grid=(n_seq // n_sub, GLA_HEADS),
        in_specs=in_specs,
        out_specs=out_specs,
        out_shape=out_shape,
        scratch_shapes=[pltpu.VMEM((L, 2 * GLA_DK), F32),
                        pltpu.VMEM((L, 4 * GLA_DK), BF16),
                        pltpu.VMEM((L, 2 * GLA_DK), BF16),
                        pltpu.VMEM((L, 2 * GLA_DK), BF16),
                        pltpu.VMEM((L, GLA_DV), BF16),
                        pltpu.VMEM((L, CHUNK), BF16),
                        pltpu.VMEM((nc, GLA_DV, 2 * GLA_DK), F32),
                        pltpu.VMEM((nc, SUBLANES, 2 * GLA_DK), F32),
                        pltpu.VMEM((nc, GLA_DV, 2 * GLA_DK), BF16)],
        compiler_params=_params(2),
        name="gla",
    )(*args)


def _ssd_body(*refs, seq_len, grid_rows, has_h0, want_state, n_cast):
    it = iter(refs)
    (grp_ref, sm_ref, cwx_ref, cwb_ref, cwc_ref, cbx_ref, cbb_ref, cbc_ref,
     par_ref, dsk_ref) = (next(it) for _ in range(10))
    h0_ref = next(it) if has_h0 else None
    cast_src = [next(it) for _ in range(n_cast)]
    y_ref = next(it)
    st_ref = next(it) if want_state else None
    _cast_blocks(cast_src, [next(it) for _ in range(n_cast)])
    (pad_scr, xs_s, b_s, c_s, dt_scr, xy_scr, tr_scr, rows_scr, sc_scr, ed_scr, xw_scr, dec_scr,
     cs_scr, sprev_scr, state_scr) = (next(it) for _ in range(15))

    L = seq_len
    nc = L // CHUNK
    width = L // grid_rows
    pad = pad_scr.shape[0] - L
    pad //= 2
    grp = pl.program_id(1)

    def conv_into(c0, cw_ref, cb_ref, dst_ref):
        ch = dst_ref.shape[1]
        pad_scr[0:pad, 0:ch] = jnp.zeros((pad, ch), F32)
        pad_scr[pad + L:pad + L + pad, 0:ch] = jnp.zeros((pad, ch), F32)
        pad_scr[pad:pad + L, 0:ch] = grp_ref[0, :, c0:c0 + ch]
        rc = min(2 * CHUNK, L)
        col = lax.broadcasted_iota(jnp.int32, (rc, ch), 0) % width
        di_taps = range(CONV_K) if grid_rows > 1 else (CONV_K // 2,)
        for r0 in range(0, L, rc):
            acc = jnp.broadcast_to(cb_ref[...], (rc, ch))
            for dj in range(CONV_K):
                inner = None
                for di in di_taps:
                    off = pad + r0 + (di - 1) * width + (dj - 1)
                    term = cw_ref[di, dj:dj + 1, :] * pad_scr[off:off + rc, 0:ch]
                    inner = term if inner is None else inner + term
                if grid_rows > 1 and dj == 0:
                    inner = jnp.where(col >= 1, inner, 0.0)
                if grid_rows > 1 and dj == CONV_K - 1:
                    inner = jnp.where(col <= width - 2, inner, 0.0)
                acc = acc + inner
            dst_ref[r0:r0 + rc, :] = _silu(acc).astype(dst_ref.dtype)

    conv_into(0, cwx_ref, cbx_ref, xs_s)
    conv_into(GRP_B, cwb_ref, cbb_ref, b_s)
    conv_into(GRP_C, cwc_ref, cbc_ref, c_s)

    gw = GROUP_W
    unroll = min(SSD_UNROLL, nc)
    rows = lambda c: pl.ds(pl.multiple_of(c * CHUNK, CHUNK), CHUNK)
    bias_row = par_ref[0:1, :]
    a_row = -jnp.exp(par_ref[1:2, :]) * par_ref[2:3, :] * LOG2_E
    lower, upper = _tri_masks()
    lower_b = jnp.where(lower, 1.0, 0.0).astype(BF16)
    upper_b = jnp.where(upper, 1.0, 0.0).astype(BF16)
    lane = lax.broadcasted_iota(jnp.int32, (CHUNK, LANES), 1)
    dt_lanes = (lane >= SM_DTF) & (lane < SM_DTF + TERM_STRIDE)
    src = lax.broadcasted_iota(jnp.int32, (LANES, 2 * gw), 0)
    dst = lax.broadcasted_iota(jnp.int32, (LANES, 2 * gw), 1)
    dst_slot = (grp * HEADS_PER_GROUP
                + (lax.shift_right_logical(dst, HEAD_DIM_SHIFT) & (HEADS_PER_GROUP - 1))
                + jnp.where(dst >= gw, SSD_HEADS, 0))
    expand = jnp.where((src >= SM_DTF) & ((src & (TERM_STRIDE - 1)) == dst_slot),
                       1.0, 0.0).astype(BF16)
    t_idx = lax.broadcasted_iota(jnp.int32, (CHUNK, gw), 0)
    s_idx = lax.broadcasted_iota(jnp.int32, (CHUNK, gw), 1) & (CHUNK - 1)
    diag = t_idx == s_idx
    blk_r = lax.shift_right_logical(lax.broadcasted_iota(jnp.int32, (MXU_TILE, MXU_TILE), 0),
                                    HEAD_DIM_SHIFT)
    blk_c = lax.shift_right_logical(lax.broadcasted_iota(jnp.int32, (MXU_TILE, MXU_TILE), 1),
                                    HEAD_DIM_SHIFT)
    same_head = blk_r == blk_c

    dt_scr[...] = _softplus(sm_ref[...] + bias_row)

    def place3(v):
        hi = v.astype(BF16).astype(F32)
        rest = v - hi
        mid = rest.astype(BF16).astype(F32)
        lo = rest - mid
        keep = lambda t: jnp.where(dt_lanes, t, 0.0)
        out = (keep(hi) + pltpu.roll(keep(mid), TERM_STRIDE, axis=1)
               + pltpu.roll(keep(lo), 2 * TERM_STRIDE, axis=1))
        return out.astype(BF16)

    def cumulate(c, carry):
        sl = rows(c)
        dt = dt_scr[sl, :]
        da = dt * a_row
        hi = da.astype(BF16)
        both = jnp.concatenate([hi, (da - hi.astype(F32)).astype(BF16)], axis=1)
        pf = _dot(lower_b, both)
        pb = _dot(upper_b, both)
        cum = jnp.where(lane < SM_DTB, pf[:, 0:LANES] + pf[:, LANES:], pb[:, 0:LANES] + pb[:, LANES:])
        xy_scr[sl, 0:LANES] = place3(cum)
        xy_scr[sl, LANES:2 * LANES] = place3(dt)
        tr_scr[c, 0] = cum.T
        tr_scr[c, 1] = dt.T
        head0 = pl.multiple_of(SM_DTF + grp * HEADS_PER_GROUP, SUBLANES)
        for q, (k, d) in enumerate(((0, 0), (0, 1), (1, 0), (1, 1))):
            slab = tr_scr[c, k, pl.ds(head0 + d * SSD_HEADS, HEADS_PER_GROUP), :]
            row = jnp.concatenate(
                [jnp.broadcast_to(slab[h:h + 1, :], (SUBLANES, CHUNK))
                 for h in range(HEADS_PER_GROUP)], axis=1)
            rows_scr[c, :, q * gw:(q + 1) * gw] = row
        return carry

    lax.fori_loop(0, nc, cumulate, 0, unroll=unroll)

    def weights(c, carry):
        sl = rows(c)
        xc = xs_s[sl, :]
        cum_e = _dot(xy_scr[sl, 0:LANES], expand)
        dt_e = _dot(xy_scr[sl, LANES:2 * LANES], expand)
        cb = _dot_nt(c_s[sl, :], jnp.concatenate([b_s[sl, :]] * HEADS_PER_GROUP, axis=0))
        segs, dt_rows, dec_rows = [], [], []
        for d, (a, last_i) in enumerate(((0, CHUNK - 1), (gw, 0))):
            ce = cum_e[:, a:a + gw]
            de = dt_e[:, a:a + gw]
            segs.append(ce - rows_scr[c, 0:1, d * gw:(d + 1) * gw])
            dt_rows.append(rows_scr[c, 0:1, (2 + d) * gw:(3 + d) * gw])
            cum_last = ce[last_i:last_i + 1, :]
            ed_scr[sl, a:a + gw] = jnp.exp2(ce)
            xw_scr[sl, a:a + gw] = (xc * (jnp.exp2(cum_last - ce) * de)).astype(BF16)
            dec_rows.append(jnp.exp2(cum_last))
        fwd = t_idx >= s_idx
        w = (jnp.exp2(jnp.where(fwd, segs[0], segs[1])) * jnp.where(fwd, dt_rows[0], dt_rows[1])
             + jnp.where(diag, dt_rows[1], 0.0))
        sc_scr[sl, :] = (cb * w).astype(BF16)
        dec_scr[c] = jnp.broadcast_to(jnp.concatenate(dec_rows, axis=1), (SUBLANES, 2 * gw))
        return carry

    lax.fori_loop(0, nc, weights, 0, unroll=unroll)

    def products(c, carry):
        sl = rows(c)
        xc = xs_s[sl, :]
        xb = xc.astype(BF16)
        parts = []
        for j in range(gw // MXU_TILE):
            xh = xb[:, j * MXU_TILE:(j + 1) * MXU_TILE]
            rep = jnp.concatenate([xh] * (MXU_TILE // SSD_HEAD_DIM), axis=0)
            bd = jnp.where(same_head, rep, jnp.zeros_like(rep))
            parts.append(_dot(sc_scr[sl, j * MXU_TILE:(j + 1) * MXU_TILE], bd))
        y_ref[sl, :] = jnp.concatenate(parts, axis=1) + xc * dsk_ref[...]
        cs_scr[c] = _dot_tn(b_s[sl, :], xw_scr[sl, :])
        return carry

    lax.fori_loop(0, nc, products, 0, unroll=unroll)

    pair_w = 2 * SSD_HEAD_DIM
    if has_h0:
        for d in range(2):
            for j in range(HEADS_PER_GROUP // 2):
                pair = jnp.concatenate([h0_ref[0, 0, d, 2 * j], h0_ref[0, 0, d, 2 * j + 1]], axis=0)
                state_scr[:, d * gw + j * pair_w:d * gw + (j + 1) * pair_w] = pair.T
    else:
        state_scr[...] = jnp.zeros_like(state_scr)

    for d in range(2):
        for a in range(d * gw, (d + 1) * gw, MXU_TILE):

            def recur(i, state, d=d, a=a):
                c = i if d == 0 else nc - 1 - i
                sprev_scr[c, :, a:a + MXU_TILE] = state.astype(BF16)
                return (state * dec_scr[c, 0:1, a:a + MXU_TILE]
                        + cs_scr[c, :, a:a + MXU_TILE])

            state_scr[:, a:a + MXU_TILE] = lax.fori_loop(
                0, nc, recur, state_scr[:, a:a + MXU_TILE], unroll=min(nc, 4))

    def finish(c, carry):
        sl = rows(c)
        yi = _dot(c_s[sl, :], sprev_scr[c]) * ed_scr[sl, :]
        y_ref[sl, :] += yi[:, 0:gw] + yi[:, gw:2 * gw]
        return carry

    lax.fori_loop(0, nc, finish, 0, unroll=unroll)

    if want_state:
        for d in range(2):
            for j in range(HEADS_PER_GROUP // 2):
                pair = state_scr[:, d * gw + j * pair_w:d * gw + (j + 1) * pair_w].T
                st_ref[0, 0, d, 2 * j] = pair[0:SSD_HEAD_DIM, :]
                st_ref[0, 0, d, 2 * j + 1] = pair[SSD_HEAD_DIM:pair_w, :]


def _ssd_call(groups, zs, conv_w, conv_b, par, dsk, h0, layer, *, n_seq, seq_len, grid_rows,
              want_state, cast=()):
    L = seq_len
    in_specs = [pl.BlockSpec((1, L, GRP_W), lambda s, g: (g, s, 0)),
                pl.BlockSpec((L, LANES), lambda s, g: (s, ZS_SMALL // LANES)),
                pl.BlockSpec((CONV_K, CONV_K, GROUP_W), lambda s, g: (0, 0, g)),
                pl.BlockSpec((CONV_K, CONV_K, SSD_STATE),
                             lambda s, g: (0, 0, SSD_INNER // SSD_STATE + g)),
                pl.BlockSpec((CONV_K, CONV_K, SSD_STATE),
                             lambda s, g: (0, 0, (SSD_INNER + SSD_BC) // SSD_STATE + g)),
                pl.BlockSpec((1, GROUP_W), lambda s, g: (0, g)),
                pl.BlockSpec((1, SSD_STATE), lambda s, g: (0, SSD_INNER // SSD_STATE + g)),
                pl.BlockSpec((1, SSD_STATE),
                             lambda s, g: (0, (SSD_INNER + SSD_BC) // SSD_STATE + g)),
                pl.BlockSpec((SUBLANES, LANES), lambda s, g: (0, 0)),
                pl.BlockSpec((1, GROUP_W), lambda s, g: (0, g))]
    args = [groups, zs, conv_w, conv_w, conv_w, conv_b, conv_b, conv_b, par, dsk]
    state_blk = (1, 1, 2, HEADS_PER_GROUP, SSD_HEAD_DIM, SSD_STATE)
    if h0 is not None:
        in_specs.append(pl.BlockSpec(state_blk, lambda s, g: (s, layer, 0, g, 0, 0)))
        args.append(h0)
    out_specs = [pl.BlockSpec((L, GROUP_W), lambda s, g: (s, g))]
    out_shape = [jax.ShapeDtypeStruct((n_seq * L, SSD_INNER), F32)]
    if want_state:
        out_specs.append(pl.BlockSpec(state_blk, lambda s, g: (s, 0, 0, g, 0, 0)))
        out_shape.append(jax.ShapeDtypeStruct(
            (n_seq, 1, 2, SSD_HEADS, SSD_HEAD_DIM, SSD_STATE), F32))
    cast_in, cast_out, cast_shape = _cast_plumbing(cast, n_seq * SSD_GROUPS,
                                                   lambda s, g: s * SSD_GROUPS + g)
    in_specs += cast_in
    args += list(cast)
    out_specs += cast_out
    out_shape += cast_shape
    nc = L // CHUNK
    conv_pad = (L // grid_rows + SUBLANES) if grid_rows > 1 else SUBLANES
    body = functools.partial(_ssd_body, seq_len=L, grid_rows=grid_rows,
                             has_h0=h0 is not None, want_state=want_state, n_cast=len(cast))
    return pl.pallas_call(
        body,
        grid=(n_seq, SSD_GROUPS),
        in_specs=in_specs,
        out_specs=out_specs,
        out_shape=out_shape,
        scratch_shapes=[pltpu.VMEM((L + 2 * conv_pad, GROUP_W), F32),
                        pltpu.VMEM((L, GROUP_W), F32),
                        pltpu.VMEM((L, SSD_STATE), BF16),
                        pltpu.VMEM((L, SSD_STATE), BF16),
                        pltpu.VMEM((L, LANES), F32),
                        pltpu.VMEM((L, 2 * LANES), BF16),
                        pltpu.VMEM((nc, 2, LANES, CHUNK), F32),
                        pltpu.VMEM((nc, SUBLANES, 4 * GROUP_W), F32),
                        pltpu.VMEM((L, GROUP_W), BF16),
                        pltpu.VMEM((L, 2 * GROUP_W), F32),
                        pltpu.VMEM((L, 2 * GROUP_W), BF16),
                        pltpu.VMEM((nc, SUBLANES, 2 * GROUP_W), F32),
                        pltpu.VMEM((nc, SSD_STATE, 2 * GROUP_W), F32),
                        pltpu.VMEM((nc, SSD_STATE, 2 * GROUP_W), BF16),
                        pltpu.VMEM((SSD_STATE, 2 * GROUP_W), F32)],
        compiler_params=_params(2),
        name="ssd",
    )(*args)


def _layer_path(x, mod, lw, raw, h0_gla, h0_ssd, layer, *, n_seq, seq_len, grid_rows,
                want_state, final_w):
    if raw is not None:
        x1, lw["w_all"] = _ffn_call(x, mod, lw["norm_ffn1"], lw["ffn1_w_in"], lw["ffn1_w_out"],
                                    sub=0, seq_len=seq_len, cast=(raw["w_in"],))
    else:
        x1 = _ffn_call(x, mod, lw["norm_ffn1"], lw["ffn1_w_in"], lw["ffn1_w_out"],
                       sub=0, seq_len=seq_len)
    heads, groups, zs = _proj_call(x1, mod, lw["norm_mix"], lw["w_all"], seq_len=seq_len)
    gla_out = _gla_call(heads, zs, lw["wa_cat"], lw["ba_cat"], lw["gla_norm_w"], h0_gla, layer,
                        n_seq=n_seq, seq_len=seq_len, want_state=want_state)
    late = ("ffn2_w_in", "ffn2_w_out", "w_out")
    ssd_out = _ssd_call(groups, zs, lw["conv_w"], lw["conv_b"], lw["ssd_par"], lw["d_skip_row"],
                        h0_ssd, layer, n_seq=n_seq, seq_len=seq_len, grid_rows=grid_rows,
                        want_state=want_state,
                        cast=tuple(raw[k] for k in late) if raw is not None else ())
    if raw is not None:
        lw.update(zip(late, ssd_out[-len(late):]))
    o, y = gla_out[0], ssd_out[0]
    out = _ffn_call(x1, mod, lw["norm_ffn2"], lw["ffn2_w_in"], lw["ffn2_w_out"],
                    sub=2, seq_len=seq_len, mix=(o, y, zs, lw["ssd_norm_w"], lw["w_out"]),
                    final_w=final_w)
    if want_state:
        return out, gla_out[1], ssd_out[1]
    return out, None, None


def kernel(x_prompt, x_sample, state_gla, state_ssd, c, c_ctx, norm_ffn1, norm_mix, norm_ffn2, w_mod, b_mod, ffn1_w_in, ffn1_w_out, ffn2_w_in, ffn2_w_out, w_in, gla_w_a2, gla_b_a, gla_norm_w, conv_w, conv_b, dt_bias, a_log, d_skip, ssd_norm_w, w_out, final_norm):
    nb, seq, _ = x_prompt.shape
    db, dseq, _ = x_sample.shape
    grid_rows = dseq // GRID_W
    xp = x_prompt.reshape(nb * seq, D_MODEL)
    xs = x_sample.reshape(db * dseq, D_MODEL)
    row = lambda v: v.reshape(1, -1)
    gla_states, ssd_states = [], []
    for i in range(DEPTH):
        last = i == DEPTH - 1
        n_rows = -(-(db + 1) // SUBLANES) * SUBLANES
        cc = jnp.concatenate([c, c_ctx[None, :], jnp.zeros((n_rows - db - 1, D_MODEL), F32)], 0)
        mod = _mod_call(cc, w_mod[i], row(b_mod[i])).reshape(n_rows, N_MOD, D_MODEL)
        mod_lat, mod_ctx = mod[:db], mod[db:db + 1]

        wa_cat = jnp.stack(
            [jnp.pad(gla_w_a2[i, d].reshape(GLA_LOWRANK, GLA_HEADS, GLA_DK),
                     ((lo, LANES - lo - GLA_LOWRANK), (0, 0), (0, 0)))
             for d, lo in ((0, SM_AF), (1, SM_AB))], axis=2).reshape(LANES, 2 * GLA_QK)
        ba_cat = jnp.stack([gla_b_a[i, 0].reshape(GLA_HEADS, GLA_DK),
                            gla_b_a[i, 1].reshape(GLA_HEADS, GLA_DK)], axis=1).reshape(1, 2 * GLA_QK)
        ssd_par = jnp.pad(
            jnp.stack([dt_bias[i].reshape(-1), a_log[i].reshape(-1), jnp.ones((2 * SSD_HEADS,), F32)]),
            ((0, SUBLANES - 3), (SM_DTF, LANES - SM_DTF - 2 * SSD_HEADS)))
        lw = {
            "norm_ffn1": row(norm_ffn1[i]), "norm_mix": row(norm_mix[i]),
            "norm_ffn2": row(norm_ffn2[i]),
            "ffn1_w_in": ffn1_w_in[i].astype(BF16), "ffn1_w_out": ffn1_w_out[i].astype(BF16),
            "wa_cat": wa_cat, "ba_cat": ba_cat,
            "gla_norm_w": row(gla_norm_w[i]), "conv_w": conv_w[i], "conv_b": row(conv_b[i]),
            "ssd_par": ssd_par, "d_skip_row": row(jnp.repeat(d_skip[i], SSD_HEAD_DIM)),
            "ssd_norm_w": row(ssd_norm_w[i]),
        }
        raw = {"w_in": jnp.swapaxes(w_in[i], 0, 1), "ffn2_w_in": ffn2_w_in[i], "ffn2_w_out": ffn2_w_out[i],
               "w_out": w_out[i]}
        fw = row(final_norm) if last else None

        xp, sg, ss = _layer_path(xp, mod_ctx, lw, raw, None, None, i, n_seq=nb, seq_len=seq,
                                 grid_rows=1, want_state=True, final_w=fw)
        xs, _, _ = _layer_path(xs, mod_lat, lw, None, state_gla, state_ssd, i, n_seq=db,
                               seq_len=dseq, grid_rows=grid_rows, want_state=False, final_w=fw)
        gla_states.append(sg)
        ssd_states.append(ss)
    y_prompt = xp.reshape(nb, seq, D_MODEL)
    y_sample = xs.reshape(db, dseq, D_MODEL)
    return (y_prompt, y_sample, jnp.concatenate(gla_states, axis=1),
            jnp.concatenate(ssd_states, axis=1))
```

```python
import functools

import jax
import jax.numpy as jnp
from jax import lax
from jax.experimental import pallas as pl
from jax.experimental.pallas import tpu as pltpu

F32 = jnp.float32
BF16 = jnp.bfloat16

D_MODEL = 1024
DEPTH = 1
GRID_W = 64
CHUNK = 64
EPS = 1e-6
N_MOD = 9
D_FF = 2816
GLA_HEADS = 4
GLA_DK = 128
GLA_DV = 256
GLA_LOWRANK = 16
GLA_TAU = 16.0
GLA_QK = GLA_HEADS * GLA_DK
GLA_V = GLA_HEADS * GLA_DV
SSD_HEADS = 16
SSD_HEAD_DIM = 64
SSD_GROUPS = 2
SSD_STATE = 128
SSD_INNER = SSD_HEADS * SSD_HEAD_DIM
SSD_BC = SSD_GROUPS * SSD_STATE
SSD_CONV_DIM = SSD_INNER + 2 * SSD_BC
CONV_K = 3
D_MIX = GLA_V + SSD_INNER

LOG2_E = 1.4426950408889634
LANES = 128
SUBLANES = 8
BF16_SUBLANES = 16
VMEM_LIMIT_BYTES = 56 * 1024 * 1024

COL_Q = 0
COL_K = COL_Q + GLA_QK
COL_V = COL_K + GLA_QK
COL_R = COL_V + GLA_V
COL_Z = COL_R + GLA_V
HEAD_K = GLA_DK
HEAD_V = 2 * GLA_DK
HEAD_R = HEAD_V + GLA_DV
HEAD_W = HEAD_R + GLA_DV
GRP_B = SSD_INNER // SSD_GROUPS
GRP_C = GRP_B + SSD_STATE
GRP_W = GRP_C + SSD_STATE
ZS_SMALL = SSD_INNER
ZS_W = ZS_SMALL + LANES
SM_AF = 0
SM_AB = SM_AF + GLA_LOWRANK
SM_DTF = SM_AB + GLA_LOWRANK
SM_DTB = SM_DTF + SSD_HEADS
HEADS_PER_GROUP = SSD_HEADS // SSD_GROUPS
GROUP_W = GRP_B

TOKEN_TILE = 512
GLA_UNROLL = 32
SSD_UNROLL = 16
MXU_TILE = 256
TERM_STRIDE = 2 * SSD_HEADS
HEAD_DIM_SHIFT = SSD_HEAD_DIM.bit_length() - 1
assert 1 << HEAD_DIM_SHIFT == SSD_HEAD_DIM == CHUNK
SCAN_BLOCK_TOKENS = 2048
SCAN_BLOCK_SEQS = 4


def _dot(a, b):
    return jnp.dot(a, b, preferred_element_type=F32)


def _dot_nt(a, b):
    return lax.dot_general(a, b, (((1,), (1,)), ((), ())), preferred_element_type=F32)


def _dot_tn(a, b):
    return lax.dot_general(a, b, (((0,), (0,)), ((), ())), preferred_element_type=F32)


def _silu(x):
    return x * jax.nn.sigmoid(x)


def _log1p_exp_neg_abs(x):
    return jnp.log(1.0 + jnp.exp(-jnp.abs(x)))


def _softplus(x):
    return jnp.maximum(x, 0.0) + _log1p_exp_neg_abs(x)


def _log_sigmoid(x):
    return jnp.minimum(x, 0.0) - _log1p_exp_neg_abs(x)


def _rmsnorm(x, w):
    ms = jnp.mean(x * x, axis=-1, keepdims=True)
    return x * lax.rsqrt(ms + EPS) * w


def _resident(shape):
    nd = len(shape)
    return pl.BlockSpec(shape, lambda *_: (0,) * nd, pipeline_mode=pl.Buffered(1))


def _params(n_axes):
    return pltpu.CompilerParams(dimension_semantics=("arbitrary",) * n_axes,
                                vmem_limit_bytes=VMEM_LIMIT_BYTES)


def _cast_plumbing(cast, n_steps, step_of):
    in_specs, out_specs, out_shape = [], [], []
    for w in cast:
        rows, cols = w.shape
        per_step = -(-rows // n_steps)
        per_step = -(-per_step // BF16_SUBLANES) * BF16_SUBLANES
        last = -(-rows // per_step) - 1
        blk = pl.BlockSpec((per_step, cols),
                           lambda *g, last=last: (jnp.minimum(step_of(*g), last), 0))
        in_specs.append(blk)
        out_specs.append(blk)
        out_shape.append(jax.ShapeDtypeStruct((rows, cols), BF16))
    return in_specs, out_specs, out_shape


def _cast_blocks(src_refs, dst_refs):
    for src_ref, dst_ref in zip(src_refs, dst_refs):
        dst_ref[...] = src_ref[...].astype(dst_ref.dtype)


def _mod_body(c_ref, w_ref, b_ref, out_ref):
    a = _silu(c_ref[...]).astype(BF16)
    out_ref[...] = _dot(a, w_ref[...].astype(BF16)) + b_ref[...]


def _mod_call(cc, w_mod, b_mod):
    n_rows = cc.shape[0]
    tn = D_MODEL
    return pl.pallas_call(
        _mod_body,
        grid=(N_MOD * D_MODEL // tn,),
        in_specs=[pl.BlockSpec((n_rows, D_MODEL), lambda j: (0, 0)),
                  pl.BlockSpec((D_MODEL, tn), lambda j: (0, j)),
                  pl.BlockSpec((1, tn), lambda j: (0, j))],
        out_specs=pl.BlockSpec((n_rows, tn), lambda j: (0, j)),
        out_shape=jax.ShapeDtypeStruct((n_rows, N_MOD * D_MODEL), F32),
        compiler_params=_params(1),
        name="mod",
    )(cc, w_mod, b_mod)


def _ffn_body(*refs, sub, has_mix, has_final, n_cast):
    it = iter(refs)
    x_ref, mod_ref, nw_ref, win_ref, wout_ref = (next(it) for _ in range(5))
    if has_mix:
        o_ref, y_ref, z_ref, snw_ref, wo_ref = (next(it) for _ in range(5))
    if has_final:
        fn_ref = next(it)
    cast_src = [next(it) for _ in range(n_cast)]
    out_ref = next(it)
    _cast_blocks(cast_src, [next(it) for _ in range(n_cast)])

    x = x_ref[...]
    if has_mix:
        g2 = mod_ref[0, 5:6, :]
        yn = _rmsnorm(y_ref[...] * _silu(z_ref[...]), snw_ref[...])
        m = _dot(jnp.concatenate([o_ref[...], yn.astype(BF16)], axis=1), wo_ref[...])
        x = x + g2 * m
    sh = mod_ref[0, 3 * sub:3 * sub + 1, :]
    sc = mod_ref[0, 3 * sub + 1:3 * sub + 2, :]
    gate = mod_ref[0, 3 * sub + 2:3 * sub + 3, :]
    h = (_rmsnorm(x, nw_ref[...]) * (1.0 + sc) + sh).astype(BF16)
    g = _dot(h, win_ref[:, :D_FF])
    u = _dot(h, win_ref[:, D_FF:])
    act = (_silu(g) * u).astype(BF16)
    x = x + (0.5 * gate) * _dot(act, wout_ref[...])
    if has_final:
        x = _rmsnorm(x, fn_ref[...])
    out_ref[...] = x


def _ffn_call(x, mod, norm_w, w_in, w_out, *, sub, seq_len, mix=None, final_w=None, cast=()):
    m_tok = x.shape[0]
    tm = TOKEN_TILE
    tiles_per_seq = seq_len // tm
    shared_mod = mod.shape[0] == 1
    mod_map = (lambda i: (0, 0, 0)) if shared_mod else (lambda i: (i // tiles_per_seq, 0, 0))
    tok = lambda width: pl.BlockSpec((tm, width), lambda i: (i, 0))
    in_specs = [tok(D_MODEL), pl.BlockSpec((1, N_MOD, D_MODEL), mod_map),
                _resident((1, D_MODEL)), _resident(w_in.shape), _resident(w_out.shape)]
    args = [x, mod, norm_w, w_in, w_out]
    if mix is not None:
        o, y, proj, ssd_norm_w, w_mix_out = mix
        in_specs += [tok(GLA_V), tok(SSD_INNER),
                     pl.BlockSpec((tm, SSD_INNER), lambda i: (i, 0)),
                     _resident((1, SSD_INNER)), _resident(w_mix_out.shape)]
        args += [o, y, proj, ssd_norm_w, w_mix_out]
    if final_w is not None:
        in_specs.append(_resident((1, D_MODEL)))
        args.append(final_w)
    n_steps = m_tok // tm
    cast_in, cast_out, cast_shape = _cast_plumbing(cast, n_steps, lambda i: i)
    in_specs += cast_in
    args += list(cast)
    out_specs = [tok(D_MODEL)] + cast_out
    out_shape = [jax.ShapeDtypeStruct((m_tok, D_MODEL), F32)] + cast_shape
    body = functools.partial(_ffn_body, sub=sub, has_mix=mix is not None,
                             has_final=final_w is not None, n_cast=len(cast))
    outs = pl.pallas_call(
        body,
        grid=(n_steps,),
        in_specs=in_specs,
        out_specs=out_specs,
        out_shape=out_shape,
        compiler_params=_params(1),
        name="ffn_mix" if mix is not None else "ffn",
    )(*args)
    return outs if cast else outs[0]


def _proj_body(x_ref, mod_ref, nw_ref, w_ref, heads_ref, groups_ref, zs_ref):
    sh = mod_ref[0, 3:4, :]
    sc = mod_ref[0, 4:5, :]
    h = (_rmsnorm(x_ref[...], nw_ref[...]) * (1.0 + sc) + sh).astype(BF16)
    o_z = COL_Z + 2 * GLA_LOWRANK
    o_dt = o_z + SSD_INNER + SSD_CONV_DIM
    qkvr = _dot_nt(h, w_ref[0:COL_Z, :])
    for hd in range(GLA_HEADS):
        for dst, src, wid in ((0, COL_Q + hd * GLA_DK, GLA_DK), (HEAD_K, COL_K + hd * GLA_DK, GLA_DK),
                              (HEAD_V, COL_V + hd * GLA_DV, GLA_DV), (HEAD_R, COL_R + hd * GLA_DV, GLA_DV)):
            heads_ref[hd, :, dst:dst + wid] = qkvr[:, src:src + wid]
    zx = _dot_nt(h, w_ref[o_z:o_dt, :])
    zs_ref[:, 0:ZS_SMALL] = zx[:, 0:SSD_INNER]
    for g in range(SSD_GROUPS):
        for dst, src, wid in ((0, g * GRP_B, GRP_B), (GRP_B, SSD_INNER + g * SSD_STATE, SSD_STATE),
                              (GRP_C, SSD_INNER + SSD_BC + g * SSD_STATE, SSD_STATE)):
            groups_ref[g, :, dst:dst + wid] = zx[:, SSD_INNER + src:SSD_INNER + src + wid]
    w_small = jnp.concatenate(
        [w_ref[COL_Z:o_z, :], w_ref[o_dt:, :],
         jnp.zeros((LANES - 2 * GLA_LOWRANK - 2 * SSD_HEADS, D_MODEL), BF16)], axis=0)
    zs_ref[:, ZS_SMALL:ZS_W] = _dot_nt(h, w_small)


def _proj_call(x, mod, norm_w, w_all, *, seq_len):
    m_tok = x.shape[0]
    tm = TOKEN_TILE
    tiles_per_seq = seq_len // tm
    shared_mod = mod.shape[0] == 1
    mod_map = (lambda i: (0, 0, 0)) if shared_mod else (lambda i: (i // tiles_per_seq, 0, 0))
    return pl.pallas_call(
        _proj_body,
        grid=(m_tok // tm,),
        in_specs=[pl.BlockSpec((tm, D_MODEL), lambda i: (i, 0)),
                  pl.BlockSpec((1, N_MOD, D_MODEL), mod_map),
                  _resident((1, D_MODEL)), _resident(w_all.shape)],
        out_specs=[pl.BlockSpec((GLA_HEADS, tm, HEAD_W), lambda i: (0, i, 0)),
                   pl.BlockSpec((SSD_GROUPS, tm, GRP_W), lambda i: (0, i, 0)),
                   pl.BlockSpec((tm, ZS_W), lambda i: (i, 0))],
        out_shape=[jax.ShapeDtypeStruct((GLA_HEADS, m_tok, HEAD_W), F32),
                   jax.ShapeDtypeStruct((SSD_GROUPS, m_tok, GRP_W), F32),
                   jax.ShapeDtypeStruct((m_tok, ZS_W), F32)],
        compiler_params=_params(1),
        name="proj",
    )(x, mod, norm_w, w_all)


def _tri_masks():
    row = lax.broadcasted_iota(jnp.int32, (CHUNK, CHUNK), 0)
    col = lax.broadcasted_iota(jnp.int32, (CHUNK, CHUNK), 1)
    return row >= col, row <= col


def _gla_body(*refs, seq_len, n_sub, has_h0, want_state):
    it = iter(refs)
    hd_ref, sm_ref, wa_ref, ba_ref, nw_ref = (next(it) for _ in range(5))
    h0_ref = next(it) if has_h0 else None
    o_ref = next(it)
    st_ref = next(it) if want_state else None
    (la_scr, qk_scr, qg_scr, ks_scr, vb_scr, sc_scr, kv_scr, dec_scr,
     sprev_scr) = (next(it) for _ in range(9))

    nc = seq_len // CHUNK
    nct = n_sub * nc
    dk = GLA_DK
    lower, upper = _tri_masks()
    lower_b = jnp.where(lower, 1.0, 0.0).astype(BF16)
    upper_b = jnp.where(upper, 1.0, 0.0).astype(BF16)
    unroll = min(GLA_UNROLL, nct)
    rows = lambda c: pl.ds(pl.multiple_of(c * CHUNK, CHUNK), CHUNK)

    pre = _dot(sm_ref[...].astype(BF16), wa_ref[...].astype(BF16)) + ba_ref[...]
    la_scr[...] = _log_sigmoid(pre) * (LOG2_E / GLA_TAU)

    def prep(c, carry):
        sl = rows(c)
        qc = hd_ref[0, sl, 0:HEAD_K] * (GLA_DK ** -0.5)
        kc = hd_ref[0, sl, HEAD_K:HEAD_V]
        vb_scr[sl, :] = hd_ref[0, sl, HEAD_V:HEAD_R].astype(BF16)
        la = la_scr[sl, :]
        hi = la.astype(BF16)
        lo = (la - hi.astype(F32)).astype(BF16)

        def cumulative(tri_b, a):
            p = _dot(tri_b, jnp.concatenate([hi[:, a:a + dk], lo[:, a:a + dk]], axis=1))
            return p[:, 0:dk] + p[:, dk:2 * dk]

        g_f = cumulative(lower_b, 0)
        g_b = cumulative(upper_b, dk)
        mid_f = g_f[CHUNK // 2:CHUNK // 2 + 1, :]
        mid_b = g_b[CHUNK - 1 - CHUNK // 2:CHUNK - CHUNK // 2, :]
        end_f = g_f[CHUNK - 1:CHUNK, :]
        end_b = g_b[0:1, :]
        qk_scr[sl, 0:dk] = (qc * jnp.exp2(g_f - mid_f)).astype(BF16)
        qk_scr[sl, dk:2 * dk] = (qc * jnp.exp2(g_b - mid_b)).astype(BF16)
        qk_scr[sl, 2 * dk:3 * dk] = (kc * jnp.exp2(mid_f - g_f)).astype(BF16)
        qk_scr[sl, 3 * dk:4 * dk] = (kc * jnp.exp2(mid_b - g_b)).astype(BF16)
        qg_scr[sl, 0:dk] = (qc * jnp.exp2(g_f)).astype(BF16)
        qg_scr[sl, dk:2 * dk] = (qc * jnp.exp2(g_b)).astype(BF16)
        ks_scr[sl, 0:dk] = (kc * jnp.exp2(end_f - g_f)).astype(BF16)
        ks_scr[sl, dk:2 * dk] = (kc * jnp.exp2(end_b - g_b)).astype(BF16)
        dec = jnp.exp2(jnp.concatenate([end_f, end_b], axis=1))
        dec_scr[c] = jnp.broadcast_to(dec, (SUBLANES, 2 * dk))
        return carry

    lax.fori_loop(0, nct, prep, 0, unroll=unroll)

    def products(c, carry):
        sl = rows(c)
        s_f = _dot_nt(qk_scr[sl, 0:dk], qk_scr[sl, 2 * dk:3 * dk])
        s_b = _dot_nt(qk_scr[sl, dk:2 * dk], qk_scr[sl, 3 * dk:4 * dk])
        sc_scr[sl, :] = (jnp.where(lower, s_f, 0.0) + jnp.where(upper, s_b, 0.0)).astype(BF16)
        kv_scr[c] = _dot_tn(vb_scr[sl, :], ks_scr[sl, :])
        return carry

    lax.fori_loop(0, nct, products, 0, unroll=unroll)

    for b in range(n_sub):
        for d, a in ((0, 0), (1, dk)):
            state0 = h0_ref[b, 0, d, 0].T if has_h0 else jnp.zeros((GLA_DV, dk), F32)

            def recur(i, state, b=b, d=d, a=a):
                c = b * nc + (i if d == 0 else nc - 1 - i)
                sprev_scr[c, :, a:a + dk] = state.astype(BF16)
                return state * dec_scr[c, 0:1, a:a + dk] + kv_scr[c, :, a:a + dk]

            state = lax.fori_loop(0, nc, recur, state0, unroll=min(nc, 4))
            if want_state:
                st_ref[b, 0, d, 0] = state.T

    def finish(c, carry):
        sl = rows(c)
        o = _dot(sc_scr[sl, :], vb_scr[sl, :]) + _dot_nt(qg_scr[sl, :], sprev_scr[c])
        o_ref[sl, :] = (_rmsnorm(o, nw_ref[...]) * _silu(hd_ref[0, sl, HEAD_R:HEAD_W])).astype(o_ref.dtype)
        return carry

    lax.fori_loop(0, nct, finish, 0, unroll=unroll)


def _gla_call(heads, zs, wa_cat, ba_cat, norm_w, h0, layer, *, n_seq, seq_len, want_state):
    n_sub = max(1, min(SCAN_BLOCK_TOKENS // seq_len, SCAN_BLOCK_SEQS))
    L = n_sub * seq_len
    nc = L // CHUNK
    in_specs = [pl.BlockSpec((1, L, HEAD_W), lambda s, h: (h, s, 0)),
                pl.BlockSpec((L, LANES), lambda s, h: (s, ZS_SMALL // LANES)),
                pl.BlockSpec((LANES, 2 * GLA_DK), lambda s, h: (0, h)),
                pl.BlockSpec((1, 2 * GLA_DK), lambda s, h: (0, h)),
                pl.BlockSpec((1, GLA_DV), lambda s, h: (0, 0))]
    args = [heads, zs, wa_cat, ba_cat, norm_w]
    state_blk = (n_sub, 1, 2, 1, GLA_DK, GLA_DV)
    if h0 is not None:
        in_specs.append(pl.BlockSpec(state_blk, lambda s, h: (s, layer, 0, h, 0, 0)))
        args.append(h0)
    out_specs = [pl.BlockSpec((L, GLA_DV), lambda s, h: (s, h))]
    out_shape = [jax.ShapeDtypeStruct((n_seq * seq_len, GLA_V), BF16)]
    if want_state:
        out_specs.append(pl.BlockSpec(state_blk, lambda s, h: (s, 0, 0, h, 0, 0)))
        out_shape.append(jax.ShapeDtypeStruct((n_seq, 1, 2, GLA_HEADS, GLA_DK, GLA_DV), F32))
    body = functools.partial(_gla_body, seq_len=seq_len, n_sub=n_sub, has_h0=h0 is not None,
                             want_state=want_state)
    return pl.pallas_call(
        body,
        grid=(n_seq // n_sub, GLA_HEADS),
        in_specs=in_specs,
        out_specs=out_specs,
        out_shape=out_shape,
        scratch_shapes=[pltpu.VMEM((L, 2 * GLA_DK), F32),
                        pltpu.VMEM((L, 4 * GLA_DK), BF16),
                        pltpu.VMEM((L, 2 * GLA_DK), BF16),
                        pltpu.VMEM((L, 2 * GLA_DK), BF16),
                        pltpu.VMEM((L, GLA_DV), BF16),
                        pltpu.VMEM((L, CHUNK), BF16),
                        pltpu.VMEM((nc, GLA_DV, 2 * GLA_DK), F32),
                        pltpu.VMEM((nc, SUBLANES, 2 * GLA_DK), F32),
                        pltpu.VMEM((nc, GLA_DV, 2 * GLA_DK), BF16)],
        compiler_params=_params(2),
        name="gla",
    )(*args)


def _ssd_body(*refs, seq_len, grid_rows, has_h0, want_state, n_cast):
    it = iter(refs)
    (grp_ref, sm_ref, cwx_ref, cwb_ref, cwc_ref, cbx_ref, cbb_ref, cbc_ref,
     par_ref, dsk_ref) = (next(it) for _ in range(10))
    h0_ref = next(it) if has_h0 else None
    cast_src = [next(it) for _ in range(n_cast)]
    y_ref = next(it)
    st_ref = next(it) if want_state else None
    _cast_blocks(cast_src, [next(it) for _ in range(n_cast)])
    (pad_scr, xs_s, b_s, c_s, dt_scr, xy_scr, tr_scr, rows_scr, sc_scr, ed_scr, xw_scr, dec_scr,
     cs_scr, sprev_scr, state_scr) = (next(it) for _ in range(15))

    L = seq_len
    nc = L // CHUNK
    width = L // grid_rows
    pad = pad_scr.shape[0] - L
    pad //= 2
    grp = pl.program_id(1)

    def conv_into(c0, cw_ref, cb_ref, dst_ref):
        ch = dst_ref.shape[1]
        pad_scr[0:pad, 0:ch] = jnp.zeros((pad, ch), F32)
        pad_scr[pad + L:pad + L + pad, 0:ch] = jnp.zeros((pad, ch), F32)
        pad_scr[pad:pad + L, 0:ch] = grp_ref[0, :, c0:c0 + ch]
        rc = min(2 * CHUNK, L)
        col = lax.broadcasted_iota(jnp.int32, (rc, ch), 0) % width
        di_taps = range(CONV_K) if grid_rows > 1 else (CONV_K // 2,)
        for r0 in range(0, L, rc):
            acc = jnp.broadcast_to(cb_ref[...], (rc, ch))
            for dj in range(CONV_K):
                inner = None
                for di in di_taps:
                    off = pad + r0 + (di - 1) * width + (dj - 1)
                    term = cw_ref[di, dj:dj + 1, :] * pad_scr[off:off + rc, 0:ch]
                    inner = term if inner is None else inner + term
                if grid_rows > 1 and dj == 0:
                    inner = jnp.where(col >= 1, inner, 0.0)
                if grid_rows > 1 and dj == CONV_K - 1:
                    inner = jnp.where(col <= width - 2, inner, 0.0)
                acc = acc + inner
            dst_ref[r0:r0 + rc, :] = _silu(acc).astype(dst_ref.dtype)

    conv_into(0, cwx_ref, cbx_ref, xs_s)
    conv_into(GRP_B, cwb_ref, cbb_ref, b_s)
    conv_into(GRP_C, cwc_ref, cbc_ref, c_s)

    gw = GROUP_W
    unroll = min(SSD_UNROLL, nc)
    rows = lambda c: pl.ds(pl.multiple_of(c * CHUNK, CHUNK), CHUNK)
    bias_row = par_ref[0:1, :]
    a_row = -jnp.exp(par_ref[1:2, :]) * par_ref[2:3, :] * LOG2_E
    lower, upper = _tri_masks()
    lower_b = jnp.where(lower, 1.0, 0.0).astype(BF16)
    upper_b = jnp.where(upper, 1.0, 0.0).astype(BF16)
    lane = lax.broadcasted_iota(jnp.int32, (CHUNK, LANES), 1)
    dt_lanes = (lane >= SM_DTF) & (lane < SM_DTF + TERM_STRIDE)
    src = lax.broadcasted_iota(jnp.int32, (LANES, 2 * gw), 0)
    dst = lax.broadcasted_iota(jnp.int32, (LANES, 2 * gw), 1)
    dst_slot = (grp * HEADS_PER_GROUP
                + (lax.shift_right_logical(dst, HEAD_DIM_SHIFT) & (HEADS_PER_GROUP - 1))
                + jnp.where(dst >= gw, SSD_HEADS, 0))
    expand = jnp.where((src >= SM_DTF) & ((src & (TERM_STRIDE - 1)) == dst_slot),
                       1.0, 0.0).astype(BF16)
    t_idx = lax.broadcasted_iota(jnp.int32, (CHUNK, gw), 0)
    s_idx = lax.broadcasted_iota(jnp.int32, (CHUNK, gw), 1) & (CHUNK - 1)
    diag = t_idx == s_idx
    blk_r = lax.shift_right_logical(lax.broadcasted_iota(jnp.int32, (MXU_TILE, MXU_TILE), 0),
                                    HEAD_DIM_SHIFT)
    blk_c = lax.shift_right_logical(lax.broadcasted_iota(jnp.int32, (MXU_TILE, MXU_TILE), 1),
                                    HEAD_DIM_SHIFT)
    same_head = blk_r == blk_c

    dt_scr[...] = _softplus(sm_ref[...] + bias_row)

    def place3(v):
        hi = v.astype(BF16).astype(F32)
        rest = v - hi
        mid = rest.astype(BF16).astype(F32)
        lo = rest - mid
        keep = lambda t: jnp.where(dt_lanes, t, 0.0)
        out = (keep(hi) + pltpu.roll(keep(mid), TERM_STRIDE, axis=1)
               + pltpu.roll(keep(lo), 2 * TERM_STRIDE, axis=1))
        return out.astype(BF16)

    def cumulate(c, carry):
        sl = rows(c)
        dt = dt_scr[sl, :]
        da = dt * a_row
        hi = da.astype(BF16)
        both = jnp.concatenate([hi, (da - hi.astype(F32)).astype(BF16)], axis=1)
        pf = _dot(lower_b, both)
        pb = _dot(upper_b, both)
        cum = jnp.where(lane < SM_DTB, pf[:, 0:LANES] + pf[:, LANES:], pb[:, 0:LANES] + pb[:, LANES:])
        xy_scr[sl, 0:LANES] = place3(cum)
        xy_scr[sl, LANES:2 * LANES] = place3(dt)
        tr_scr[c, 0] = cum.T
        tr_scr[c, 1] = dt.T
        head0 = pl.multiple_of(SM_DTF + grp * HEADS_PER_GROUP, SUBLANES)
        for q, (k, d) in enumerate(((0, 0), (0, 1), (1, 0), (1, 1))):
            slab = tr_scr[c, k, pl.ds(head0 + d * SSD_HEADS, HEADS_PER_GROUP), :]
            row = jnp.concatenate(
                [jnp.broadcast_to(slab[h:h + 1, :], (SUBLANES, CHUNK))
                 for h in range(HEADS_PER_GROUP)], axis=1)
            rows_scr[c, :, q * gw:(q + 1) * gw] = row
        return carry

    lax.fori_loop(0, nc, cumulate, 0, unroll=unroll)

    def weights(c, carry):
        sl = rows(c)
        xc = xs_s[sl, :]
        cum_e = _dot(xy_scr[sl, 0:LANES], expand)
        dt_e = _dot(xy_scr[sl, LANES:2 * LANES], expand)
        cb = _dot_nt(c_s[sl, :], jnp.concatenate([b_s[sl, :]] * HEADS_PER_GROUP, axis=0))
        segs, dt_rows, dec_rows = [], [], []
        for d, (a, last_i) in enumerate(((0, CHUNK - 1), (gw, 0))):
            ce = cum_e[:, a:a + gw]
            de = dt_e[:, a:a + gw]
            segs.append(ce - rows_scr[c, 0:1, d * gw:(d + 1) * gw])
            dt_rows.append(rows_scr[c, 0:1, (2 + d) * gw:(3 + d) * gw])
            cum_last = ce[last_i:last_i + 1, :]
            ed_scr[sl, a:a + gw] = jnp.exp2(ce)
            xw_scr[sl, a:a + gw] = (xc * (jnp.exp2(cum_last - ce) * de)).astype(BF16)
            dec_rows.append(jnp.exp2(cum_last))
        fwd = t_idx >= s_idx
        w = (jnp.exp2(jnp.where(fwd, segs[0], segs[1])) * jnp.where(fwd, dt_rows[0], dt_rows[1])
             + jnp.where(diag, dt_rows[1], 0.0))
        sc_scr[sl, :] = (cb * w).astype(BF16)
        dec_scr[c] = jnp.broadcast_to(jnp.concatenate(dec_rows, axis=1), (SUBLANES, 2 * gw))
        return carry

    lax.fori_loop(0, nc, weights, 0, unroll=unroll)

    def products(c, carry):
        sl = rows(c)
        xc = xs_s[sl, :]
        xb = xc.astype(BF16)
        parts = []
        for j in range(gw // MXU_TILE):
            xh = xb[:, j * MXU_TILE:(j + 1) * MXU_TILE]
            rep = jnp.concatenate([xh] * (MXU_TILE // SSD_HEAD_DIM), axis=0)
            bd = jnp.where(same_head, rep, jnp.zeros_like(rep))
            parts.append(_dot(sc_scr[sl, j * MXU_TILE:(j + 1) * MXU_TILE], bd))
        y_ref[sl, :] = jnp.concatenate(parts, axis=1) + xc * dsk_ref[...]
        cs_scr[c] = _dot_tn(b_s[sl, :], xw_scr[sl, :])
        return carry

    lax.fori_loop(0, nc, products, 0, unroll=unroll)

    pair_w = 2 * SSD_HEAD_DIM
    if has_h0:
        for d in range(2):
            for j in range(HEADS_PER_GROUP // 2):
                pair = jnp.concatenate([h0_ref[0, 0, d, 2 * j], h0_ref[0, 0, d, 2 * j + 1]], axis=0)
                state_scr[:, d * gw + j * pair_w:d * gw + (j + 1) * pair_w] = pair.T
    else:
        state_scr[...] = jnp.zeros_like(state_scr)

    for d in range(2):
        for a in range(d * gw, (d + 1) * gw, MXU_TILE):

            def recur(i, state, d=d, a=a):
                c = i if d == 0 else nc - 1 - i
                sprev_scr[c, :, a:a + MXU_TILE] = state.astype(BF16)
                return (state * dec_scr[c, 0:1, a:a + MXU_TILE]
                        + cs_scr[c, :, a:a + MXU_TILE])

            state_scr[:, a:a + MXU_TILE] = lax.fori_loop(
                0, nc, recur, state_scr[:, a:a + MXU_TILE], unroll=min(nc, 4))

    def finish(c, carry):
        sl = rows(c)
        yi = _dot(c_s[sl, :], sprev_scr[c]) * ed_scr[sl, :]
        y_ref[sl, :] += yi[:, 0:gw] + yi[:, gw:2 * gw]
        return carry

    lax.fori_loop(0, nc, finish, 0, unroll=unroll)

    if want_state:
        for d in range(2):
            for j in range(HEADS_PER_GROUP // 2):
                pair = state_scr[:, d * gw + j * pair_w:d * gw + (j + 1) * pair_w].T
                st_ref[0, 0, d, 2 * j] = pair[0:SSD_HEAD_DIM, :]
                st_ref[0, 0, d, 2 * j + 1] = pair[SSD_HEAD_DIM:pair_w, :]


def _ssd_call(groups, zs, conv_w, conv_b, par, dsk, h0, layer, *, n_seq, seq_len, grid_rows,
              want_state, cast=()):
    L = seq_len
    in_specs = [pl.BlockSpec((1, L, GRP_W), lambda s, g: (g, s, 0)),
                pl.BlockSpec((L, LANES), lambda s, g: (s, ZS_SMALL // LANES)),
                pl.BlockSpec((CONV_K, CONV_K, GROUP_W), lambda s, g: (0, 0, g)),
                pl.BlockSpec((CONV_K, CONV_K, SSD_STATE),
                             lambda s, g: (0, 0, SSD_INNER // SSD_STATE + g)),
                pl.BlockSpec((CONV_K, CONV_K, SSD_STATE),
                             lambda s, g: (0, 0, (SSD_INNER + SSD_BC) // SSD_STATE + g)),
                pl.BlockSpec((1, GROUP_W), lambda s, g: (0, g)),
                pl.BlockSpec((1, SSD_STATE), lambda s, g: (0, SSD_INNER // SSD_STATE + g)),
                pl.BlockSpec((1, SSD_STATE),
                             lambda s, g: (0, (SSD_INNER + SSD_BC) // SSD_STATE + g)),
                pl.BlockSpec((SUBLANES, LANES), lambda s, g: (0, 0)),
                pl.BlockSpec((1, GROUP_W), lambda s, g: (0, g))]
    args = [groups, zs, conv_w, conv_w, conv_w, conv_b, conv_b, conv_b, par, dsk]
    state_blk = (1, 1, 2, HEADS_PER_GROUP, SSD_HEAD_DIM, SSD_STATE)
    if h0 is not None:
        in_specs.append(pl.BlockSpec(state_blk, lambda s, g: (s, layer, 0, g, 0, 0)))
        args.append(h0)
    out_specs = [pl.BlockSpec((L, GROUP_W), lambda s, g: (s, g))]
    out_shape = [jax.ShapeDtypeStruct((n_seq * L, SSD_INNER), F32)]
    if want_state:
        out_specs.append(pl.BlockSpec(state_blk, lambda s, g: (s, 0, 0, g, 0, 0)))
        out_shape.append(jax.ShapeDtypeStruct(
            (n_seq, 1, 2, SSD_HEADS, SSD_HEAD_DIM, SSD_STATE), F32))
    cast_in, cast_out, cast_shape = _cast_plumbing(cast, n_seq * SSD_GROUPS,
                                                   lambda s, g: s * SSD_GROUPS + g)
    in_specs += cast_in
    args += list(cast)
    out_specs += cast_out
    out_shape += cast_shape
    nc = L // CHUNK
    conv_pad = (L // grid_rows + SUBLANES) if grid_rows > 1 else SUBLANES
    body = functools.partial(_ssd_body, seq_len=L, grid_rows=grid_rows,
                             has_h0=h0 is not None, want_state=want_state, n_cast=len(cast))
    return pl.pallas_call(
        body,
        grid=(n_seq, SSD_GROUPS),
        in_specs=in_specs,
        out_specs=out_specs,
        out_shape=out_shape,
        scratch_shapes=[pltpu.VMEM((L + 2 * conv_pad, GROUP_W), F32),
                        pltpu.VMEM((L, GROUP_W), F32),
                        pltpu.VMEM((L, SSD_STATE), BF16),
                        pltpu.VMEM((L, SSD_STATE), BF16),
                        pltpu.VMEM((L, LANES), F32),
                        pltpu.VMEM((L, 2 * LANES), BF16),
                        pltpu.VMEM((nc, 2, LANES, CHUNK), F32),
                        pltpu.VMEM((nc, SUBLANES, 4 * GROUP_W), F32),
                        pltpu.VMEM((L, GROUP_W), BF16),
                        pltpu.VMEM((L, 2 * GROUP_W), F32),
                        pltpu.VMEM((L, 2 * GROUP_W), BF16),
                        pltpu.VMEM((nc, SUBLANES, 2 * GROUP_W), F32),
                        pltpu.VMEM((nc, SSD_STATE, 2 * GROUP_W), F32),
                        pltpu.VMEM((nc, SSD_STATE, 2 * GROUP_W), BF16),
                        pltpu.VMEM((SSD_STATE, 2 * GROUP_W), F32)],
        compiler_params=_params(2),
        name="ssd",
    )(*args)


def _layer_path(x, mod, lw, raw, h0_gla, h0_ssd, layer, *, n_seq, seq_len, grid_rows,
                want_state, final_w):
    if raw is not None:
        x1, lw["w_all"] = _ffn_call(x, mod, lw["norm_ffn1"], lw["ffn1_w_in"], lw["ffn1_w_out"],
                                    sub=0, seq_len=seq_len, cast=(raw["w_in"],))
    else:
        x1 = _ffn_call(x, mod, lw["norm_ffn1"], lw["ffn1_w_in"], lw["ffn1_w_out"],
                       sub=0, seq_len=seq_len)
    heads, groups, zs = _proj_call(x1, mod, lw["norm_mix"], lw["w_all"], seq_len=seq_len)
    gla_out = _gla_call(heads, zs, lw["wa_cat"], lw["ba_cat"], lw["gla_norm_w"], h0_gla, layer,
                        n_seq=n_seq, seq_len=seq_len, want_state=want_state)
    late = ("ffn2_w_in", "ffn2_w_out", "w_out")
    ssd_out = _ssd_call(groups, zs, lw["conv_w"], lw["conv_b"], lw["ssd_par"], lw["d_skip_row"],
                        h0_ssd, layer, n_seq=n_seq, seq_len=seq_len, grid_rows=grid_rows,
                        want_state=want_state,
                        cast=tuple(raw[k] for k in late) if raw is not None else ())
    if raw is not None:
        lw.update(zip(late, ssd_out[-len(late):]))
    o, y = gla_out[0], ssd_out[0]
    out = _ffn_call(x1, mod, lw["norm_ffn2"], lw["ffn2_w_in"], lw["ffn2_w_out"],
                    sub=2, seq_len=seq_len, mix=(o, y, zs, lw["ssd_norm_w"], lw["w_out"]),
                    final_w=final_w)
    if want_state:
        return out, gla_out[1], ssd_out[1]
    return out, None, None


def kernel(x_prompt, x_sample, state_gla, state_ssd, c, c_ctx, norm_ffn1, norm_mix, norm_ffn2, w_mod, b_mod, ffn1_w_in, ffn1_w_out, ffn2_w_in, ffn2_w_out, w_in, gla_w_a2, gla_b_a, gla_norm_w, conv_w, conv_b, dt_bias, a_log, d_skip, ssd_norm_w, w_out, final_norm):
    nb, seq, _ = x_prompt.shape
    db, dseq, _ = x_sample.shape
    grid_rows = dseq // GRID_W
    xp = x_prompt.reshape(nb * seq, D_MODEL)
    xs = x_sample.reshape(db * dseq, D_MODEL)
    row = lambda v: v.reshape(1, -1)
    gla_states, ssd_states = [], []
    for i in range(DEPTH):
        last = i == DEPTH - 1
        n_rows = -(-(db + 1) // SUBLANES) * SUBLANES
        cc = jnp.concatenate([c, c_ctx[None, :], jnp.zeros((n_rows - db - 1, D_MODEL), F32)], 0)
        mod = _mod_call(cc, w_mod[i], row(b_mod[i])).reshape(n_rows, N_MOD, D_MODEL)
        mod_lat, mod_ctx = mod[:db], mod[db:db + 1]

        wa_cat = jnp.stack(
            [jnp.pad(gla_w_a2[i, d].reshape(GLA_LOWRANK, GLA_HEADS, GLA_DK),
                     ((lo, LANES - lo - GLA_LOWRANK), (0, 0), (0, 0)))
             for d, lo in ((0, SM_AF), (1, SM_AB))], axis=2).reshape(LANES, 2 * GLA_QK)
        ba_cat = jnp.stack([gla_b_a[i, 0].reshape(GLA_HEADS, GLA_DK),
                            gla_b_a[i, 1].reshape(GLA_HEADS, GLA_DK)], axis=1).reshape(1, 2 * GLA_QK)
        ssd_par = jnp.pad(
            jnp.stack([dt_bias[i].reshape(-1), a_log[i].reshape(-1), jnp.ones((2 * SSD_HEADS,), F32)]),
            ((0, SUBLANES - 3), (SM_DTF, LANES - SM_DTF - 2 * SSD_HEADS)))
        lw = {
            "norm_ffn1": row(norm_ffn1[i]), "norm_mix": row(norm_mix[i]),
            "norm_ffn2": row(norm_ffn2[i]),
            "ffn1_w_in": ffn1_w_in[i].astype(BF16), "ffn1_w_out": ffn1_w_out[i].astype(BF16),
            "wa_cat": wa_cat, "ba_cat": ba_cat,
            "gla_norm_w": row(gla_norm_w[i]), "conv_w": conv_w[i], "conv_b": row(conv_b[i]),
            "ssd_par": ssd_par, "d_skip_row": row(jnp.repeat(d_skip[i], SSD_HEAD_DIM)),
            "ssd_norm_w": row(ssd_norm_w[i]),
        }
        raw = {"w_in": jnp.swapaxes(w_in[i], 0, 1), "ffn2_w_in": ffn2_w_in[i], "ffn2_w_out": ffn2_w_out[i],
               "w_out": w_out[i]}
        fw = row(final_norm) if last else None

        xp, sg, ss = _layer_path(xp, mod_ctx, lw, raw, None, None, i, n_seq=nb, seq_len=seq,
                                 grid_rows=1, want_state=True, final_w=fw)
        xs, _, _ = _layer_path(xs, mod_lat, lw, None, state_gla, state_ssd, i, n_seq=db,
                               seq_len=dseq, grid_rows=grid_rows, want_state=False, final_w=fw)
        gla_states.append(sg)
        ssd_states.append(ss)
    y_prompt = xp.reshape(nb, seq, D_MODEL)
    y_sample = xs.reshape(db, dseq, D_MODEL)
    return (y_prompt, y_sample, jnp.concatenate(gla_states, axis=1),
            jnp.concatenate(ssd_states, axis=1))
```

```python
import functools

import jax
import jax.numpy as jnp
from jax import lax
from jax.experimental import pallas as pl
from jax.experimental.pallas import tpu as pltpu

F32 = jnp.float32
BF16 = jnp.bfloat16

D_MODEL = 1024
DEPTH = 1
GRID_W = 64
CHUNK = 64
EPS = 1e-6
N_MOD = 9
D_FF = 2816
GLA_HEADS = 4
GLA_DK = 128
GLA_DV = 256
GLA_LOWRANK = 16
GLA_TAU = 16.0
GLA_QK = GLA_HEADS * GLA_DK
GLA_V = GLA_HEADS * GLA_DV
SSD_HEADS = 16
SSD_HEAD_DIM = 64
SSD_GROUPS = 2
SSD_STATE = 128
SSD_INNER = SSD_HEADS * SSD_HEAD_DIM
SSD_BC = SSD_GROUPS * SSD_STATE
SSD_CONV_DIM = SSD_INNER + 2 * SSD_BC
CONV_K = 3
D_MIX = GLA_V + SSD_INNER

LOG2_E = 1.4426950408889634
LANES = 128
SUBLANES = 8
BF16_SUBLANES = 16
VMEM_LIMIT_BYTES = 56 * 1024 * 1024

COL_Q = 0
COL_K = COL_Q + GLA_QK
COL_V = COL_K + GLA_QK
COL_R = COL_V + GLA_V
COL_Z = COL_R + GLA_V
HEAD_K = GLA_DK
HEAD_V = 2 * GLA_DK
HEAD_R = HEAD_V + GLA_DV
HEAD_W = HEAD_R + GLA_DV
GRP_B = SSD_INNER // SSD_GROUPS
GRP_C = GRP_B + SSD_STATE
GRP_W = GRP_C + SSD_STATE
ZS_SMALL = SSD_INNER
ZS_W = ZS_SMALL + LANES
SM_AF = 0
SM_AB = SM_AF + GLA_LOWRANK
SM_DTF = SM_AB + GLA_LOWRANK
SM_DTB = SM_DTF + SSD_HEADS
HEADS_PER_GROUP = SSD_HEADS // SSD_GROUPS
GROUP_W = GRP_B

TOKEN_TILE = 512
MOD_TILE = 3 * D_MODEL
GLA_UNROLL = 32
SSD_UNROLL = 16
MXU_TILE = 256
TERM_STRIDE = 2 * SSD_HEADS
HEAD_DIM_SHIFT = SSD_HEAD_DIM.bit_length() - 1
assert 1 << HEAD_DIM_SHIFT == SSD_HEAD_DIM == CHUNK
SCAN_BLOCK_TOKENS = 2048
SCAN_BLOCK_SEQS = 4


def _dot(a, b):
    return jnp.dot(a, b, preferred_element_type=F32)


def _dot_nt(a, b):
    return lax.dot_general(a, b, (((1,), (1,)), ((), ())), preferred_element_type=F32)


def _dot_tn(a, b):
    return lax.dot_general(a, b, (((0,), (0,)), ((), ())), preferred_element_type=F32)


def _silu(x):
    return x * jax.nn.sigmoid(x)


def _log1p_exp_neg_abs(x):
    return jnp.log(1.0 + jnp.exp(-jnp.abs(x)))


def _softplus(x):
    return jnp.maximum(x, 0.0) + _log1p_exp_neg_abs(x)


def _log_sigmoid(x):
    return jnp.minimum(x, 0.0) - _log1p_exp_neg_abs(x)


def _rmsnorm(x, w):
    ms = jnp.mean(x * x, axis=-1, keepdims=True)
    return x * lax.rsqrt(ms + EPS) * w


def _resident(shape):
    nd = len(shape)
    return pl.BlockSpec(shape, lambda *_: (0,) * nd, pipeline_mode=pl.Buffered(1))


def _params(n_axes):
    return pltpu.CompilerParams(dimension_semantics=("arbitrary",) * n_axes,
                                vmem_limit_bytes=VMEM_LIMIT_BYTES)


def _cast_plumbing(cast, n_steps, step_of):
    in_specs, out_specs, out_shape = [], [], []
    for w in cast:
        rows, cols = w.shape
        per_step = -(-rows // n_steps)
        per_step = -(-per_step // BF16_SUBLANES) * BF16_SUBLANES
        last = -(-rows // per_step) - 1
        blk = pl.BlockSpec((per_step, cols),
                           lambda *g, last=last: (jnp.minimum(step_of(*g), last), 0))
        in_specs.append(blk)
        out_specs.append(blk)
        out_shape.append(jax.ShapeDtypeStruct((rows, cols), BF16))
    return in_specs, out_specs, out_shape


def _cast_blocks(src_refs, dst_refs):
    for src_ref, dst_ref in zip(src_refs, dst_refs):
        dst_ref[...] = src_ref[...].astype(dst_ref.dtype)


def _mod_body(c_ref, w_ref, b_ref, out_ref):
    a = _silu(c_ref[...]).astype(BF16)
    out_ref[...] = _dot(a, w_ref[...].astype(BF16)) + b_ref[...]


def _mod_call(cc, w_mod, b_mod):
    n_rows = cc.shape[0]
    tn = MOD_TILE
    return pl.pallas_call(
        _mod_body,
        grid=(N_MOD * D_MODEL // tn,),
        in_specs=[pl.BlockSpec((n_rows, D_MODEL), lambda j: (0, 0)),
                  pl.BlockSpec((D_MODEL, tn), lambda j: (0, j)),
                  pl.BlockSpec((1, tn), lambda j: (0, j))],
        out_specs=pl.BlockSpec((n_rows, tn), lambda j: (0, j)),
        out_shape=jax.ShapeDtypeStruct((n_rows, N_MOD * D_MODEL), F32),
        compiler_params=_params(1),
        name="mod",
    )(cc, w_mod, b_mod)


def _ffn_body(*refs, sub, has_mix, has_final, n_cast):
    it = iter(refs)
    x_ref, mod_ref, nw_ref, win_ref, wout_ref = (next(it) for _ in range(5))
    if has_mix:
        o_ref, y_ref, z_ref, snw_ref, wo_ref = (next(it) for _ in range(5))
    if has_final:
        fn_ref = next(it)
    cast_src = [next(it) for _ in range(n_cast)]
    out_ref = next(it)
    _cast_blocks(cast_src, [next(it) for _ in range(n_cast)])

    x = x_ref[...]
    if has_mix:
        g2 = mod_ref[0, 5:6, :]
        yn = _rmsnorm(y_ref[...] * _silu(z_ref[...]), snw_ref[...])
        m = _dot(jnp.concatenate([o_ref[...], yn.astype(BF16)], axis=1), wo_ref[...])
        x = x + g2 * m
    sh = mod_ref[0, 3 * sub:3 * sub + 1, :]
    sc = mod_ref[0, 3 * sub + 1:3 * sub + 2, :]
    gate = mod_ref[0, 3 * sub + 2:3 * sub + 3, :]
    h = (_rmsnorm(x, nw_ref[...]) * (1.0 + sc) + sh).astype(BF16)
    g = _dot(h, win_ref[:, :D_FF])
    u = _dot(h, win_ref[:, D_FF:])
    act = (_silu(g) * u).astype(BF16)
    x = x + (0.5 * gate) * _dot(act, wout_ref[...])
    if has_final:
        x = _rmsnorm(x, fn_ref[...])
    out_ref[...] = x


def _ffn_call(x, mod, norm_w, w_in, w_out, *, sub, seq_len, mix=None, final_w=None, cast=()):
    m_tok = x.shape[0]
    tm = TOKEN_TILE
    tiles_per_seq = seq_len // tm
    shared_mod = mod.shape[0] == 1
    mod_map = (lambda i: (0, 0, 0)) if shared_mod else (lambda i: (i // tiles_per_seq, 0, 0))
    tok = lambda width: pl.BlockSpec((tm, width), lambda i: (i, 0))
    in_specs = [tok(D_MODEL), pl.BlockSpec((1, N_MOD, D_MODEL), mod_map),
                _resident((1, D_MODEL)), _resident(w_in.shape), _resident(w_out.shape)]
    args = [x, mod, norm_w, w_in, w_out]
    if mix is not None:
        o, y, proj, ssd_norm_w, w_mix_out = mix
        in_specs += [tok(GLA_V), tok(SSD_INNER),
                     pl.BlockSpec((tm, SSD_INNER), lambda i: (i, 0)),
                     _resident((1, SSD_INNER)), _resident(w_mix_out.shape)]
        args += [o, y, proj, ssd_norm_w, w_mix_out]
    if final_w is not None:
        in_specs.append(_resident((1, D_MODEL)))
        args.append(final_w)
    n_steps = m_tok // tm
    cast_in, cast_out, cast_shape = _cast_plumbing(cast, n_steps, lambda i: i)
    in_specs += cast_in
    args += list(cast)
    out_specs = [tok(D_MODEL)] + cast_out
    out_shape = [jax.ShapeDtypeStruct((m_tok, D_MODEL), F32)] + cast_shape
    body = functools.partial(_ffn_body, sub=sub, has_mix=mix is not None,
                             has_final=final_w is not None, n_cast=len(cast))
    outs = pl.pallas_call(
        body,
        grid=(n_steps,),
        in_specs=in_specs,
        out_specs=out_specs,
        out_shape=out_shape,
        compiler_params=_params(1),
        name="ffn_mix" if mix is not None else "ffn",
    )(*args)
    return outs if cast else outs[0]


def _proj_body(x_ref, mod_ref, nw_ref, w_ref, heads_ref, groups_ref, zs_ref):
    sh = mod_ref[0, 3:4, :]
    sc = mod_ref[0, 4:5, :]
    h = (_rmsnorm(x_ref[...], nw_ref[...]) * (1.0 + sc) + sh).astype(BF16)
    o_z = COL_Z + 2 * GLA_LOWRANK
    o_dt = o_z + SSD_INNER + SSD_CONV_DIM
    qkvr = _dot_nt(h, w_ref[0:COL_Z, :])
    for hd in range(GLA_HEADS):
        for dst, src, wid in ((0, COL_Q + hd * GLA_DK, GLA_DK), (HEAD_K, COL_K + hd * GLA_DK, GLA_DK),
                              (HEAD_V, COL_V + hd * GLA_DV, GLA_DV), (HEAD_R, COL_R + hd * GLA_DV, GLA_DV)):
            heads_ref[hd, :, dst:dst + wid] = qkvr[:, src:src + wid]
    zx = _dot_nt(h, w_ref[o_z:o_dt, :])
    zs_ref[:, 0:ZS_SMALL] = zx[:, 0:SSD_INNER]
    for g in range(SSD_GROUPS):
        for dst, src, wid in ((0, g * GRP_B, GRP_B), (GRP_B, SSD_INNER + g * SSD_STATE, SSD_STATE),
                              (GRP_C, SSD_INNER + SSD_BC + g * SSD_STATE, SSD_STATE)):
            groups_ref[g, :, dst:dst + wid] = zx[:, SSD_INNER + src:SSD_INNER + src + wid]
    w_small = jnp.concatenate(
        [w_ref[COL_Z:o_z, :], w_ref[o_dt:, :],
         jnp.zeros((LANES - 2 * GLA_LOWRANK - 2 * SSD_HEADS, D_MODEL), BF16)], axis=0)
    zs_ref[:, ZS_SMALL:ZS_W] = _dot_nt(h, w_small)


def _proj_call(x, mod, norm_w, w_all, *, seq_len):
    m_tok = x.shape[0]
    tm = TOKEN_TILE
    tiles_per_seq = seq_len // tm
    shared_mod = mod.shape[0] == 1
    mod_map = (lambda i: (0, 0, 0)) if shared_mod else (lambda i: (i // tiles_per_seq, 0, 0))
    return pl.pallas_call(
        _proj_body,
        grid=(m_tok // tm,),
        in_specs=[pl.BlockSpec((tm, D_MODEL), lambda i: (i, 0)),
                  pl.BlockSpec((1, N_MOD, D_MODEL), mod_map),
                  _resident((1, D_MODEL)), _resident(w_all.shape)],
        out_specs=[pl.BlockSpec((GLA_HEADS, tm, HEAD_W), lambda i: (0, i, 0)),
                   pl.BlockSpec((SSD_GROUPS, tm, GRP_W), lambda i: (0, i, 0)),
                   pl.BlockSpec((tm, ZS_W), lambda i: (i, 0))],
        out_shape=[jax.ShapeDtypeStruct((GLA_HEADS, m_tok, HEAD_W), F32),
                   jax.ShapeDtypeStruct((SSD_GROUPS, m_tok, GRP_W), F32),
                   jax.ShapeDtypeStruct((m_tok, ZS_W), F32)],
        compiler_params=_params(1),
        name="proj",
    )(x, mod, norm_w, w_all)


def _tri_masks():
    row = lax.broadcasted_iota(jnp.int32, (CHUNK, CHUNK), 0)
    col = lax.broadcasted_iota(jnp.int32, (CHUNK, CHUNK), 1)
    return row >= col, row <= col


def _gla_body(*refs, seq_len, n_sub, has_h0, want_state):
    it = iter(refs)
    hd_ref, sm_ref, wa_ref, ba_ref, nw_ref = (next(it) for _ in range(5))
    h0_ref = next(it) if has_h0 else None
    o_ref = next(it)
    st_ref = next(it) if want_state else None
    (la_scr, qk_scr, qg_scr, ks_scr, vb_scr, sc_scr, kv_scr, dec_scr,
     sprev_scr) = (next(it) for _ in range(9))

    nc = seq_len // CHUNK
    nct = n_sub * nc
    dk = GLA_DK
    lower, upper = _tri_masks()
    lower_b = jnp.where(lower, 1.0, 0.0).astype(BF16)
    upper_b = jnp.where(upper, 1.0, 0.0).astype(BF16)
    unroll = min(GLA_UNROLL, nct)
    rows = lambda c: pl.ds(pl.multiple_of(c * CHUNK, CHUNK), CHUNK)

    pre = _dot(sm_ref[...].astype(BF16), wa_ref[...].astype(BF16)) + ba_ref[...]
    la_scr[...] = _log_sigmoid(pre) * (LOG2_E / GLA_TAU)

    def prep(c, carry):
        sl = rows(c)
        qc = hd_ref[0, sl, 0:HEAD_K] * (GLA_DK ** -0.5)
        kc = hd_ref[0, sl, HEAD_K:HEAD_V]
        vb_scr[sl, :] = hd_ref[0, sl, HEAD_V:HEAD_R].astype(BF16)
        la = la_scr[sl, :]
        hi = la.astype(BF16)
        lo = (la - hi.astype(F32)).astype(BF16)

        def cumulative(tri_b, a):
            p = _dot(tri_b, jnp.concatenate([hi[:, a:a + dk], lo[:, a:a + dk]], axis=1))
            return p[:, 0:dk] + p[:, dk:2 * dk]

        g_f = cumulative(lower_b, 0)
        g_b = cumulative(upper_b, dk)
        mid_f = g_f[CHUNK // 2:CHUNK // 2 + 1, :]
        mid_b = g_b[CHUNK - 1 - CHUNK // 2:CHUNK - CHUNK // 2, :]
        end_f = g_f[CHUNK - 1:CHUNK, :]
        end_b = g_b[0:1, :]
        qk_scr[sl, 0:dk] = (qc * jnp.exp2(g_f - mid_f)).astype(BF16)
        qk_scr[sl, dk:2 * dk] = (qc * jnp.exp2(g_b - mid_b)).astype(BF16)
        qk_scr[sl, 2 * dk:3 * dk] = (kc * jnp.exp2(mid_f - g_f)).astype(BF16)
        qk_scr[sl, 3 * dk:4 * dk] = (kc * jnp.exp2(mid_b - g_b)).astype(BF16)
        qg_scr[sl, 0:dk] = (qc * jnp.exp2(g_f)).astype(BF16)
        qg_scr[sl, dk:2 * dk] = (qc * jnp.exp2(g_b)).astype(BF16)
        ks_scr[sl, 0:dk] = (kc * jnp.exp2(end_f - g_f)).astype(BF16)
        ks_scr[sl, dk:2 * dk] = (kc * jnp.exp2(end_b - g_b)).astype(BF16)
        dec = jnp.exp2(jnp.concatenate([end_f, end_b], axis=1))
        dec_scr[c] = jnp.broadcast_to(dec, (SUBLANES, 2 * dk))
        return carry

    lax.fori_loop(0, nct, prep, 0, unroll=unroll)

    def products(c, carry):
        sl = rows(c)
        s_f = _dot_nt(qk_scr[sl, 0:dk], qk_scr[sl, 2 * dk:3 * dk])
        s_b = _dot_nt(qk_scr[sl, dk:2 * dk], qk_scr[sl, 3 * dk:4 * dk])
        sc_scr[sl, :] = (jnp.where(lower, s_f, 0.0) + jnp.where(upper, s_b, 0.0)).astype(BF16)
        kv_scr[c] = _dot_tn(vb_scr[sl, :], ks_scr[sl, :])
        return carry

    lax.fori_loop(0, nct, products, 0, unroll=unroll)

    for b in range(n_sub):
        for d, a in ((0, 0), (1, dk)):
            state0 = h0_ref[b, 0, d, 0].T if has_h0 else jnp.zeros((GLA_DV, dk), F32)

            def recur(i, state, b=b, d=d, a=a):
                c = b * nc + (i if d == 0 else nc - 1 - i)
                sprev_scr[c, :, a:a + dk] = state.astype(BF16)
                return state * dec_scr[c, 0:1, a:a + dk] + kv_scr[c, :, a:a + dk]

            state = lax.fori_loop(0, nc, recur, state0, unroll=min(nc, 4))
            if want_state:
                st_ref[b, 0, d, 0] = state.T

    def finish(c, carry):
        sl = rows(c)
        o = _dot(sc_scr[sl, :], vb_scr[sl, :]) + _dot_nt(qg_scr[sl, :], sprev_scr[c])
        o_ref[sl, :] = (_rmsnorm(o, nw_ref[...]) * _silu(hd_ref[0, sl, HEAD_R:HEAD_W])).astype(o_ref.dtype)
        return carry

    lax.fori_loop(0, nct, finish, 0, unroll=unroll)


def _gla_call(heads, zs, wa_cat, ba_cat, norm_w, h0, layer, *, n_seq, seq_len, want_state):
    n_sub = max(1, min(SCAN_BLOCK_TOKENS // seq_len, SCAN_BLOCK_SEQS))
    L = n_sub * seq_len
    nc = L // CHUNK
    in_specs = [pl.BlockSpec((1, L, HEAD_W), lambda s, h: (h, s, 0)),
                pl.BlockSpec((L, LANES), lambda s, h: (s, ZS_SMALL // LANES)),
                pl.BlockSpec((LANES, 2 * GLA_DK), lambda s, h: (0, h)),
                pl.BlockSpec((1, 2 * GLA_DK), lambda s, h: (0, h)),
                pl.BlockSpec((1, GLA_DV), lambda s, h: (0, 0))]
    args = [heads, zs, wa_cat, ba_cat, norm_w]
    state_blk = (n_sub, 1, 2, 1, GLA_DK, GLA_DV)
    if h0 is not None:
        in_specs.append(pl.BlockSpec(state_blk, lambda s, h: (s, layer, 0, h, 0, 0)))
        args.append(h0)
    out_specs = [pl.BlockSpec((L, GLA_DV), lambda s, h: (s, h))]
    out_shape = [jax.ShapeDtypeStruct((n_seq * seq_len, GLA_V), BF16)]
    if want_state:
        out_specs.append(pl.BlockSpec(state_blk, lambda s, h: (s, 0, 0, h, 0, 0)))
        out_shape.append(jax.ShapeDtypeStruct((n_seq, 1, 2, GLA_HEADS, GLA_DK, GLA_DV), F32))
    body = functools.partial(_gla_body, seq_len=seq_len, n_sub=n_sub, has_h0=h0 is not None,
                             want_state=want_state)
    return pl.pallas_call(
        body,
        grid=(n_seq // n_sub, GLA_HEADS),
        in_specs=in_specs,
        out_specs=out_specs,
        out_shape=out_shape,
        scratch_shapes=[pltpu.VMEM((L, 2 * GLA_DK), F32),
                        pltpu.VMEM((L, 4 * GLA_DK), BF16),
                        pltpu.VMEM((L, 2 * GLA_DK), BF16),
                        pltpu.VMEM((L, 2 * GLA_DK), BF16),
                        pltpu.VMEM((L, GLA_DV), BF16),
                        pltpu.VMEM((L, CHUNK), BF16),
                        pltpu.VMEM((nc, GLA_DV, 2 * GLA_DK), F32),
                        pltpu.VMEM((nc, SUBLANES, 2 * GLA_DK), F32),
                        pltpu.VMEM((nc, GLA_DV, 2 * GLA_DK), BF16)],
        compiler_params=_params(2),
        name="gla",
    )(*args)


def _ssd_body(*refs, seq_len, grid_rows, has_h0, want_state, n_cast):
    it = iter(refs)
    (grp_ref, sm_ref, cwx_ref, cwb_ref, cwc_ref, cbx_ref, cbb_ref, cbc_ref,
     par_ref, dsk_ref) = (next(it) for _ in range(10))
    h0_ref = next(it) if has_h0 else None
    cast_src = [next(it) for _ in range(n_cast)]
    y_ref = next(it)
    st_ref = next(it) if want_state else None
    _cast_blocks(cast_src, [next(it) for _ in range(n_cast)])
    (pad_scr, xs_s, b_s, c_s, dt_scr, xy_scr, tr_scr, rows_scr, sc_scr, ed_scr, xw_scr, dec_scr,
     cs_scr, sprev_scr, state_scr) = (next(it) for _ in range(15))

    L = seq_len
    nc = L // CHUNK
    width = L // grid_rows
    pad = pad_scr.shape[0] - L
    pad //= 2
    grp = pl.program_id(1)

    def conv_into(c0, cw_ref, cb_ref, dst_ref):
        ch = dst_ref.shape[1]
        pad_scr[0:pad, 0:ch] = jnp.zeros((pad, ch), F32)
        pad_scr[pad + L:pad + L + pad, 0:ch] = jnp.zeros((pad, ch), F32)
        pad_scr[pad:pad + L, 0:ch] = grp_ref[0, :, c0:c0 + ch]
        rc = min(2 * CHUNK, L)
        col = lax.broadcasted_iota(jnp.int32, (rc, ch), 0) % width
        di_taps = range(CONV_K) if grid_rows > 1 else (CONV_K // 2,)
        for r0 in range(0, L, rc):
            acc = jnp.broadcast_to(cb_ref[...], (rc, ch))
            for dj in range(CONV_K):
                inner = None
                for di in di_taps:
                    off = pad + r0 + (di - 1) * width + (dj - 1)
                    term = cw_ref[di, dj:dj + 1, :] * pad_scr[off:off + rc, 0:ch]
                    inner = term if inner is None else inner + term
                if grid_rows > 1 and dj == 0:
                    inner = jnp.where(col >= 1, inner, 0.0)
                if grid_rows > 1 and dj == CONV_K - 1:
                    inner = jnp.where(col <= width - 2, inner, 0.0)
                acc = acc + inner
            dst_ref[r0:r0 + rc, :] = _silu(acc).astype(dst_ref.dtype)

    conv_into(0, cwx_ref, cbx_ref, xs_s)
    conv_into(GRP_B, cwb_ref, cbb_ref, b_s)
    conv_into(GRP_C, cwc_ref, cbc_ref, c_s)

    gw = GROUP_W
    unroll = min(SSD_UNROLL, nc)
    rows = lambda c: pl.ds(pl.multiple_of(c * CHUNK, CHUNK), CHUNK)
    bias_row = par_ref[0:1, :]
    a_row = -jnp.exp(par_ref[1:2, :]) * par_ref[2:3, :] * LOG2_E
    lower, upper = _tri_masks()
    lower_b = jnp.where(lower, 1.0, 0.0).astype(BF16)
    upper_b = jnp.where(upper, 1.0, 0.0).astype(BF16)
    lane = lax.broadcasted_iota(jnp.int32, (CHUNK, LANES), 1)
    dt_lanes = (lane >= SM_DTF) & (lane < SM_DTF + TERM_STRIDE)
    src = lax.broadcasted_iota(jnp.int32, (LANES, 2 * gw), 0)
    dst = lax.broadcasted_iota(jnp.int32, (LANES, 2 * gw), 1)
    dst_slot = (grp * HEADS_PER_GROUP
                + (lax.shift_right_logical(dst, HEAD_DIM_SHIFT) & (HEADS_PER_GROUP - 1))
                + jnp.where(dst >= gw, SSD_HEADS, 0))
    expand = jnp.where((src >= SM_DTF) & ((src & (TERM_STRIDE - 1)) == dst_slot),
                       1.0, 0.0).astype(BF16)
    t_idx = lax.broadcasted_iota(jnp.int32, (CHUNK, gw), 0)
    s_idx = lax.broadcasted_iota(jnp.int32, (CHUNK, gw), 1) & (CHUNK - 1)
    diag = t_idx == s_idx
    blk_r = lax.shift_right_logical(lax.broadcasted_iota(jnp.int32, (MXU_TILE, MXU_TILE), 0),
                                    HEAD_DIM_SHIFT)
    blk_c = lax.shift_right_logical(lax.broadcasted_iota(jnp.int32, (MXU_TILE, MXU_TILE), 1),
                                    HEAD_DIM_SHIFT)
    same_head = blk_r == blk_c

    dt_scr[...] = _softplus(sm_ref[...] + bias_row)

    def place3(v):
        hi = v.astype(BF16).astype(F32)
        rest = v - hi
        mid = rest.astype(BF16).astype(F32)
        lo = rest - mid
        keep = lambda t: jnp.where(dt_lanes, t, 0.0)
        out = (keep(hi) + pltpu.roll(keep(mid), TERM_STRIDE, axis=1)
               + pltpu.roll(keep(lo), 2 * TERM_STRIDE, axis=1))
        return out.astype(BF16)

    def cumulate(c, carry):
        sl = rows(c)
        dt = dt_scr[sl, :]
        da = dt * a_row
        hi = da.astype(BF16)
        both = jnp.concatenate([hi, (da - hi.astype(F32)).astype(BF16)], axis=1)
        pf = _dot(lower_b, both)
        pb = _dot(upper_b, both)
        cum = jnp.where(lane < SM_DTB, pf[:, 0:LANES] + pf[:, LANES:], pb[:, 0:LANES] + pb[:, LANES:])
        xy_scr[sl, 0:LANES] = place3(cum)
        xy_scr[sl, LANES:2 * LANES] = place3(dt)
        tr_scr[c, 0] = cum.T
        tr_scr[c, 1] = dt.T
        head0 = pl.multiple_of(SM_DTF + grp * HEADS_PER_GROUP, SUBLANES)
        for q, (k, d) in enumerate(((0, 0), (0, 1), (1, 0), (1, 1))):
            slab = tr_scr[c, k, pl.ds(head0 + d * SSD_HEADS, HEADS_PER_GROUP), :]
            row = jnp.concatenate(
                [jnp.broadcast_to(slab[h:h + 1, :], (SUBLANES, CHUNK))
                 for h in range(HEADS_PER_GROUP)], axis=1)
            rows_scr[c, :, q * gw:(q + 1) * gw] = row
        return carry

    lax.fori_loop(0, nc, cumulate, 0, unroll=unroll)

    def weights(c, carry):
        sl = rows(c)
        xc = xs_s[sl, :]
        cum_e = _dot(xy_scr[sl, 0:LANES], expand)
        dt_e = _dot(xy_scr[sl, LANES:2 * LANES], expand)
        cb = _dot_nt(c_s[sl, :], jnp.concatenate([b_s[sl, :]] * HEADS_PER_GROUP, axis=0))
        segs, dt_rows, dec_rows = [], [], []
        for d, (a, last_i) in enumerate(((0, CHUNK - 1), (gw, 0))):
            ce = cum_e[:, a:a + gw]
            de = dt_e[:, a:a + gw]
            segs.append(ce - rows_scr[c, 0:1, d * gw:(d + 1) * gw])
            dt_rows.append(rows_scr[c, 0:1, (2 + d) * gw:(3 + d) * gw])
            cum_last = ce[last_i:last_i + 1, :]
            ed_scr[sl, a:a + gw] = jnp.exp2(ce)
            xw_scr[sl, a:a + gw] = (xc * (jnp.exp2(cum_last - ce) * de)).astype(BF16)
            dec_rows.append(jnp.exp2(cum_last))
        fwd = t_idx >= s_idx
        w = (jnp.exp2(jnp.where(fwd, segs[0], segs[1])) * jnp.where(fwd, dt_rows[0], dt_rows[1])
             + jnp.where(diag, dt_rows[1], 0.0))
        sc_scr[sl, :] = (cb * w).astype(BF16)
        dec_scr[c] = jnp.broadcast_to(jnp.concatenate(dec_rows, axis=1), (SUBLANES, 2 * gw))
        return carry

    lax.fori_loop(0, nc, weights, 0, unroll=unroll)

    def products(c, carry):
        sl = rows(c)
        xc = xs_s[sl, :]
        xb = xc.astype(BF16)
        parts = []
        for j in range(gw // MXU_TILE):
            xh = xb[:, j * MXU_TILE:(j + 1) * MXU_TILE]
            rep = jnp.concatenate([xh] * (MXU_TILE // SSD_HEAD_DIM), axis=0)
            bd = jnp.where(same_head, rep, jnp.zeros_like(rep))
            parts.append(_dot(sc_scr[sl, j * MXU_TILE:(j + 1) * MXU_TILE], bd))
        y_ref[sl, :] = jnp.concatenate(parts, axis=1) + xc * dsk_ref[...]
        cs_scr[c] = _dot_tn(b_s[sl, :], xw_scr[sl, :])
        return carry

    lax.fori_loop(0, nc, products, 0, unroll=unroll)

    pair_w = 2 * SSD_HEAD_DIM
    if has_h0:
        for d in range(2):
            for j in range(HEADS_PER_GROUP // 2):
                pair = jnp.concatenate([h0_ref[0, 0, d, 2 * j], h0_ref[0, 0, d, 2 * j + 1]], axis=0)
                state_scr[:, d * gw + j * pair_w:d * gw + (j + 1) * pair_w] = pair.T
    else:
        state_scr[...] = jnp.zeros_like(state_scr)

    for d in range(2):
        for a in range(d * gw, (d + 1) * gw, MXU_TILE):

            def recur(i, state, d=d, a=a):
                c = i if d == 0 else nc - 1 - i
                sprev_scr[c, :, a:a + MXU_TILE] = state.astype(BF16)
                return (state * dec_scr[c, 0:1, a:a + MXU_TILE]
                        + cs_scr[c, :, a:a + MXU_TILE])

            state_scr[:, a:a + MXU_TILE] = lax.fori_loop(
                0, nc, recur, state_scr[:, a:a + MXU_TILE], unroll=min(nc, 4))

    def finish(c, carry):
        sl = rows(c)
        yi = _dot(c_s[sl, :], sprev_scr[c]) * ed_scr[sl, :]
        y_ref[sl, :] += yi[:, 0:gw] + yi[:, gw:2 * gw]
        return carry

    lax.fori_loop(0, nc, finish, 0, unroll=unroll)

    if want_state:
        for d in range(2):
            for j in range(HEADS_PER_GROUP // 2):
                pair = state_scr[:, d * gw + j * pair_w:d * gw + (j + 1) * pair_w].T
                st_ref[0, 0, d, 2 * j] = pair[0:SSD_HEAD_DIM, :]
                st_ref[0, 0, d, 2 * j + 1] = pair[SSD_HEAD_DIM:pair_w, :]


def _ssd_call(groups, zs, conv_w, conv_b, par, dsk, h0, layer, *, n_seq, seq_len, grid_rows,
              want_state, cast=()):
    L = seq_len
    in_specs = [pl.BlockSpec((1, L, GRP_W), lambda s, g: (g, s, 0)),
                pl.BlockSpec((L, LANES), lambda s, g: (s, ZS_SMALL // LANES)),
                pl.BlockSpec((CONV_K, CONV_K, GROUP_W), lambda s, g: (0, 0, g)),
                pl.BlockSpec((CONV_K, CONV_K, SSD_STATE),
                             lambda s, g: (0, 0, SSD_INNER // SSD_STATE + g)),
                pl.BlockSpec((CONV_K, CONV_K, SSD_STATE),
                             lambda s, g: (0, 0, (SSD_INNER + SSD_BC) // SSD_STATE + g)),
                pl.BlockSpec((1, GROUP_W), lambda s, g: (0, g)),
                pl.BlockSpec((1, SSD_STATE), lambda s, g: (0, SSD_INNER // SSD_STATE + g)),
                pl.BlockSpec((1, SSD_STATE),
                             lambda s, g: (0, (SSD_INNER + SSD_BC) // SSD_STATE + g)),
                pl.BlockSpec((SUBLANES, LANES), lambda s, g: (0, 0)),
                pl.BlockSpec((1, GROUP_W), lambda s, g: (0, g))]
    args = [groups, zs, conv_w, conv_w, conv_w, conv_b, conv_b, conv_b, par, dsk]
    state_blk = (1, 1, 2, HEADS_PER_GROUP, SSD_HEAD_DIM, SSD_STATE)
    if h0 is not None:
        in_specs.append(pl.BlockSpec(state_blk, lambda s, g: (s, layer, 0, g, 0, 0)))
        args.append(h0)
    out_specs = [pl.BlockSpec((L, GROUP_W), lambda s, g: (s, g))]
    out_shape = [jax.ShapeDtypeStruct((n_seq * L, SSD_INNER), F32)]
    if want_state:
        out_specs.append(pl.BlockSpec(state_blk, lambda s, g: (s, 0, 0, g, 0, 0)))
        out_shape.append(jax.ShapeDtypeStruct(
            (n_seq, 1, 2, SSD_HEADS, SSD_HEAD_DIM, SSD_STATE), F32))
    cast_in, cast_out, cast_shape = _cast_plumbing(cast, n_seq * SSD_GROUPS,
                                                   lambda s, g: s * SSD_GROUPS + g)
    in_specs += cast_in
    args += list(cast)
    out_specs += cast_out
    out_shape += cast_shape
    nc = L // CHUNK
    conv_pad = (L // grid_rows + SUBLANES) if grid_rows > 1 else SUBLANES
    body = functools.partial(_ssd_body, seq_len=L, grid_rows=grid_rows,
                             has_h0=h0 is not None, want_state=want_state, n_cast=len(cast))
    return pl.pallas_call(
        body,
        grid=(n_seq, SSD_GROUPS),
        in_specs=in_specs,
        out_specs=out_specs,
        out_shape=out_shape,
        scratch_shapes=[pltpu.VMEM((L + 2 * conv_pad, GROUP_W), F32),
                        pltpu.VMEM((L, GROUP_W), F32),
                        pltpu.VMEM((L, SSD_STATE), BF16),
                        pltpu.VMEM((L, SSD_STATE), BF16),
                        pltpu.VMEM((L, LANES), F32),
                        pltpu.VMEM((L, 2 * LANES), BF16),
                        pltpu.VMEM((nc, 2, LANES, CHUNK), F32),
                        pltpu.VMEM((nc, SUBLANES, 4 * GROUP_W), F32),
                        pltpu.VMEM((L, GROUP_W), BF16),
                        pltpu.VMEM((L, 2 * GROUP_W), F32),
                        pltpu.VMEM((L, 2 * GROUP_W), BF16),
                        pltpu.VMEM((nc, SUBLANES, 2 * GROUP_W), F32),
                        pltpu.VMEM((nc, SSD_STATE, 2 * GROUP_W), F32),
                        pltpu.VMEM((nc, SSD_STATE, 2 * GROUP_W), BF16),
                        pltpu.VMEM((SSD_STATE, 2 * GROUP_W), F32)],
        compiler_params=_params(2),
        name="ssd",
    )(*args)


def _layer_path(x, mod, lw, raw, h0_gla, h0_ssd, layer, *, n_seq, seq_len, grid_rows,
                want_state, final_w):
    if raw is not None:
        x1, lw["w_all"] = _ffn_call(x, mod, lw["norm_ffn1"], lw["ffn1_w_in"], lw["ffn1_w_out"],
                                    sub=0, seq_len=seq_len, cast=(raw["w_in"],))
    else:
        x1 = _ffn_call(x, mod, lw["norm_ffn1"], lw["ffn1_w_in"], lw["ffn1_w_out"],
                       sub=0, seq_len=seq_len)
    heads, groups, zs = _proj_call(x1, mod, lw["norm_mix"], lw["w_all"], seq_len=seq_len)
    gla_out = _gla_call(heads, zs, lw["wa_cat"], lw["ba_cat"], lw["gla_norm_w"], h0_gla, layer,
                        n_seq=n_seq, seq_len=seq_len, want_state=want_state)
    late = ("ffn2_w_in", "ffn2_w_out", "w_out")
    ssd_out = _ssd_call(groups, zs, lw["conv_w"], lw["conv_b"], lw["ssd_par"], lw["d_skip_row"],
                        h0_ssd, layer, n_seq=n_seq, seq_len=seq_len, grid_rows=grid_rows,
                        want_state=want_state,
                        cast=tuple(raw[k] for k in late) if raw is not None else ())
    if raw is not None:
        lw.update(zip(late, ssd_out[-len(late):]))
    o, y = gla_out[0], ssd_out[0]
    out = _ffn_call(x1, mod, lw["norm_ffn2"], lw["ffn2_w_in"], lw["ffn2_w_out"],
                    sub=2, seq_len=seq_len, mix=(o, y, zs, lw["ssd_norm_w"], lw["w_out"]),
                    final_w=final_w)
    if want_state:
        return out, gla_out[1], ssd_out[1]
    return out, None, None


def kernel(x_prompt, x_sample, state_gla, state_ssd, c, c_ctx, norm_ffn1, norm_mix, norm_ffn2, w_mod, b_mod, ffn1_w_in, ffn1_w_out, ffn2_w_in, ffn2_w_out, w_in, gla_w_a2, gla_b_a, gla_norm_w, conv_w, conv_b, dt_bias, a_log, d_skip, ssd_norm_w, w_out, final_norm):
    nb, seq, _ = x_prompt.shape
    db, dseq, _ = x_sample.shape
    grid_rows = dseq // GRID_W
    xp = x_prompt.reshape(nb * seq, D_MODEL)
    xs = x_sample.reshape(db * dseq, D_MODEL)
    row = lambda v: v.reshape(1, -1)
    gla_states, ssd_states = [], []
    for i in range(DEPTH):
        last = i == DEPTH - 1
        n_rows = -(-(db + 1) // SUBLANES) * SUBLANES
        cc = jnp.concatenate([c, c_ctx[None, :], jnp.zeros((n_rows - db - 1, D_MODEL), F32)], 0)
        mod = _mod_call(cc, w_mod[i], row(b_mod[i])).reshape(n_rows, N_MOD, D_MODEL)
        mod_lat, mod_ctx = mod[:db], mod[db:db + 1]

        wa_cat = jnp.stack(
            [jnp.pad(gla_w_a2[i, d].reshape(GLA_LOWRANK, GLA_HEADS, GLA_DK),
                     ((lo, LANES - lo - GLA_LOWRANK), (0, 0), (0, 0)))
             for d, lo in ((0, SM_AF), (1, SM_AB))], axis=2).reshape(LANES, 2 * GLA_QK)
        ba_cat = jnp.stack([gla_b_a[i, 0].reshape(GLA_HEADS, GLA_DK),
                            gla_b_a[i, 1].reshape(GLA_HEADS, GLA_DK)], axis=1).reshape(1, 2 * GLA_QK)
        ssd_par = jnp.pad(
            jnp.stack([dt_bias[i].reshape(-1), a_log[i].reshape(-1), jnp.ones((2 * SSD_HEADS,), F32)]),
            ((0, SUBLANES - 3), (SM_DTF, LANES - SM_DTF - 2 * SSD_HEADS)))
        lw = {
            "norm_ffn1": row(norm_ffn1[i]), "norm_mix": row(norm_mix[i]),
            "norm_ffn2": row(norm_ffn2[i]),
            "ffn1_w_in": ffn1_w_in[i].astype(BF16), "ffn1_w_out": ffn1_w_out[i].astype(BF16),
            "wa_cat": wa_cat, "ba_cat": ba_cat,
            "gla_norm_w": row(gla_norm_w[i]), "conv_w": conv_w[i], "conv_b": row(conv_b[i]),
            "ssd_par": ssd_par, "d_skip_row": row(jnp.repeat(d_skip[i], SSD_HEAD_DIM)),
            "ssd_norm_w": row(ssd_norm_w[i]),
        }
        raw = {"w_in": jnp.swapaxes(w_in[i], 0, 1), "ffn2_w_in": ffn2_w_in[i], "ffn2_w_out": ffn2_w_out[i],
               "w_out": w_out[i]}
        fw = row(final_norm) if last else None

        xp, sg, ss = _layer_path(xp, mod_ctx, lw, raw, None, None, i, n_seq=nb, seq_len=seq,
                                 grid_rows=1, want_state=True, final_w=fw)
        xs, _, _ = _layer_path(xs, mod_lat, lw, None, state_gla, state_ssd, i, n_seq=db,
                               seq_len=dseq, grid_rows=grid_rows, want_state=False, final_w=fw)
        gla_states.append(sg)
        ssd_states.append(ss)
    y_prompt = xp.reshape(nb, seq, D_MODEL)
    y_sample = xs.reshape(db, dseq, D_MODEL)
    return (y_prompt, y_sample, jnp.concatenate(gla_states, axis=1),
            jnp.concatenate(ssd_states, axis=1))
```

```python
import functools

import jax
import jax.numpy as jnp
from jax import lax
from jax.experimental import pallas as pl
from jax.experimental.pallas import tpu as pltpu

F32 = jnp.float32
BF16 = jnp.bfloat16

D_MODEL = 1024
DEPTH = 1
GRID_W = 64
CHUNK = 64
EPS = 1e-6
N_MOD = 9
D_FF = 2816
GLA_HEADS = 4
GLA_DK = 128
GLA_DV = 256
GLA_LOWRANK = 16
GLA_TAU = 16.0
GLA_QK = GLA_HEADS * GLA_DK
GLA_V = GLA_HEADS * GLA_DV
SSD_HEADS = 16
SSD_HEAD_DIM = 64
SSD_GROUPS = 2
SSD_STATE = 128
SSD_INNER = SSD_HEADS * SSD_HEAD_DIM
SSD_BC = SSD_GROUPS * SSD_STATE
SSD_CONV_DIM = SSD_INNER + 2 * SSD_BC
CONV_K = 3

LOG2_E = 1.4426950408889634
LANES = 128
SUBLANES = 8
BF16_SUBLANES = 16
VMEM_LIMIT_BYTES = 56 * 1024 * 1024

COL_Q = 0
COL_K = COL_Q + GLA_QK
COL_V = COL_K + GLA_QK
COL_R = COL_V + GLA_V
COL_Z = COL_R + GLA_V
HEAD_K = GLA_DK
HEAD_V = 2 * GLA_DK
HEAD_R = HEAD_V + GLA_DV
HEAD_W = HEAD_R + GLA_DV
GRP_B = SSD_INNER // SSD_GROUPS
GRP_C = GRP_B + SSD_STATE
GRP_W = GRP_C + SSD_STATE
ZS_SMALL = SSD_INNER
ZS_W = ZS_SMALL + LANES
SM_AF = 0
SM_AB = SM_AF + GLA_LOWRANK
SM_DTF = SM_AB + GLA_LOWRANK
SM_DTB = SM_DTF + SSD_HEADS
HEADS_PER_GROUP = SSD_HEADS // SSD_GROUPS
GROUP_W = GRP_B

TOKEN_TILE = 512
FFN_PART = 128
PROJ_PART = 256
MOD_TILE = 3 * D_MODEL
GLA_UNROLL = 32
SSD_UNROLL = 16
MXU_TILE = 256
TERM_STRIDE = 2 * SSD_HEADS
HEAD_DIM_SHIFT = SSD_HEAD_DIM.bit_length() - 1
assert 1 << HEAD_DIM_SHIFT == SSD_HEAD_DIM == CHUNK
SCAN_BLOCK_TOKENS = 2048
SCAN_BLOCK_SEQS = 4


def _dot(a, b):
    return jnp.dot(a, b, preferred_element_type=F32)


def _dot_nt(a, b):
    return lax.dot_general(a, b, (((1,), (1,)), ((), ())), preferred_element_type=F32)


def _dot_tn(a, b):
    return lax.dot_general(a, b, (((0,), (0,)), ((), ())), preferred_element_type=F32)


def _silu(x):
    return x * jax.nn.sigmoid(x)


def _log1p_exp_neg_abs(x):
    return jnp.log(1.0 + jnp.exp(-jnp.abs(x)))


def _softplus(x):
    return jnp.maximum(x, 0.0) + _log1p_exp_neg_abs(x)


def _log_sigmoid(x):
    return jnp.minimum(x, 0.0) - _log1p_exp_neg_abs(x)


def _rmsnorm(x, w):
    ms = jnp.mean(x * x, axis=-1, keepdims=True)
    return x * lax.rsqrt(ms + EPS) * w


def _resident(shape):
    nd = len(shape)
    return pl.BlockSpec(shape, lambda *_: (0,) * nd, pipeline_mode=pl.Buffered(1))


def _params(n_axes):
    return pltpu.CompilerParams(dimension_semantics=("arbitrary",) * n_axes,
                                vmem_limit_bytes=VMEM_LIMIT_BYTES)


def _cast_plumbing(cast, n_steps, step_of):
    in_specs, out_specs, out_shape = [], [], []
    for w in cast:
        rows, cols = w.shape
        per_step = -(-rows // n_steps)
        per_step = -(-per_step // BF16_SUBLANES) * BF16_SUBLANES
        last = -(-rows // per_step) - 1
        blk = pl.BlockSpec((per_step, cols),
                           lambda *g, last=last: (jnp.minimum(step_of(*g), last), 0))
        in_specs.append(blk)
        out_specs.append(blk)
        out_shape.append(jax.ShapeDtypeStruct((rows, cols), BF16))
    return in_specs, out_specs, out_shape


def _cast_blocks(src_refs, dst_refs):
    for src_ref, dst_ref in zip(src_refs, dst_refs):
        dst_ref[...] = src_ref[...].astype(dst_ref.dtype)


def _mod_body(c_ref, w_ref, b_ref, out_ref):
    a = _silu(c_ref[...]).astype(BF16)
    out_ref[...] = _dot(a, w_ref[...].astype(BF16)) + b_ref[...]


def _mod_call(cc, w_mod, b_mod):
    n_rows = cc.shape[0]
    tn = MOD_TILE
    return pl.pallas_call(
        _mod_body,
        grid=(N_MOD * D_MODEL // tn,),
        in_specs=[pl.BlockSpec((n_rows, D_MODEL), lambda j: (0, 0)),
                  pl.BlockSpec((D_MODEL, tn), lambda j: (0, j)),
                  pl.BlockSpec((1, tn), lambda j: (0, j))],
        out_specs=pl.BlockSpec((n_rows, tn), lambda j: (0, j)),
        out_shape=jax.ShapeDtypeStruct((n_rows, N_MOD * D_MODEL), F32),
        compiler_params=_params(1),
        name="mod",
    )(cc, w_mod, b_mod)


def _ffn_body(*refs, sub, has_mix, has_final, n_cast, part_tokens):
    it = iter(refs)
    x_ref, mod_ref, nw_ref, win_ref, wout_ref = (next(it) for _ in range(5))
    if has_mix:
        o_ref, y_ref, z_ref, snw_ref, wo_ref = (next(it) for _ in range(5))
    if has_final:
        fn_ref = next(it)
    cast_src = [next(it) for _ in range(n_cast)]
    out_ref = next(it)
    _cast_blocks(cast_src, [next(it) for _ in range(n_cast)])

    for p0 in range(0, x_ref.shape[0], part_tokens):
        rs = slice(p0, p0 + part_tokens)
        x = x_ref[rs, :]
        if has_mix:
            g2 = mod_ref[0, 5:6, :]
            yn = _rmsnorm(y_ref[rs, :] * _silu(z_ref[rs, :]), snw_ref[...])
            m = _dot(jnp.concatenate([o_ref[rs, :], yn.astype(BF16)], axis=1), wo_ref[...])
            x = x + g2 * m
        sh = mod_ref[0, 3 * sub:3 * sub + 1, :]
        sc = mod_ref[0, 3 * sub + 1:3 * sub + 2, :]
        gate = mod_ref[0, 3 * sub + 2:3 * sub + 3, :]
        h = (_rmsnorm(x, nw_ref[...]) * (1.0 + sc) + sh).astype(BF16)
        g = _dot(h, win_ref[:, :D_FF])
        u = _dot(h, win_ref[:, D_FF:])
        act = (_silu(g) * u).astype(BF16)
        x = x + (0.5 * gate) * _dot(act, wout_ref[...])
        if has_final:
            x = _rmsnorm(x, fn_ref[...])
        out_ref[rs, :] = x


def _ffn_call(x, mod, norm_w, w_in, w_out, *, sub, seq_len, mix=None, final_w=None, cast=()):
    m_tok = x.shape[0]
    if mix is not None:
        tm, part = TOKEN_TILE, TOKEN_TILE
    elif cast:
        tm, part = TOKEN_TILE, FFN_PART
    else:
        tm, part = 2 * TOKEN_TILE, FFN_PART
    tiles_per_seq = seq_len // tm
    shared_mod = mod.shape[0] == 1
    assert shared_mod or tiles_per_seq >= 1, "a block may not span differently modulated sequences"
    mod_map = (lambda i: (0, 0, 0)) if shared_mod else (lambda i: (i // tiles_per_seq, 0, 0))
    tok = lambda width: pl.BlockSpec((tm, width), lambda i: (i, 0))
    in_specs = [tok(D_MODEL), pl.BlockSpec((1, N_MOD, D_MODEL), mod_map),
                _resident((1, D_MODEL)), _resident(w_in.shape), _resident(w_out.shape)]
    args = [x, mod, norm_w, w_in, w_out]
    if mix is not None:
        o, y, proj, ssd_norm_w, w_mix_out = mix
        in_specs += [tok(GLA_V), tok(SSD_INNER),
                     pl.BlockSpec((tm, SSD_INNER), lambda i: (i, 0)),
                     _resident((1, SSD_INNER)), _resident(w_mix_out.shape)]
        args += [o, y, proj, ssd_norm_w, w_mix_out]
    if final_w is not None:
        in_specs.append(_resident((1, D_MODEL)))
        args.append(final_w)
    n_steps = m_tok // tm
    cast_in, cast_out, cast_shape = _cast_plumbing(cast, n_steps, lambda i: i)
    in_specs += cast_in
    args += list(cast)
    out_specs = [tok(D_MODEL)] + cast_out
    out_shape = [jax.ShapeDtypeStruct((m_tok, D_MODEL), F32)] + cast_shape
    body = functools.partial(_ffn_body, sub=sub, has_mix=mix is not None,
                             has_final=final_w is not None, n_cast=len(cast), part_tokens=part)
    outs = pl.pallas_call(
        body,
        grid=(n_steps,),
        in_specs=in_specs,
        out_specs=out_specs,
        out_shape=out_shape,
        compiler_params=_params(1),
        name="ffn_mix" if mix is not None else "ffn",
    )(*args)
    return outs if cast else outs[0]


def _proj_body(x_ref, mod_ref, nw_ref, w_ref, heads_ref, groups_ref, zs_ref):
    sh = mod_ref[0, 3:4, :]
    sc = mod_ref[0, 4:5, :]
    o_z = COL_Z + 2 * GLA_LOWRANK
    o_dt = o_z + SSD_INNER + SSD_CONV_DIM
    w_small = jnp.concatenate(
        [w_ref[COL_Z:o_z, :], w_ref[o_dt:, :],
         jnp.zeros((LANES - 2 * GLA_LOWRANK - 2 * SSD_HEADS, D_MODEL), BF16)], axis=0)
    for p0 in range(0, x_ref.shape[0], PROJ_PART):
        rs = slice(p0, p0 + PROJ_PART)
        h = (_rmsnorm(x_ref[rs, :], nw_ref[...]) * (1.0 + sc) + sh).astype(BF16)
        qkvr = _dot_nt(h, w_ref[0:COL_Z, :])
        for hd in range(GLA_HEADS):
            for dst, src, wid in ((0, COL_Q + hd * GLA_DK, GLA_DK), (HEAD_K, COL_K + hd * GLA_DK, GLA_DK),
                                  (HEAD_V, COL_V + hd * GLA_DV, GLA_DV), (HEAD_R, COL_R + hd * GLA_DV, GLA_DV)):
                heads_ref[hd, rs, dst:dst + wid] = qkvr[:, src:src + wid]
        zx = _dot_nt(h, w_ref[o_z:o_dt, :])
        zs_ref[rs, 0:ZS_SMALL] = zx[:, 0:SSD_INNER]
        for g in range(SSD_GROUPS):
            for dst, src, wid in ((0, g * GRP_B, GRP_B), (GRP_B, SSD_INNER + g * SSD_STATE, SSD_STATE),
                                  (GRP_C, SSD_INNER + SSD_BC + g * SSD_STATE, SSD_STATE)):
                groups_ref[g, rs, dst:dst + wid] = zx[:, SSD_INNER + src:SSD_INNER + src + wid]
        zs_ref[rs, ZS_SMALL:ZS_W] = _dot_nt(h, w_small)


def _proj_call(x, mod, norm_w, w_all, *, seq_len):
    m_tok = x.shape[0]
    tm = TOKEN_TILE
    tiles_per_seq = seq_len // tm
    shared_mod = mod.shape[0] == 1
    mod_map = (lambda i: (0, 0, 0)) if shared_mod else (lambda i: (i // tiles_per_seq, 0, 0))
    return pl.pallas_call(
        _proj_body,
        grid=(m_tok // tm,),
        in_specs=[pl.BlockSpec((tm, D_MODEL), lambda i: (i, 0)),
                  pl.BlockSpec((1, N_MOD, D_MODEL), mod_map),
                  _resident((1, D_MODEL)), _resident(w_all.shape)],
        out_specs=[pl.BlockSpec((GLA_HEADS, tm, HEAD_W), lambda i: (0, i, 0)),
                   pl.BlockSpec((SSD_GROUPS, tm, GRP_W), lambda i: (0, i, 0)),
                   pl.BlockSpec((tm, ZS_W), lambda i: (i, 0))],
        out_shape=[jax.ShapeDtypeStruct((GLA_HEADS, m_tok, HEAD_W), F32),
                   jax.ShapeDtypeStruct((SSD_GROUPS, m_tok, GRP_W), F32),
                   jax.ShapeDtypeStruct((m_tok, ZS_W), F32)],
        compiler_params=_params(1),
        name="proj",
    )(x, mod, norm_w, w_all)


def _tri_masks():
    row = lax.broadcasted_iota(jnp.int32, (CHUNK, CHUNK), 0)
    col = lax.broadcasted_iota(jnp.int32, (CHUNK, CHUNK), 1)
    return row >= col, row <= col


def _gla_body(*refs, seq_len, n_sub, has_h0, want_state):
    it = iter(refs)
    hd_ref, sm_ref, wa_ref, ba_ref, nw_ref = (next(it) for _ in range(5))
    h0_ref = next(it) if has_h0 else None
    o_ref = next(it)
    st_ref = next(it) if want_state else None
    (la_scr, qk_scr, qg_scr, ks_scr, vb_scr, sc_scr, kv_scr, dec_scr,
     sprev_scr) = (next(it) for _ in range(9))

    nc = seq_len // CHUNK
    nct = n_sub * nc
    dk = GLA_DK
    lower, upper = _tri_masks()
    lower_b = jnp.where(lower, 1.0, 0.0).astype(BF16)
    upper_b = jnp.where(upper, 1.0, 0.0).astype(BF16)
    unroll = min(GLA_UNROLL, nct)
    rows = lambda c: pl.ds(pl.multiple_of(c * CHUNK, CHUNK), CHUNK)

    pre = _dot(sm_ref[...].astype(BF16), wa_ref[...].astype(BF16)) + ba_ref[...]
    la_scr[...] = _log_sigmoid(pre) * (LOG2_E / GLA_TAU)

    def prep(c, carry):
        sl = rows(c)
        qc = hd_ref[0, sl, 0:HEAD_K] * (GLA_DK ** -0.5)
        kc = hd_ref[0, sl, HEAD_K:HEAD_V]
        vb_scr[sl, :] = hd_ref[0, sl, HEAD_V:HEAD_R].astype(BF16)
        la = la_scr[sl, :]
        hi = la.astype(BF16)
        lo = (la - hi.astype(F32)).astype(BF16)

        def cumulative(tri_b, a):
            p = _dot(tri_b, jnp.concatenate([hi[:, a:a + dk], lo[:, a:a + dk]], axis=1))
            return p[:, 0:dk] + p[:, dk:2 * dk]

        g_f = cumulative(lower_b, 0)
        g_b = cumulative(upper_b, dk)
        mid_f = g_f[CHUNK // 2:CHUNK // 2 + 1, :]
        mid_b = g_b[CHUNK - 1 - CHUNK // 2:CHUNK - CHUNK // 2, :]
        end_f = g_f[CHUNK - 1:CHUNK, :]
        end_b = g_b[0:1, :]
        qk_scr[sl, 0:dk] = (qc * jnp.exp2(g_f - mid_f)).astype(BF16)
        qk_scr[sl, dk:2 * dk] = (qc * jnp.exp2(g_b - mid_b)).astype(BF16)
        qk_scr[sl, 2 * dk:3 * dk] = (kc * jnp.exp2(mid_f - g_f)).astype(BF16)
        qk_scr[sl, 3 * dk:4 * dk] = (kc * jnp.exp2(mid_b - g_b)).astype(BF16)
        qg_scr[sl, 0:dk] = (qc * jnp.exp2(g_f)).astype(BF16)
        qg_scr[sl, dk:2 * dk] = (qc * jnp.exp2(g_b)).astype(BF16)
        ks_scr[sl, 0:dk] = (kc * jnp.exp2(end_f - g_f)).astype(BF16)
        ks_scr[sl, dk:2 * dk] = (kc * jnp.exp2(end_b - g_b)).astype(BF16)
        dec = jnp.exp2(jnp.concatenate([end_f, end_b], axis=1))
        dec_scr[c] = jnp.broadcast_to(dec, (SUBLANES, 2 * dk))
        return carry

    lax.fori_loop(0, nct, prep, 0, unroll=unroll)

    def products(c, carry):
        sl = rows(c)
        s_f = _dot_nt(qk_scr[sl, 0:dk], qk_scr[sl, 2 * dk:3 * dk])
        s_b = _dot_nt(qk_scr[sl, dk:2 * dk], qk_scr[sl, 3 * dk:4 * dk])
        sc_scr[sl, :] = (jnp.where(lower, s_f, 0.0) + jnp.where(upper, s_b, 0.0)).astype(BF16)
        kv_scr[c] = _dot_tn(vb_scr[sl, :], ks_scr[sl, :])
        return carry

    lax.fori_loop(0, nct, products, 0, unroll=unroll)

    for b in range(n_sub):
        for d, a in ((0, 0), (1, dk)):
            state0 = h0_ref[b, 0, d, 0].T if has_h0 else jnp.zeros((GLA_DV, dk), F32)

            def recur(i, state, b=b, d=d, a=a):
                c = b * nc + (i if d == 0 else nc - 1 - i)
                sprev_scr[c, :, a:a + dk] = state.astype(BF16)
                return state * dec_scr[c, 0:1, a:a + dk] + kv_scr[c, :, a:a + dk]

            state = lax.fori_loop(0, nc, recur, state0, unroll=min(nc, 4))
            if want_state:
                st_ref[b, 0, d, 0] = state.T

    def finish(c, carry):
        sl = rows(c)
        o = _dot(sc_scr[sl, :], vb_scr[sl, :]) + _dot_nt(qg_scr[sl, :], sprev_scr[c])
        o_ref[sl, :] = (_rmsnorm(o, nw_ref[...]) * _silu(hd_ref[0, sl, HEAD_R:HEAD_W])).astype(o_ref.dtype)
        return carry

    lax.fori_loop(0, nct, finish, 0, unroll=unroll)


def _gla_call(heads, zs, wa_cat, ba_cat, norm_w, h0, layer, *, n_seq, seq_len, want_state):
    n_sub = max(1, min(SCAN_BLOCK_TOKENS // seq_len, SCAN_BLOCK_SEQS))
    L = n_sub * seq_len
    nc = L // CHUNK
    in_specs = [pl.BlockSpec((1, L, HEAD_W), lambda s, h: (h, s, 0)),
                pl.BlockSpec((L, LANES), lambda s, h: (s, ZS_SMALL // LANES)),
                pl.BlockSpec((LANES, 2 * GLA_DK), lambda s, h: (0, h)),
                pl.BlockSpec((1, 2 * GLA_DK), lambda s, h: (0, h)),
                pl.BlockSpec((1, GLA_DV), lambda s, h: (0, 0))]
    args = [heads, zs, wa_cat, ba_cat, norm_w]
    state_blk = (n_sub, 1, 2, 1, GLA_DK, GLA_DV)
    if h0 is not None:
        in_specs.append(pl.BlockSpec(state_blk, lambda s, h: (s, layer, 0, h, 0, 0)))
        args.append(h0)
    out_specs = [pl.BlockSpec((L, GLA_DV), lambda s, h: (s, h))]
    out_shape = [jax.ShapeDtypeStruct((n_seq * seq_len, GLA_V), BF16)]
    if want_state:
        out_specs.append(pl.BlockSpec(state_blk, lambda s, h: (s, 0, 0, h, 0, 0)))
        out_shape.append(jax.ShapeDtypeStruct((n_seq, 1, 2, GLA_HEADS, GLA_DK, GLA_DV), F32))
    body = functools.partial(_gla_body, seq_len=seq_len, n_sub=n_sub, has_h0=h0 is not None,
                             want_state=want_state)
    return pl.pallas_call(
        body,
        grid=(n_seq // n_sub, GLA_HEADS),
        in_specs=in_specs,
        out_specs=out_specs,
        out_shape=out_shape,
        scratch_shapes=[pltpu.VMEM((L, 2 * GLA_DK), F32),
                        pltpu.VMEM((L, 4 * GLA_DK), BF16),
                        pltpu.VMEM((L, 2 * GLA_DK), BF16),
                        pltpu.VMEM((L, 2 * GLA_DK), BF16),
                        pltpu.VMEM((L, GLA_DV), BF16),
                        pltpu.VMEM((L, CHUNK), BF16),
                        pltpu.VMEM((nc, GLA_DV, 2 * GLA_DK), F32),
                        pltpu.VMEM((nc, SUBLANES, 2 * GLA_DK), F32),
                        pltpu.VMEM((nc, GLA_DV, 2 * GLA_DK), BF16)],
        compiler_params=_params(2),
        name="gla",
    )(*args)


def _ssd_body(*refs, seq_len, grid_rows, has_h0, want_state, n_cast):
    it = iter(refs)
    (grp_ref, sm_ref, cwx_ref, cwb_ref, cwc_ref, cbx_ref, cbb_ref, cbc_ref,
     par_ref, dsk_ref) = (next(it) for _ in range(10))
    h0_ref = next(it) if has_h0 else None
    cast_src = [next(it) for _ in range(n_cast)]
    y_ref = next(it)
    st_ref = next(it) if want_state else None
    _cast_blocks(cast_src, [next(it) for _ in range(n_cast)])
    (pad_scr, xs_s, b_s, c_s, dt_scr, xy_scr, tr_scr, rows_scr, sc_scr, ed_scr, xw_scr, dec_scr,
     cs_scr, sprev_scr, state_scr) = (next(it) for _ in range(15))

    L = seq_len
    nc = L // CHUNK
    width = L // grid_rows
    pad = pad_scr.shape[0] - L
    pad //= 2
    grp = pl.program_id(1)

    def conv_into(c0, cw_ref, cb_ref, dst_ref):
        ch = dst_ref.shape[1]
        pad_scr[0:pad, 0:ch] = jnp.zeros((pad, ch), F32)
        pad_scr[pad + L:pad + L + pad, 0:ch] = jnp.zeros((pad, ch), F32)
        pad_scr[pad:pad + L, 0:ch] = grp_ref[0, :, c0:c0 + ch]
        rc = min(2 * CHUNK, L)
        col = lax.broadcasted_iota(jnp.int32, (rc, ch), 0) % width
        di_taps = range(CONV_K) if grid_rows > 1 else (CONV_K // 2,)
        for r0 in range(0, L, rc):
            acc = jnp.broadcast_to(cb_ref[...], (rc, ch))
            for dj in range(CONV_K):
                inner = None
                for di in di_taps:
                    off = pad + r0 + (di - 1) * width + (dj - 1)
                    term = cw_ref[di, dj:dj + 1, :] * pad_scr[off:off + rc, 0:ch]
                    inner = term if inner is None else inner + term
                if grid_rows > 1 and dj == 0:
                    inner = jnp.where(col >= 1, inner, 0.0)
                if grid_rows > 1 and dj == CONV_K - 1:
                    inner = jnp.where(col <= width - 2, inner, 0.0)
                acc = acc + inner
            dst_ref[r0:r0 + rc, :] = _silu(acc).astype(dst_ref.dtype)

    conv_into(0, cwx_ref, cbx_ref, xs_s)
    conv_into(GRP_B, cwb_ref, cbb_ref, b_s)
    conv_into(GRP_C, cwc_ref, cbc_ref, c_s)

    gw = GROUP_W
    unroll = min(SSD_UNROLL, nc)
    rows = lambda c: pl.ds(pl.multiple_of(c * CHUNK, CHUNK), CHUNK)
    bias_row = par_ref[0:1, :]
    a_row = -jnp.exp(par_ref[1:2, :]) * par_ref[2:3, :] * LOG2_E
    lower, upper = _tri_masks()
    lower_b = jnp.where(lower, 1.0, 0.0).astype(BF16)
    upper_b = jnp.where(upper, 1.0, 0.0).astype(BF16)
    lane = lax.broadcasted_iota(jnp.int32, (CHUNK, LANES), 1)
    dt_lanes = (lane >= SM_DTF) & (lane < SM_DTF + TERM_STRIDE)
    src = lax.broadcasted_iota(jnp.int32, (LANES, 2 * gw), 0)
    dst = lax.broadcasted_iota(jnp.int32, (LANES, 2 * gw), 1)
    dst_slot = (grp * HEADS_PER_GROUP
                + (lax.shift_right_logical(dst, HEAD_DIM_SHIFT) & (HEADS_PER_GROUP - 1))
                + jnp.where(dst >= gw, SSD_HEADS, 0))
    expand = jnp.where((src >= SM_DTF) & ((src & (TERM_STRIDE - 1)) == dst_slot),
                       1.0, 0.0).astype(BF16)
    t_idx = lax.broadcasted_iota(jnp.int32, (CHUNK, gw), 0)
    s_idx = lax.broadcasted_iota(jnp.int32, (CHUNK, gw), 1) & (CHUNK - 1)
    diag = t_idx == s_idx
    blk_r = lax.shift_right_logical(lax.broadcasted_iota(jnp.int32, (MXU_TILE, MXU_TILE), 0),
                                    HEAD_DIM_SHIFT)
    blk_c = lax.shift_right_logical(lax.broadcasted_iota(jnp.int32, (MXU_TILE, MXU_TILE), 1),
                                    HEAD_DIM_SHIFT)
    same_head = blk_r == blk_c

    dt_scr[...] = _softplus(sm_ref[...] + bias_row)

    def place3(v):
        hi = v.astype(BF16).astype(F32)
        rest = v - hi
        mid = rest.astype(BF16).astype(F32)
        lo = rest - mid
        keep = lambda t: jnp.where(dt_lanes, t, 0.0)
        out = (keep(hi) + pltpu.roll(keep(mid), TERM_STRIDE, axis=1)
               + pltpu.roll(keep(lo), 2 * TERM_STRIDE, axis=1))
        return out.astype(BF16)

    def cumulate(c, carry):
        sl = rows(c)
        dt = dt_scr[sl, :]
        da = dt * a_row
        hi = da.astype(BF16)
        both = jnp.concatenate([hi, (da - hi.astype(F32)).astype(BF16)], axis=1)
        pf = _dot(lower_b, both)
        pb = _dot(upper_b, both)
        cum = jnp.where(lane < SM_DTB, pf[:, 0:LANES] + pf[:, LANES:], pb[:, 0:LANES] + pb[:, LANES:])
        xy_scr[sl, 0:LANES] = place3(cum)
        xy_scr[sl, LANES:2 * LANES] = place3(dt)
        tr_scr[c, 0] = cum.T
        tr_scr[c, 1] = dt.T
        head0 = pl.multiple_of(SM_DTF + grp * HEADS_PER_GROUP, SUBLANES)
        for q, (k, d) in enumerate(((0, 0), (0, 1), (1, 0), (1, 1))):
            slab = tr_scr[c, k, pl.ds(head0 + d * SSD_HEADS, HEADS_PER_GROUP), :]
            row = jnp.concatenate(
                [jnp.broadcast_to(slab[h:h + 1, :], (SUBLANES, CHUNK))
                 for h in range(HEADS_PER_GROUP)], axis=1)
            rows_scr[c, :, q * gw:(q + 1) * gw] = row
        return carry

    lax.fori_loop(0, nc, cumulate, 0, unroll=unroll)

    def weights(c, carry):
        sl = rows(c)
        xc = xs_s[sl, :]
        cum_e = _dot(xy_scr[sl, 0:LANES], expand)
        dt_e = _dot(xy_scr[sl, LANES:2 * LANES], expand)
        cb = _dot_nt(c_s[sl, :], jnp.concatenate([b_s[sl, :]] * HEADS_PER_GROUP, axis=0))
        segs, dt_rows, dec_rows = [], [], []
        for d, (a, last_i) in enumerate(((0, CHUNK - 1), (gw, 0))):
            ce = cum_e[:, a:a + gw]
            de = dt_e[:, a:a + gw]
            segs.append(ce - rows_scr[c, 0:1, d * gw:(d + 1) * gw])
            dt_rows.append(rows_scr[c, 0:1, (2 + d) * gw:(3 + d) * gw])
            cum_last = ce[last_i:last_i + 1, :]
            ed_scr[sl, a:a + gw] = jnp.exp2(ce)
            xw_scr[sl, a:a + gw] = (xc * (jnp.exp2(cum_last - ce) * de)).astype(BF16)
            dec_rows.append(jnp.exp2(cum_last))
        fwd = t_idx >= s_idx
        w = (jnp.exp2(jnp.where(fwd, segs[0], segs[1])) * jnp.where(fwd, dt_rows[0], dt_rows[1])
             + jnp.where(diag, dt_rows[1], 0.0))
        sc_scr[sl, :] = (cb * w).astype(BF16)
        dec_scr[c] = jnp.broadcast_to(jnp.concatenate(dec_rows, axis=1), (SUBLANES, 2 * gw))
        return carry

    lax.fori_loop(0, nc, weights, 0, unroll=unroll)

    def products(c, carry):
        sl = rows(c)
        xc = xs_s[sl, :]
        xb = xc.astype(BF16)
        parts = []
        for j in range(gw // MXU_TILE):
            xh = xb[:, j * MXU_TILE:(j + 1) * MXU_TILE]
            rep = jnp.concatenate([xh] * (MXU_TILE // SSD_HEAD_DIM), axis=0)
            bd = jnp.where(same_head, rep, jnp.zeros_like(rep))
            parts.append(_dot(sc_scr[sl, j * MXU_TILE:(j + 1) * MXU_TILE], bd))
        y_ref[sl, :] = jnp.concatenate(parts, axis=1) + xc * dsk_ref[...]
        cs_scr[c] = _dot_tn(b_s[sl, :], xw_scr[sl, :])
        return carry

    lax.fori_loop(0, nc, products, 0, unroll=unroll)

    pair_w = 2 * SSD_HEAD_DIM
    if has_h0:
        for d in range(2):
            for j in range(HEADS_PER_GROUP // 2):
                pair = jnp.concatenate([h0_ref[0, 0, d, 2 * j], h0_ref[0, 0, d, 2 * j + 1]], axis=0)
                state_scr[:, d * gw + j * pair_w:d * gw + (j + 1) * pair_w] = pair.T
    else:
        state_scr[...] = jnp.zeros_like(state_scr)

    for d in range(2):
        for a in range(d * gw, (d + 1) * gw, MXU_TILE):

            def recur(i, state, d=d, a=a):
                c = i if d == 0 else nc - 1 - i
                sprev_scr[c, :, a:a + MXU_TILE] = state.astype(BF16)
                return (state * dec_scr[c, 0:1, a:a + MXU_TILE]
                        + cs_scr[c, :, a:a + MXU_TILE])

            state_scr[:, a:a + MXU_TILE] = lax.fori_loop(
                0, nc, recur, state_scr[:, a:a + MXU_TILE], unroll=min(nc, 4))

    def finish(c, carry):
        sl = rows(c)
        yi = _dot(c_s[sl, :], sprev_scr[c]) * ed_scr[sl, :]
        y_ref[sl, :] += yi[:, 0:gw] + yi[:, gw:2 * gw]
        return carry

    lax.fori_loop(0, nc, finish, 0, unroll=unroll)

    if want_state:
        for d in range(2):
            for j in range(HEADS_PER_GROUP // 2):
                pair = state_scr[:, d * gw + j * pair_w:d * gw + (j + 1) * pair_w].T
                st_ref[0, 0, d, 2 * j] = pair[0:SSD_HEAD_DIM, :]
                st_ref[0, 0, d, 2 * j + 1] = pair[SSD_HEAD_DIM:pair_w, :]


def _ssd_call(groups, zs, conv_w, conv_b, par, dsk, h0, layer, *, n_seq, seq_len, grid_rows,
              want_state, cast=()):
    L = seq_len
    in_specs = [pl.BlockSpec((1, L, GRP_W), lambda s, g: (g, s, 0)),
                pl.BlockSpec((L, LANES), lambda s, g: (s, ZS_SMALL // LANES)),
                pl.BlockSpec((CONV_K, CONV_K, GROUP_W), lambda s, g: (0, 0, g)),
                pl.BlockSpec((CONV_K, CONV_K, SSD_STATE),
                             lambda s, g: (0, 0, SSD_INNER // SSD_STATE + g)),
                pl.BlockSpec((CONV_K, CONV_K, SSD_STATE),
                             lambda s, g: (0, 0, (SSD_INNER + SSD_BC) // SSD_STATE + g)),
                pl.BlockSpec((1, GROUP_W), lambda s, g: (0, g)),
                pl.BlockSpec((1, SSD_STATE), lambda s, g: (0, SSD_INNER // SSD_STATE + g)),
                pl.BlockSpec((1, SSD_STATE),
                             lambda s, g: (0, (SSD_INNER + SSD_BC) // SSD_STATE + g)),
                pl.BlockSpec((SUBLANES, LANES), lambda s, g: (0, 0)),
                pl.BlockSpec((1, GROUP_W), lambda s, g: (0, g))]
    args = [groups, zs, conv_w, conv_w, conv_w, conv_b, conv_b, conv_b, par, dsk]
    state_blk = (1, 1, 2, HEADS_PER_GROUP, SSD_HEAD_DIM, SSD_STATE)
    if h0 is not None:
        in_specs.append(pl.BlockSpec(state_blk, lambda s, g: (s, layer, 0, g, 0, 0)))
        args.append(h0)
    out_specs = [pl.BlockSpec((L, GROUP_W), lambda s, g: (s, g))]
    out_shape = [jax.ShapeDtypeStruct((n_seq * L, SSD_INNER), F32)]
    if want_state:
        out_specs.append(pl.BlockSpec(state_blk, lambda s, g: (s, 0, 0, g, 0, 0)))
        out_shape.append(jax.ShapeDtypeStruct(
            (n_seq, 1, 2, SSD_HEADS, SSD_HEAD_DIM, SSD_STATE), F32))
    cast_in, cast_out, cast_shape = _cast_plumbing(cast, n_seq * SSD_GROUPS,
                                                   lambda s, g: s * SSD_GROUPS + g)
    in_specs += cast_in
    args += list(cast)
    out_specs += cast_out
    out_shape += cast_shape
    nc = L // CHUNK
    conv_pad = (L // grid_rows + SUBLANES) if grid_rows > 1 else SUBLANES
    body = functools.partial(_ssd_body, seq_len=L, grid_rows=grid_rows,
                             has_h0=h0 is not None, want_state=want_state, n_cast=len(cast))
    return pl.pallas_call(
        body,
        grid=(n_seq, SSD_GROUPS),
        in_specs=in_specs,
        out_specs=out_specs,
        out_shape=out_shape,
        scratch_shapes=[pltpu.VMEM((L + 2 * conv_pad, GROUP_W), F32),
                        pltpu.VMEM((L, GROUP_W), F32),
                        pltpu.VMEM((L, SSD_STATE), BF16),
                        pltpu.VMEM((L, SSD_STATE), BF16),
                        pltpu.VMEM((L, LANES), F32),
                        pltpu.VMEM((L, 2 * LANES), BF16),
                        pltpu.VMEM((nc, 2, LANES, CHUNK), F32),
                        pltpu.VMEM((nc, SUBLANES, 4 * GROUP_W), F32),
                        pltpu.VMEM((L, GROUP_W), BF16),
                        pltpu.VMEM((L, 2 * GROUP_W), F32),
                        pltpu.VMEM((L, 2 * GROUP_W), BF16),
                        pltpu.VMEM((nc, SUBLANES, 2 * GROUP_W), F32),
                        pltpu.VMEM((nc, SSD_STATE, 2 * GROUP_W), F32),
                        pltpu.VMEM((nc, SSD_STATE, 2 * GROUP_W), BF16),
                        pltpu.VMEM((SSD_STATE, 2 * GROUP_W), F32)],
        compiler_params=_params(2),
        name="ssd",
    )(*args)


def _layer_path(x, mod, lw, raw, h0_gla, h0_ssd, layer, *, n_seq, seq_len, grid_rows,
                want_state, final_w):
    if raw is not None:
        x1, lw["w_all"] = _ffn_call(x, mod, lw["norm_ffn1"], lw["ffn1_w_in"], lw["ffn1_w_out"],
                                    sub=0, seq_len=seq_len, cast=(raw["w_in"],))
    else:
        x1 = _ffn_call(x, mod, lw["norm_ffn1"], lw["ffn1_w_in"], lw["ffn1_w_out"],
                       sub=0, seq_len=seq_len)
    heads, groups, zs = _proj_call(x1, mod, lw["norm_mix"], lw["w_all"], seq_len=seq_len)
    gla_out = _gla_call(heads, zs, lw["wa_cat"], lw["ba_cat"], lw["gla_norm_w"], h0_gla, layer,
                        n_seq=n_seq, seq_len=seq_len, want_state=want_state)
    late = ("ffn2_w_in", "ffn2_w_out", "w_out")
    ssd_out = _ssd_call(groups, zs, lw["conv_w"], lw["conv_b"], lw["ssd_par"], lw["d_skip_row"],
                        h0_ssd, layer, n_seq=n_seq, seq_len=seq_len, grid_rows=grid_rows,
                        want_state=want_state,
                        cast=tuple(raw[k] for k in late) if raw is not None else ())
    if raw is not None:
        lw.update(zip(late, ssd_out[-len(late):]))
    o, y = gla_out[0], ssd_out[0]
    out = _ffn_call(x1, mod, lw["norm_ffn2"], lw["ffn2_w_in"], lw["ffn2_w_out"],
                    sub=2, seq_len=seq_len, mix=(o, y, zs, lw["ssd_norm_w"], lw["w_out"]),
                    final_w=final_w)
    if want_state:
        return out, gla_out[1], ssd_out[1]
    return out, None, None


def kernel(x_prompt, x_sample, state_gla, state_ssd, c, c_ctx, norm_ffn1, norm_mix, norm_ffn2, w_mod, b_mod, ffn1_w_in, ffn1_w_out, ffn2_w_in, ffn2_w_out, w_in, gla_w_a2, gla_b_a, gla_norm_w, conv_w, conv_b, dt_bias, a_log, d_skip, ssd_norm_w, w_out, final_norm):
    nb, seq, _ = x_prompt.shape
    db, dseq, _ = x_sample.shape
    grid_rows = dseq // GRID_W
    xp = x_prompt.reshape(nb * seq, D_MODEL)
    xs = x_sample.reshape(db * dseq, D_MODEL)
    row = lambda v: v.reshape(1, -1)
    gla_states, ssd_states = [], []
    for i in range(DEPTH):
        last = i == DEPTH - 1
        n_rows = -(-(db + 1) // SUBLANES) * SUBLANES
        cc = jnp.concatenate([c, c_ctx[None, :], jnp.zeros((n_rows - db - 1, D_MODEL), F32)], 0)
        mod = _mod_call(cc, w_mod[i], row(b_mod[i])).reshape(n_rows, N_MOD, D_MODEL)
        mod_lat, mod_ctx = mod[:db], mod[db:db + 1]

        wa_cat = jnp.stack(
            [jnp.pad(gla_w_a2[i, d].reshape(GLA_LOWRANK, GLA_HEADS, GLA_DK),
                     ((lo, LANES - lo - GLA_LOWRANK), (0, 0), (0, 0)))
             for d, lo in ((0, SM_AF), (1, SM_AB))], axis=2).reshape(LANES, 2 * GLA_QK)
        ba_cat = jnp.stack([gla_b_a[i, 0].reshape(GLA_HEADS, GLA_DK),
                            gla_b_a[i, 1].reshape(GLA_HEADS, GLA_DK)], axis=1).reshape(1, 2 * GLA_QK)
        ssd_par = jnp.pad(
            jnp.stack([dt_bias[i].reshape(-1), a_log[i].reshape(-1), jnp.ones((2 * SSD_HEADS,), F32)]),
            ((0, SUBLANES - 3), (SM_DTF, LANES - SM_DTF - 2 * SSD_HEADS)))
        lw = {
            "norm_ffn1": row(norm_ffn1[i]), "norm_mix": row(norm_mix[i]),
            "norm_ffn2": row(norm_ffn2[i]),
            "ffn1_w_in": ffn1_w_in[i].astype(BF16), "ffn1_w_out": ffn1_w_out[i].astype(BF16),
            "wa_cat": wa_cat, "ba_cat": ba_cat,
            "gla_norm_w": row(gla_norm_w[i]), "conv_w": conv_w[i], "conv_b": row(conv_b[i]),
            "ssd_par": ssd_par, "d_skip_row": row(jnp.repeat(d_skip[i], SSD_HEAD_DIM)),
            "ssd_norm_w": row(ssd_norm_w[i]),
        }
        raw = {"w_in": jnp.swapaxes(w_in[i], 0, 1), "ffn2_w_in": ffn2_w_in[i], "ffn2_w_out": ffn2_w_out[i],
               "w_out": w_out[i]}
        fw = row(final_norm) if last else None

        xp, sg, ss = _layer_path(xp, mod_ctx, lw, raw, None, None, i, n_seq=nb, seq_len=seq,
                                 grid_rows=1, want_state=True, final_w=fw)
        xs, _, _ = _layer_path(xs, mod_lat, lw, None, state_gla, state_ssd, i, n_seq=db,
                               seq_len=dseq, grid_rows=grid_rows, want_state=False, final_w=fw)
        gla_states.append(sg)
        ssd_states.append(ss)
    y_prompt = xp.reshape(nb, seq, D_MODEL)
    y_sample = xs.reshape(db, dseq, D_MODEL)
    return (y_prompt, y_sample, jnp.concatenate(gla_states, axis=1),
            jnp.concatenate(ssd_states, axis=1))
```

```python
import functools

import jax
import jax.numpy as jnp
from jax import lax
from jax.experimental import pallas as pl
from jax.experimental.pallas import tpu as pltpu

F32 = jnp.float32
BF16 = jnp.bfloat16

D_MODEL = 1024
DEPTH = 1
GRID_W = 64
CHUNK = 64
EPS = 1e-6
N_MOD = 9
D_FF = 2816
GLA_HEADS = 4
GLA_DK = 128
GLA_DV = 256
GLA_LOWRANK = 16
GLA_TAU = 16.0
GLA_QK = GLA_HEADS * GLA_DK
GLA_V = GLA_HEADS * GLA_DV
SSD_HEADS = 16
SSD_HEAD_DIM = 64
SSD_GROUPS = 2
SSD_STATE = 128
SSD_INNER = SSD_HEADS * SSD_HEAD_DIM
SSD_BC = SSD_GROUPS * SSD_STATE
SSD_CONV_DIM = SSD_INNER + 2 * SSD_BC
CONV_K = 3
D_MIX = GLA_V + SSD_INNER

LOG2_E = 1.4426950408889634
LANES = 128
SUBLANES = 8
BF16_SUBLANES = 16
VMEM_LIMIT_BYTES = 56 * 1024 * 1024

COL_Q = 0
COL_K = COL_Q + GLA_QK
COL_V = COL_K + GLA_QK
COL_R = COL_V + GLA_V
COL_Z = COL_R + GLA_V
HEAD_K = GLA_DK
HEAD_V = 2 * GLA_DK
HEAD_R = HEAD_V + GLA_DV
HEAD_W = HEAD_R + GLA_DV
GRP_B = SSD_INNER // SSD_GROUPS
GRP_C = GRP_B + SSD_STATE
GRP_W = GRP_C + SSD_STATE
ZS_SMALL = SSD_INNER
ZS_W = ZS_SMALL + LANES
SM_AF = 0
SM_AB = SM_AF + GLA_LOWRANK
SM_DTF = SM_AB + GLA_LOWRANK
SM_DTB = SM_DTF + SSD_HEADS
HEADS_PER_GROUP = SSD_HEADS // SSD_GROUPS
GROUP_W = GRP_B

TOKEN_TILE = 512
MOD_TILE = 3 * D_MODEL
GLA_UNROLL = 32
SSD_UNROLL = 16
MXU_TILE = 256
TERM_STRIDE = 2 * SSD_HEADS
HEAD_DIM_SHIFT = SSD_HEAD_DIM.bit_length() - 1
assert 1 << HEAD_DIM_SHIFT == SSD_HEAD_DIM == CHUNK
SCAN_BLOCK_TOKENS = 2048
SCAN_BLOCK_SEQS = 4


def _dot(a, b):
    return jnp.dot(a, b, preferred_element_type=F32)


def _dot_nt(a, b):
    return lax.dot_general(a, b, (((1,), (1,)), ((), ())), preferred_element_type=F32)


def _dot_tn(a, b):
    return lax.dot_general(a, b, (((0,), (0,)), ((), ())), preferred_element_type=F32)


def _silu(x):
    return x * jax.nn.sigmoid(x)


def _log1p_exp_neg_abs(x):
    return jnp.log(1.0 + jnp.exp(-jnp.abs(x)))


def _softplus(x):
    return jnp.maximum(x, 0.0) + _log1p_exp_neg_abs(x)


def _log_sigmoid(x):
    return jnp.minimum(x, 0.0) - _log1p_exp_neg_abs(x)


def _rmsnorm(x, w):
    ms = jnp.mean(x * x, axis=-1, keepdims=True)
    return x * lax.rsqrt(ms + EPS) * w


def _resident(shape):
    nd = len(shape)
    return pl.BlockSpec(shape, lambda *_: (0,) * nd, pipeline_mode=pl.Buffered(1))


def _params(n_axes):
    return pltpu.CompilerParams(dimension_semantics=("arbitrary",) * n_axes,
                                vmem_limit_bytes=VMEM_LIMIT_BYTES)


def _cast_plumbing(cast, n_steps, step_of):
    in_specs, out_specs, out_shape = [], [], []
    for w in cast:
        rows, cols = w.shape
        per_step = -(-rows // n_steps)
        per_step = -(-per_step // BF16_SUBLANES) * BF16_SUBLANES
        last = -(-rows // per_step) - 1
        blk = pl.BlockSpec((per_step, cols),
                           lambda *g, last=last: (jnp.minimum(step_of(*g), last), 0))
        in_specs.append(blk)
        out_specs.append(blk)
        out_shape.append(jax.ShapeDtypeStruct((rows, cols), BF16))
    return in_specs, out_specs, out_shape


def _cast_blocks(src_refs, dst_refs):
    for src_ref, dst_ref in zip(src_refs, dst_refs):
        dst_ref[...] = src_ref[...].astype(dst_ref.dtype)


def _mod_body(c_ref, w_ref, b_ref, out_ref):
    a = _silu(c_ref[...]).astype(BF16)
    out_ref[...] = _dot(a, w_ref[...].astype(BF16)) + b_ref[...]


def _mod_call(cc, w_mod, b_mod):
    n_rows = cc.shape[0]
    tn = MOD_TILE
    return pl.pallas_call(
        _mod_body,
        grid=(N_MOD * D_MODEL // tn,),
        in_specs=[pl.BlockSpec((n_rows, D_MODEL), lambda j: (0, 0)),
                  pl.BlockSpec((D_MODEL, tn), lambda j: (0, j)),
                  pl.BlockSpec((1, tn), lambda j: (0, j))],
        out_specs=pl.BlockSpec((n_rows, tn), lambda j: (0, j)),
        out_shape=jax.ShapeDtypeStruct((n_rows, N_MOD * D_MODEL), F32),
        compiler_params=_params(1),
        name="mod",
    )(cc, w_mod, b_mod)


def _ffn_body(*refs, sub, has_mix, has_final, n_cast, part_tokens):
    it = iter(refs)
    x_ref, mod_ref, nw_ref, win_ref, wout_ref = (next(it) for _ in range(5))
    if has_mix:
        o_ref, y_ref, z_ref, snw_ref, wo_ref = (next(it) for _ in range(5))
    if has_final:
        fn_ref = next(it)
    cast_src = [next(it) for _ in range(n_cast)]
    out_ref = next(it)
    _cast_blocks(cast_src, [next(it) for _ in range(n_cast)])

    for p0 in range(0, x_ref.shape[0], part_tokens):
        rs = slice(p0, p0 + part_tokens)
        x = x_ref[rs, :]
        if has_mix:
            g2 = mod_ref[0, 5:6, :]
            yn = _rmsnorm(y_ref[rs, :] * _silu(z_ref[rs, :]), snw_ref[...])
            m = _dot(jnp.concatenate([o_ref[rs, :], yn.astype(BF16)], axis=1), wo_ref[...])
            x = x + g2 * m
        sh = mod_ref[0, 3 * sub:3 * sub + 1, :]
        sc = mod_ref[0, 3 * sub + 1:3 * sub + 2, :]
        gate = mod_ref[0, 3 * sub + 2:3 * sub + 3, :]
        h = (_rmsnorm(x, nw_ref[...]) * (1.0 + sc) + sh).astype(BF16)
        g = _dot(h, win_ref[:, :D_FF])
        u = _dot(h, win_ref[:, D_FF:])
        act = (_silu(g) * u).astype(BF16)
        x = x + (0.5 * gate) * _dot(act, wout_ref[...])
        if has_final:
            x = _rmsnorm(x, fn_ref[...])
        out_ref[rs, :] = x


def _ffn_call(x, mod, norm_w, w_in, w_out, *, sub, seq_len, mix=None, final_w=None, cast=()):
    m_tok = x.shape[0]
    if mix is not None:
        tm, part = TOKEN_TILE, TOKEN_TILE
    elif cast:
        tm, part = TOKEN_TILE, TOKEN_TILE // 4
    else:
        tm, part = 2 * TOKEN_TILE, TOKEN_TILE
    tiles_per_seq = seq_len // tm
    shared_mod = mod.shape[0] == 1
    assert shared_mod or tiles_per_seq >= 1, "a block may not span differently modulated sequences"
    mod_map = (lambda i: (0, 0, 0)) if shared_mod else (lambda i: (i // tiles_per_seq, 0, 0))
    tok = lambda width: pl.BlockSpec((tm, width), lambda i: (i, 0))
    in_specs = [tok(D_MODEL), pl.BlockSpec((1, N_MOD, D_MODEL), mod_map),
                _resident((1, D_MODEL)), _resident(w_in.shape), _resident(w_out.shape)]
    args = [x, mod, norm_w, w_in, w_out]
    if mix is not None:
        o, y, proj, ssd_norm_w, w_mix_out = mix
        in_specs += [tok(GLA_V), tok(SSD_INNER),
                     pl.BlockSpec((tm, SSD_INNER), lambda i: (i, 0)),
                     _resident((1, SSD_INNER)), _resident(w_mix_out.shape)]
        args += [o, y, proj, ssd_norm_w, w_mix_out]
    if final_w is not None:
        in_specs.append(_resident((1, D_MODEL)))
        args.append(final_w)
    n_steps = m_tok // tm
    cast_in, cast_out, cast_shape = _cast_plumbing(cast, n_steps, lambda i: i)
    in_specs += cast_in
    args += list(cast)
    out_specs = [tok(D_MODEL)] + cast_out
    out_shape = [jax.ShapeDtypeStruct((m_tok, D_MODEL), F32)] + cast_shape
    body = functools.partial(_ffn_body, sub=sub, has_mix=mix is not None,
                             has_final=final_w is not None, n_cast=len(cast), part_tokens=part)
    outs = pl.pallas_call(
        body,
        grid=(n_steps,),
        in_specs=in_specs,
        out_specs=out_specs,
        out_shape=out_shape,
        compiler_params=_params(1),
        name="ffn_mix" if mix is not None else "ffn",
    )(*args)
    return outs if cast else outs[0]


def _proj_body(x_ref, mod_ref, nw_ref, w_ref, heads_ref, groups_ref, zs_ref):
    sh = mod_ref[0, 3:4, :]
    sc = mod_ref[0, 4:5, :]
    h = (_rmsnorm(x_ref[...], nw_ref[...]) * (1.0 + sc) + sh).astype(BF16)
    o_z = COL_Z + 2 * GLA_LOWRANK
    o_dt = o_z + SSD_INNER + SSD_CONV_DIM
    qkvr = _dot_nt(h, w_ref[0:COL_Z, :])
    for hd in range(GLA_HEADS):
        for dst, src, wid in ((0, COL_Q + hd * GLA_DK, GLA_DK), (HEAD_K, COL_K + hd * GLA_DK, GLA_DK),
                              (HEAD_V, COL_V + hd * GLA_DV, GLA_DV), (HEAD_R, COL_R + hd * GLA_DV, GLA_DV)):
            heads_ref[hd, :, dst:dst + wid] = qkvr[:, src:src + wid]
    zx = _dot_nt(h, w_ref[o_z:o_dt, :])
    zs_ref[:, 0:ZS_SMALL] = zx[:, 0:SSD_INNER]
    for g in range(SSD_GROUPS):
        for dst, src, wid in ((0, g * GRP_B, GRP_B), (GRP_B, SSD_INNER + g * SSD_STATE, SSD_STATE),
                              (GRP_C, SSD_INNER + SSD_BC + g * SSD_STATE, SSD_STATE)):
            groups_ref[g, :, dst:dst + wid] = zx[:, SSD_INNER + src:SSD_INNER + src + wid]
    w_small = jnp.concatenate(
        [w_ref[COL_Z:o_z, :], w_ref[o_dt:, :],
         jnp.zeros((LANES - 2 * GLA_LOWRANK - 2 * SSD_HEADS, D_MODEL), BF16)], axis=0)
    zs_ref[:, ZS_SMALL:ZS_W] = _dot_nt(h, w_small)


def _proj_call(x, mod, norm_w, w_all, *, seq_len):
    m_tok = x.shape[0]
    tm = TOKEN_TILE
    tiles_per_seq = seq_len // tm
    shared_mod = mod.shape[0] == 1
    mod_map = (lambda i: (0, 0, 0)) if shared_mod else (lambda i: (i // tiles_per_seq, 0, 0))
    return pl.pallas_call(
        _proj_body,
        grid=(m_tok // tm,),
        in_specs=[pl.BlockSpec((tm, D_MODEL), lambda i: (i, 0)),
                  pl.BlockSpec((1, N_MOD, D_MODEL), mod_map),
                  _resident((1, D_MODEL)), _resident(w_all.shape)],
        out_specs=[pl.BlockSpec((GLA_HEADS, tm, HEAD_W), lambda i: (0, i, 0)),
                   pl.BlockSpec((SSD_GROUPS, tm, GRP_W), lambda i: (0, i, 0)),
                   pl.BlockSpec((tm, ZS_W), lambda i: (i, 0))],
        out_shape=[jax.ShapeDtypeStruct((GLA_HEADS, m_tok, HEAD_W), F32),
                   jax.ShapeDtypeStruct((SSD_GROUPS, m_tok, GRP_W), F32),
                   jax.ShapeDtypeStruct((m_tok, ZS_W), F32)],
        compiler_params=_params(1),
        name="proj",
    )(x, mod, norm_w, w_all)


def _tri_masks():
    row = lax.broadcasted_iota(jnp.int32, (CHUNK, CHUNK), 0)
    col = lax.broadcasted_iota(jnp.int32, (CHUNK, CHUNK), 1)
    return row >= col, row <= col


def _gla_body(*refs, seq_len, n_sub, has_h0, want_state):
    it = iter(refs)
    hd_ref, sm_ref, wa_ref, ba_ref, nw_ref = (next(it) for _ in range(5))
    h0_ref = next(it) if has_h0 else None
    o_ref = next(it)
    st_ref = next(it) if want_state else None
    (la_scr, qk_scr, qg_scr, ks_scr, vb_scr, sc_scr, kv_scr, dec_scr,
     sprev_scr) = (next(it) for _ in range(9))

    nc = seq_len // CHUNK
    nct = n_sub * nc
    dk = GLA_DK
    lower, upper = _tri_masks()
    lower_b = jnp.where(lower, 1.0, 0.0).astype(BF16)
    upper_b = jnp.where(upper, 1.0, 0.0).astype(BF16)
    unroll = min(GLA_UNROLL, nct)
    rows = lambda c: pl.ds(pl.multiple_of(c * CHUNK, CHUNK), CHUNK)

    pre = _dot(sm_ref[...].astype(BF16), wa_ref[...].astype(BF16)) + ba_ref[...]
    la_scr[...] = _log_sigmoid(pre) * (LOG2_E / GLA_TAU)

    def prep(c, carry):
        sl = rows(c)
        qc = hd_ref[0, sl, 0:HEAD_K] * (GLA_DK ** -0.5)
        kc = hd_ref[0, sl, HEAD_K:HEAD_V]
        vb_scr[sl, :] = hd_ref[0, sl, HEAD_V:HEAD_R].astype(BF16)
        la = la_scr[sl, :]
        hi = la.astype(BF16)
        lo = (la - hi.astype(F32)).astype(BF16)

        def cumulative(tri_b, a):
            p = _dot(tri_b, jnp.concatenate([hi[:, a:a + dk], lo[:, a:a + dk]], axis=1))
            return p[:, 0:dk] + p[:, dk:2 * dk]

        g_f = cumulative(lower_b, 0)
        g_b = cumulative(upper_b, dk)
        mid_f = g_f[CHUNK // 2:CHUNK // 2 + 1, :]
        mid_b = g_b[CHUNK - 1 - CHUNK // 2:CHUNK - CHUNK // 2, :]
        end_f = g_f[CHUNK - 1:CHUNK, :]
        end_b = g_b[0:1, :]
        qk_scr[sl, 0:dk] = (qc * jnp.exp2(g_f - mid_f)).astype(BF16)
        qk_scr[sl, dk:2 * dk] = (qc * jnp.exp2(g_b - mid_b)).astype(BF16)
        qk_scr[sl, 2 * dk:3 * dk] = (kc * jnp.exp2(mid_f - g_f)).astype(BF16)
        qk_scr[sl, 3 * dk:4 * dk] = (kc * jnp.exp2(mid_b - g_b)).astype(BF16)
        qg_scr[sl, 0:dk] = (qc * jnp.exp2(g_f)).astype(BF16)
        qg_scr[sl, dk:2 * dk] = (qc * jnp.exp2(g_b)).astype(BF16)
        ks_scr[sl, 0:dk] = (kc * jnp.exp2(end_f - g_f)).astype(BF16)
        ks_scr[sl, dk:2 * dk] = (kc * jnp.exp2(end_b - g_b)).astype(BF16)
        dec = jnp.exp2(jnp.concatenate([end_f, end_b], axis=1))
        dec_scr[c] = jnp.broadcast_to(dec, (SUBLANES, 2 * dk))
        return carry

    lax.fori_loop(0, nct, prep, 0, unroll=unroll)

    def products(c, carry):
        sl = rows(c)
        s_f = _dot_nt(qk_scr[sl, 0:dk], qk_scr[sl, 2 * dk:3 * dk])
        s_b = _dot_nt(qk_scr[sl, dk:2 * dk], qk_scr[sl, 3 * dk:4 * dk])
        sc_scr[sl, :] = (jnp.where(lower, s_f, 0.0) + jnp.where(upper, s_b, 0.0)).astype(BF16)
        kv_scr[c] = _dot_tn(vb_scr[sl, :], ks_scr[sl, :])
        return carry

    lax.fori_loop(0, nct, products, 0, unroll=unroll)

    for b in range(n_sub):
        for d, a in ((0, 0), (1, dk)):
            state0 = h0_ref[b, 0, d, 0].T if has_h0 else jnp.zeros((GLA_DV, dk), F32)

            def recur(i, state, b=b, d=d, a=a):
                c = b * nc + (i if d == 0 else nc - 1 - i)
                sprev_scr[c, :, a:a + dk] = state.astype(BF16)
                return state * dec_scr[c, 0:1, a:a + dk] + kv_scr[c, :, a:a + dk]

            state = lax.fori_loop(0, nc, recur, state0, unroll=min(nc, 4))
            if want_state:
                st_ref[b, 0, d, 0] = state.T

    def finish(c, carry):
        sl = rows(c)
        o = _dot(sc_scr[sl, :], vb_scr[sl, :]) + _dot_nt(qg_scr[sl, :], sprev_scr[c])
        o_ref[sl, :] = (_rmsnorm(o, nw_ref[...]) * _silu(hd_ref[0, sl, HEAD_R:HEAD_W])).astype(o_ref.dtype)
        return carry

    lax.fori_loop(0, nct, finish, 0, unroll=unroll)


def _gla_call(heads, zs, wa_cat, ba_cat, norm_w, h0, layer, *, n_seq, seq_len, want_state):
    n_sub = max(1, min(SCAN_BLOCK_TOKENS // seq_len, SCAN_BLOCK_SEQS))
    L = n_sub * seq_len
    nc = L // CHUNK
    in_specs = [pl.BlockSpec((1, L, HEAD_W), lambda s, h: (h, s, 0)),
                pl.BlockSpec((L, LANES), lambda s, h: (s, ZS_SMALL // LANES)),
                pl.BlockSpec((LANES, 2 * GLA_DK), lambda s, h: (0, h)),
                pl.BlockSpec((1, 2 * GLA_DK), lambda s, h: (0, h)),
                pl.BlockSpec((1, GLA_DV), lambda s, h: (0, 0))]
    args = [heads, zs, wa_cat, ba_cat, norm_w]
    state_blk = (n_sub, 1, 2, 1, GLA_DK, GLA_DV)
    if h0 is not None:
        in_specs.append(pl.BlockSpec(state_blk, lambda s, h: (s, layer, 0, h, 0, 0)))
        args.append(h0)
    out_specs = [pl.BlockSpec((L, GLA_DV), lambda s, h: (s, h))]
    out_shape = [jax.ShapeDtypeStruct((n_seq * seq_len, GLA_V), BF16)]
    if want_state:
        out_specs.append(pl.BlockSpec(state_blk, lambda s, h: (s, 0, 0, h, 0, 0)))
        out_shape.append(jax.ShapeDtypeStruct((n_seq, 1, 2, GLA_HEADS, GLA_DK, GLA_DV), F32))
    body = functools.partial(_gla_body, seq_len=seq_len, n_sub=n_sub, has_h0=h0 is not None,
                             want_state=want_state)
    return pl.pallas_call(
        body,
        grid=(n_seq // n_sub, GLA_HEADS),
        in_specs=in_specs,
        out_specs=out_specs,
        out_shape=out_shape,
        scratch_shapes=[pltpu.VMEM((L, 2 * GLA_DK), F32),
                        pltpu.VMEM((L, 4 * GLA_DK), BF16),
                        pltpu.VMEM((L, 2 * GLA_DK), BF16),
                        pltpu.VMEM((L, 2 * GLA_DK), BF16),
                        pltpu.VMEM((L, GLA_DV), BF16),
                        pltpu.VMEM((L, CHUNK), BF16),
                        pltpu.VMEM((nc, GLA_DV, 2 * GLA_DK), F32),
                        pltpu.VMEM((nc, SUBLANES, 2 * GLA_DK), F32),
                        pltpu.VMEM((nc, GLA_DV, 2 * GLA_DK), BF16)],
        compiler_params=_params(2),
        name="gla",
    )(*args)


def _ssd_body(*refs, seq_len, grid_rows, has_h0, want_state, n_cast):
    it = iter(refs)
    (grp_ref, sm_ref, cwx_ref, cwb_ref, cwc_ref, cbx_ref, cbb_ref, cbc_ref,
     par_ref, dsk_ref) = (next(it) for _ in range(10))
    h0_ref = next(it) if has_h0 else None
    cast_src = [next(it) for _ in range(n_cast)]
    y_ref = next(it)
    st_ref = next(it) if want_state else None
    _cast_blocks(cast_src, [next(it) for _ in range(n_cast)])
    (pad_scr, xs_s, b_s, c_s, dt_scr, xy_scr, tr_scr, rows_scr, sc_scr, ed_scr, xw_scr, dec_scr,
     cs_scr, sprev_scr, state_scr) = (next(it) for _ in range(15))

    L = seq_len
    nc = L // CHUNK
    width = L // grid_rows
    pad = pad_scr.shape[0] - L
    pad //= 2
    grp = pl.program_id(1)

    def conv_into(c0, cw_ref, cb_ref, dst_ref):
        ch = dst_ref.shape[1]
        pad_scr[0:pad, 0:ch] = jnp.zeros((pad, ch), F32)
        pad_scr[pad + L:pad + L + pad, 0:ch] = jnp.zeros((pad, ch), F32)
        pad_scr[pad:pad + L, 0:ch] = grp_ref[0, :, c0:c0 + ch]
        rc = min(2 * CHUNK, L)
        col = lax.broadcasted_iota(jnp.int32, (rc, ch), 0) % width
        di_taps = range(CONV_K) if grid_rows > 1 else (CONV_K // 2,)
        for r0 in range(0, L, rc):
            acc = jnp.broadcast_to(cb_ref[...], (rc, ch))
            for dj in range(CONV_K):
                inner = None
                for di in di_taps:
                    off = pad + r0 + (di - 1) * width + (dj - 1)
                    term = cw_ref[di, dj:dj + 1, :] * pad_scr[off:off + rc, 0:ch]
                    inner = term if inner is None else inner + term
                if grid_rows > 1 and dj == 0:
                    inner = jnp.where(col >= 1, inner, 0.0)
                if grid_rows > 1 and dj == CONV_K - 1:
                    inner = jnp.where(col <= width - 2, inner, 0.0)
                acc = acc + inner
            dst_ref[r0:r0 + rc, :] = _silu(acc).astype(dst_ref.dtype)

    conv_into(0, cwx_ref, cbx_ref, xs_s)
    conv_into(GRP_B, cwb_ref, cbb_ref, b_s)
    conv_into(GRP_C, cwc_ref, cbc_ref, c_s)

    gw = GROUP_W
    unroll = min(SSD_UNROLL, nc)
    rows = lambda c: pl.ds(pl.multiple_of(c * CHUNK, CHUNK), CHUNK)
    bias_row = par_ref[0:1, :]
    a_row = -jnp.exp(par_ref[1:2, :]) * par_ref[2:3, :] * LOG2_E
    lower, upper = _tri_masks()
    lower_b = jnp.where(lower, 1.0, 0.0).astype(BF16)
    upper_b = jnp.where(upper, 1.0, 0.0).astype(BF16)
    lane = lax.broadcasted_iota(jnp.int32, (CHUNK, LANES), 1)
    dt_lanes = (lane >= SM_DTF) & (lane < SM_DTF + TERM_STRIDE)
    src = lax.broadcasted_iota(jnp.int32, (LANES, 2 * gw), 0)
    dst = lax.broadcasted_iota(jnp.int32, (LANES, 2 * gw), 1)
    dst_slot = (grp * HEADS_PER_GROUP
                + (lax.shift_right_logical(dst, HEAD_DIM_SHIFT) & (HEADS_PER_GROUP - 1))
                + jnp.where(dst >= gw, SSD_HEADS, 0))
    expand = jnp.where((src >= SM_DTF) & ((src & (TERM_STRIDE - 1)) == dst_slot),
                       1.0, 0.0).astype(BF16)
    t_idx = lax.broadcasted_iota(jnp.int32, (CHUNK, gw), 0)
    s_idx = lax.broadcasted_iota(jnp.int32, (CHUNK, gw), 1) & (CHUNK - 1)
    diag = t_idx == s_idx
    blk_r = lax.shift_right_logical(lax.broadcasted_iota(jnp.int32, (MXU_TILE, MXU_TILE), 0),
                                    HEAD_DIM_SHIFT)
    blk_c = lax.shift_right_logical(lax.broadcasted_iota(jnp.int32, (MXU_TILE, MXU_TILE), 1),
                                    HEAD_DIM_SHIFT)
    same_head = blk_r == blk_c

    dt_scr[...] = _softplus(sm_ref[...] + bias_row)

    def place3(v):
        hi = v.astype(BF16).astype(F32)
        rest = v - hi
        mid = rest.astype(BF16).astype(F32)
        lo = rest - mid
        keep = lambda t: jnp.where(dt_lanes, t, 0.0)
        out = (keep(hi) + pltpu.roll(keep(mid), TERM_STRIDE, axis=1)
               + pltpu.roll(keep(lo), 2 * TERM_STRIDE, axis=1))
        return out.astype(BF16)

    def cumulate(c, carry):
        sl = rows(c)
        dt = dt_scr[sl, :]
        da = dt * a_row
        hi = da.astype(BF16)
        both = jnp.concatenate([hi, (da - hi.astype(F32)).astype(BF16)], axis=1)
        pf = _dot(lower_b, both)
        pb = _dot(upper_b, both)
        cum = jnp.where(lane < SM_DTB, pf[:, 0:LANES] + pf[:, LANES:], pb[:, 0:LANES] + pb[:, LANES:])
        xy_scr[sl, 0:LANES] = place3(cum)
        xy_scr[sl, LANES:2 * LANES] = place3(dt)
        tr_scr[c, 0] = cum.T
        tr_scr[c, 1] = dt.T
        head0 = pl.multiple_of(SM_DTF + grp * HEADS_PER_GROUP, SUBLANES)
        for q, (k, d) in enumerate(((0, 0), (0, 1), (1, 0), (1, 1))):
            slab = tr_scr[c, k, pl.ds(head0 + d * SSD_HEADS, HEADS_PER_GROUP), :]
            row = jnp.concatenate(
                [jnp.broadcast_to(slab[h:h + 1, :], (SUBLANES, CHUNK))
                 for h in range(HEADS_PER_GROUP)], axis=1)
            rows_scr[c, :, q * gw:(q + 1) * gw] = row
        return carry

    lax.fori_loop(0, nc, cumulate, 0, unroll=unroll)

    def weights(c, carry):
        sl = rows(c)
        xc = xs_s[sl, :]
        cum_e = _dot(xy_scr[sl, 0:LANES], expand)
        dt_e = _dot(xy_scr[sl, LANES:2 * LANES], expand)
        cb = _dot_nt(c_s[sl, :], jnp.concatenate([b_s[sl, :]] * HEADS_PER_GROUP, axis=0))
        segs, dt_rows, dec_rows = [], [], []
        for d, (a, last_i) in enumerate(((0, CHUNK - 1), (gw, 0))):
            ce = cum_e[:, a:a + gw]
            de = dt_e[:, a:a + gw]
            segs.append(ce - rows_scr[c, 0:1, d * gw:(d + 1) * gw])
            dt_rows.append(rows_scr[c, 0:1, (2 + d) * gw:(3 + d) * gw])
            cum_last = ce[last_i:last_i + 1, :]
            ed_scr[sl, a:a + gw] = jnp.exp2(ce)
            xw_scr[sl, a:a + gw] = (xc * (jnp.exp2(cum_last - ce) * de)).astype(BF16)
            dec_rows.append(jnp.exp2(cum_last))
        fwd = t_idx >= s_idx
        w = (jnp.exp2(jnp.where(fwd, segs[0], segs[1])) * jnp.where(fwd, dt_rows[0], dt_rows[1])
             + jnp.where(diag, dt_rows[1], 0.0))
        sc_scr[sl, :] = (cb * w).astype(BF16)
        dec_scr[c] = jnp.broadcast_to(jnp.concatenate(dec_rows, axis=1), (SUBLANES, 2 * gw))
        return carry

    lax.fori_loop(0, nc, weights, 0, unroll=unroll)

    def products(c, carry):
        sl = rows(c)
        xc = xs_s[sl, :]
        xb = xc.astype(BF16)
        parts = []
        for j in range(gw // MXU_TILE):
            xh = xb[:, j * MXU_TILE:(j + 1) * MXU_TILE]
            rep = jnp.concatenate([xh] * (MXU_TILE // SSD_HEAD_DIM), axis=0)
            bd = jnp.where(same_head, rep, jnp.zeros_like(rep))
            parts.append(_dot(sc_scr[sl, j * MXU_TILE:(j + 1) * MXU_TILE], bd))
        y_ref[sl, :] = jnp.concatenate(parts, axis=1) + xc * dsk_ref[...]
        cs_scr[c] = _dot_tn(b_s[sl, :], xw_scr[sl, :])
        return carry

    lax.fori_loop(0, nc, products, 0, unroll=unroll)

    pair_w = 2 * SSD_HEAD_DIM
    if has_h0:
        for d in range(2):
            for j in range(HEADS_PER_GROUP // 2):
                pair = jnp.concatenate([h0_ref[0, 0, d, 2 * j], h0_ref[0, 0, d, 2 * j + 1]], axis=0)
                state_scr[:, d * gw + j * pair_w:d * gw + (j + 1) * pair_w] = pair.T
    else:
        state_scr[...] = jnp.zeros_like(state_scr)

    for d in range(2):
        for a in range(d * gw, (d + 1) * gw, MXU_TILE):

            def recur(i, state, d=d, a=a):
                c = i if d == 0 else nc - 1 - i
                sprev_scr[c, :, a:a + MXU_TILE] = state.astype(BF16)
                return (state * dec_scr[c, 0:1, a:a + MXU_TILE]
                        + cs_scr[c, :, a:a + MXU_TILE])

            state_scr[:, a:a + MXU_TILE] = lax.fori_loop(
                0, nc, recur, state_scr[:, a:a + MXU_TILE], unroll=min(nc, 4))

    def finish(c, carry):
        sl = rows(c)
        yi = _dot(c_s[sl, :], sprev_scr[c]) * ed_scr[sl, :]
        y_ref[sl, :] += yi[:, 0:gw] + yi[:, gw:2 * gw]
        return carry

    lax.fori_loop(0, nc, finish, 0, unroll=unroll)

    if want_state:
        for d in range(2):
            for j in range(HEADS_PER_GROUP // 2):
                pair = state_scr[:, d * gw + j * pair_w:d * gw + (j + 1) * pair_w].T
                st_ref[0, 0, d, 2 * j] = pair[0:SSD_HEAD_DIM, :]
                st_ref[0, 0, d, 2 * j + 1] = pair[SSD_HEAD_DIM:pair_w, :]


def _ssd_call(groups, zs, conv_w, conv_b, par, dsk, h0, layer, *, n_seq, seq_len, grid_rows,
              want_state, cast=()):
    L = seq_len
    in_specs = [pl.BlockSpec((1, L, GRP_W), lambda s, g: (g, s, 0)),
                pl.BlockSpec((L, LANES), lambda s, g: (s, ZS_SMALL // LANES)),
                pl.BlockSpec((CONV_K, CONV_K, GROUP_W), lambda s, g: (0, 0, g)),
                pl.BlockSpec((CONV_K, CONV_K, SSD_STATE),
                             lambda s, g: (0, 0, SSD_INNER // SSD_STATE + g)),
                pl.BlockSpec((CONV_K, CONV_K, SSD_STATE),
                             lambda s, g: (0, 0, (SSD_INNER + SSD_BC) // SSD_STATE + g)),
                pl.BlockSpec((1, GROUP_W), lambda s, g: (0, g)),
                pl.BlockSpec((1, SSD_STATE), lambda s, g: (0, SSD_INNER // SSD_STATE + g)),
                pl.BlockSpec((1, SSD_STATE),
                             lambda s, g: (0, (SSD_INNER + SSD_BC) // SSD_STATE + g)),
                pl.BlockSpec((SUBLANES, LANES), lambda s, g: (0, 0)),
                pl.BlockSpec((1, GROUP_W), lambda s, g: (0, g))]
    args = [groups, zs, conv_w, conv_w, conv_w, conv_b, conv_b, conv_b, par, dsk]
    state_blk = (1, 1, 2, HEADS_PER_GROUP, SSD_HEAD_DIM, SSD_STATE)
    if h0 is not None:
        in_specs.append(pl.BlockSpec(state_blk, lambda s, g: (s, layer, 0, g, 0, 0)))
        args.append(h0)
    out_specs = [pl.BlockSpec((L, GROUP_W), lambda s, g: (s, g))]
    out_shape = [jax.ShapeDtypeStruct((n_seq * L, SSD_INNER), F32)]
    if want_state:
        out_specs.append(pl.BlockSpec(state_blk, lambda s, g: (s, 0, 0, g, 0, 0)))
        out_shape.append(jax.ShapeDtypeStruct(
            (n_seq, 1, 2, SSD_HEADS, SSD_HEAD_DIM, SSD_STATE), F32))
    cast_in, cast_out, cast_shape = _cast_plumbing(cast, n_seq * SSD_GROUPS,
                                                   lambda s, g: s * SSD_GROUPS + g)
    in_specs += cast_in
    args += list(cast)
    out_specs += cast_out
    out_shape += cast_shape
    nc = L // CHUNK
    conv_pad = (L // grid_rows + SUBLANES) if grid_rows > 1 else SUBLANES
    body = functools.partial(_ssd_body, seq_len=L, grid_rows=grid_rows,
                             has_h0=h0 is not None, want_state=want_state, n_cast=len(cast))
    return pl.pallas_call(
        body,
        grid=(n_seq, SSD_GROUPS),
        in_specs=in_specs,
        out_specs=out_specs,
        out_shape=out_shape,
        scratch_shapes=[pltpu.VMEM((L + 2 * conv_pad, GROUP_W), F32),
                        pltpu.VMEM((L, GROUP_W), F32),
                        pltpu.VMEM((L, SSD_STATE), BF16),
                        pltpu.VMEM((L, SSD_STATE), BF16),
                        pltpu.VMEM((L, LANES), F32),
                        pltpu.VMEM((L, 2 * LANES), BF16),
                        pltpu.VMEM((nc, 2, LANES, CHUNK), F32),
                        pltpu.VMEM((nc, SUBLANES, 4 * GROUP_W), F32),
                        pltpu.VMEM((L, GROUP_W), BF16),
                        pltpu.VMEM((L, 2 * GROUP_W), F32),
                        pltpu.VMEM((L, 2 * GROUP_W), BF16),
                        pltpu.VMEM((nc, SUBLANES, 2 * GROUP_W), F32),
                        pltpu.VMEM((nc, SSD_STATE, 2 * GROUP_W), F32),
                        pltpu.VMEM((nc, SSD_STATE, 2 * GROUP_W), BF16),
                        pltpu.VMEM((SSD_STATE, 2 * GROUP_W), F32)],
        compiler_params=_params(2),
        name="ssd",
    )(*args)


def _layer_path(x, mod, lw, raw, h0_gla, h0_ssd, layer, *, n_seq, seq_len, grid_rows,
                want_state, final_w):
    if raw is not None:
        x1, lw["w_all"] = _ffn_call(x, mod, lw["norm_ffn1"], lw["ffn1_w_in"], lw["ffn1_w_out"],
                                    sub=0, seq_len=seq_len, cast=(raw["w_in"],))
    else:
        x1 = _ffn_call(x, mod, lw["norm_ffn1"], lw["ffn1_w_in"], lw["ffn1_w_out"],
                       sub=0, seq_len=seq_len)
    heads, groups, zs = _proj_call(x1, mod, lw["norm_mix"], lw["w_all"], seq_len=seq_len)
    gla_out = _gla_call(heads, zs, lw["wa_cat"], lw["ba_cat"], lw["gla_norm_w"], h0_gla, layer,
                        n_seq=n_seq, seq_len=seq_len, want_state=want_state)
    late = ("ffn2_w_in", "ffn2_w_out", "w_out")
    ssd_out = _ssd_call(groups, zs, lw["conv_w"], lw["conv_b"], lw["ssd_par"], lw["d_skip_row"],
                        h0_ssd, layer, n_seq=n_seq, seq_len=seq_len, grid_rows=grid_rows,
                        want_state=want_state,
                        cast=tuple(raw[k] for k in late) if raw is not None else ())
    if raw is not None:
        lw.update(zip(late, ssd_out[-len(late):]))
    o, y = gla_out[0], ssd_out[0]
    out = _ffn_call(x1, mod, lw["norm_ffn2"], lw["ffn2_w_in"], lw["ffn2_w_out"],
                    sub=2, seq_len=seq_len, mix=(o, y, zs, lw["ssd_norm_w"], lw["w_out"]),
                    final_w=final_w)
    if want_state:
        return out, gla_out[1], ssd_out[1]
    return out, None, None


def kernel(x_prompt, x_sample, state_gla, state_ssd, c, c_ctx, norm_ffn1, norm_mix, norm_ffn2, w_mod, b_mod, ffn1_w_in, ffn1_w_out, ffn2_w_in, ffn2_w_out, w_in, gla_w_a2, gla_b_a, gla_norm_w, conv_w, conv_b, dt_bias, a_log, d_skip, ssd_norm_w, w_out, final_norm):
    nb, seq, _ = x_prompt.shape
    db, dseq, _ = x_sample.shape
    grid_rows = dseq // GRID_W
    xp = x_prompt.reshape(nb * seq, D_MODEL)
    xs = x_sample.reshape(db * dseq, D_MODEL)
    row = lambda v: v.reshape(1, -1)
    gla_states, ssd_states = [], []
    for i in range(DEPTH):
        last = i == DEPTH - 1
        n_rows = -(-(db + 1) // SUBLANES) * SUBLANES
        cc = jnp.concatenate([c, c_ctx[None, :], jnp.zeros((n_rows - db - 1, D_MODEL), F32)], 0)
        mod = _mod_call(cc, w_mod[i], row(b_mod[i])).reshape(n_rows, N_MOD, D_MODEL)
        mod_lat, mod_ctx = mod[:db], mod[db:db + 1]

        wa_cat = jnp.stack(
            [jnp.pad(gla_w_a2[i, d].reshape(GLA_LOWRANK, GLA_HEADS, GLA_DK),
                     ((lo, LANES - lo - GLA_LOWRANK), (0, 0), (0, 0)))
             for d, lo in ((0, SM_AF), (1, SM_AB))], axis=2).reshape(LANES, 2 * GLA_QK)
        ba_cat = jnp.stack([gla_b_a[i, 0].reshape(GLA_HEADS, GLA_DK),
                            gla_b_a[i, 1].reshape(GLA_HEADS, GLA_DK)], axis=1).reshape(1, 2 * GLA_QK)
        ssd_par = jnp.pad(
            jnp.stack([dt_bias[i].reshape(-1), a_log[i].reshape(-1), jnp.ones((2 * SSD_HEADS,), F32)]),
            ((0, SUBLANES - 3), (SM_DTF, LANES - SM_DTF - 2 * SSD_HEADS)))
        lw = {
            "norm_ffn1": row(norm_ffn1[i]), "norm_mix": row(norm_mix[i]),
            "norm_ffn2": row(norm_ffn2[i]),
            "ffn1_w_in": ffn1_w_in[i].astype(BF16), "ffn1_w_out": ffn1_w_out[i].astype(BF16),
            "wa_cat": wa_cat, "ba_cat": ba_cat,
            "gla_norm_w": row(gla_norm_w[i]), "conv_w": conv_w[i], "conv_b": row(conv_b[i]),
            "ssd_par": ssd_par, "d_skip_row": row(jnp.repeat(d_skip[i], SSD_HEAD_DIM)),
            "ssd_norm_w": row(ssd_norm_w[i]),
        }
        raw = {"w_in": jnp.swapaxes(w_in[i], 0, 1), "ffn2_w_in": ffn2_w_in[i], "ffn2_w_out": ffn2_w_out[i],
               "w_out": w_out[i]}
        fw = row(final_norm) if last else None

        xp, sg, ss = _layer_path(xp, mod_ctx, lw, raw, None, None, i, n_seq=nb, seq_len=seq,
                                 grid_rows=1, want_state=True, final_w=fw)
        xs, _, _ = _layer_path(xs, mod_lat, lw, None, state_gla, state_ssd, i, n_seq=db,
                               seq_len=dseq, grid_rows=grid_rows, want_state=False, final_w=fw)
        gla_states.append(sg)
        ssd_states.append(ss)
    y_prompt = xp.reshape(nb, seq, D_MODEL)
    y_sample = xs.reshape(db, dseq, D_MODEL)
    return (y_prompt, y_sample, jnp.concatenate(gla_states, axis=1),
            jnp.concatenate(ssd_states, axis=1))
```

```python
import functools

import jax
import jax.numpy as jnp
from jax import lax
from jax.experimental import pallas as pl
from jax.experimental.pallas import tpu as pltpu

F32 = jnp.float32
BF16 = jnp.bfloat16

D_MODEL = 1024
DEPTH = 1
GRID_W = 64
CHUNK = 64
EPS = 1e-6
N_MOD = 9
D_FF = 2816
GLA_HEADS = 4
GLA_DK = 128
GLA_DV = 256
GLA_LOWRANK = 16
GLA_TAU = 16.0
GLA_QK = GLA_HEADS * GLA_DK
GLA_V = GLA_HEADS * GLA_DV
SSD_HEADS = 16
SSD_HEAD_DIM = 64
SSD_GROUPS = 2
SSD_STATE = 128
SSD_INNER = SSD_HEADS * SSD_HEAD_DIM
SSD_BC = SSD_GROUPS * SSD_STATE
SSD_CONV_DIM = SSD_INNER + 2 * SSD_BC
CONV_K = 3
D_MIX = GLA_V + SSD_INNER

LOG2_E = 1.4426950408889634
LANES = 128
SUBLANES = 8
BF16_SUBLANES = 16
VMEM_LIMIT_BYTES = 56 * 1024 * 1024

COL_Q = 0
COL_K = COL_Q + GLA_QK
COL_V = COL_K + GLA_QK
COL_R = COL_V + GLA_V
COL_Z = COL_R + GLA_V
HEAD_K = GLA_DK
HEAD_V = 2 * GLA_DK
HEAD_R = HEAD_V + GLA_DV
HEAD_W = HEAD_R + GLA_DV
GRP_B = SSD_INNER // SSD_GROUPS
GRP_C = GRP_B + SSD_STATE
GRP_W = GRP_C + SSD_STATE
ZS_SMALL = SSD_INNER
ZS_W = ZS_SMALL + LANES
SM_AF = 0
SM_AB = SM_AF + GLA_LOWRANK
SM_DTF = SM_AB + GLA_LOWRANK
SM_DTB = SM_DTF + SSD_HEADS
HEADS_PER_GROUP = SSD_HEADS // SSD_GROUPS
GROUP_W = GRP_B

TOKEN_TILE = 512
MOD_ROWS = 128
MOD_BUFS = 3
GLA_UNROLL = 32
SSD_UNROLL = 16
MXU_TILE = 256
TERM_STRIDE = 2 * SSD_HEADS
HEAD_DIM_SHIFT = SSD_HEAD_DIM.bit_length() - 1
assert 1 << HEAD_DIM_SHIFT == SSD_HEAD_DIM == CHUNK
SCAN_BLOCK_TOKENS = 2048
SCAN_BLOCK_SEQS = 4


def _dot(a, b):
    return jnp.dot(a, b, preferred_element_type=F32)


def _dot_nt(a, b):
    return lax.dot_general(a, b, (((1,), (1,)), ((), ())), preferred_element_type=F32)


def _dot_tn(a, b):
    return lax.dot_general(a, b, (((0,), (0,)), ((), ())), preferred_element_type=F32)


def _silu(x):
    return x * jax.nn.sigmoid(x)


def _log1p_exp_neg_abs(x):
    return jnp.log(1.0 + jnp.exp(-jnp.abs(x)))


def _softplus(x):
    return jnp.maximum(x, 0.0) + _log1p_exp_neg_abs(x)


def _log_sigmoid(x):
    return jnp.minimum(x, 0.0) - _log1p_exp_neg_abs(x)


def _rmsnorm(x, w):
    ms = jnp.mean(x * x, axis=-1, keepdims=True)
    return x * lax.rsqrt(ms + EPS) * w


def _resident(shape):
    nd = len(shape)
    return pl.BlockSpec(shape, lambda *_: (0,) * nd, pipeline_mode=pl.Buffered(1))


def _params(n_axes):
    return pltpu.CompilerParams(dimension_semantics=("arbitrary",) * n_axes,
                                vmem_limit_bytes=VMEM_LIMIT_BYTES)


def _cast_plumbing(cast, n_steps, step_of):
    in_specs, out_specs, out_shape = [], [], []
    for w in cast:
        rows, cols = w.shape
        per_step = -(-rows // n_steps)
        per_step = -(-per_step // BF16_SUBLANES) * BF16_SUBLANES
        last = -(-rows // per_step) - 1
        blk = pl.BlockSpec((per_step, cols),
                           lambda *g, last=last: (jnp.minimum(step_of(*g), last), 0))
        in_specs.append(blk)
        out_specs.append(blk)
        out_shape.append(jax.ShapeDtypeStruct((rows, cols), BF16))
    return in_specs, out_specs, out_shape


def _cast_blocks(src_refs, dst_refs):
    for src_ref, dst_ref in zip(src_refs, dst_refs):
        dst_ref[...] = src_ref[...].astype(dst_ref.dtype)


def _mod_copy(w_hbm, wbuf, sem, k):
    slot = k % MOD_BUFS
    return pltpu.make_async_copy(w_hbm.at[pl.ds(k * MOD_ROWS, MOD_ROWS), :], wbuf.at[slot],
                                 sem.at[slot])


def _mod_body(c_ref, w_hbm, b_ref, out_ref, wbuf, sem):
    n_chunks = D_MODEL // MOD_ROWS
    for k in range(min(MOD_BUFS, n_chunks)):
        _mod_copy(w_hbm, wbuf, sem, k).start()
    a = _silu(c_ref[...]).astype(BF16)
    out_ref[...] = jnp.broadcast_to(b_ref[...], out_ref.shape)
    for k in range(n_chunks):
        _mod_copy(w_hbm, wbuf, sem, k).wait()
        out_ref[...] += _dot(a[:, k * MOD_ROWS:(k + 1) * MOD_ROWS],
                             wbuf[k % MOD_BUFS].astype(BF16))
        if k + MOD_BUFS < n_chunks:
            _mod_copy(w_hbm, wbuf, sem, k + MOD_BUFS).start()


def _mod_call(cc, w_mod, b_mod):
    n_rows = cc.shape[0]
    n_out = N_MOD * D_MODEL
    whole = lambda shape: pl.BlockSpec(shape, lambda: (0,) * len(shape))
    return pl.pallas_call(
        _mod_body,
        in_specs=[whole((n_rows, D_MODEL)), pl.BlockSpec(memory_space=pl.ANY), whole((1, n_out))],
        out_specs=whole((n_rows, n_out)),
        out_shape=jax.ShapeDtypeStruct((n_rows, n_out), F32),
        scratch_shapes=[pltpu.VMEM((MOD_BUFS, MOD_ROWS, n_out), F32),
                        pltpu.SemaphoreType.DMA((MOD_BUFS,))],
        compiler_params=pltpu.CompilerParams(vmem_limit_bytes=VMEM_LIMIT_BYTES),
        name="mod",
    )(cc, w_mod, b_mod)


def _ffn_body(*refs, sub, has_mix, has_final, n_cast, part_tokens):
    it = iter(refs)
    x_ref, mod_ref, nw_ref, win_ref, wout_ref = (next(it) for _ in range(5))
    if has_mix:
        o_ref, y_ref, z_ref, snw_ref, wo_ref = (next(it) for _ in range(5))
    if has_final:
        fn_ref = next(it)
    cast_src = [next(it) for _ in range(n_cast)]
    out_ref = next(it)
    _cast_blocks(cast_src, [next(it) for _ in range(n_cast)])

    for p0 in range(0, x_ref.shape[0], part_tokens):
        rs = slice(p0, p0 + part_tokens)
        x = x_ref[rs, :]
        if has_mix:
            g2 = mod_ref[0, 5:6, :]
            yn = _rmsnorm(y_ref[rs, :] * _silu(z_ref[rs, :]), snw_ref[...])
            m = _dot(jnp.concatenate([o_ref[rs, :], yn.astype(BF16)], axis=1), wo_ref[...])
            x = x + g2 * m
        sh = mod_ref[0, 3 * sub:3 * sub + 1, :]
        sc = mod_ref[0, 3 * sub + 1:3 * sub + 2, :]
        gate = mod_ref[0, 3 * sub + 2:3 * sub + 3, :]
        h = (_rmsnorm(x, nw_ref[...]) * (1.0 + sc) + sh).astype(BF16)
        g = _dot(h, win_ref[:, :D_FF])
        u = _dot(h, win_ref[:, D_FF:])
        act = (_silu(g) * u).astype(BF16)
        x = x + (0.5 * gate) * _dot(act, wout_ref[...])
        if has_final:
            x = _rmsnorm(x, fn_ref[...])
        out_ref[rs, :] = x


def _ffn_call(x, mod, norm_w, w_in, w_out, *, sub, seq_len, mix=None, final_w=None, cast=()):
    m_tok = x.shape[0]
    if mix is not None:
        tm, part = TOKEN_TILE, TOKEN_TILE
    elif cast:
        tm, part = TOKEN_TILE, TOKEN_TILE // 4
    else:
        tm, part = 2 * TOKEN_TILE, TOKEN_TILE
    tiles_per_seq = seq_len // tm
    shared_mod = mod.shape[0] == 1
    assert shared_mod or tiles_per_seq >= 1, "a block may not span differently modulated sequences"
    mod_map = (lambda i: (0, 0, 0)) if shared_mod else (lambda i: (i // tiles_per_seq, 0, 0))
    tok = lambda width: pl.BlockSpec((tm, width), lambda i: (i, 0))
    in_specs = [tok(D_MODEL), pl.BlockSpec((1, N_MOD, D_MODEL), mod_map),
                _resident((1, D_MODEL)), _resident(w_in.shape), _resident(w_out.shape)]
    args = [x, mod, norm_w, w_in, w_out]
    if mix is not None:
        o, y, proj, ssd_norm_w, w_mix_out = mix
        in_specs += [tok(GLA_V), tok(SSD_INNER),
                     pl.BlockSpec((tm, SSD_INNER), lambda i: (i, 0)),
                     _resident((1, SSD_INNER)), _resident(w_mix_out.shape)]
        args += [o, y, proj, ssd_norm_w, w_mix_out]
    if final_w is not None:
        in_specs.append(_resident((1, D_MODEL)))
        args.append(final_w)
    n_steps = m_tok // tm
    cast_in, cast_out, cast_shape = _cast_plumbing(cast, n_steps, lambda i: i)
    in_specs += cast_in
    args += list(cast)
    out_specs = [tok(D_MODEL)] + cast_out
    out_shape = [jax.ShapeDtypeStruct((m_tok, D_MODEL), F32)] + cast_shape
    body = functools.partial(_ffn_body, sub=sub, has_mix=mix is not None,
                             has_final=final_w is not None, n_cast=len(cast), part_tokens=part)
    outs = pl.pallas_call(
        body,
        grid=(n_steps,),
        in_specs=in_specs,
        out_specs=out_specs,
        out_shape=out_shape,
        compiler_params=_params(1),
        name="ffn_mix" if mix is not None else "ffn",
    )(*args)
    return outs if cast else outs[0]


def _proj_body(x_ref, mod_ref, nw_ref, w_ref, heads_ref, groups_ref, zs_ref):
    sh = mod_ref[0, 3:4, :]
    sc = mod_ref[0, 4:5, :]
    h = (_rmsnorm(x_ref[...], nw_ref[...]) * (1.0 + sc) + sh).astype(BF16)
    o_z = COL_Z + 2 * GLA_LOWRANK
    o_dt = o_z + SSD_INNER + SSD_CONV_DIM
    qkvr = _dot_nt(h, w_ref[0:COL_Z, :])
    for hd in range(GLA_HEADS):
        for dst, src, wid in ((0, COL_Q + hd * GLA_DK, GLA_DK), (HEAD_K, COL_K + hd * GLA_DK, GLA_DK),
                              (HEAD_V, COL_V + hd * GLA_DV, GLA_DV), (HEAD_R, COL_R + hd * GLA_DV, GLA_DV)):
            heads_ref[hd, :, dst:dst + wid] = qkvr[:, src:src + wid]
    zx = _dot_nt(h, w_ref[o_z:o_dt, :])
    zs_ref[:, 0:ZS_SMALL] = zx[:, 0:SSD_INNER]
    for g in range(SSD_GROUPS):
        for dst, src, wid in ((0, g * GRP_B, GRP_B), (GRP_B, SSD_INNER + g * SSD_STATE, SSD_STATE),
                              (GRP_C, SSD_INNER + SSD_BC + g * SSD_STATE, SSD_STATE)):
            groups_ref[g, :, dst:dst + wid] = zx[:, SSD_INNER + src:SSD_INNER + src + wid]
    w_small = jnp.concatenate(
        [w_ref[COL_Z:o_z, :], w_ref[o_dt:, :],
         jnp.zeros((LANES - 2 * GLA_LOWRANK - 2 * SSD_HEADS, D_MODEL), BF16)], axis=0)
    zs_ref[:, ZS_SMALL:ZS_W] = _dot_nt(h, w_small)


def _proj_call(x, mod, norm_w, w_all, *, seq_len):
    m_tok = x.shape[0]
    tm = TOKEN_TILE
    tiles_per_seq = seq_len // tm
    shared_mod = mod.shape[0] == 1
    mod_map = (lambda i: (0, 0, 0)) if shared_mod else (lambda i: (i // tiles_per_seq, 0, 0))
    return pl.pallas_call(
        _proj_body,
        grid=(m_tok // tm,),
        in_specs=[pl.BlockSpec((tm, D_MODEL), lambda i: (i, 0)),
                  pl.BlockSpec((1, N_MOD, D_MODEL), mod_map),
                  _resident((1, D_MODEL)), _resident(w_all.shape)],
        out_specs=[pl.BlockSpec((GLA_HEADS, tm, HEAD_W), lambda i: (0, i, 0)),
                   pl.BlockSpec((SSD_GROUPS, tm, GRP_W), lambda i: (0, i, 0)),
                   pl.BlockSpec((tm, ZS_W), lambda i: (i, 0))],
        out_shape=[jax.ShapeDtypeStruct((GLA_HEADS, m_tok, HEAD_W), F32),
                   jax.ShapeDtypeStruct((SSD_GROUPS, m_tok, GRP_W), F32),
                   jax.ShapeDtypeStruct((m_tok, ZS_W), F32)],
        compiler_params=_params(1),
        name="proj",
    )(x, mod, norm_w, w_all)


def _tri_masks():
    row = lax.broadcasted_iota(jnp.int32, (CHUNK, CHUNK), 0)
    col = lax.broadcasted_iota(jnp.int32, (CHUNK, CHUNK), 1)
    return row >= col, row <= col


def _gla_body(*refs, seq_len, n_sub, has_h0, want_state):
    it = iter(refs)
    hd_ref, sm_ref, wa_ref, ba_ref, nw_ref = (next(it) for _ in range(5))
    h0_ref = next(it) if has_h0 else None
    o_ref = next(it)
    st_ref = next(it) if want_state else None
    (la_scr, qk_scr, qg_scr, ks_scr, vb_scr, sc_scr, kv_scr, dec_scr,
     sprev_scr) = (next(it) for _ in range(9))

    nc = seq_len // CHUNK
    nct = n_sub * nc
    dk = GLA_DK
    lower, upper = _tri_masks()
    lower_b = jnp.where(lower, 1.0, 0.0).astype(BF16)
    upper_b = jnp.where(upper, 1.0, 0.0).astype(BF16)
    unroll = min(GLA_UNROLL, nct)
    rows = lambda c: pl.ds(pl.multiple_of(c * CHUNK, CHUNK), CHUNK)

    pre = _dot(sm_ref[...].astype(BF16), wa_ref[...].astype(BF16)) + ba_ref[...]
    la_scr[...] = _log_sigmoid(pre) * (LOG2_E / GLA_TAU)

    def prep(c, carry):
        sl = rows(c)
        qc = hd_ref[0, sl, 0:HEAD_K] * (GLA_DK ** -0.5)
        kc = hd_ref[0, sl, HEAD_K:HEAD_V]
        vb_scr[sl, :] = hd_ref[0, sl, HEAD_V:HEAD_R].astype(BF16)
        la = la_scr[sl, :]
        hi = la.astype(BF16)
        lo = (la - hi.astype(F32)).astype(BF16)

        def cumulative(tri_b, a):
            p = _dot(tri_b, jnp.concatenate([hi[:, a:a + dk], lo[:, a:a + dk]], axis=1))
            return p[:, 0:dk] + p[:, dk:2 * dk]

        g_f = cumulative(lower_b, 0)
        g_b = cumulative(upper_b, dk)
        mid_f = g_f[CHUNK // 2:CHUNK // 2 + 1, :]
        mid_b = g_b[CHUNK - 1 - CHUNK // 2:CHUNK - CHUNK // 2, :]
        end_f = g_f[CHUNK - 1:CHUNK, :]
        end_b = g_b[0:1, :]
        qk_scr[sl, 0:dk] = (qc * jnp.exp2(g_f - mid_f)).astype(BF16)
        qk_scr[sl, dk:2 * dk] = (qc * jnp.exp2(g_b - mid_b)).astype(BF16)
        qk_scr[sl, 2 * dk:3 * dk] = (kc * jnp.exp2(mid_f - g_f)).astype(BF16)
        qk_scr[sl, 3 * dk:4 * dk] = (kc * jnp.exp2(mid_b - g_b)).astype(BF16)
        qg_scr[sl, 0:dk] = (qc * jnp.exp2(g_f)).astype(BF16)
        qg_scr[sl, dk:2 * dk] = (qc * jnp.exp2(g_b)).astype(BF16)
        ks_scr[sl, 0:dk] = (kc * jnp.exp2(end_f - g_f)).astype(BF16)
        ks_scr[sl, dk:2 * dk] = (kc * jnp.exp2(end_b - g_b)).astype(BF16)
        dec = jnp.exp2(jnp.concatenate([end_f, end_b], axis=1))
        dec_scr[c] = jnp.broadcast_to(dec, (SUBLANES, 2 * dk))
        return carry

    lax.fori_loop(0, nct, prep, 0, unroll=unroll)

    def products(c, carry):
        sl = rows(c)
        s_f = _dot_nt(qk_scr[sl, 0:dk], qk_scr[sl, 2 * dk:3 * dk])
        s_b = _dot_nt(qk_scr[sl, dk:2 * dk], qk_scr[sl, 3 * dk:4 * dk])
        sc_scr[sl, :] = (jnp.where(lower, s_f, 0.0) + jnp.where(upper, s_b, 0.0)).astype(BF16)
        kv_scr[c] = _dot_tn(vb_scr[sl, :], ks_scr[sl, :])
        return carry

    lax.fori_loop(0, nct, products, 0, unroll=unroll)

    for b in range(n_sub):
        for d, a in ((0, 0), (1, dk)):
            state0 = h0_ref[b, 0, d, 0].T if has_h0 else jnp.zeros((GLA_DV, dk), F32)

            def recur(i, state, b=b, d=d, a=a):
                c = b * nc + (i if d == 0 else nc - 1 - i)
                sprev_scr[c, :, a:a + dk] = state.astype(BF16)
                return state * dec_scr[c, 0:1, a:a + dk] + kv_scr[c, :, a:a + dk]

            state = lax.fori_loop(0, nc, recur, state0, unroll=min(nc, 4))
            if want_state:
                st_ref[b, 0, d, 0] = state.T

    def finish(c, carry):
        sl = rows(c)
        o = _dot(sc_scr[sl, :], vb_scr[sl, :]) + _dot_nt(qg_scr[sl, :], sprev_scr[c])
        o_ref[sl, :] = (_rmsnorm(o, nw_ref[...]) * _silu(hd_ref[0, sl, HEAD_R:HEAD_W])).astype(o_ref.dtype)
        return carry

    lax.fori_loop(0, nct, finish, 0, unroll=unroll)


def _gla_call(heads, zs, wa_cat, ba_cat, norm_w, h0, layer, *, n_seq, seq_len, want_state):
    n_sub = max(1, min(SCAN_BLOCK_TOKENS // seq_len, SCAN_BLOCK_SEQS))
    L = n_sub * seq_len
    nc = L // CHUNK
    in_specs = [pl.BlockSpec((1, L, HEAD_W), lambda s, h: (h, s, 0)),
                pl.BlockSpec((L, LANES), lambda s, h: (s, ZS_SMALL // LANES)),
                pl.BlockSpec((LANES, 2 * GLA_DK), lambda s, h: (0, h)),
                pl.BlockSpec((1, 2 * GLA_DK), lambda s, h: (0, h)),
                pl.BlockSpec((1, GLA_DV), lambda s, h: (0, 0))]
    args = [heads, zs, wa_cat, ba_cat, norm_w]
    state_blk = (n_sub, 1, 2, 1, GLA_DK, GLA_DV)
    if h0 is not None:
        in_specs.append(pl.BlockSpec(state_blk, lambda s, h: (s, layer, 0, h, 0, 0)))
        args.append(h0)
    out_specs = [pl.BlockSpec((L, GLA_DV), lambda s, h: (s, h))]
    out_shape = [jax.ShapeDtypeStruct((n_seq * seq_len, GLA_V), BF16)]
    if want_state:
        out_specs.append(pl.BlockSpec(state_blk, lambda s, h: (s, 0, 0, h, 0, 0)))
        out_shape.append(jax.ShapeDtypeStruct((n_seq, 1, 2, GLA_HEADS, GLA_DK, GLA_DV), F32))
    body = functools.partial(_gla_body, seq_len=seq_len, n_sub=n_sub, has_h0=h0 is not None,
                             want_state=want_state)
    return pl.pallas_call(
        body,
        grid=(n_seq // n_sub, GLA_HEADS),
        in_specs=in_specs,
        out_specs=out_specs,
        out_shape=out_shape,
        scratch_shapes=[pltpu.VMEM((L, 2 * GLA_DK), F32),
                        pltpu.VMEM((L, 4 * GLA_DK), BF16),
                        pltpu.VMEM((L, 2 * GLA_DK), BF16),
                        pltpu.VMEM((L, 2 * GLA_DK), BF16),
                        pltpu.VMEM((L, GLA_DV), BF16),
                        pltpu.VMEM((L, CHUNK), BF16),
                        pltpu.VMEM((nc, GLA_DV, 2 * GLA_DK), F32),
                        pltpu.VMEM((nc, SUBLANES, 2 * GLA_DK), F32),
                        pltpu.VMEM((nc, GLA_DV, 2 * GLA_DK), BF16)],
        compiler_params=_params(2),
        name="gla",
    )(*args)


def _ssd_body(*refs, seq_len, grid_rows, has_h0, want_state, n_cast):
    it = iter(refs)
    (grp_ref, sm_ref, cwx_ref, cwb_ref, cwc_ref, cbx_ref, cbb_ref, cbc_ref,
     par_ref, dsk_ref) = (next(it) for _ in range(10))
    h0_ref = next(it) if has_h0 else None
    cast_src = [next(it) for _ in range(n_cast)]
    y_ref = next(it)
    st_ref = next(it) if want_state else None
    _cast_blocks(cast_src, [next(it) for _ in range(n_cast)])
    (pad_scr, xs_s, b_s, c_s, dt_scr, xy_scr, tr_scr, rows_scr, sc_scr, ed_scr, xw_scr, dec_scr,
     cs_scr, sprev_scr, state_scr) = (next(it) for _ in range(15))

    L = seq_len
    nc = L // CHUNK
    width = L // grid_rows
    pad = pad_scr.shape[0] - L
    pad //= 2
    grp = pl.program_id(1)

    def conv_into(c0, cw_ref, cb_ref, dst_ref):
        ch = dst_ref.shape[1]
        pad_scr[0:pad, 0:ch] = jnp.zeros((pad, ch), F32)
        pad_scr[pad + L:pad + L + pad, 0:ch] = jnp.zeros((pad, ch), F32)
        pad_scr[pad:pad + L, 0:ch] = grp_ref[0, :, c0:c0 + ch]
        rc = min(2 * CHUNK, L)
        col = lax.broadcasted_iota(jnp.int32, (rc, ch), 0) % width
        di_taps = range(CONV_K) if grid_rows > 1 else (CONV_K // 2,)
        for r0 in range(0, L, rc):
            acc = jnp.broadcast_to(cb_ref[...], (rc, ch))
            for dj in range(CONV_K):
                inner = None
                for di in di_taps:
                    off = pad + r0 + (di - 1) * width + (dj - 1)
                    term = cw_ref[di, dj:dj + 1, :] * pad_scr[off:off + rc, 0:ch]
                    inner = term if inner is None else inner + term
                if grid_rows > 1 and dj == 0:
                    inner = jnp.where(col >= 1, inner, 0.0)
                if grid_rows > 1 and dj == CONV_K - 1:
                    inner = jnp.where(col <= width - 2, inner, 0.0)
                acc = acc + inner
            dst_ref[r0:r0 + rc, :] = _silu(acc).astype(dst_ref.dtype)

    conv_into(0, cwx_ref, cbx_ref, xs_s)
    conv_into(GRP_B, cwb_ref, cbb_ref, b_s)
    conv_into(GRP_C, cwc_ref, cbc_ref, c_s)

    gw = GROUP_W
    unroll = min(SSD_UNROLL, nc)
    rows = lambda c: pl.ds(pl.multiple_of(c * CHUNK, CHUNK), CHUNK)
    bias_row = par_ref[0:1, :]
    a_row = -jnp.exp(par_ref[1:2, :]) * par_ref[2:3, :] * LOG2_E
    lower, upper = _tri_masks()
    lower_b = jnp.where(lower, 1.0, 0.0).astype(BF16)
    upper_b = jnp.where(upper, 1.0, 0.0).astype(BF16)
    lane = lax.broadcasted_iota(jnp.int32, (CHUNK, LANES), 1)
    dt_lanes = (lane >= SM_DTF) & (lane < SM_DTF + TERM_STRIDE)
    src = lax.broadcasted_iota(jnp.int32, (LANES, 2 * gw), 0)
    dst = lax.broadcasted_iota(jnp.int32, (LANES, 2 * gw), 1)
    dst_slot = (grp * HEADS_PER_GROUP
                + (lax.shift_right_logical(dst, HEAD_DIM_SHIFT) & (HEADS_PER_GROUP - 1))
                + jnp.where(dst >= gw, SSD_HEADS, 0))
    expand = jnp.where((src >= SM_DTF) & ((src & (TERM_STRIDE - 1)) == dst_slot),
                       1.0, 0.0).astype(BF16)
    t_idx = lax.broadcasted_iota(jnp.int32, (CHUNK, gw), 0)
    s_idx = lax.broadcasted_iota(jnp.int32, (CHUNK, gw), 1) & (CHUNK - 1)
    diag = t_idx == s_idx
    blk_r = lax.shift_right_logical(lax.broadcasted_iota(jnp.int32, (MXU_TILE, MXU_TILE), 0),
                                    HEAD_DIM_SHIFT)
    blk_c = lax.shift_right_logical(lax.broadcasted_iota(jnp.int32, (MXU_TILE, MXU_TILE), 1),
                                    HEAD_DIM_SHIFT)
    same_head = blk_r == blk_c

    dt_scr[...] = _softplus(sm_ref[...] + bias_row)

    def place3(v):
        hi = v.astype(BF16).astype(F32)
        rest = v - hi
        mid = rest.astype(BF16).astype(F32)
        lo = rest - mid
        keep = lambda t: jnp.where(dt_lanes, t, 0.0)
        out = (keep(hi) + pltpu.roll(keep(mid), TERM_STRIDE, axis=1)
               + pltpu.roll(keep(lo), 2 * TERM_STRIDE, axis=1))
        return out.astype(BF16)

    def cumulate(c, carry):
        sl = rows(c)
        dt = dt_scr[sl, :]
        da = dt * a_row
        hi = da.astype(BF16)
        both = jnp.concatenate([hi, (da - hi.astype(F32)).astype(BF16)], axis=1)
        pf = _dot(lower_b, both)
        pb = _dot(upper_b, both)
        cum = jnp.where(lane < SM_DTB, pf[:, 0:LANES] + pf[:, LANES:], pb[:, 0:LANES] + pb[:, LANES:])
        xy_scr[sl, 0:LANES] = place3(cum)
        xy_scr[sl, LANES:2 * LANES] = place3(dt)
        tr_scr[c, 0] = cum.T
        tr_scr[c, 1] = dt.T
        head0 = pl.multiple_of(SM_DTF + grp * HEADS_PER_GROUP, SUBLANES)
        for q, (k, d) in enumerate(((0, 0), (0, 1), (1, 0), (1, 1))):
            slab = tr_scr[c, k, pl.ds(head0 + d * SSD_HEADS, HEADS_PER_GROUP), :]
            row = jnp.concatenate(
                [jnp.broadcast_to(slab[h:h + 1, :], (SUBLANES, CHUNK))
                 for h in range(HEADS_PER_GROUP)], axis=1)
            rows_scr[c, :, q * gw:(q + 1) * gw] = row
        return carry

    lax.fori_loop(0, nc, cumulate, 0, unroll=unroll)

    def weights(c, carry):
        sl = rows(c)
        xc = xs_s[sl, :]
        cum_e = _dot(xy_scr[sl, 0:LANES], expand)
        dt_e = _dot(xy_scr[sl, LANES:2 * LANES], expand)
        cb = _dot_nt(c_s[sl, :], jnp.concatenate([b_s[sl, :]] * HEADS_PER_GROUP, axis=0))
        segs, dt_rows, dec_rows = [], [], []
        for d, (a, last_i) in enumerate(((0, CHUNK - 1), (gw, 0))):
            ce = cum_e[:, a:a + gw]
            de = dt_e[:, a:a + gw]
            segs.append(ce - rows_scr[c, 0:1, d * gw:(d + 1) * gw])
            dt_rows.append(rows_scr[c, 0:1, (2 + d) * gw:(3 + d) * gw])
            cum_last = ce[last_i:last_i + 1, :]
            ed_scr[sl, a:a + gw] = jnp.exp2(ce)
            xw_scr[sl, a:a + gw] = (xc * (jnp.exp2(cum_last - ce) * de)).astype(BF16)
            dec_rows.append(jnp.exp2(cum_last))
        fwd = t_idx >= s_idx
        w = (jnp.exp2(jnp.where(fwd, segs[0], segs[1])) * jnp.where(fwd, dt_rows[0], dt_rows[1])
             + jnp.where(diag, dt_rows[1], 0.0))
        sc_scr[sl, :] = (cb * w).astype(BF16)
        dec_scr[c] = jnp.broadcast_to(jnp.concatenate(dec_rows, axis=1), (SUBLANES, 2 * gw))
        return carry

    lax.fori_loop(0, nc, weights, 0, unroll=unroll)

    def products(c, carry):
        sl = rows(c)
        xc = xs_s[sl, :]
        xb = xc.astype(BF16)
        parts = []
        for j in range(gw // MXU_TILE):
            xh = xb[:, j * MXU_TILE:(j + 1) * MXU_TILE]
            rep = jnp.concatenate([xh] * (MXU_TILE // SSD_HEAD_DIM), axis=0)
            bd = jnp.where(same_head, rep, jnp.zeros_like(rep))
            parts.append(_dot(sc_scr[sl, j * MXU_TILE:(j + 1) * MXU_TILE], bd))
        y_ref[sl, :] = jnp.concatenate(parts, axis=1) + xc * dsk_ref[...]
        cs_scr[c] = _dot_tn(b_s[sl, :], xw_scr[sl, :])
        return carry

    lax.fori_loop(0, nc, products, 0, unroll=unroll)

    pair_w = 2 * SSD_HEAD_DIM
    if has_h0:
        for d in range(2):
            for j in range(HEADS_PER_GROUP // 2):
                pair = jnp.concatenate([h0_ref[0, 0, d, 2 * j], h0_ref[0, 0, d, 2 * j + 1]], axis=0)
                state_scr[:, d * gw + j * pair_w:d * gw + (j + 1) * pair_w] = pair.T
    else:
        state_scr[...] = jnp.zeros_like(state_scr)

    for d in range(2):
        for a in range(d * gw, (d + 1) * gw, MXU_TILE):

            def recur(i, state, d=d, a=a):
                c = i if d == 0 else nc - 1 - i
                sprev_scr[c, :, a:a + MXU_TILE] = state.astype(BF16)
                return (state * dec_scr[c, 0:1, a:a + MXU_TILE]
                        + cs_scr[c, :, a:a + MXU_TILE])

            state_scr[:, a:a + MXU_TILE] = lax.fori_loop(
                0, nc, recur, state_scr[:, a:a + MXU_TILE], unroll=min(nc, 4))

    def finish(c, carry):
        sl = rows(c)
        yi = _dot(c_s[sl, :], sprev_scr[c]) * ed_scr[sl, :]
        y_ref[sl, :] += yi[:, 0:gw] + yi[:, gw:2 * gw]
        return carry

    lax.fori_loop(0, nc, finish, 0, unroll=unroll)

    if want_state:
        for d in range(2):
            for j in range(HEADS_PER_GROUP // 2):
                pair = state_scr[:, d * gw + j * pair_w:d * gw + (j + 1) * pair_w].T
                st_ref[0, 0, d, 2 * j] = pair[0:SSD_HEAD_DIM, :]
                st_ref[0, 0, d, 2 * j + 1] = pair[SSD_HEAD_DIM:pair_w, :]


def _ssd_call(groups, zs, conv_w, conv_b, par, dsk, h0, layer, *, n_seq, seq_len, grid_rows,
              want_state, cast=()):
    L = seq_len
    in_specs = [pl.BlockSpec((1, L, GRP_W), lambda s, g: (g, s, 0)),
                pl.BlockSpec((L, LANES), lambda s, g: (s, ZS_SMALL // LANES)),
                pl.BlockSpec((CONV_K, CONV_K, GROUP_W), lambda s, g: (0, 0, g)),
                pl.BlockSpec((CONV_K, CONV_K, SSD_STATE),
                             lambda s, g: (0, 0, SSD_INNER // SSD_STATE + g)),
                pl.BlockSpec((CONV_K, CONV_K, SSD_STATE),
                             lambda s, g: (0, 0, (SSD_INNER + SSD_BC) // SSD_STATE + g)),
                pl.BlockSpec((1, GROUP_W), lambda s, g: (0, g)),
                pl.BlockSpec((1, SSD_STATE), lambda s, g: (0, SSD_INNER // SSD_STATE + g)),
                pl.BlockSpec((1, SSD_STATE),
                             lambda s, g: (0, (SSD_INNER + SSD_BC) // SSD_STATE + g)),
                pl.BlockSpec((SUBLANES, LANES), lambda s, g: (0, 0)),
                pl.BlockSpec((1, GROUP_W), lambda s, g: (0, g))]
    args = [groups, zs, conv_w, conv_w, conv_w, conv_b, conv_b, conv_b, par, dsk]
    state_blk = (1, 1, 2, HEADS_PER_GROUP, SSD_HEAD_DIM, SSD_STATE)
    if h0 is not None:
        in_specs.append(pl.BlockSpec(state_blk, lambda s, g: (s, layer, 0, g, 0, 0)))
        args.append(h0)
    out_specs = [pl.BlockSpec((L, GROUP_W), lambda s, g: (s, g))]
    out_shape = [jax.ShapeDtypeStruct((n_seq * L, SSD_INNER), F32)]
    if want_state:
        out_specs.append(pl.BlockSpec(state_blk, lambda s, g: (s, 0, 0, g, 0, 0)))
        out_shape.append(jax.ShapeDtypeStruct(
            (n_seq, 1, 2, SSD_HEADS, SSD_HEAD_DIM, SSD_STATE), F32))
    cast_in, cast_out, cast_shape = _cast_plumbing(cast, n_seq * SSD_GROUPS,
                                                   lambda s, g: s * SSD_GROUPS + g)
    in_specs += cast_in
    args += list(cast)
    out_specs += cast_out
    out_shape += cast_shape
    nc = L // CHUNK
    conv_pad = (L // grid_rows + SUBLANES) if grid_rows > 1 else SUBLANES
    body = functools.partial(_ssd_body, seq_len=L, grid_rows=grid_rows,
                             has_h0=h0 is not None, want_state=want_state, n_cast=len(cast))
    return pl.pallas_call(
        body,
        grid=(n_seq, SSD_GROUPS),
        in_specs=in_specs,
        out_specs=out_specs,
        out_shape=out_shape,
        scratch_shapes=[pltpu.VMEM((L + 2 * conv_pad, GROUP_W), F32),
                        pltpu.VMEM((L, GROUP_W), F32),
                        pltpu.VMEM((L, SSD_STATE), BF16),
                        pltpu.VMEM((L, SSD_STATE), BF16),
                        pltpu.VMEM((L, LANES), F32),
                        pltpu.VMEM((L, 2 * LANES), BF16),
                        pltpu.VMEM((nc, 2, LANES, CHUNK), F32),
                        pltpu.VMEM((nc, SUBLANES, 4 * GROUP_W), F32),
                        pltpu.VMEM((L, GROUP_W), BF16),
                        pltpu.VMEM((L, 2 * GROUP_W), F32),
                        pltpu.VMEM((L, 2 * GROUP_W), BF16),
                        pltpu.VMEM((nc, SUBLANES, 2 * GROUP_W), F32),
                        pltpu.VMEM((nc, SSD_STATE, 2 * GROUP_W), F32),
                        pltpu.VMEM((nc, SSD_STATE, 2 * GROUP_W), BF16),
                        pltpu.VMEM((SSD_STATE, 2 * GROUP_W), F32)],
        compiler_params=_params(2),
        name="ssd",
    )(*args)


def _layer_path(x, mod, lw, raw, h0_gla, h0_ssd, layer, *, n_seq, seq_len, grid_rows,
                want_state, final_w):
    if raw is not None:
        x1, lw["w_all"] = _ffn_call(x, mod, lw["norm_ffn1"], lw["ffn1_w_in"], lw["ffn1_w_out"],
                                    sub=0, seq_len=seq_len, cast=(raw["w_in"],))
    else:
        x1 = _ffn_call(x, mod, lw["norm_ffn1"], lw["ffn1_w_in"], lw["ffn1_w_out"],
                       sub=0, seq_len=seq_len)
    heads, groups, zs = _proj_call(x1, mod, lw["norm_mix"], lw["w_all"], seq_len=seq_len)
    gla_out = _gla_call(heads, zs, lw["wa_cat"], lw["ba_cat"], lw["gla_norm_w"], h0_gla, layer,
                        n_seq=n_seq, seq_len=seq_len, want_state=want_state)
    late = ("ffn2_w_in", "ffn2_w_out", "w_out")
    ssd_out = _ssd_call(groups, zs, lw["conv_w"], lw["conv_b"], lw["ssd_par"], lw["d_skip_row"],
                        h0_ssd, layer, n_seq=n_seq, seq_len=seq_len, grid_rows=grid_rows,
                        want_state=want_state,
                        cast=tuple(raw[k] for k in late) if raw is not None else ())
    if raw is not None:
        lw.update(zip(late, ssd_out[-len(late):]))
    o, y = gla_out[0], ssd_out[0]
    out = _ffn_call(x1, mod, lw["norm_ffn2"], lw["ffn2_w_in"], lw["ffn2_w_out"],
                    sub=2, seq_len=seq_len, mix=(o, y, zs, lw["ssd_norm_w"], lw["w_out"]),
                    final_w=final_w)
    if want_state:
        return out, gla_out[1], ssd_out[1]
    return out, None, None


def kernel(x_prompt, x_sample, state_gla, state_ssd, c, c_ctx, norm_ffn1, norm_mix, norm_ffn2, w_mod, b_mod, ffn1_w_in, ffn1_w_out, ffn2_w_in, ffn2_w_out, w_in, gla_w_a2, gla_b_a, gla_norm_w, conv_w, conv_b, dt_bias, a_log, d_skip, ssd_norm_w, w_out, final_norm):
    nb, seq, _ = x_prompt.shape
    db, dseq, _ = x_sample.shape
    grid_rows = dseq // GRID_W
    xp = x_prompt.reshape(nb * seq, D_MODEL)
    xs = x_sample.reshape(db * dseq, D_MODEL)
    row = lambda v: v.reshape(1, -1)
    gla_states, ssd_states = [], []
    for i in range(DEPTH):
        last = i == DEPTH - 1
        n_rows = -(-(db + 1) // SUBLANES) * SUBLANES
        cc = jnp.concatenate([c, c_ctx[None, :], jnp.zeros((n_rows - db - 1, D_MODEL), F32)], 0)
        mod = _mod_call(cc, w_mod[i], row(b_mod[i])).reshape(n_rows, N_MOD, D_MODEL)
        mod_lat, mod_ctx = mod[:db], mod[db:db + 1]

        wa_cat = jnp.stack(
            [jnp.pad(gla_w_a2[i, d].reshape(GLA_LOWRANK, GLA_HEADS, GLA_DK),
                     ((lo, LANES - lo - GLA_LOWRANK), (0, 0), (0, 0)))
             for d, lo in ((0, SM_AF), (1, SM_AB))], axis=2).reshape(LANES, 2 * GLA_QK)
        ba_cat = jnp.stack([gla_b_a[i, 0].reshape(GLA_HEADS, GLA_DK),
                            gla_b_a[i, 1].reshape(GLA_HEADS, GLA_DK)], axis=1).reshape(1, 2 * GLA_QK)
        ssd_par = jnp.pad(
            jnp.stack([dt_bias[i].reshape(-1), a_log[i].reshape(-1), jnp.ones((2 * SSD_HEADS,), F32)]),
            ((0, SUBLANES - 3), (SM_DTF, LANES - SM_DTF - 2 * SSD_HEADS)))
        lw = {
            "norm_ffn1": row(norm_ffn1[i]), "norm_mix": row(norm_mix[i]),
            "norm_ffn2": row(norm_ffn2[i]),
            "ffn1_w_in": ffn1_w_in[i].astype(BF16), "ffn1_w_out": ffn1_w_out[i].astype(BF16),
            "wa_cat": wa_cat, "ba_cat": ba_cat,
            "gla_norm_w": row(gla_norm_w[i]), "conv_w": conv_w[i], "conv_b": row(conv_b[i]),
            "ssd_par": ssd_par, "d_skip_row": row(jnp.repeat(d_skip[i], SSD_HEAD_DIM)),
            "ssd_norm_w": row(ssd_norm_w[i]),
        }
        raw = {"w_in": jnp.swapaxes(w_in[i], 0, 1), "ffn2_w_in": ffn2_w_in[i], "ffn2_w_out": ffn2_w_out[i],
               "w_out": w_out[i]}
        fw = row(final_norm) if last else None

        xp, sg, ss = _layer_path(xp, mod_ctx, lw, raw, None, None, i, n_seq=nb, seq_len=seq,
                                 grid_rows=1, want_state=True, final_w=fw)
        xs, _, _ = _layer_path(xs, mod_lat, lw, None, state_gla, state_ssd, i, n_seq=db,
                               seq_len=dseq, grid_rows=grid_rows, want_state=False, final_w=fw)
        gla_states.append(sg)
        ssd_states.append(ss)
    y_prompt = xp.reshape(nb, seq, D_MODEL)
    y_sample = xs.reshape(db, dseq, D_MODEL)
    return (y_prompt, y_sample, jnp.concatenate(gla_states, axis=1),
            jnp.concatenate(ssd_states, axis=1))
```

```python
import functools

import jax
import jax.numpy as jnp
from jax import lax
from jax.experimental import pallas as pl
from jax.experimental.pallas import tpu as pltpu

F32 = jnp.float32
BF16 = jnp.bfloat16

D_MODEL = 1024
DEPTH = 1
GRID_W = 64
CHUNK = 64
EPS = 1e-6
N_MOD = 9
D_FF = 2816
GLA_HEADS = 4
GLA_DK = 128
GLA_DV = 256
GLA_LOWRANK = 16
GLA_TAU = 16.0
GLA_QK = GLA_HEADS * GLA_DK
GLA_V = GLA_HEADS * GLA_DV
SSD_HEADS = 16
SSD_HEAD_DIM = 64
SSD_GROUPS = 2
SSD_STATE = 128
SSD_INNER = SSD_HEADS * SSD_HEAD_DIM
SSD_BC = SSD_GROUPS * SSD_STATE
SSD_CONV_DIM = SSD_INNER + 2 * SSD_BC
CONV_K = 3
D_MIX = GLA_V + SSD_INNER

LOG2_E = 1.4426950408889634
LANES = 128
SUBLANES = 8
BF16_SUBLANES = 16
VMEM_LIMIT_BYTES = 56 * 1024 * 1024

COL_Q = 0
COL_K = COL_Q + GLA_QK
COL_V = COL_K + GLA_QK
COL_R = COL_V + GLA_V
COL_Z = COL_R + GLA_V
HEAD_K = GLA_DK
HEAD_V = 2 * GLA_DK
HEAD_R = HEAD_V + GLA_DV
HEAD_W = HEAD_R + GLA_DV
GRP_B = SSD_INNER // SSD_GROUPS
GRP_C = GRP_B + SSD_STATE
GRP_W = GRP_C + SSD_STATE
ZS_SMALL = SSD_INNER
ZS_W = ZS_SMALL + LANES
SM_AF = 0
SM_AB = SM_AF + GLA_LOWRANK
SM_DTF = SM_AB + GLA_LOWRANK
SM_DTB = SM_DTF + SSD_HEADS
HEADS_PER_GROUP = SSD_HEADS // SSD_GROUPS
GROUP_W = GRP_B

TOKEN_TILE = 512
MOD_CHUNKS = 8
MOD_BUFS = 3
GLA_UNROLL = 32
SSD_UNROLL = 16
MXU_TILE = 256
TERM_STRIDE = 2 * SSD_HEADS
HEAD_DIM_SHIFT = SSD_HEAD_DIM.bit_length() - 1
assert 1 << HEAD_DIM_SHIFT == SSD_HEAD_DIM == CHUNK
SCAN_BLOCK_TOKENS = 2048
SCAN_BLOCK_SEQS = 4


def _dot(a, b):
    return jnp.dot(a, b, preferred_element_type=F32)


def _dot_nt(a, b):
    return lax.dot_general(a, b, (((1,), (1,)), ((), ())), preferred_element_type=F32)


def _dot_tn(a, b):
    return lax.dot_general(a, b, (((0,), (0,)), ((), ())), preferred_element_type=F32)


def _silu(x):
    return x * jax.nn.sigmoid(x)


def _log1p_exp_neg_abs(x):
    return jnp.log(1.0 + jnp.exp(-jnp.abs(x)))


def _softplus(x):
    return jnp.maximum(x, 0.0) + _log1p_exp_neg_abs(x)


def _log_sigmoid(x):
    return jnp.minimum(x, 0.0) - _log1p_exp_neg_abs(x)


def _rmsnorm(x, w):
    ms = jnp.mean(x * x, axis=-1, keepdims=True)
    return x * lax.rsqrt(ms + EPS) * w


def _resident(shape):
    nd = len(shape)
    return pl.BlockSpec(shape, lambda *_: (0,) * nd, pipeline_mode=pl.Buffered(1))


def _params(n_axes):
    return pltpu.CompilerParams(dimension_semantics=("arbitrary",) * n_axes,
                                vmem_limit_bytes=VMEM_LIMIT_BYTES)


def _cast_plumbing(cast, n_steps, step_of):
    in_specs, out_specs, out_shape = [], [], []
    for w in cast:
        rows, cols = w.shape
        per_step = -(-rows // n_steps)
        per_step = -(-per_step // BF16_SUBLANES) * BF16_SUBLANES
        last = -(-rows // per_step) - 1
        blk = pl.BlockSpec((per_step, cols),
                           lambda *g, last=last: (jnp.minimum(step_of(*g), last), 0))
        in_specs.append(blk)
        out_specs.append(blk)
        out_shape.append(jax.ShapeDtypeStruct((rows, cols), BF16))
    return in_specs, out_specs, out_shape


def _cast_blocks(src_refs, dst_refs):
    for src_ref, dst_ref in zip(src_refs, dst_refs):
        dst_ref[...] = src_ref[...].astype(dst_ref.dtype)


def _ring_copy(src_hbm, buf, sem, k):
    slot = k % MOD_BUFS
    rows = buf.shape[1]
    return pltpu.make_async_copy(src_hbm.at[pl.ds(k * rows, rows), :], buf.at[slot], sem.at[slot])


def _mod_body(c_ref, w_hbm, b_ref, f_in_hbm, f_out_hbm, out_ref, win_ref, wout_ref,
              wbuf, ibuf, obuf, sem_w, sem_i, sem_o):
    streams = ((w_hbm, wbuf, sem_w), (f_in_hbm, ibuf, sem_i), (f_out_hbm, obuf, sem_o))
    for k in range(min(MOD_BUFS, MOD_CHUNKS)):
        for st in streams:
            _ring_copy(*st, k).start()
    a = _silu(c_ref[...]).astype(BF16)
    out_ref[...] = jnp.broadcast_to(b_ref[...], out_ref.shape)
    for k in range(MOD_CHUNKS):
        slot = k % MOD_BUFS
        for st in streams:
            _ring_copy(*st, k).wait()
        rw, ri, ro = wbuf.shape[1], ibuf.shape[1], obuf.shape[1]
        out_ref[...] += _dot(a[:, k * rw:(k + 1) * rw], wbuf[slot].astype(BF16))
        win_ref[k * ri:(k + 1) * ri, :] = ibuf[slot].astype(BF16)
        wout_ref[k * ro:(k + 1) * ro, :] = obuf[slot].astype(BF16)
        if k + MOD_BUFS < MOD_CHUNKS:
            for st in streams:
                _ring_copy(*st, k + MOD_BUFS).start()


def _mod_call(cc, w_mod, b_mod, f_in, f_out):
    n_rows = cc.shape[0]
    n_out = N_MOD * D_MODEL
    whole = lambda shape: pl.BlockSpec(shape, lambda: (0,) * len(shape))
    hbm = pl.BlockSpec(memory_space=pl.ANY)
    chunk = lambda w: w.shape[0] // MOD_CHUNKS
    assert all(w.shape[0] % (MOD_CHUNKS * BF16_SUBLANES) == 0 for w in (w_mod, f_in, f_out))
    return pl.pallas_call(
        _mod_body,
        in_specs=[whole((n_rows, D_MODEL)), hbm, whole((1, n_out)), hbm, hbm],
        out_specs=[whole((n_rows, n_out)), whole(f_in.shape), whole(f_out.shape)],
        out_shape=[jax.ShapeDtypeStruct((n_rows, n_out), F32),
                   jax.ShapeDtypeStruct(f_in.shape, BF16), jax.ShapeDtypeStruct(f_out.shape, BF16)],
        scratch_shapes=[pltpu.VMEM((MOD_BUFS, chunk(w_mod), n_out), F32),
                        pltpu.VMEM((MOD_BUFS, chunk(f_in), f_in.shape[1]), F32),
                        pltpu.VMEM((MOD_BUFS, chunk(f_out), f_out.shape[1]), F32),
                        pltpu.SemaphoreType.DMA((MOD_BUFS,)),
                        pltpu.SemaphoreType.DMA((MOD_BUFS,)),
                        pltpu.SemaphoreType.DMA((MOD_BUFS,))],
        compiler_params=pltpu.CompilerParams(vmem_limit_bytes=VMEM_LIMIT_BYTES),
        name="mod",
    )(cc, w_mod, b_mod, f_in, f_out)


def _ffn_body(*refs, sub, has_mix, has_final, n_cast, part_tokens):
    it = iter(refs)
    x_ref, mod_ref, nw_ref, win_ref, wout_ref = (next(it) for _ in range(5))
    if has_mix:
        o_ref, y_ref, z_ref, snw_ref, wo_ref = (next(it) for _ in range(5))
    if has_final:
        fn_ref = next(it)
    cast_src = [next(it) for _ in range(n_cast)]
    out_ref = next(it)
    _cast_blocks(cast_src, [next(it) for _ in range(n_cast)])

    for p0 in range(0, x_ref.shape[0], part_tokens):
        rs = slice(p0, p0 + part_tokens)
        x = x_ref[rs, :]
        if has_mix:
            g2 = mod_ref[0, 5:6, :]
            yn = _rmsnorm(y_ref[rs, :] * _silu(z_ref[rs, :]), snw_ref[...])
            m = _dot(jnp.concatenate([o_ref[rs, :], yn.astype(BF16)], axis=1), wo_ref[...])
            x = x + g2 * m
        sh = mod_ref[0, 3 * sub:3 * sub + 1, :]
        sc = mod_ref[0, 3 * sub + 1:3 * sub + 2, :]
        gate = mod_ref[0, 3 * sub + 2:3 * sub + 3, :]
        h = (_rmsnorm(x, nw_ref[...]) * (1.0 + sc) + sh).astype(BF16)
        g = _dot(h, win_ref[:, :D_FF])
        u = _dot(h, win_ref[:, D_FF:])
        act = (_silu(g) * u).astype(BF16)
        x = x + (0.5 * gate) * _dot(act, wout_ref[...])
        if has_final:
            x = _rmsnorm(x, fn_ref[...])
        out_ref[rs, :] = x


def _ffn_call(x, mod, norm_w, w_in, w_out, *, sub, seq_len, mix=None, final_w=None, cast=()):
    m_tok = x.shape[0]
    if mix is not None:
        tm, part = TOKEN_TILE, TOKEN_TILE
    elif cast:
        tm, part = TOKEN_TILE, TOKEN_TILE // 4
    else:
        tm, part = 2 * TOKEN_TILE, TOKEN_TILE
    tiles_per_seq = seq_len // tm
    shared_mod = mod.shape[0] == 1
    assert shared_mod or tiles_per_seq >= 1, "a block may not span differently modulated sequences"
    mod_map = (lambda i: (0, 0, 0)) if shared_mod else (lambda i: (i // tiles_per_seq, 0, 0))
    tok = lambda width: pl.BlockSpec((tm, width), lambda i: (i, 0))
    in_specs = [tok(D_MODEL), pl.BlockSpec((1, N_MOD, D_MODEL), mod_map),
                _resident((1, D_MODEL)), _resident(w_in.shape), _resident(w_out.shape)]
    args = [x, mod, norm_w, w_in, w_out]
    if mix is not None:
        o, y, proj, ssd_norm_w, w_mix_out = mix
        in_specs += [tok(GLA_V), tok(SSD_INNER),
                     pl.BlockSpec((tm, SSD_INNER), lambda i: (i, 0)),
                     _resident((1, SSD_INNER)), _resident(w_mix_out.shape)]
        args += [o, y, proj, ssd_norm_w, w_mix_out]
    if final_w is not None:
        in_specs.append(_resident((1, D_MODEL)))
        args.append(final_w)
    n_steps = m_tok // tm
    cast_in, cast_out, cast_shape = _cast_plumbing(cast, n_steps, lambda i: i)
    in_specs += cast_in
    args += list(cast)
    out_specs = [tok(D_MODEL)] + cast_out
    out_shape = [jax.ShapeDtypeStruct((m_tok, D_MODEL), F32)] + cast_shape
    body = functools.partial(_ffn_body, sub=sub, has_mix=mix is not None,
                             has_final=final_w is not None, n_cast=len(cast), part_tokens=part)
    outs = pl.pallas_call(
        body,
        grid=(n_steps,),
        in_specs=in_specs,
        out_specs=out_specs,
        out_shape=out_shape,
        compiler_params=_params(1),
        name="ffn_mix" if mix is not None else "ffn",
    )(*args)
    return outs if cast else outs[0]


def _proj_body(x_ref, mod_ref, nw_ref, w_ref, heads_ref, groups_ref, zs_ref):
    sh = mod_ref[0, 3:4, :]
    sc = mod_ref[0, 4:5, :]
    h = (_rmsnorm(x_ref[...], nw_ref[...]) * (1.0 + sc) + sh).astype(BF16)
    o_z = COL_Z + 2 * GLA_LOWRANK
    o_dt = o_z + SSD_INNER + SSD_CONV_DIM
    qkvr = _dot_nt(h, w_ref[0:COL_Z, :])
    for hd in range(GLA_HEADS):
        for dst, src, wid in ((0, COL_Q + hd * GLA_DK, GLA_DK), (HEAD_K, COL_K + hd * GLA_DK, GLA_DK),
                              (HEAD_V, COL_V + hd * GLA_DV, GLA_DV), (HEAD_R, COL_R + hd * GLA_DV, GLA_DV)):
            heads_ref[hd, :, dst:dst + wid] = qkvr[:, src:src + wid]
    zx = _dot_nt(h, w_ref[o_z:o_dt, :])
    zs_ref[:, 0:ZS_SMALL] = zx[:, 0:SSD_INNER]
    for g in range(SSD_GROUPS):
        for dst, src, wid in ((0, g * GRP_B, GRP_B), (GRP_B, SSD_INNER + g * SSD_STATE, SSD_STATE),
                              (GRP_C, SSD_INNER + SSD_BC + g * SSD_STATE, SSD_STATE)):
            groups_ref[g, :, dst:dst + wid] = zx[:, SSD_INNER + src:SSD_INNER + src + wid]
    w_small = jnp.concatenate(
        [w_ref[COL_Z:o_z, :], w_ref[o_dt:, :],
         jnp.zeros((LANES - 2 * GLA_LOWRANK - 2 * SSD_HEADS, D_MODEL), BF16)], axis=0)
    zs_ref[:, ZS_SMALL:ZS_W] = _dot_nt(h, w_small)


def _proj_call(x, mod, norm_w, w_all, *, seq_len):
    m_tok = x.shape[0]
    tm = TOKEN_TILE
    tiles_per_seq = seq_len // tm
    shared_mod = mod.shape[0] == 1
    mod_map = (lambda i: (0, 0, 0)) if shared_mod else (lambda i: (i // tiles_per_seq, 0, 0))
    return pl.pallas_call(
        _proj_body,
        grid=(m_tok // tm,),
        in_specs=[pl.BlockSpec((tm, D_MODEL), lambda i: (i, 0)),
                  pl.BlockSpec((1, N_MOD, D_MODEL), mod_map),
                  _resident((1, D_MODEL)), _resident(w_all.shape)],
        out_specs=[pl.BlockSpec((GLA_HEADS, tm, HEAD_W), lambda i: (0, i, 0)),
                   pl.BlockSpec((SSD_GROUPS, tm, GRP_W), lambda i: (0, i, 0)),
                   pl.BlockSpec((tm, ZS_W), lambda i: (i, 0))],
        out_shape=[jax.ShapeDtypeStruct((GLA_HEADS, m_tok, HEAD_W), F32),
                   jax.ShapeDtypeStruct((SSD_GROUPS, m_tok, GRP_W), F32),
                   jax.ShapeDtypeStruct((m_tok, ZS_W), F32)],
        compiler_params=_params(1),
        name="proj",
    )(x, mod, norm_w, w_all)


def _tri_masks():
    row = lax.broadcasted_iota(jnp.int32, (CHUNK, CHUNK), 0)
    col = lax.broadcasted_iota(jnp.int32, (CHUNK, CHUNK), 1)
    return row >= col, row <= col


def _gla_body(*refs, seq_len, n_sub, has_h0, want_state):
    it = iter(refs)
    hd_ref, sm_ref, wa_ref, ba_ref, nw_ref = (next(it) for _ in range(5))
    h0_ref = next(it) if has_h0 else None
    o_ref = next(it)
    st_ref = next(it) if want_state else None
    (la_scr, qk_scr, qg_scr, ks_scr, vb_scr, sc_scr, kv_scr, dec_scr,
     sprev_scr) = (next(it) for _ in range(9))

    nc = seq_len // CHUNK
    nct = n_sub * nc
    dk = GLA_DK
    lower, upper = _tri_masks()
    lower_b = jnp.where(lower, 1.0, 0.0).astype(BF16)
    upper_b = jnp.where(upper, 1.0, 0.0).astype(BF16)
    unroll = min(GLA_UNROLL, nct)
    rows = lambda c: pl.ds(pl.multiple_of(c * CHUNK, CHUNK), CHUNK)

    pre = _dot(sm_ref[...].astype(BF16), wa_ref[...].astype(BF16)) + ba_ref[...]
    la_scr[...] = _log_sigmoid(pre) * (LOG2_E / GLA_TAU)

    def prep(c, carry):
        sl = rows(c)
        qc = hd_ref[0, sl, 0:HEAD_K] * (GLA_DK ** -0.5)
        kc = hd_ref[0, sl, HEAD_K:HEAD_V]
        vb_scr[sl, :] = hd_ref[0, sl, HEAD_V:HEAD_R].astype(BF16)
        la = la_scr[sl, :]
        hi = la.astype(BF16)
        lo = (la - hi.astype(F32)).astype(BF16)

        def cumulative(tri_b, a):
            p = _dot(tri_b, jnp.concatenate([hi[:, a:a + dk], lo[:, a:a + dk]], axis=1))
            return p[:, 0:dk] + p[:, dk:2 * dk]

        g_f = cumulative(lower_b, 0)
        g_b = cumulative(upper_b, dk)
        mid_f = g_f[CHUNK // 2:CHUNK // 2 + 1, :]
        mid_b = g_b[CHUNK - 1 - CHUNK // 2:CHUNK - CHUNK // 2, :]
        end_f = g_f[CHUNK - 1:CHUNK, :]
        end_b = g_b[0:1, :]
        qk_scr[sl, 0:dk] = (qc * jnp.exp2(g_f - mid_f)).astype(BF16)
        qk_scr[sl, dk:2 * dk] = (qc * jnp.exp2(g_b - mid_b)).astype(BF16)
        qk_scr[sl, 2 * dk:3 * dk] = (kc * jnp.exp2(mid_f - g_f)).astype(BF16)
        qk_scr[sl, 3 * dk:4 * dk] = (kc * jnp.exp2(mid_b - g_b)).astype(BF16)
        qg_scr[sl, 0:dk] = (qc * jnp.exp2(g_f)).astype(BF16)
        qg_scr[sl, dk:2 * dk] = (qc * jnp.exp2(g_b)).astype(BF16)
        ks_scr[sl, 0:dk] = (kc * jnp.exp2(end_f - g_f)).astype(BF16)
        ks_scr[sl, dk:2 * dk] = (kc * jnp.exp2(end_b - g_b)).astype(BF16)
        dec = jnp.exp2(jnp.concatenate([end_f, end_b], axis=1))
        dec_scr[c] = jnp.broadcast_to(dec, (SUBLANES, 2 * dk))
        return carry

    lax.fori_loop(0, nct, prep, 0, unroll=unroll)

    def products(c, carry):
        sl = rows(c)
        s_f = _dot_nt(qk_scr[sl, 0:dk], qk_scr[sl, 2 * dk:3 * dk])
        s_b = _dot_nt(qk_scr[sl, dk:2 * dk], qk_scr[sl, 3 * dk:4 * dk])
        sc_scr[sl, :] = (jnp.where(lower, s_f, 0.0) + jnp.where(upper, s_b, 0.0)).astype(BF16)
        kv_scr[c] = _dot_tn(vb_scr[sl, :], ks_scr[sl, :])
        return carry

    lax.fori_loop(0, nct, products, 0, unroll=unroll)

    for b in range(n_sub):
        for d, a in ((0, 0), (1, dk)):
            state0 = h0_ref[b, 0, d, 0].T if has_h0 else jnp.zeros((GLA_DV, dk), F32)

            def recur(i, state, b=b, d=d, a=a):
                c = b * nc + (i if d == 0 else nc - 1 - i)
                sprev_scr[c, :, a:a + dk] = state.astype(BF16)
                return state * dec_scr[c, 0:1, a:a + dk] + kv_scr[c, :, a:a + dk]

            state = lax.fori_loop(0, nc, recur, state0, unroll=min(nc, 4))
            if want_state:
                st_ref[b, 0, d, 0] = state.T

    def finish(c, carry):
        sl = rows(c)
        o = _dot(sc_scr[sl, :], vb_scr[sl, :]) + _dot_nt(qg_scr[sl, :], sprev_scr[c])
        o_ref[sl, :] = (_rmsnorm(o, nw_ref[...]) * _silu(hd_ref[0, sl, HEAD_R:HEAD_W])).astype(o_ref.dtype)
        return carry

    lax.fori_loop(0, nct, finish, 0, unroll=unroll)


def _gla_call(heads, zs, wa_cat, ba_cat, norm_w, h0, layer, *, n_seq, seq_len, want_state):
    n_sub = max(1, min(SCAN_BLOCK_TOKENS // seq_len, SCAN_BLOCK_SEQS))
    L = n_sub * seq_len
    nc = L // CHUNK
    in_specs = [pl.BlockSpec((1, L, HEAD_W), lambda s, h: (h, s, 0)),
                pl.BlockSpec((L, LANES), lambda s, h: (s, ZS_SMALL // LANES)),
                pl.BlockSpec((LANES, 2 * GLA_DK), lambda s, h: (0, h)),
                pl.BlockSpec((1, 2 * GLA_DK), lambda s, h: (0, h)),
                pl.BlockSpec((1, GLA_DV), lambda s, h: (0, 0))]
    args = [heads, zs, wa_cat, ba_cat, norm_w]
    state_blk = (n_sub, 1, 2, 1, GLA_DK, GLA_DV)
    if h0 is not None:
        in_specs.append(pl.BlockSpec(state_blk, lambda s, h: (s, layer, 0, h, 0, 0)))
        args.append(h0)
    out_specs = [pl.BlockSpec((L, GLA_DV), lambda s, h: (s, h))]
    out_shape = [jax.ShapeDtypeStruct((n_seq * seq_len, GLA_V), BF16)]
    if want_state:
        out_specs.append(pl.BlockSpec(state_blk, lambda s, h: (s, 0, 0, h, 0, 0)))
        out_shape.append(jax.ShapeDtypeStruct((n_seq, 1, 2, GLA_HEADS, GLA_DK, GLA_DV), F32))
    body = functools.partial(_gla_body, seq_len=seq_len, n_sub=n_sub, has_h0=h0 is not None,
                             want_state=want_state)
    return pl.pallas_call(
        body,
        grid=(n_seq // n_sub, GLA_HEADS),
        in_specs=in_specs,
        out_specs=out_specs,
        out_shape=out_shape,
        scratch_shapes=[pltpu.VMEM((L, 2 * GLA_DK), F32),
                        pltpu.VMEM((L, 4 * GLA_DK), BF16),
                        pltpu.VMEM((L, 2 * GLA_DK), BF16),
                        pltpu.VMEM((L, 2 * GLA_DK), BF16),
                        pltpu.VMEM((L, GLA_DV), BF16),
                        pltpu.VMEM((L, CHUNK), BF16),
                        pltpu.VMEM((nc, GLA_DV, 2 * GLA_DK), F32),
                        pltpu.VMEM((nc, SUBLANES, 2 * GLA_DK), F32),
                        pltpu.VMEM((nc, GLA_DV, 2 * GLA_DK), BF16)],
        compiler_params=_params(2),
        name="gla",
    )(*args)


def _ssd_body(*refs, seq_len, grid_rows, has_h0, want_state, n_cast):
    it = iter(refs)
    (grp_ref, sm_ref, cwx_ref, cwb_ref, cwc_ref, cbx_ref, cbb_ref, cbc_ref,
     par_ref, dsk_ref) = (next(it) for _ in range(10))
    h0_ref = next(it) if has_h0 else None
    cast_src = [next(it) for _ in range(n_cast)]
    y_ref = next(it)
    st_ref = next(it) if want_state else None
    _cast_blocks(cast_src, [next(it) for _ in range(n_cast)])
    (pad_scr, xs_s, b_s, c_s, dt_scr, xy_scr, tr_scr, rows_scr, sc_scr, ed_scr, xw_scr, dec_scr,
     cs_scr, sprev_scr, state_scr) = (next(it) for _ in range(15))

    L = seq_len
    nc = L // CHUNK
    width = L // grid_rows
    pad = pad_scr.shape[0] - L
    pad //= 2
    grp = pl.program_id(1)

    def conv_into(c0, cw_ref, cb_ref, dst_ref):
        ch = dst_ref.shape[1]
        pad_scr[0:pad, 0:ch] = jnp.zeros((pad, ch), F32)
        pad_scr[pad + L:pad + L + pad, 0:ch] = jnp.zeros((pad, ch), F32)
        pad_scr[pad:pad + L, 0:ch] = grp_ref[0, :, c0:c0 + ch]
        rc = min(2 * CHUNK, L)
        col = lax.broadcasted_iota(jnp.int32, (rc, ch), 0) % width
        di_taps = range(CONV_K) if grid_rows > 1 else (CONV_K // 2,)
        for r0 in range(0, L, rc):
            acc = jnp.broadcast_to(cb_ref[...], (rc, ch))
            for dj in range(CONV_K):
                inner = None
                for di in di_taps:
                    off = pad + r0 + (di - 1) * width + (dj - 1)
                    term = cw_ref[di, dj:dj + 1, :] * pad_scr[off:off + rc, 0:ch]
                    inner = term if inner is None else inner + term
                if grid_rows > 1 and dj == 0:
                    inner = jnp.where(col >= 1, inner, 0.0)
                if grid_rows > 1 and dj == CONV_K - 1:
                    inner = jnp.where(col <= width - 2, inner, 0.0)
                acc = acc + inner
            dst_ref[r0:r0 + rc, :] = _silu(acc).astype(dst_ref.dtype)

    conv_into(0, cwx_ref, cbx_ref, xs_s)
    conv_into(GRP_B, cwb_ref, cbb_ref, b_s)
    conv_into(GRP_C, cwc_ref, cbc_ref, c_s)

    gw = GROUP_W
    unroll = min(SSD_UNROLL, nc)
    rows = lambda c: pl.ds(pl.multiple_of(c * CHUNK, CHUNK), CHUNK)
    bias_row = par_ref[0:1, :]
    a_row = -jnp.exp(par_ref[1:2, :]) * par_ref[2:3, :] * LOG2_E
    lower, upper = _tri_masks()
    lower_b = jnp.where(lower, 1.0, 0.0).astype(BF16)
    upper_b = jnp.where(upper, 1.0, 0.0).astype(BF16)
    lane = lax.broadcasted_iota(jnp.int32, (CHUNK, LANES), 1)
    dt_lanes = (lane >= SM_DTF) & (lane < SM_DTF + TERM_STRIDE)
    src = lax.broadcasted_iota(jnp.int32, (LANES, 2 * gw), 0)
    dst = lax.broadcasted_iota(jnp.int32, (LANES, 2 * gw), 1)
    dst_slot = (grp * HEADS_PER_GROUP
                + (lax.shift_right_logical(dst, HEAD_DIM_SHIFT) & (HEADS_PER_GROUP - 1))
                + jnp.where(dst >= gw, SSD_HEADS, 0))
    expand = jnp.where((src >= SM_DTF) & ((src & (TERM_STRIDE - 1)) == dst_slot),
                       1.0, 0.0).astype(BF16)
    t_idx = lax.broadcasted_iota(jnp.int32, (CHUNK, gw), 0)
    s_idx = lax.broadcasted_iota(jnp.int32, (CHUNK, gw), 1) & (CHUNK - 1)
    diag = t_idx == s_idx
    blk_r = lax.shift_right_logical(lax.broadcasted_iota(jnp.int32, (MXU_TILE, MXU_TILE), 0),
                                    HEAD_DIM_SHIFT)
    blk_c = lax.shift_right_logical(lax.broadcasted_iota(jnp.int32, (MXU_TILE, MXU_TILE), 1),
                                    HEAD_DIM_SHIFT)
    same_head = blk_r == blk_c

    dt_scr[...] = _softplus(sm_ref[...] + bias_row)

    def place3(v):
        hi = v.astype(BF16).astype(F32)
        rest = v - hi
        mid = rest.astype(BF16).astype(F32)
        lo = rest - mid
        keep = lambda t: jnp.where(dt_lanes, t, 0.0)
        out = (keep(hi) + pltpu.roll(keep(mid), TERM_STRIDE, axis=1)
               + pltpu.roll(keep(lo), 2 * TERM_STRIDE, axis=1))
        return out.astype(BF16)

    def cumulate(c, carry):
        sl = rows(c)
        dt = dt_scr[sl, :]
        da = dt * a_row
        hi = da.astype(BF16)
        both = jnp.concatenate([hi, (da - hi.astype(F32)).astype(BF16)], axis=1)
        pf = _dot(lower_b, both)
        pb = _dot(upper_b, both)
        cum = jnp.where(lane < SM_DTB, pf[:, 0:LANES] + pf[:, LANES:], pb[:, 0:LANES] + pb[:, LANES:])
        xy_scr[sl, 0:LANES] = place3(cum)
        xy_scr[sl, LANES:2 * LANES] = place3(dt)
        tr_scr[c, 0] = cum.T
        tr_scr[c, 1] = dt.T
        head0 = pl.multiple_of(SM_DTF + grp * HEADS_PER_GROUP, SUBLANES)
        for q, (k, d) in enumerate(((0, 0), (0, 1), (1, 0), (1, 1))):
            slab = tr_scr[c, k, pl.ds(head0 + d * SSD_HEADS, HEADS_PER_GROUP), :]
            row = jnp.concatenate(
                [jnp.broadcast_to(slab[h:h + 1, :], (SUBLANES, CHUNK))
                 for h in range(HEADS_PER_GROUP)], axis=1)
            rows_scr[c, :, q * gw:(q + 1) * gw] = row
        return carry

    lax.fori_loop(0, nc, cumulate, 0, unroll=unroll)

    def weights(c, carry):
        sl = rows(c)
        xc = xs_s[sl, :]
        cum_e = _dot(xy_scr[sl, 0:LANES], expand)
        dt_e = _dot(xy_scr[sl, LANES:2 * LANES], expand)
        cb = _dot_nt(c_s[sl, :], jnp.concatenate([b_s[sl, :]] * HEADS_PER_GROUP, axis=0))
        segs, dt_rows, dec_rows = [], [], []
        for d, (a, last_i) in enumerate(((0, CHUNK - 1), (gw, 0))):
            ce = cum_e[:, a:a + gw]
            de = dt_e[:, a:a + gw]
            segs.append(ce - rows_scr[c, 0:1, d * gw:(d + 1) * gw])
            dt_rows.append(rows_scr[c, 0:1, (2 + d) * gw:(3 + d) * gw])
            cum_last = ce[last_i:last_i + 1, :]
            ed_scr[sl, a:a + gw] = jnp.exp2(ce)
            xw_scr[sl, a:a + gw] = (xc * (jnp.exp2(cum_last - ce) * de)).astype(BF16)
            dec_rows.append(jnp.exp2(cum_last))
        fwd = t_idx >= s_idx
        w = (jnp.exp2(jnp.where(fwd, segs[0], segs[1])) * jnp.where(fwd, dt_rows[0], dt_rows[1])
             + jnp.where(diag, dt_rows[1], 0.0))
        sc_scr[sl, :] = (cb * w).astype(BF16)
        dec_scr[c] = jnp.broadcast_to(jnp.concatenate(dec_rows, axis=1), (SUBLANES, 2 * gw))
        return carry

    lax.fori_loop(0, nc, weights, 0, unroll=unroll)

    def products(c, carry):
        sl = rows(c)
        xc = xs_s[sl, :]
        xb = xc.astype(BF16)
        parts = []
        for j in range(gw // MXU_TILE):
            xh = xb[:, j * MXU_TILE:(j + 1) * MXU_TILE]
            rep = jnp.concatenate([xh] * (MXU_TILE // SSD_HEAD_DIM), axis=0)
            bd = jnp.where(same_head, rep, jnp.zeros_like(rep))
            parts.append(_dot(sc_scr[sl, j * MXU_TILE:(j + 1) * MXU_TILE], bd))
        y_ref[sl, :] = jnp.concatenate(parts, axis=1) + xc * dsk_ref[...]
        cs_scr[c] = _dot_tn(b_s[sl, :], xw_scr[sl, :])
        return carry

    lax.fori_loop(0, nc, products, 0, unroll=unroll)

    pair_w = 2 * SSD_HEAD_DIM
    if has_h0:
        for d in range(2):
            for j in range(HEADS_PER_GROUP // 2):
                pair = jnp.concatenate([h0_ref[0, 0, d, 2 * j], h0_ref[0, 0, d, 2 * j + 1]], axis=0)
                state_scr[:, d * gw + j * pair_w:d * gw + (j + 1) * pair_w] = pair.T
    else:
        state_scr[...] = jnp.zeros_like(state_scr)

    for d in range(2):
        for a in range(d * gw, (d + 1) * gw, MXU_TILE):

            def recur(i, state, d=d, a=a):
                c = i if d == 0 else nc - 1 - i
                sprev_scr[c, :, a:a + MXU_TILE] = state.astype(BF16)
                return (state * dec_scr[c, 0:1, a:a + MXU_TILE]
                        + cs_scr[c, :, a:a + MXU_TILE])

            state_scr[:, a:a + MXU_TILE] = lax.fori_loop(
                0, nc, recur, state_scr[:, a:a + MXU_TILE], unroll=min(nc, 4))

    def finish(c, carry):
        sl = rows(c)
        yi = _dot(c_s[sl, :], sprev_scr[c]) * ed_scr[sl, :]
        y_ref[sl, :] += yi[:, 0:gw] + yi[:, gw:2 * gw]
        return carry

    lax.fori_loop(0, nc, finish, 0, unroll=unroll)

    if want_state:
        for d in range(2):
            for j in range(HEADS_PER_GROUP // 2):
                pair = state_scr[:, d * gw + j * pair_w:d * gw + (j + 1) * pair_w].T
                st_ref[0, 0, d, 2 * j] = pair[0:SSD_HEAD_DIM, :]
                st_ref[0, 0, d, 2 * j + 1] = pair[SSD_HEAD_DIM:pair_w, :]


def _ssd_call(groups, zs, conv_w, conv_b, par, dsk, h0, layer, *, n_seq, seq_len, grid_rows,
              want_state, cast=()):
    L = seq_len
    in_specs = [pl.BlockSpec((1, L, GRP_W), lambda s, g: (g, s, 0)),
                pl.BlockSpec((L, LANES), lambda s, g: (s, ZS_SMALL // LANES)),
                pl.BlockSpec((CONV_K, CONV_K, GROUP_W), lambda s, g: (0, 0, g)),
                pl.BlockSpec((CONV_K, CONV_K, SSD_STATE),
                             lambda s, g: (0, 0, SSD_INNER // SSD_STATE + g)),
                pl.BlockSpec((CONV_K, CONV_K, SSD_STATE),
                             lambda s, g: (0, 0, (SSD_INNER + SSD_BC) // SSD_STATE + g)),
                pl.BlockSpec((1, GROUP_W), lambda s, g: (0, g)),
                pl.BlockSpec((1, SSD_STATE), lambda s, g: (0, SSD_INNER // SSD_STATE + g)),
                pl.BlockSpec((1, SSD_STATE),
                             lambda s, g: (0, (SSD_INNER + SSD_BC) // SSD_STATE + g)),
                pl.BlockSpec((SUBLANES, LANES), lambda s, g: (0, 0)),
                pl.BlockSpec((1, GROUP_W), lambda s, g: (0, g))]
    args = [groups, zs, conv_w, conv_w, conv_w, conv_b, conv_b, conv_b, par, dsk]
    state_blk = (1, 1, 2, HEADS_PER_GROUP, SSD_HEAD_DIM, SSD_STATE)
    if h0 is not None:
        in_specs.append(pl.BlockSpec(state_blk, lambda s, g: (s, layer, 0, g, 0, 0)))
        args.append(h0)
    out_specs = [pl.BlockSpec((L, GROUP_W), lambda s, g: (s, g))]
    out_shape = [jax.ShapeDtypeStruct((n_seq * L, SSD_INNER), F32)]
    if want_state:
        out_specs.append(pl.BlockSpec(state_blk, lambda s, g: (s, 0, 0, g, 0, 0)))
        out_shape.append(jax.ShapeDtypeStruct(
            (n_seq, 1, 2, SSD_HEADS, SSD_HEAD_DIM, SSD_STATE), F32))
    cast_in, cast_out, cast_shape = _cast_plumbing(cast, n_seq * SSD_GROUPS,
                                                   lambda s, g: s * SSD_GROUPS + g)
    in_specs += cast_in
    args += list(cast)
    out_specs += cast_out
    out_shape += cast_shape
    nc = L // CHUNK
    conv_pad = (L // grid_rows + SUBLANES) if grid_rows > 1 else SUBLANES
    body = functools.partial(_ssd_body, seq_len=L, grid_rows=grid_rows,
                             has_h0=h0 is not None, want_state=want_state, n_cast=len(cast))
    return pl.pallas_call(
        body,
        grid=(n_seq, SSD_GROUPS),
        in_specs=in_specs,
        out_specs=out_specs,
        out_shape=out_shape,
        scratch_shapes=[pltpu.VMEM((L + 2 * conv_pad, GROUP_W), F32),
                        pltpu.VMEM((L, GROUP_W), F32),
                        pltpu.VMEM((L, SSD_STATE), BF16),
                        pltpu.VMEM((L, SSD_STATE), BF16),
                        pltpu.VMEM((L, LANES), F32),
                        pltpu.VMEM((L, 2 * LANES), BF16),
                        pltpu.VMEM((nc, 2, LANES, CHUNK), F32),
                        pltpu.VMEM((nc, SUBLANES, 4 * GROUP_W), F32),
                        pltpu.VMEM((L, GROUP_W), BF16),
                        pltpu.VMEM((L, 2 * GROUP_W), F32),
                        pltpu.VMEM((L, 2 * GROUP_W), BF16),
                        pltpu.VMEM((nc, SUBLANES, 2 * GROUP_W), F32),
                        pltpu.VMEM((nc, SSD_STATE, 2 * GROUP_W), F32),
                        pltpu.VMEM((nc, SSD_STATE, 2 * GROUP_W), BF16),
                        pltpu.VMEM((SSD_STATE, 2 * GROUP_W), F32)],
        compiler_params=_params(2),
        name="ssd",
    )(*args)


def _layer_path(x, mod, lw, raw, h0_gla, h0_ssd, layer, *, n_seq, seq_len, grid_rows,
                want_state, final_w):
    if raw is not None:
        x1, lw["w_all"] = _ffn_call(x, mod, lw["norm_ffn1"], lw["ffn1_w_in"], lw["ffn1_w_out"],
                                    sub=0, seq_len=seq_len, cast=(raw["w_in"],))
    else:
        x1 = _ffn_call(x, mod, lw["norm_ffn1"], lw["ffn1_w_in"], lw["ffn1_w_out"],
                       sub=0, seq_len=seq_len)
    heads, groups, zs = _proj_call(x1, mod, lw["norm_mix"], lw["w_all"], seq_len=seq_len)
    gla_out = _gla_call(heads, zs, lw["wa_cat"], lw["ba_cat"], lw["gla_norm_w"], h0_gla, layer,
                        n_seq=n_seq, seq_len=seq_len, want_state=want_state)
    late = ("ffn2_w_in", "ffn2_w_out", "w_out")
    ssd_out = _ssd_call(groups, zs, lw["conv_w"], lw["conv_b"], lw["ssd_par"], lw["d_skip_row"],
                        h0_ssd, layer, n_seq=n_seq, seq_len=seq_len, grid_rows=grid_rows,
                        want_state=want_state,
                        cast=tuple(raw[k] for k in late) if raw is not None else ())
    if raw is not None:
        lw.update(zip(late, ssd_out[-len(late):]))
    o, y = gla_out[0], ssd_out[0]
    out = _ffn_call(x1, mod, lw["norm_ffn2"], lw["ffn2_w_in"], lw["ffn2_w_out"],
                    sub=2, seq_len=seq_len, mix=(o, y, zs, lw["ssd_norm_w"], lw["w_out"]),
                    final_w=final_w)
    if want_state:
        return out, gla_out[1], ssd_out[1]
    return out, None, None


def kernel(x_prompt, x_sample, state_gla, state_ssd, c, c_ctx, norm_ffn1, norm_mix, norm_ffn2, w_mod, b_mod, ffn1_w_in, ffn1_w_out, ffn2_w_in, ffn2_w_out, w_in, gla_w_a2, gla_b_a, gla_norm_w, conv_w, conv_b, dt_bias, a_log, d_skip, ssd_norm_w, w_out, final_norm):
    nb, seq, _ = x_prompt.shape
    db, dseq, _ = x_sample.shape
    grid_rows = dseq // GRID_W
    xp = x_prompt.reshape(nb * seq, D_MODEL)
    xs = x_sample.reshape(db * dseq, D_MODEL)
    row = lambda v: v.reshape(1, -1)
    gla_states, ssd_states = [], []
    for i in range(DEPTH):
        last = i == DEPTH - 1
        n_rows = -(-(db + 1) // SUBLANES) * SUBLANES
        cc = jnp.concatenate([c, c_ctx[None, :], jnp.zeros((n_rows - db - 1, D_MODEL), F32)], 0)
        mod, ffn1_in_bf, ffn1_out_bf = _mod_call(cc, w_mod[i], row(b_mod[i]), ffn1_w_in[i],
                                                 ffn1_w_out[i])
        mod = mod.reshape(n_rows, N_MOD, D_MODEL)
        mod_lat, mod_ctx = mod[:db], mod[db:db + 1]

        wa_cat = jnp.stack(
            [jnp.pad(gla_w_a2[i, d].reshape(GLA_LOWRANK, GLA_HEADS, GLA_DK),
                     ((lo, LANES - lo - GLA_LOWRANK), (0, 0), (0, 0)))
             for d, lo in ((0, SM_AF), (1, SM_AB))], axis=2).reshape(LANES, 2 * GLA_QK)
        ba_cat = jnp.stack([gla_b_a[i, 0].reshape(GLA_HEADS, GLA_DK),
                            gla_b_a[i, 1].reshape(GLA_HEADS, GLA_DK)], axis=1).reshape(1, 2 * GLA_QK)
        ssd_par = jnp.pad(
            jnp.stack([dt_bias[i].reshape(-1), a_log[i].reshape(-1), jnp.ones((2 * SSD_HEADS,), F32)]),
            ((0, SUBLANES - 3), (SM_DTF, LANES - SM_DTF - 2 * SSD_HEADS)))
        lw = {
            "norm_ffn1": row(norm_ffn1[i]), "norm_mix": row(norm_mix[i]),
            "norm_ffn2": row(norm_ffn2[i]),
            "ffn1_w_in": ffn1_in_bf, "ffn1_w_out": ffn1_out_bf,
            "wa_cat": wa_cat, "ba_cat": ba_cat,
            "gla_norm_w": row(gla_norm_w[i]), "conv_w": conv_w[i], "conv_b": row(conv_b[i]),
            "ssd_par": ssd_par, "d_skip_row": row(jnp.repeat(d_skip[i], SSD_HEAD_DIM)),
            "ssd_norm_w": row(ssd_norm_w[i]),
        }
        raw = {"w_in": jnp.swapaxes(w_in[i], 0, 1), "ffn2_w_in": ffn2_w_in[i], "ffn2_w_out": ffn2_w_out[i],
               "w_out": w_out[i]}
        fw = row(final_norm) if last else None

        xp, sg, ss = _layer_path(xp, mod_ctx, lw, raw, None, None, i, n_seq=nb, seq_len=seq,
                                 grid_rows=1, want_state=True, final_w=fw)
        xs, _, _ = _layer_path(xs, mod_lat, lw, None, state_gla, state_ssd, i, n_seq=db,
                               seq_len=dseq, grid_rows=grid_rows, want_state=False, final_w=fw)
        gla_states.append(sg)
        ssd_states.append(ss)
    y_prompt = xp.reshape(nb, seq, D_MODEL)
    y_sample = xs.reshape(db, dseq, D_MODEL)
    return (y_prompt, y_sample, jnp.concatenate(gla_states, axis=1),
            jnp.concatenate(ssd_states, axis=1))
```

```python
import functools

import jax
import jax.numpy as jnp
from jax import lax
from jax.experimental import pallas as pl
from jax.experimental.pallas import tpu as pltpu

F32 = jnp.float32
BF16 = jnp.bfloat16

D_MODEL = 1024
DEPTH = 1
GRID_W = 64
CHUNK = 64
EPS = 1e-6
N_MOD = 9
D_FF = 2816
GLA_HEADS = 4
GLA_DK = 128
GLA_DV = 256
GLA_LOWRANK = 16
GLA_TAU = 16.0
GLA_QK = GLA_HEADS * GLA_DK
GLA_V = GLA_HEADS * GLA_DV
SSD_HEADS = 16
SSD_HEAD_DIM = 64
SSD_GROUPS = 2
SSD_STATE = 128
SSD_INNER = SSD_HEADS * SSD_HEAD_DIM
SSD_BC = SSD_GROUPS * SSD_STATE
SSD_CONV_DIM = SSD_INNER + 2 * SSD_BC
CONV_K = 3
D_MIX = GLA_V + SSD_INNER

LOG2_E = 1.4426950408889634
LANES = 128
SUBLANES = 8
BF16_SUBLANES = 16
VMEM_LIMIT_BYTES = 56 * 1024 * 1024

COL_Q = 0
COL_K = COL_Q + GLA_QK
COL_V = COL_K + GLA_QK
COL_R = COL_V + GLA_V
COL_Z = COL_R + GLA_V
HEAD_K = GLA_DK
HEAD_V = 2 * GLA_DK
HEAD_R = HEAD_V + GLA_DV
HEAD_W = HEAD_R + GLA_DV
GRP_B = SSD_INNER // SSD_GROUPS
GRP_C = GRP_B + SSD_STATE
GRP_W = GRP_C + SSD_STATE
ZS_SMALL = SSD_INNER
ZS_W = ZS_SMALL + LANES
SM_AF = 0
SM_AB = SM_AF + GLA_LOWRANK
SM_DTF = SM_AB + GLA_LOWRANK
SM_DTB = SM_DTF + SSD_HEADS
HEADS_PER_GROUP = SSD_HEADS // SSD_GROUPS
GROUP_W = GRP_B

TOKEN_TILE = 512
MOD_CHUNKS = 8
MOD_BUFS = 3
GLA_UNROLL = 32
SSD_UNROLL = 16
MXU_TILE = 256
TERM_STRIDE = 2 * SSD_HEADS
HEAD_DIM_SHIFT = SSD_HEAD_DIM.bit_length() - 1
assert 1 << HEAD_DIM_SHIFT == SSD_HEAD_DIM == CHUNK
SCAN_BLOCK_TOKENS = 2048
SCAN_BLOCK_SEQS = 4


def _dot(a, b):
    return jnp.dot(a, b, preferred_element_type=F32)


def _dot_nt(a, b):
    return lax.dot_general(a, b, (((1,), (1,)), ((), ())), preferred_element_type=F32)


def _dot_tn(a, b):
    return lax.dot_general(a, b, (((0,), (0,)), ((), ())), preferred_element_type=F32)


def _silu(x):
    return x * jax.nn.sigmoid(x)


def _log1p_exp_neg_abs(x):
    return jnp.log(1.0 + jnp.exp(-jnp.abs(x)))


def _softplus(x):
    return jnp.maximum(x, 0.0) + _log1p_exp_neg_abs(x)


def _log_sigmoid(x):
    return jnp.minimum(x, 0.0) - _log1p_exp_neg_abs(x)


def _rmsnorm(x, w):
    ms = jnp.mean(x * x, axis=-1, keepdims=True)
    return x * lax.rsqrt(ms + EPS) * w


def _resident(shape):
    nd = len(shape)
    return pl.BlockSpec(shape, lambda *_: (0,) * nd, pipeline_mode=pl.Buffered(1))


def _params(n_axes):
    return pltpu.CompilerParams(dimension_semantics=("parallel",) + ("arbitrary",) * (n_axes - 1),
                                vmem_limit_bytes=VMEM_LIMIT_BYTES)


def _cast_plumbing(cast, n_steps, step_of):
    in_specs, out_specs, out_shape = [], [], []
    for w in cast:
        rows, cols = w.shape
        per_step = -(-rows // n_steps)
        per_step = -(-per_step // BF16_SUBLANES) * BF16_SUBLANES
        last = -(-rows // per_step) - 1
        blk = pl.BlockSpec((per_step, cols),
                           lambda *g, last=last: (jnp.minimum(step_of(*g), last), 0))
        in_specs.append(blk)
        out_specs.append(blk)
        out_shape.append(jax.ShapeDtypeStruct((rows, cols), BF16))
    return in_specs, out_specs, out_shape


def _cast_blocks(src_refs, dst_refs):
    for src_ref, dst_ref in zip(src_refs, dst_refs):
        dst_ref[...] = src_ref[...].astype(dst_ref.dtype)


def _ring_copy(src_hbm, buf, sem, k):
    slot = k % MOD_BUFS
    rows = buf.shape[1]
    return pltpu.make_async_copy(src_hbm.at[pl.ds(k * rows, rows), :], buf.at[slot], sem.at[slot])


def _mod_body(c_ref, w_hbm, b_ref, f_in_hbm, f_out_hbm, out_ref, win_ref, wout_ref,
              wbuf, ibuf, obuf, sem_w, sem_i, sem_o):
    streams = ((w_hbm, wbuf, sem_w), (f_in_hbm, ibuf, sem_i), (f_out_hbm, obuf, sem_o))
    for k in range(min(MOD_BUFS, MOD_CHUNKS)):
        for st in streams:
            _ring_copy(*st, k).start()
    a = _silu(c_ref[...]).astype(BF16)
    out_ref[...] = jnp.broadcast_to(b_ref[...], out_ref.shape)
    for k in range(MOD_CHUNKS):
        slot = k % MOD_BUFS
        for st in streams:
            _ring_copy(*st, k).wait()
        rw, ri, ro = wbuf.shape[1], ibuf.shape[1], obuf.shape[1]
        out_ref[...] += _dot(a[:, k * rw:(k + 1) * rw], wbuf[slot].astype(BF16))
        win_ref[k * ri:(k + 1) * ri, :] = ibuf[slot].astype(BF16)
        wout_ref[k * ro:(k + 1) * ro, :] = obuf[slot].astype(BF16)
        if k + MOD_BUFS < MOD_CHUNKS:
            for st in streams:
                _ring_copy(*st, k + MOD_BUFS).start()


def _mod_call(cc, w_mod, b_mod, f_in, f_out):
    n_rows = cc.shape[0]
    n_out = N_MOD * D_MODEL
    whole = lambda shape: pl.BlockSpec(shape, lambda: (0,) * len(shape))
    hbm = pl.BlockSpec(memory_space=pl.ANY)
    chunk = lambda w: w.shape[0] // MOD_CHUNKS
    assert all(w.shape[0] % (MOD_CHUNKS * BF16_SUBLANES) == 0 for w in (w_mod, f_in, f_out))
    return pl.pallas_call(
        _mod_body,
        in_specs=[whole((n_rows, D_MODEL)), hbm, whole((1, n_out)), hbm, hbm],
        out_specs=[whole((n_rows, n_out)), whole(f_in.shape), whole(f_out.shape)],
        out_shape=[jax.ShapeDtypeStruct((n_rows, n_out), F32),
                   jax.ShapeDtypeStruct(f_in.shape, BF16), jax.ShapeDtypeStruct(f_out.shape, BF16)],
        scratch_shapes=[pltpu.VMEM((MOD_BUFS, chunk(w_mod), n_out), F32),
                        pltpu.VMEM((MOD_BUFS, chunk(f_in), f_in.shape[1]), F32),
                        pltpu.VMEM((MOD_BUFS, chunk(f_out), f_out.shape[1]), F32),
                        pltpu.SemaphoreType.DMA((MOD_BUFS,)),
                        pltpu.SemaphoreType.DMA((MOD_BUFS,)),
                        pltpu.SemaphoreType.DMA((MOD_BUFS,))],
        compiler_params=pltpu.CompilerParams(vmem_limit_bytes=VMEM_LIMIT_BYTES),
        name="mod",
    )(cc, w_mod, b_mod, f_in, f_out)


def _ffn_body(*refs, sub, has_mix, has_final, n_cast, part_tokens):
    it = iter(refs)
    x_ref, mod_ref, nw_ref, win_ref, wout_ref = (next(it) for _ in range(5))
    if has_mix:
        o_ref, y_ref, z_ref, snw_ref, wo_ref = (next(it) for _ in range(5))
    if has_final:
        fn_ref = next(it)
    cast_src = [next(it) for _ in range(n_cast)]
    out_ref = next(it)
    _cast_blocks(cast_src, [next(it) for _ in range(n_cast)])

    for p0 in range(0, x_ref.shape[0], part_tokens):
        rs = slice(p0, p0 + part_tokens)
        x = x_ref[rs, :]
        if has_mix:
            g2 = mod_ref[0, 5:6, :]
            yn = _rmsnorm(y_ref[rs, :] * _silu(z_ref[rs, :]), snw_ref[...])
            m = _dot(jnp.concatenate([o_ref[rs, :], yn.astype(BF16)], axis=1), wo_ref[...])
            x = x + g2 * m
        sh = mod_ref[0, 3 * sub:3 * sub + 1, :]
        sc = mod_ref[0, 3 * sub + 1:3 * sub + 2, :]
        gate = mod_ref[0, 3 * sub + 2:3 * sub + 3, :]
        h = (_rmsnorm(x, nw_ref[...]) * (1.0 + sc) + sh).astype(BF16)
        g = _dot(h, win_ref[:, :D_FF])
        u = _dot(h, win_ref[:, D_FF:])
        act = (_silu(g) * u).astype(BF16)
        x = x + (0.5 * gate) * _dot(act, wout_ref[...])
        if has_final:
            x = _rmsnorm(x, fn_ref[...])
        out_ref[rs, :] = x


def _ffn_call(x, mod, norm_w, w_in, w_out, *, sub, seq_len, mix=None, final_w=None, cast=()):
    m_tok = x.shape[0]
    if mix is not None:
        tm, part = TOKEN_TILE, TOKEN_TILE
    elif cast:
        tm, part = TOKEN_TILE, TOKEN_TILE // 4
    else:
        tm, part = 2 * TOKEN_TILE, TOKEN_TILE
    tiles_per_seq = seq_len // tm
    shared_mod = mod.shape[0] == 1
    assert shared_mod or tiles_per_seq >= 1, "a block may not span differently modulated sequences"
    mod_map = (lambda i: (0, 0, 0)) if shared_mod else (lambda i: (i // tiles_per_seq, 0, 0))
    tok = lambda width: pl.BlockSpec((tm, width), lambda i: (i, 0))
    in_specs = [tok(D_MODEL), pl.BlockSpec((1, N_MOD, D_MODEL), mod_map),
                _resident((1, D_MODEL)), _resident(w_in.shape), _resident(w_out.shape)]
    args = [x, mod, norm_w, w_in, w_out]
    if mix is not None:
        o, y, proj, ssd_norm_w, w_mix_out = mix
        in_specs += [tok(GLA_V), tok(SSD_INNER),
                     pl.BlockSpec((tm, SSD_INNER), lambda i: (i, 0)),
                     _resident((1, SSD_INNER)), _resident(w_mix_out.shape)]
        args += [o, y, proj, ssd_norm_w, w_mix_out]
    if final_w is not None:
        in_specs.append(_resident((1, D_MODEL)))
        args.append(final_w)
    n_steps = m_tok // tm
    cast_in, cast_out, cast_shape = _cast_plumbing(cast, n_steps, lambda i: i)
    in_specs += cast_in
    args += list(cast)
    out_specs = [tok(D_MODEL)] + cast_out
    out_shape = [jax.ShapeDtypeStruct((m_tok, D_MODEL), F32)] + cast_shape
    body = functools.partial(_ffn_body, sub=sub, has_mix=mix is not None,
                             has_final=final_w is not None, n_cast=len(cast), part_tokens=part)
    outs = pl.pallas_call(
        body,
        grid=(n_steps,),
        in_specs=in_specs,
        out_specs=out_specs,
        out_shape=out_shape,
        compiler_params=_params(1),
        name="ffn_mix" if mix is not None else "ffn",
    )(*args)
    return outs if cast else outs[0]


def _proj_body(x_ref, mod_ref, nw_ref, w_ref, heads_ref, groups_ref, zs_ref):
    sh = mod_ref[0, 3:4, :]
    sc = mod_ref[0, 4:5, :]
    h = (_rmsnorm(x_ref[...], nw_ref[...]) * (1.0 + sc) + sh).astype(BF16)
    o_z = COL_Z + 2 * GLA_LOWRANK
    o_dt = o_z + SSD_INNER + SSD_CONV_DIM
    qkvr = _dot_nt(h, w_ref[0:COL_Z, :])
    for hd in range(GLA_HEADS):
        for dst, src, wid in ((0, COL_Q + hd * GLA_DK, GLA_DK), (HEAD_K, COL_K + hd * GLA_DK, GLA_DK),
                              (HEAD_V, COL_V + hd * GLA_DV, GLA_DV), (HEAD_R, COL_R + hd * GLA_DV, GLA_DV)):
            heads_ref[hd, :, dst:dst + wid] = qkvr[:, src:src + wid]
    zx = _dot_nt(h, w_ref[o_z:o_dt, :])
    zs_ref[:, 0:ZS_SMALL] = zx[:, 0:SSD_INNER]
    for g in range(SSD_GROUPS):
        for dst, src, wid in ((0, g * GRP_B, GRP_B), (GRP_B, SSD_INNER + g * SSD_STATE, SSD_STATE),
                              (GRP_C, SSD_INNER + SSD_BC + g * SSD_STATE, SSD_STATE)):
            groups_ref[g, :, dst:dst + wid] = zx[:, SSD_INNER + src:SSD_INNER + src + wid]
    w_small = jnp.concatenate(
        [w_ref[COL_Z:o_z, :], w_ref[o_dt:, :],
         jnp.zeros((LANES - 2 * GLA_LOWRANK - 2 * SSD_HEADS, D_MODEL), BF16)], axis=0)
    zs_ref[:, ZS_SMALL:ZS_W] = _dot_nt(h, w_small)


def _proj_call(x, mod, norm_w, w_all, *, seq_len):
    m_tok = x.shape[0]
    tm = TOKEN_TILE
    tiles_per_seq = seq_len // tm
    shared_mod = mod.shape[0] == 1
    mod_map = (lambda i: (0, 0, 0)) if shared_mod else (lambda i: (i // tiles_per_seq, 0, 0))
    return pl.pallas_call(
        _proj_body,
        grid=(m_tok // tm,),
        in_specs=[pl.BlockSpec((tm, D_MODEL), lambda i: (i, 0)),
                  pl.BlockSpec((1, N_MOD, D_MODEL), mod_map),
                  _resident((1, D_MODEL)), _resident(w_all.shape)],
        out_specs=[pl.BlockSpec((GLA_HEADS, tm, HEAD_W), lambda i: (0, i, 0)),
                   pl.BlockSpec((SSD_GROUPS, tm, GRP_W), lambda i: (0, i, 0)),
                   pl.BlockSpec((tm, ZS_W), lambda i: (i, 0))],
        out_shape=[jax.ShapeDtypeStruct((GLA_HEADS, m_tok, HEAD_W), F32),
                   jax.ShapeDtypeStruct((SSD_GROUPS, m_tok, GRP_W), F32),
                   jax.ShapeDtypeStruct((m_tok, ZS_W), F32)],
        compiler_params=_params(1),
        name="proj",
    )(x, mod, norm_w, w_all)


def _tri_masks():
    row = lax.broadcasted_iota(jnp.int32, (CHUNK, CHUNK), 0)
    col = lax.broadcasted_iota(jnp.int32, (CHUNK, CHUNK), 1)
    return row >= col, row <= col


def _gla_body(*refs, seq_len, n_sub, has_h0, want_state):
    it = iter(refs)
    hd_ref, sm_ref, wa_ref, ba_ref, nw_ref = (next(it) for _ in range(5))
    h0_ref = next(it) if has_h0 else None
    o_ref = next(it)
    st_ref = next(it) if want_state else None
    (la_scr, qk_scr, qg_scr, ks_scr, vb_scr, sc_scr, kv_scr, dec_scr,
     sprev_scr) = (next(it) for _ in range(9))

    nc = seq_len // CHUNK
    nct = n_sub * nc
    dk = GLA_DK
    lower, upper = _tri_masks()
    lower_b = jnp.where(lower, 1.0, 0.0).astype(BF16)
    upper_b = jnp.where(upper, 1.0, 0.0).astype(BF16)
    unroll = min(GLA_UNROLL, nct)
    rows = lambda c: pl.ds(pl.multiple_of(c * CHUNK, CHUNK), CHUNK)

    pre = _dot(sm_ref[...].astype(BF16), wa_ref[...].astype(BF16)) + ba_ref[...]
    la_scr[...] = _log_sigmoid(pre) * (LOG2_E / GLA_TAU)

    def prep(c, carry):
        sl = rows(c)
        qc = hd_ref[0, sl, 0:HEAD_K] * (GLA_DK ** -0.5)
        kc = hd_ref[0, sl, HEAD_K:HEAD_V]
        vb_scr[sl, :] = hd_ref[0, sl, HEAD_V:HEAD_R].astype(BF16)
        la = la_scr[sl, :]
        hi = la.astype(BF16)
        lo = (la - hi.astype(F32)).astype(BF16)

        def cumulative(tri_b, a):
            p = _dot(tri_b, jnp.concatenate([hi[:, a:a + dk], lo[:, a:a + dk]], axis=1))
            return p[:, 0:dk] + p[:, dk:2 * dk]

        g_f = cumulative(lower_b, 0)
        g_b = cumulative(upper_b, dk)
        mid_f = g_f[CHUNK // 2:CHUNK // 2 + 1, :]
        mid_b = g_b[CHUNK - 1 - CHUNK // 2:CHUNK - CHUNK // 2, :]
        end_f = g_f[CHUNK - 1:CHUNK, :]
        end_b = g_b[0:1, :]
        qk_scr[sl, 0:dk] = (qc * jnp.exp2(g_f - mid_f)).astype(BF16)
        qk_scr[sl, dk:2 * dk] = (qc * jnp.exp2(g_b - mid_b)).astype(BF16)
        qk_scr[sl, 2 * dk:3 * dk] = (kc * jnp.exp2(mid_f - g_f)).astype(BF16)
        qk_scr[sl, 3 * dk:4 * dk] = (kc * jnp.exp2(mid_b - g_b)).astype(BF16)
        qg_scr[sl, 0:dk] = (qc * jnp.exp2(g_f)).astype(BF16)
        qg_scr[sl, dk:2 * dk] = (qc * jnp.exp2(g_b)).astype(BF16)
        ks_scr[sl, 0:dk] = (kc * jnp.exp2(end_f - g_f)).astype(BF16)
        ks_scr[sl, dk:2 * dk] = (kc * jnp.exp2(end_b - g_b)).astype(BF16)
        dec = jnp.exp2(jnp.concatenate([end_f, end_b], axis=1))
        dec_scr[c] = jnp.broadcast_to(dec, (SUBLANES, 2 * dk))
        return carry

    lax.fori_loop(0, nct, prep, 0, unroll=unroll)

    def products(c, carry):
        sl = rows(c)
        s_f = _dot_nt(qk_scr[sl, 0:dk], qk_scr[sl, 2 * dk:3 * dk])
        s_b = _dot_nt(qk_scr[sl, dk:2 * dk], qk_scr[sl, 3 * dk:4 * dk])
        sc_scr[sl, :] = (jnp.where(lower, s_f, 0.0) + jnp.where(upper, s_b, 0.0)).astype(BF16)
        kv_scr[c] = _dot_tn(vb_scr[sl, :], ks_scr[sl, :])
        return carry

    lax.fori_loop(0, nct, products, 0, unroll=unroll)

    for b in range(n_sub):
        for d, a in ((0, 0), (1, dk)):
            state0 = h0_ref[b, 0, d, 0].T if has_h0 else jnp.zeros((GLA_DV, dk), F32)

            def recur(i, state, b=b, d=d, a=a):
                c = b * nc + (i if d == 0 else nc - 1 - i)
                sprev_scr[c, :, a:a + dk] = state.astype(BF16)
                return state * dec_scr[c, 0:1, a:a + dk] + kv_scr[c, :, a:a + dk]

            state = lax.fori_loop(0, nc, recur, state0, unroll=min(nc, 4))
            if want_state:
                st_ref[b, 0, d, 0] = state.T

    def finish(c, carry):
        sl = rows(c)
        o = _dot(sc_scr[sl, :], vb_scr[sl, :]) + _dot_nt(qg_scr[sl, :], sprev_scr[c])
        o_ref[sl, :] = (_rmsnorm(o, nw_ref[...]) * _silu(hd_ref[0, sl, HEAD_R:HEAD_W])).astype(o_ref.dtype)
        return carry

    lax.fori_loop(0, nct, finish, 0, unroll=unroll)


def _gla_call(heads, zs, wa_cat, ba_cat, norm_w, h0, layer, *, n_seq, seq_len, want_state):
    n_sub = max(1, min(SCAN_BLOCK_TOKENS // seq_len, SCAN_BLOCK_SEQS))
    L = n_sub * seq_len
    nc = L // CHUNK
    in_specs = [pl.BlockSpec((1, L, HEAD_W), lambda s, h: (h, s, 0)),
                pl.BlockSpec((L, LANES), lambda s, h: (s, ZS_SMALL // LANES)),
                pl.BlockSpec((LANES, 2 * GLA_DK), lambda s, h: (0, h)),
                pl.BlockSpec((1, 2 * GLA_DK), lambda s, h: (0, h)),
                pl.BlockSpec((1, GLA_DV), lambda s, h: (0, 0))]
    args = [heads, zs, wa_cat, ba_cat, norm_w]
    state_blk = (n_sub, 1, 2, 1, GLA_DK, GLA_DV)
    if h0 is not None:
        in_specs.append(pl.BlockSpec(state_blk, lambda s, h: (s, layer, 0, h, 0, 0)))
        args.append(h0)
    out_specs = [pl.BlockSpec((L, GLA_DV), lambda s, h: (s, h))]
    out_shape = [jax.ShapeDtypeStruct((n_seq * seq_len, GLA_V), BF16)]
    if want_state:
        out_specs.append(pl.BlockSpec(state_blk, lambda s, h: (s, 0, 0, h, 0, 0)))
        out_shape.append(jax.ShapeDtypeStruct((n_seq, 1, 2, GLA_HEADS, GLA_DK, GLA_DV), F32))
    body = functools.partial(_gla_body, seq_len=seq_len, n_sub=n_sub, has_h0=h0 is not None,
                             want_state=want_state)
    return pl.pallas_call(
        body,
        grid=(n_seq // n_sub, GLA_HEADS),
        in_specs=in_specs,
        out_specs=out_specs,
        out_shape=out_shape,
        scratch_shapes=[pltpu.VMEM((L, 2 * GLA_DK), F32),
                        pltpu.VMEM((L, 4 * GLA_DK), BF16),
                        pltpu.VMEM((L, 2 * GLA_DK), BF16),
                        pltpu.VMEM((L, 2 * GLA_DK), BF16),
                        pltpu.VMEM((L, GLA_DV), BF16),
                        pltpu.VMEM((L, CHUNK), BF16),
                        pltpu.VMEM((nc, GLA_DV, 2 * GLA_DK), F32),
                        pltpu.VMEM((nc, SUBLANES, 2 * GLA_DK), F32),
                        pltpu.VMEM((nc, GLA_DV, 2 * GLA_DK), BF16)],
        compiler_params=_params(2),
        name="gla",
    )(*args)


def _ssd_body(*refs, seq_len, grid_rows, has_h0, want_state, n_cast):
    it = iter(refs)
    (grp_ref, sm_ref, cwx_ref, cwb_ref, cwc_ref, cbx_ref, cbb_ref, cbc_ref,
     par_ref, dsk_ref) = (next(it) for _ in range(10))
    h0_ref = next(it) if has_h0 else None
    cast_src = [next(it) for _ in range(n_cast)]
    y_ref = next(it)
    st_ref = next(it) if want_state else None
    _cast_blocks(cast_src, [next(it) for _ in range(n_cast)])
    (pad_scr, xs_s, b_s, c_s, dt_scr, xy_scr, tr_scr, rows_scr, sc_scr, ed_scr, xw_scr, dec_scr,
     cs_scr, sprev_scr, state_scr) = (next(it) for _ in range(15))

    L = seq_len
    nc = L // CHUNK
    width = L // grid_rows
    pad = pad_scr.shape[0] - L
    pad //= 2
    grp = pl.program_id(1)

    def conv_into(c0, cw_ref, cb_ref, dst_ref):
        ch = dst_ref.shape[1]
        pad_scr[0:pad, 0:ch] = jnp.zeros((pad, ch), F32)
        pad_scr[pad + L:pad + L + pad, 0:ch] = jnp.zeros((pad, ch), F32)
        pad_scr[pad:pad + L, 0:ch] = grp_ref[0, :, c0:c0 + ch]
        rc = min(2 * CHUNK, L)
        col = lax.broadcasted_iota(jnp.int32, (rc, ch), 0) % width
        di_taps = range(CONV_K) if grid_rows > 1 else (CONV_K // 2,)
        for r0 in range(0, L, rc):
            acc = jnp.broadcast_to(cb_ref[...], (rc, ch))
            for dj in range(CONV_K):
                inner = None
                for di in di_taps:
                    off = pad + r0 + (di - 1) * width + (dj - 1)
                    term = cw_ref[di, dj:dj + 1, :] * pad_scr[off:off + rc, 0:ch]
                    inner = term if inner is None else inner + term
                if grid_rows > 1 and dj == 0:
                    inner = jnp.where(col >= 1, inner, 0.0)
                if grid_rows > 1 and dj == CONV_K - 1:
                    inner = jnp.where(col <= width - 2, inner, 0.0)
                acc = acc + inner
            dst_ref[r0:r0 + rc, :] = _silu(acc).astype(dst_ref.dtype)

    conv_into(0, cwx_ref, cbx_ref, xs_s)
    conv_into(GRP_B, cwb_ref, cbb_ref, b_s)
    conv_into(GRP_C, cwc_ref, cbc_ref, c_s)

    gw = GROUP_W
    unroll = min(SSD_UNROLL, nc)
    rows = lambda c: pl.ds(pl.multiple_of(c * CHUNK, CHUNK), CHUNK)
    bias_row = par_ref[0:1, :]
    a_row = -jnp.exp(par_ref[1:2, :]) * par_ref[2:3, :] * LOG2_E
    lower, upper = _tri_masks()
    lower_b = jnp.where(lower, 1.0, 0.0).astype(BF16)
    upper_b = jnp.where(upper, 1.0, 0.0).astype(BF16)
    lane = lax.broadcasted_iota(jnp.int32, (CHUNK, LANES), 1)
    dt_lanes = (lane >= SM_DTF) & (lane < SM_DTF + TERM_STRIDE)
    src = lax.broadcasted_iota(jnp.int32, (LANES, 2 * gw), 0)
    dst = lax.broadcasted_iota(jnp.int32, (LANES, 2 * gw), 1)
    dst_slot = (grp * HEADS_PER_GROUP
                + (lax.shift_right_logical(dst, HEAD_DIM_SHIFT) & (HEADS_PER_GROUP - 1))
                + jnp.where(dst >= gw, SSD_HEADS, 0))
    expand = jnp.where((src >= SM_DTF) & ((src & (TERM_STRIDE - 1)) == dst_slot),
                       1.0, 0.0).astype(BF16)
    t_idx = lax.broadcasted_iota(jnp.int32, (CHUNK, gw), 0)
    s_idx = lax.broadcasted_iota(jnp.int32, (CHUNK, gw), 1) & (CHUNK - 1)
    diag = t_idx == s_idx
    blk_r = lax.shift_right_logical(lax.broadcasted_iota(jnp.int32, (MXU_TILE, MXU_TILE), 0),
                                    HEAD_DIM_SHIFT)
    blk_c = lax.shift_right_logical(lax.broadcasted_iota(jnp.int32, (MXU_TILE, MXU_TILE), 1),
                                    HEAD_DIM_SHIFT)
    same_head = blk_r == blk_c

    dt_scr[...] = _softplus(sm_ref[...] + bias_row)

    def place3(v):
        hi = v.astype(BF16).astype(F32)
        rest = v - hi
        mid = rest.astype(BF16).astype(F32)
        lo = rest - mid
        keep = lambda t: jnp.where(dt_lanes, t, 0.0)
        out = (keep(hi) + pltpu.roll(keep(mid), TERM_STRIDE, axis=1)
               + pltpu.roll(keep(lo), 2 * TERM_STRIDE, axis=1))
        return out.astype(BF16)

    def cumulate(c, carry):
        sl = rows(c)
        dt = dt_scr[sl, :]
        da = dt * a_row
        hi = da.astype(BF16)
        both = jnp.concatenate([hi, (da - hi.astype(F32)).astype(BF16)], axis=1)
        pf = _dot(lower_b, both)
        pb = _dot(upper_b, both)
        cum = jnp.where(lane < SM_DTB, pf[:, 0:LANES] + pf[:, LANES:], pb[:, 0:LANES] + pb[:, LANES:])
        xy_scr[sl, 0:LANES] = place3(cum)
        xy_scr[sl, LANES:2 * LANES] = place3(dt)
        tr_scr[c, 0] = cum.T
        tr_scr[c, 1] = dt.T
        head0 = pl.multiple_of(SM_DTF + grp * HEADS_PER_GROUP, SUBLANES)
        for q, (k, d) in enumerate(((0, 0), (0, 1), (1, 0), (1, 1))):
            slab = tr_scr[c, k, pl.ds(head0 + d * SSD_HEADS, HEADS_PER_GROUP), :]
            row = jnp.concatenate(
                [jnp.broadcast_to(slab[h:h + 1, :], (SUBLANES, CHUNK))
                 for h in range(HEADS_PER_GROUP)], axis=1)
            rows_scr[c, :, q * gw:(q + 1) * gw] = row
        return carry

    lax.fori_loop(0, nc, cumulate, 0, unroll=unroll)

    def weights(c, carry):
        sl = rows(c)
        xc = xs_s[sl, :]
        cum_e = _dot(xy_scr[sl, 0:LANES], expand)
        dt_e = _dot(xy_scr[sl, LANES:2 * LANES], expand)
        cb = _dot_nt(c_s[sl, :], jnp.concatenate([b_s[sl, :]] * HEADS_PER_GROUP, axis=0))
        segs, dt_rows, dec_rows = [], [], []
        for d, (a, last_i) in enumerate(((0, CHUNK - 1), (gw, 0))):
            ce = cum_e[:, a:a + gw]
            de = dt_e[:, a:a + gw]
            segs.append(ce - rows_scr[c, 0:1, d * gw:(d + 1) * gw])
            dt_rows.append(rows_scr[c, 0:1, (2 + d) * gw:(3 + d) * gw])
            cum_last = ce[last_i:last_i + 1, :]
            ed_scr[sl, a:a + gw] = jnp.exp2(ce)
            xw_scr[sl, a:a + gw] = (xc * (jnp.exp2(cum_last - ce) * de)).astype(BF16)
            dec_rows.append(jnp.exp2(cum_last))
        fwd = t_idx >= s_idx
        w = (jnp.exp2(jnp.where(fwd, segs[0], segs[1])) * jnp.where(fwd, dt_rows[0], dt_rows[1])
             + jnp.where(diag, dt_rows[1], 0.0))
        sc_scr[sl, :] = (cb * w).astype(BF16)
        dec_scr[c] = jnp.broadcast_to(jnp.concatenate(dec_rows, axis=1), (SUBLANES, 2 * gw))
        return carry

    lax.fori_loop(0, nc, weights, 0, unroll=unroll)

    def products(c, carry):
        sl = rows(c)
        xc = xs_s[sl, :]
        xb = xc.astype(BF16)
        parts = []
        for j in range(gw // MXU_TILE):
            xh = xb[:, j * MXU_TILE:(j + 1) * MXU_TILE]
            rep = jnp.concatenate([xh] * (MXU_TILE // SSD_HEAD_DIM), axis=0)
            bd = jnp.where(same_head, rep, jnp.zeros_like(rep))
            parts.append(_dot(sc_scr[sl, j * MXU_TILE:(j + 1) * MXU_TILE], bd))
        y_ref[sl, :] = jnp.concatenate(parts, axis=1) + xc * dsk_ref[...]
        cs_scr[c] = _dot_tn(b_s[sl, :], xw_scr[sl, :])
        return carry

    lax.fori_loop(0, nc, products, 0, unroll=unroll)

    pair_w = 2 * SSD_HEAD_DIM
    if has_h0:
        for d in range(2):
            for j in range(HEADS_PER_GROUP // 2):
                pair = jnp.concatenate([h0_ref[0, 0, d, 2 * j], h0_ref[0, 0, d, 2 * j + 1]], axis=0)
                state_scr[:, d * gw + j * pair_w:d * gw + (j + 1) * pair_w] = pair.T
    else:
        state_scr[...] = jnp.zeros_like(state_scr)

    for d in range(2):
        for a in range(d * gw, (d + 1) * gw, MXU_TILE):

            def recur(i, state, d=d, a=a):
                c = i if d == 0 else nc - 1 - i
                sprev_scr[c, :, a:a + MXU_TILE] = state.astype(BF16)
                return (state * dec_scr[c, 0:1, a:a + MXU_TILE]
                        + cs_scr[c, :, a:a + MXU_TILE])

            state_scr[:, a:a + MXU_TILE] = lax.fori_loop(
                0, nc, recur, state_scr[:, a:a + MXU_TILE], unroll=min(nc, 4))

    def finish(c, carry):
        sl = rows(c)
        yi = _dot(c_s[sl, :], sprev_scr[c]) * ed_scr[sl, :]
        y_ref[sl, :] += yi[:, 0:gw] + yi[:, gw:2 * gw]
        return carry

    lax.fori_loop(0, nc, finish, 0, unroll=unroll)

    if want_state:
        for d in range(2):
            for j in range(HEADS_PER_GROUP // 2):
                pair = state_scr[:, d * gw + j * pair_w:d * gw + (j + 1) * pair_w].T
                st_ref[0, 0, d, 2 * j] = pair[0:SSD_HEAD_DIM, :]
                st_ref[0, 0, d, 2 * j + 1] = pair[SSD_HEAD_DIM:pair_w, :]


def _ssd_call(groups, zs, conv_w, conv_b, par, dsk, h0, layer, *, n_seq, seq_len, grid_rows,
              want_state, cast=()):
    L = seq_len
    in_specs = [pl.BlockSpec((1, L, GRP_W), lambda s, g: (g, s, 0)),
                pl.BlockSpec((L, LANES), lambda s, g: (s, ZS_SMALL // LANES)),
                pl.BlockSpec((CONV_K, CONV_K, GROUP_W), lambda s, g: (0, 0, g)),
                pl.BlockSpec((CONV_K, CONV_K, SSD_STATE),
                             lambda s, g: (0, 0, SSD_INNER // SSD_STATE + g)),
                pl.BlockSpec((CONV_K, CONV_K, SSD_STATE),
                             lambda s, g: (0, 0, (SSD_INNER + SSD_BC) // SSD_STATE + g)),
                pl.BlockSpec((1, GROUP_W), lambda s, g: (0, g)),
                pl.BlockSpec((1, SSD_STATE), lambda s, g: (0, SSD_INNER // SSD_STATE + g)),
                pl.BlockSpec((1, SSD_STATE),
                             lambda s, g: (0, (SSD_INNER + SSD_BC) // SSD_STATE + g)),
                pl.BlockSpec((SUBLANES, LANES), lambda s, g: (0, 0)),
                pl.BlockSpec((1, GROUP_W), lambda s, g: (0, g))]
    args = [groups, zs, conv_w, conv_w, conv_w, conv_b, conv_b, conv_b, par, dsk]
    state_blk = (1, 1, 2, HEADS_PER_GROUP, SSD_HEAD_DIM, SSD_STATE)
    if h0 is not None:
        in_specs.append(pl.BlockSpec(state_blk, lambda s, g: (s, layer, 0, g, 0, 0)))
        args.append(h0)
    out_specs = [pl.BlockSpec((L, GROUP_W), lambda s, g: (s, g))]
    out_shape = [jax.ShapeDtypeStruct((n_seq * L, SSD_INNER), F32)]
    if want_state:
        out_specs.append(pl.BlockSpec(state_blk, lambda s, g: (s, 0, 0, g, 0, 0)))
        out_shape.append(jax.ShapeDtypeStruct(
            (n_seq, 1, 2, SSD_HEADS, SSD_HEAD_DIM, SSD_STATE), F32))
    cast_in, cast_out, cast_shape = _cast_plumbing(cast, n_seq * SSD_GROUPS,
                                                   lambda s, g: s * SSD_GROUPS + g)
    in_specs += cast_in
    args += list(cast)
    out_specs += cast_out
    out_shape += cast_shape
    nc = L // CHUNK
    conv_pad = (L // grid_rows + SUBLANES) if grid_rows > 1 else SUBLANES
    body = functools.partial(_ssd_body, seq_len=L, grid_rows=grid_rows,
                             has_h0=h0 is not None, want_state=want_state, n_cast=len(cast))
    return pl.pallas_call(
        body,
        grid=(n_seq, SSD_GROUPS),
        in_specs=in_specs,
        out_specs=out_specs,
        out_shape=out_shape,
        scratch_shapes=[pltpu.VMEM((L + 2 * conv_pad, GROUP_W), F32),
                        pltpu.VMEM((L, GROUP_W), F32),
                        pltpu.VMEM((L, SSD_STATE), BF16),
                        pltpu.VMEM((L, SSD_STATE), BF16),
                        pltpu.VMEM((L, LANES), F32),
                        pltpu.VMEM((L, 2 * LANES), BF16),
                        pltpu.VMEM((nc, 2, LANES, CHUNK), F32),
                        pltpu.VMEM((nc, SUBLANES, 4 * GROUP_W), F32),
                        pltpu.VMEM((L, GROUP_W), BF16),
                        pltpu.VMEM((L, 2 * GROUP_W), F32),
                        pltpu.VMEM((L, 2 * GROUP_W), BF16),
                        pltpu.VMEM((nc, SUBLANES, 2 * GROUP_W), F32),
                        pltpu.VMEM((nc, SSD_STATE, 2 * GROUP_W), F32),
                        pltpu.VMEM((nc, SSD_STATE, 2 * GROUP_W), BF16),
                        pltpu.VMEM((SSD_STATE, 2 * GROUP_W), F32)],
        compiler_params=_params(2),
        name="ssd",
    )(*args)


def _layer_path(x, mod, lw, raw, h0_gla, h0_ssd, layer, *, n_seq, seq_len, grid_rows,
                want_state, final_w):
    if raw is not None:
        x1, lw["w_all"] = _ffn_call(x, mod, lw["norm_ffn1"], lw["ffn1_w_in"], lw["ffn1_w_out"],
                                    sub=0, seq_len=seq_len, cast=(raw["w_in"],))
    else:
        x1 = _ffn_call(x, mod, lw["norm_ffn1"], lw["ffn1_w_in"], lw["ffn1_w_out"],
                       sub=0, seq_len=seq_len)
    heads, groups, zs = _proj_call(x1, mod, lw["norm_mix"], lw["w_all"], seq_len=seq_len)
    gla_out = _gla_call(heads, zs, lw["wa_cat"], lw["ba_cat"], lw["gla_norm_w"], h0_gla, layer,
                        n_seq=n_seq, seq_len=seq_len, want_state=want_state)
    late = ("ffn2_w_in", "ffn2_w_out", "w_out")
    ssd_out = _ssd_call(groups, zs, lw["conv_w"], lw["conv_b"], lw["ssd_par"], lw["d_skip_row"],
                        h0_ssd, layer, n_seq=n_seq, seq_len=seq_len, grid_rows=grid_rows,
                        want_state=want_state,
                        cast=tuple(raw[k] for k in late) if raw is not None else ())
    if raw is not None:
        lw.update(zip(late, ssd_out[-len(late):]))
    o, y = gla_out[0], ssd_out[0]
    out = _ffn_call(x1, mod, lw["norm_ffn2"], lw["ffn2_w_in"], lw["ffn2_w_out"],
                    sub=2, seq_len=seq_len, mix=(o, y, zs, lw["ssd_norm_w"], lw["w_out"]),
                    final_w=final_w)
    if want_state:
        return out, gla_out[1], ssd_out[1]
    return out, None, None


def kernel(x_prompt, x_sample, state_gla, state_ssd, c, c_ctx, norm_ffn1, norm_mix, norm_ffn2, w_mod, b_mod, ffn1_w_in, ffn1_w_out, ffn2_w_in, ffn2_w_out, w_in, gla_w_a2, gla_b_a, gla_norm_w, conv_w, conv_b, dt_bias, a_log, d_skip, ssd_norm_w, w_out, final_norm):
    nb, seq, _ = x_prompt.shape
    db, dseq, _ = x_sample.shape
    grid_rows = dseq // GRID_W
    xp = x_prompt.reshape(nb * seq, D_MODEL)
    xs = x_sample.reshape(db * dseq, D_MODEL)
    row = lambda v: v.reshape(1, -1)
    gla_states, ssd_states = [], []
    for i in range(DEPTH):
        last = i == DEPTH - 1
        n_rows = -(-(db + 1) // SUBLANES) * SUBLANES
        cc = jnp.concatenate([c, c_ctx[None, :], jnp.zeros((n_rows - db - 1, D_MODEL), F32)], 0)
        mod, ffn1_in_bf, ffn1_out_bf = _mod_call(cc, w_mod[i], row(b_mod[i]), ffn1_w_in[i],
                                                 ffn1_w_out[i])
        mod = mod.reshape(n_rows, N_MOD, D_MODEL)
        mod_lat, mod_ctx = mod[:db], mod[db:db + 1]

        wa_cat = jnp.stack(
            [jnp.pad(gla_w_a2[i, d].reshape(GLA_LOWRANK, GLA_HEADS, GLA_DK),
                     ((lo, LANES - lo - GLA_LOWRANK), (0, 0), (0, 0)))
             for d, lo in ((0, SM_AF), (1, SM_AB))], axis=2).reshape(LANES, 2 * GLA_QK)
        ba_cat = jnp.stack([gla_b_a[i, 0].reshape(GLA_HEADS, GLA_DK),
                            gla_b_a[i, 1].reshape(GLA_HEADS, GLA_DK)], axis=1).reshape(1, 2 * GLA_QK)
        ssd_par = jnp.pad(
            jnp.stack([dt_bias[i].reshape(-1), a_log[i].reshape(-1), jnp.ones((2 * SSD_HEADS,), F32)]),
            ((0, SUBLANES - 3), (SM_DTF, LANES - SM_DTF - 2 * SSD_HEADS)))
        lw = {
            "norm_ffn1": row(norm_ffn1[i]), "norm_mix": row(norm_mix[i]),
            "norm_ffn2": row(norm_ffn2[i]),
            "ffn1_w_in": ffn1_in_bf, "ffn1_w_out": ffn1_out_bf,
            "wa_cat": wa_cat, "ba_cat": ba_cat,
            "gla_norm_w": row(gla_norm_w[i]), "conv_w": conv_w[i], "conv_b": row(conv_b[i]),
            "ssd_par": ssd_par, "d_skip_row": row(jnp.repeat(d_skip[i], SSD_HEAD_DIM)),
            "ssd_norm_w": row(ssd_norm_w[i]),
        }
        raw = {"w_in": jnp.swapaxes(w_in[i], 0, 1), "ffn2_w_in": ffn2_w_in[i], "ffn2_w_out": ffn2_w_out[i],
               "w_out": w_out[i]}
        fw = row(final_norm) if last else None

        xp, sg, ss = _layer_path(xp, mod_ctx, lw, raw, None, None, i, n_seq=nb, seq_len=seq,
                                 grid_rows=1, want_state=True, final_w=fw)
        xs, _, _ = _layer_path(xs, mod_lat, lw, None, state_gla, state_ssd, i, n_seq=db,
                               seq_len=dseq, grid_rows=grid_rows, want_state=False, final_w=fw)
        gla_states.append(sg)
        ssd_states.append(ss)
    y_prompt = xp.reshape(nb, seq, D_MODEL)
    y_sample = xs.reshape(db, dseq, D_MODEL)
    return (y_prompt, y_sample, jnp.concatenate(gla_states, axis=1),
            jnp.concatenate(ssd_states, axis=1))
```
